```python
import math
import jax, jax.numpy as jnp
from jax import lax
import numpy as np

D_MODEL = 1024
BATCH = 8
SEQ = 8192
DEPTH = 2

N_EVEN = (DEPTH + 1) // 2
N_ODD = DEPTH // 2
BLOCK = 128
EPS = 1e-6

MLA_HEADS = 8
MLA_NOPE = 64
MLA_ROPE = 32
MLA_V = 64
MLA_Q_RANK = 256
MLA_KV_RANK = 128
ROPE_THETA = 10000.0

DIL_HEADS = 8
DIL_HD = 64
DIL_PATTERNS = ((128, 1), (512, 4), (2048, 16))

DIFF_HEADS = 4
DIFF_HD = 64

SB_HEADS = 8
SB_HD = 64

REL_BUCKETS = 32
REL_MAX_DIST = 2048
REL_MAPS = 8

D_FF = 2816
N_EXPERTS = 8
TOP_K = 2
D_FF_EXPERT = 3584

MLA_IN = MLA_Q_RANK + MLA_KV_RANK + MLA_ROPE
EVEN_IN = MLA_IN + 3 * DIL_HEADS * DIL_HD
EVEN_MIX = MLA_HEADS * MLA_V + DIL_HEADS * DIL_HD
DIFF_W = DIFF_HEADS * 2 * DIFF_HD
ODD_IN = 3 * DIFF_W + 3 * SB_HEADS * SB_HD
ODD_MIX = DIFF_W + SB_HEADS * SB_HD

kernel_name = 'hybrid_mla_dilated_diff_stickbreak_moe'

F32 = jnp.float32


def rmsnorm(x, g):
    xf = x.astype(F32)
    y = xf * lax.rsqrt(jnp.mean(xf * xf, axis=-1, keepdims=True) + EPS) * g.astype(F32)
    return y.astype(x.dtype)


def ada_modulation(c, w, b):
    m = jax.nn.silu(c) @ w + b
    shift, scale, gate = jnp.split(m, 3, axis=-1)
    return shift, scale, gate


def modulate(h, shift, scale):
    return h * (1 + scale[:, None, :]) + shift[:, None, :]


def rope(x, pos):
    half = x.shape[-1] // 2
    freqs = ROPE_THETA ** (-jnp.arange(half, dtype=F32) / half)
    ang = pos.astype(F32)[:, None] * freqs[None, :]
    cos = jnp.cos(ang)[:, None, :]
    sin = jnp.sin(ang)[:, None, :]
    x1 = x[..., :half].astype(F32)
    x2 = x[..., half:].astype(F32)
    return jnp.concatenate([x1 * cos - x2 * sin, x1 * sin + x2 * cos], axis=-1).astype(x.dtype)


def t5_bucket(dist):
    max_exact = REL_BUCKETS // 2
    d = jnp.maximum(dist, 1).astype(F32)
    log_b = max_exact + (jnp.log(d / max_exact) / math.log(REL_MAX_DIST / max_exact)
                         * (REL_BUCKETS - max_exact)).astype(jnp.int32)
    log_b = jnp.minimum(log_b, REL_BUCKETS - 1)
    return jnp.where(dist < max_exact, dist, log_b)


def to_query_blocks(t):
    b, s = t.shape[:2]
    t = t.reshape((b, s // BLOCK, BLOCK) + t.shape[2:])
    return jnp.moveaxis(t, 1, 0)


def from_query_blocks(t):
    t = jnp.moveaxis(t, 0, 1)
    return t.reshape((t.shape[0], t.shape[1] * t.shape[2]) + t.shape[3:])


def mla_attention(q_nope, q_rope, k_nope, k_rope, v):
    s_len = q_nope.shape[1]
    scale = (MLA_NOPE + MLA_ROPE) ** -0.5
    kpos = jnp.arange(s_len)

    def block(args):
        qn, qr, bi = args
        qpos = bi * BLOCK + jnp.arange(BLOCK)
        s = (jnp.einsum('bqhd,bkhd->bhqk', qn, k_nope, preferred_element_type=F32)
             + jnp.einsum('bqhr,bkr->bhqk', qr, k_rope, preferred_element_type=F32)) * scale
        s = jnp.where(kpos[None, :] <= qpos[:, None], s, -jnp.inf)
        p = jax.nn.softmax(s, axis=-1)
        return jnp.einsum('bhqk,bkhd->bqhd', p.astype(v.dtype), v)

    nb = s_len // BLOCK
    o = lax.map(block, (to_query_blocks(q_nope), to_query_blocks(q_rope), jnp.arange(nb)))
    return from_query_blocks(o)


def dilated_attention(q, k, v, rel_bias):
    b, s_len, h, dh = q.shape
    scale = dh ** -0.5
    outs, lses = [], []
    for window, dil in DIL_PATTERNS:
        nw = window // dil
        sub_len = s_len // dil
        pad = (-sub_len) % BLOCK
        nb = (sub_len + pad) // BLOCK

        def to_blocks(t):
            t = t.reshape(b, sub_len, dil, h, dh)
            t = jnp.pad(t, ((0, 0), (0, pad), (0, 0), (0, 0), (0, 0)))
            return t.reshape(b, nb, BLOCK, dil, h, dh)

        def with_prev(t):
            prev = jnp.pad(t, ((0, 0), (1, 0), (0, 0), (0, 0), (0, 0), (0, 0)))[:, :-1]
            return jnp.concatenate([prev, t], axis=2)

        qb = to_blocks(q)
        kk = with_prev(to_blocks(k))
        vv = with_prev(to_blocks(v))
        qi = jnp.arange(BLOCK)
        kj = jnp.arange(2 * BLOCK)
        rel = BLOCK + qi[:, None] - kj[None, :]
        band = (rel >= 0) & (rel <= nw)
        blk = jnp.arange(nb)
        valid = band[None] & (((blk[:, None] - 1) * BLOCK + kj[None, :]) >= 0)[:, None, :]
        bias = rel_bias[t5_bucket(jnp.maximum(rel, 0) * dil)]
        bias = jnp.transpose(bias, (2, 0, 1)).astype(F32)
        sc = jnp.einsum('bnqrhd,bnkrhd->bnrhqk', qb, kk, preferred_element_type=F32) * scale + bias
        sc = jnp.where(valid[None, :, None, None], sc, -jnp.inf)
        m = jnp.max(sc, axis=-1, keepdims=True)
        e = jnp.exp(sc - m)
        den = jnp.sum(e, axis=-1, keepdims=True)
        o = jnp.einsum('bnrhqk,bnkrhd->bnqrhd', (e / den).astype(v.dtype), vv, preferred_element_type=F32)
        lse = (m + jnp.log(den))[..., 0]
        o = o.reshape(b, nb * BLOCK, dil, h, dh)[:, :sub_len].reshape(b, s_len, h, dh)
        lse = jnp.transpose(lse, (0, 1, 4, 2, 3)).reshape(b, nb * BLOCK, dil, h)[:, :sub_len].reshape(b, s_len, h)
        outs.append(o)
        lses.append(lse)
    wts = jax.nn.softmax(jnp.stack(lses, axis=-1), axis=-1)
    out = jnp.einsum('bshg,bshgd->bshd', wts, jnp.stack(outs, axis=3))
    return out.astype(q.dtype)


def diff_attention(q, k, v, lam, rel_bias):
    s_len = q.shape[1]
    scale = DIFF_HD ** -0.5
    kpos = jnp.arange(s_len)

    def block(args):
        qb, bi = args
        qpos = bi * BLOCK + jnp.arange(BLOCK)
        dist = qpos[:, None] - kpos[None, :]
        bias = rel_bias[t5_bucket(jnp.maximum(dist, 0))].reshape(BLOCK, s_len, DIFF_HEADS, 2)
        bias = jnp.transpose(bias, (2, 3, 0, 1)).astype(F32)
        s = jnp.einsum('bqhmd,bkhmd->bhmqk', qb, k, preferred_element_type=F32) * scale + bias
        s = jnp.where(dist >= 0, s, -jnp.inf)
        p = jax.nn.softmax(s, axis=-1)
        a = p[:, :, 0] - lam * p[:, :, 1]
        return jnp.einsum('bhqk,bkhd->bqhd', a.astype(v.dtype), v)

    nb = s_len // BLOCK
    o = lax.map(block, (to_query_blocks(q), jnp.arange(nb)))
    return from_query_blocks(o)


def stick_breaking_attention(q, k, v):
    s_len = q.shape[1]
    scale = SB_HD ** -0.5
    kpos = jnp.arange(s_len)

    def block(args):
        qb, bi = args
        qpos = bi * BLOCK + jnp.arange(BLOCK)
        strict = kpos[None, :] < qpos[:, None]
        z = jnp.einsum('bqhd,bkhd->bhqk', qb, k, preferred_element_type=F32) * scale
        log_1m = jnp.where(strict, jax.nn.log_sigmoid(-z), 0.0)
        after = lax.cumsum(log_1m, axis=3, reverse=True) - log_1m
        w = jnp.where(strict, jnp.exp(jax.nn.log_sigmoid(z) + after), 0.0)
        return jnp.einsum('bhqk,bkhd->bqhd', w.astype(v.dtype), v)

    nb = s_len // BLOCK
    o = lax.map(block, (to_query_blocks(q), jnp.arange(nb)))
    return from_query_blocks(o)


def even_mixer(h, rel_bias, w_in, q_norm_g, w_uq, kv_norm_g, w_ukv, w_out):
    b, s_len, _ = h.shape
    pos = jnp.arange(s_len)
    proj = h @ w_in
    cq, ckv, kr, qkv = jnp.split(proj, [MLA_Q_RANK, MLA_Q_RANK + MLA_KV_RANK, MLA_IN], axis=-1)
    q = (rmsnorm(cq, q_norm_g) @ w_uq).reshape(b, s_len, MLA_HEADS, MLA_NOPE + MLA_ROPE)
    q_nope = q[..., :MLA_NOPE]
    q_rope = rope(q[..., MLA_NOPE:], pos)
    kv = (rmsnorm(ckv, kv_norm_g) @ w_ukv).reshape(b, s_len, MLA_HEADS, MLA_NOPE + MLA_V)
    k_nope = kv[..., :MLA_NOPE]
    v_a = kv[..., MLA_NOPE:]
    k_rope = rope(kr[:, :, None, :], pos)[:, :, 0]
    o_a = mla_attention(q_nope, q_rope, k_nope, k_rope, v_a)
    qkv = qkv.reshape(b, s_len, 3, DIL_HEADS, DIL_HD)
    o_b = dilated_attention(qkv[:, :, 0], qkv[:, :, 1], qkv[:, :, 2], rel_bias)
    o = jnp.concatenate([o_a.reshape(b, s_len, -1), o_b.reshape(b, s_len, -1)], axis=-1)
    return o @ w_out


def odd_mixer(h, layer, rel_bias, w_in, lq1, lk1, lq2, lk2, sub_g, w_out):
    b, s_len, _ = h.shape
    proj = h @ w_in
    qc, kc, vc, sb = jnp.split(proj, [DIFF_W, 2 * DIFF_W, 3 * DIFF_W], axis=-1)
    lam_init = 0.8 - 0.6 * math.exp(-0.3 * layer)
    lam = (jnp.exp(jnp.sum(lq1.astype(F32) * lk1.astype(F32)))
           - jnp.exp(jnp.sum(lq2.astype(F32) * lk2.astype(F32))) + lam_init)
    qd = qc.reshape(b, s_len, DIFF_HEADS, 2, DIFF_HD)
    kd = kc.reshape(b, s_len, DIFF_HEADS, 2, DIFF_HD)
    vd = vc.reshape(b, s_len, DIFF_HEADS, 2 * DIFF_HD)
    o_c = diff_attention(qd, kd, vd, lam, rel_bias)
    o_c = rmsnorm(o_c, sub_g) * (1 - lam_init)
    sb = sb.reshape(b, s_len, 3, SB_HEADS, SB_HD)
    o_d = stick_breaking_attention(sb[:, :, 0], sb[:, :, 1], sb[:, :, 2])
    o = jnp.concatenate([o_c.reshape(b, s_len, -1), o_d.reshape(b, s_len, -1)], axis=-1)
    return o @ w_out


def swiglu(t, w_gate, w_up, w_down):
    return (jax.nn.silu(t @ w_gate) * (t @ w_up)) @ w_down


def moe(h, router_w, router_b, w_gate, w_up, w_down):
    b, s_len, d = h.shape
    t = h.reshape(-1, d)
    logits = jnp.dot(t, router_w, preferred_element_type=F32) + router_b.astype(F32)
    top_val, top_idx = lax.top_k(logits, TOP_K)
    top_w = jax.nn.softmax(top_val, axis=-1)
    gates = jnp.sum(jax.nn.one_hot(top_idx, N_EXPERTS, dtype=F32) * top_w[..., None], axis=1)
    out = jnp.zeros_like(t)
    for e in range(N_EXPERTS):
        out = out + gates[:, e:e + 1].astype(t.dtype) * swiglu(t, w_gate[e], w_up[e], w_down[e])
    return out.reshape(b, s_len, d)


def setup_inputs(seed: int = 0) -> dict:
    key = jax.random.key(seed)
    keys = jax.random.split(key, 40)
    counter = [0]

    def nxt():
        k = keys[counter[0]]
        counter[0] += 1
        return k

    def nrm(shape, scale):
        return jax.random.normal(nxt(), shape, F32) * scale

    def gain(shape):
        return 1.0 + 0.05 * jax.random.normal(nxt(), shape, F32)

    d = D_MODEL
    inp = {}
    inp['x'] = nrm((BATCH, SEQ, d), 1.0)
    inp['c'] = nrm((BATCH, d), 1.0)
    inp['rel_bias'] = nrm((REL_BUCKETS, REL_MAPS), 0.2)
    inp['ada_mix_w'] = nrm((DEPTH, d, 3 * d), d ** -0.5)
    inp['ada_mix_b'] = nrm((DEPTH, 3 * d), 0.01)
    inp['mix_pre_g'] = gain((DEPTH, d))
    inp['mix_post_g'] = gain((DEPTH, d))
    inp['ada_ffn_w'] = nrm((DEPTH, d, 3 * d), d ** -0.5)
    inp['ada_ffn_b'] = nrm((DEPTH, 3 * d), 0.01)
    inp['ffn_pre_g'] = gain((DEPTH, d))
    inp['ffn_post_g'] = gain((DEPTH, d))
    inp['e_w_in'] = nrm((N_EVEN, d, EVEN_IN), d ** -0.5)
    inp['e_q_norm_g'] = gain((N_EVEN, MLA_Q_RANK))
    inp['e_w_uq'] = nrm((N_EVEN, MLA_Q_RANK, MLA_HEADS * (MLA_NOPE + MLA_ROPE)), MLA_Q_RANK ** -0.5)
    inp['e_kv_norm_g'] = gain((N_EVEN, MLA_KV_RANK))
    inp['e_w_ukv'] = nrm((N_EVEN, MLA_KV_RANK, MLA_HEADS * (MLA_NOPE + MLA_V)), MLA_KV_RANK ** -0.5)
    inp['e_w_out'] = nrm((N_EVEN, EVEN_MIX, d), EVEN_MIX ** -0.5)
    inp['ffn_w_gate'] = nrm((N_EVEN, d, D_FF), d ** -0.5)
    inp['ffn_w_up'] = nrm((N_EVEN, d, D_FF), d ** -0.5)
    inp['ffn_w_down'] = nrm((N_EVEN, D_FF, d), D_FF ** -0.5)
    inp['o_w_in'] = nrm((N_ODD, d, ODD_IN), d ** -0.5)
    inp['diff_lq1'] = nrm((N_ODD, DIFF_HD), 0.1)
    inp['diff_lk1'] = nrm((N_ODD, DIFF_HD), 0.1)
    inp['diff_lq2'] = nrm((N_ODD, DIFF_HD), 0.1)
    inp['diff_lk2'] = nrm((N_ODD, DIFF_HD), 0.1)
    inp['diff_sub_g'] = gain((N_ODD, 2 * DIFF_HD))
    inp['o_w_out'] = nrm((N_ODD, ODD_MIX, d), ODD_MIX ** -0.5)
    inp['router_w'] = nrm((N_ODD, d, N_EXPERTS), d ** -0.5)
    inp['router_b'] = nrm((N_ODD, N_EXPERTS), 0.01)
    inp['moe_w_gate'] = nrm((N_ODD, N_EXPERTS, d, D_FF_EXPERT), d ** -0.5)
    inp['moe_w_up'] = nrm((N_ODD, N_EXPERTS, d, D_FF_EXPERT), d ** -0.5)
    inp['moe_w_down'] = nrm((N_ODD, N_EXPERTS, D_FF_EXPERT, d), D_FF_EXPERT ** -0.5)
    return inp


def reference(x, c, rel_bias, ada_mix_w, ada_mix_b, mix_pre_g, mix_post_g, ada_ffn_w, ada_ffn_b,
              ffn_pre_g, ffn_post_g, e_w_in, e_q_norm_g, e_w_uq, e_kv_norm_g, e_w_ukv, e_w_out,
              ffn_w_gate, ffn_w_up, ffn_w_down, o_w_in, diff_lq1, diff_lk1, diff_lq2, diff_lk2,
              diff_sub_g, o_w_out, router_w, router_b, moe_w_gate, moe_w_up, moe_w_down):
    for layer in range(DEPTH):
        i = layer // 2
        shift, scale, gate = ada_modulation(c, ada_mix_w[layer], ada_mix_b[layer])
        h = modulate(rmsnorm(x, mix_pre_g[layer]), shift, scale)
        if layer % 2 == 0:
            y = even_mixer(h, rel_bias, e_w_in[i], e_q_norm_g[i], e_w_uq[i], e_kv_norm_g[i],
                           e_w_ukv[i], e_w_out[i])
        else:
            y = odd_mixer(h, layer, rel_bias, o_w_in[i], diff_lq1[i], diff_lk1[i], diff_lq2[i],
                          diff_lk2[i], diff_sub_g[i], o_w_out[i])
        x = x + gate[:, None, :] * rmsnorm(y, mix_post_g[layer])
        shift, scale, gate = ada_modulation(c, ada_ffn_w[layer], ada_ffn_b[layer])
        h = modulate(rmsnorm(x, ffn_pre_g[layer]), shift, scale)
        if layer % 2 == 0:
            y = swiglu(h, ffn_w_gate[i], ffn_w_up[i], ffn_w_down[i])
        else:
            y = moe(h, router_w[i], router_b[i], moe_w_gate[i], moe_w_up[i], moe_w_down[i])
        x = x + gate[:, None, :] * rmsnorm(y, ffn_post_g[layer])
    return x
```

```python
import functools
import math

import jax
import jax.numpy as jnp
from jax import lax
from jax.experimental import pallas as pl
from jax.experimental.pallas import tpu as pltpu

F32 = jnp.float32
BF16 = jnp.bfloat16

D_MODEL = 1024
EPS = 1e-6

MLA_HEADS = 8
MLA_NOPE = 64
MLA_ROPE = 32
MLA_V = 64
MLA_Q_RANK = 256
MLA_KV_RANK = 128
ROPE_THETA = 10000.0

DIL_HEADS = 8
DIL_HD = 64
DIL_PATTERNS = ((128, 1), (512, 4), (2048, 16))
DIL_BLOCK = 128

DIFF_HEADS = 4
DIFF_HD = 64
SB_HEADS = 8
SB_HD = 64

REL_BUCKETS = 32
REL_MAX_DIST = 2048

D_FF = 2816
N_EXPERTS = 8
D_FF_EXPERT = 3584

LANES = 128
LOG2E = math.log2(math.e)
NEG = -1e30

TOK_TILE = 512
MLA_TILE = 512
ATT_TILE = 256
DIL_SUPER = DIL_BLOCK * 16
MOE_TILE = 512
SB_LOG_FLOOR = -104.0

VMEM_LIMIT = 56 * 1024 * 1024


def _cparams(sem):
    return pltpu.CompilerParams(dimension_semantics=sem, vmem_limit_bytes=VMEM_LIMIT)


def _resident(shape, index_map):
    return pl.BlockSpec(shape, index_map, pipeline_mode=pl.Buffered(1))


def _rms(x, g):
    return x * lax.rsqrt(jnp.mean(x * x, axis=-1, keepdims=True) + EPS) * g


def _dot(a, b):
    return jnp.dot(a, b, preferred_element_type=F32)


def _ada_kernel(c_ref, w_ref, b_ref, o_ref):
    c = c_ref[...]
    sc = c / (1.0 + jnp.exp(-c))
    o_ref[0] = _dot(sc.astype(BF16), w_ref[0].astype(BF16)) + b_ref[0]


def _ada(c, w, b):
    nl, d, d3 = w.shape
    bsz = c.shape[0]
    nb = d3 // d
    return pl.pallas_call(
        _ada_kernel,
        grid=(nl, nb),
        in_specs=[
            pl.BlockSpec((bsz, d), lambda l, j: (0, 0)),
            pl.BlockSpec((1, d, d), lambda l, j: (l, 0, j)),
            pl.BlockSpec((1, 1, d), lambda l, j: (l, 0, j)),
        ],
        out_specs=pl.BlockSpec((1, bsz, d), lambda l, j: (l, 0, j)),
        out_shape=jax.ShapeDtypeStruct((nl, bsz, d3), F32),
        compiler_params=_cparams(("arbitrary", "arbitrary")),
        name="ada",
    )(c, w, b.reshape(nl, 1, d3))


def _split_mod(m):
    b = m.shape[0]
    m = m.reshape(b, 3, 1, D_MODEL)
    return m[:, 0], m[:, 1], m[:, 2]


def _prenorm_mod(x, g, shift, scale):
    return _rms(x, g) * (1.0 + scale) + shift


def _even_in_kernel(x_ref, g_ref, sh_ref, sc_ref, w0_ref, qg_ref, wq_ref, kvg_ref, wkv_ref,
                    cq_ref, ck_ref, sn_ref,
                    qa_ref, kt_ref, va_ref, qb_ref, kb_ref, vb_ref, *, tkb):
    tm = x_ref.shape[1]
    h = _prenorm_mod(x_ref[0], g_ref[...], sh_ref[0], sc_ref[0]).astype(BF16)
    proj = _dot(h, w0_ref[...])
    cqn = _rms(proj[:, 0:256], qg_ref[...]).astype(BF16)
    qq = _dot(cqn, wq_ref[...])
    ckvn = _rms(proj[:, 256:384], kvg_ref[...]).astype(BF16)
    kv = _dot(ckvn, wkv_ref[...])
    cq = cq_ref[...]
    ck = ck_ref[...]
    sn = sn_ref[...]
    krope = proj[:, 384:512] * ck + proj[:, 512:640] * sn
    nh = MLA_HEADS
    for hd in range(nh):
        lo = hd * LANES
        qh = qq[:, lo:lo + LANES] * cq + qq[:, nh * LANES + lo:nh * LANES + lo + LANES] * sn
        qa_ref[0, :, lo:lo + LANES] = qh.astype(BF16)
        kh = kv[:, lo:lo + LANES] + krope
        for t in range(tm // tkb):
            kt_ref[0, hd, t] = kh[t * tkb:(t + 1) * tkb, :].T.astype(BF16)
    va_ref[0] = kv[:, nh * LANES:nh * LANES + 512].astype(BF16)
    qb_ref[0] = proj[:, 640:1152].astype(BF16)
    kb_ref[0] = proj[:, 1152:1664].astype(BF16)
    vb_ref[0] = proj[:, 1664:2176].astype(BF16)


def _even_in(x, g, shift, scale, w0, qg, wq, kvg, wkv, cq, ck, sn):
    b, s, d = x.shape
    tm = TOK_TILE
    tkb = MLA_TILE
    ns = s // tm
    tok = lambda w: pl.BlockSpec((1, tm, w), lambda bi, i: (bi, i, 0))
    vec = lambda w: pl.BlockSpec((1, w), lambda bi, i: (0, 0))
    mod = pl.BlockSpec((1, 1, d), lambda bi, i: (bi, 0, 0))
    tab = pl.BlockSpec((tm, LANES), lambda bi, i: (i, 0))
    full = lambda a: _resident(a.shape, lambda bi, i: (0,) * a.ndim)
    out_shapes = (
        jax.ShapeDtypeStruct((b, s, MLA_HEADS * LANES), BF16),
        jax.ShapeDtypeStruct((b, MLA_HEADS, s // tkb, LANES, tkb), BF16),
        jax.ShapeDtypeStruct((b, s, 512), BF16),
        jax.ShapeDtypeStruct((b, s, 512), BF16),
        jax.ShapeDtypeStruct((b, s, 512), BF16),
        jax.ShapeDtypeStruct((b, s, 512), BF16),
    )
    out_specs = (
        tok(MLA_HEADS * LANES),
        pl.BlockSpec((1, MLA_HEADS, tm // tkb, LANES, tkb), lambda bi, i: (bi, 0, i, 0, 0)),
        tok(512), tok(512), tok(512), tok(512),
    )
    return pl.pallas_call(
        functools.partial(_even_in_kernel, tkb=tkb),
        grid=(b, ns),
        in_specs=[tok(d), vec(d), mod, mod, full(w0), vec(MLA_Q_RANK), full(wq), vec(MLA_KV_RANK), full(wkv),
                  tab, tab, tab],
        out_specs=out_specs,
        out_shape=out_shapes,
        compiler_params=_cparams(("parallel", "parallel")),
        name="even_in",
    )(x, g, shift, scale, w0, qg, wq, kvg, wkv, cq, ck, sn)


def _odd_in_kernel(x_ref, g_ref, sh_ref, sc_ref, w_ref,
                   qd_ref, kdt_ref, vd_ref, qs_ref, kst_ref, vs_ref, *, tkb):
    tm = x_ref.shape[1]
    h = _prenorm_mod(x_ref[0], g_ref[...], sh_ref[0], sc_ref[0]).astype(BF16)
    proj = _dot(h, w_ref[...])
    qd_ref[0] = proj[:, 0:512].astype(BF16)
    vd_ref[0] = proj[:, 1024:1536].astype(BF16)
    qs_ref[0] = proj[:, 1536:2048].astype(BF16)
    vs_ref[0] = proj[:, 2560:3072].astype(BF16)
    for hd in range(4):
        kd = proj[:, 512 + hd * LANES:512 + (hd + 1) * LANES]
        ks = proj[:, 2048 + hd * LANES:2048 + (hd + 1) * LANES]
        for t in range(tm // tkb):
            kdt_ref[0, hd, t] = kd[t * tkb:(t + 1) * tkb, :].T.astype(BF16)
            kst_ref[0, hd, t] = ks[t * tkb:(t + 1) * tkb, :].T.astype(BF16)


def _odd_in(x, g, shift, scale, w):
    b, s, d = x.shape
    tm = TOK_TILE
    tkb = ATT_TILE
    ns = s // tm
    tok = lambda wd: pl.BlockSpec((1, tm, wd), lambda bi, i: (bi, i, 0))
    mod = pl.BlockSpec((1, 1, d), lambda bi, i: (bi, 0, 0))
    ktspec = pl.BlockSpec((1, 4, tm // tkb, LANES, tkb), lambda bi, i: (bi, 0, i, 0, 0))
    act = jax.ShapeDtypeStruct((b, s, 512), BF16)
    kts = jax.ShapeDtypeStruct((b, 4, s // tkb, LANES, tkb), BF16)
    return pl.pallas_call(
        functools.partial(_odd_in_kernel, tkb=tkb),
        grid=(b, ns),
        in_specs=[tok(d), pl.BlockSpec((1, d), lambda bi, i: (0, 0)), mod, mod,
                  _resident(w.shape, lambda bi, i: (0, 0))],
        out_specs=(tok(512), ktspec, tok(512), tok(512), ktspec, tok(512)),
        out_shape=(act, kts, act, act, kts, act),
        compiler_params=_cparams(("parallel", "parallel")),
        name="odd_in",
    )(x, g, shift, scale, w)


def _mla_kernel(q_ref, kt_ref, v_ref, o_ref):
    tq = q_ref.shape[1]
    tk = kt_ref.shape[4]
    qi = pl.program_id(2)
    lane = lax.broadcasted_iota(jnp.int32, (tq, LANES), 1)
    causal = (lax.broadcasted_iota(jnp.int32, (tq, tk), 1)
              <= lax.broadcasted_iota(jnp.int32, (tq, tk), 0))
    res = []
    for hd in range(2):
        q = q_ref[0, :, hd * LANES:(hd + 1) * LANES]

        def step(j, carry, masked, hd=hd, q=q):
            m, l, acc = carry
            s = _dot(q, kt_ref[0, hd, j])
            if masked:
                s = jnp.where(causal, s, NEG)
            m_new = jnp.maximum(m, jnp.max(s, axis=-1, keepdims=True))
            alpha = jnp.exp2(m - m_new)
            p = jnp.exp2(s - m_new)
            l = alpha * l + jnp.sum(p, axis=-1, keepdims=True)
            v = v_ref[0, pl.ds(pl.multiple_of(j * tk, tk), tk), :]
            acc = alpha * acc + _dot(p.astype(BF16), v)
            return m_new, l, acc

        init = (jnp.full((tq, 1), NEG, F32), jnp.zeros((tq, 1), F32), jnp.zeros((tq, LANES), F32))
        carry = lax.fori_loop(0, qi, functools.partial(step, masked=False), init)
        _, l, acc = step(qi, carry, True)
        res.append(acc / l)
    o_ref[0] = jnp.where(lane < 64, res[0], res[1]).astype(BF16)


def _mla(q, kt, v):
    b, s, _ = q.shape
    tq = MLA_TILE
    nk = kt.shape[2]
    tk = kt.shape[4]
    return pl.pallas_call(
        _mla_kernel,
        grid=(b, MLA_HEADS // 2, s // tq),
        in_specs=[
            pl.BlockSpec((1, tq, 2 * LANES), lambda bi, hp, qi: (bi, qi, hp)),
            pl.BlockSpec((1, 2, nk, LANES, tk), lambda bi, hp, qi: (bi, hp, 0, 0, 0)),
            pl.BlockSpec((1, s, LANES), lambda bi, hp, qi: (bi, 0, hp)),
        ],
        out_specs=pl.BlockSpec((1, tq, LANES), lambda bi, hp, qi: (bi, qi, hp)),
        out_shape=jax.ShapeDtypeStruct((b, s, 512), BF16),
        compiler_params=_cparams(("parallel", "parallel", "arbitrary")),
        name="mla",
    )(q, kt, v)


def _dil_kernel(q_ref, kc_ref, kp_ref, vc_ref, vp_ref, bias_ref, o_ref,
                q32, k32, v32, acc_s, m_s, d_s):
    sup = DIL_SUPER
    blk = DIL_BLOCK
    n = pl.program_id(2)
    q32[...] = q_ref[0].astype(F32)
    k32[0:sup, :] = kp_ref[0].astype(F32)
    k32[sup:2 * sup, :] = kc_ref[0].astype(F32)
    v32[0:sup, :] = vp_ref[0].astype(F32)
    v32[sup:2 * sup, :] = vc_ref[0].astype(F32)
    low = lax.broadcasted_iota(jnp.int32, (blk, LANES), 1) < 64
    before_start = jnp.where(lax.broadcasted_iota(jnp.int32, (blk, 2 * blk), 1) < blk, NEG, 0.0)

    for g, (_, dil) in enumerate(DIL_PATTERNS):

        def unit(u, carry, g=g, dil=dil):
            n_loc = u // dil
            r = u % dil
            qs = n_loc * (blk * dil) + r
            ks = sup + (n_loc - 1) * (blk * dil) + r
            if dil == 1:
                qsl = pl.ds(pl.multiple_of(qs, blk), blk)
                ksl = pl.ds(pl.multiple_of(ks, blk), 2 * blk)
            else:
                qsl = pl.ds(qs, blk, stride=dil)
                ksl = pl.ds(ks, 2 * blk, stride=dil)
            q = q32[qsl, :]
            k = k32[ksl, :].astype(BF16)
            v = v32[ksl, :].astype(BF16)
            extra = jnp.where(jnp.logical_and(n == 0, n_loc == 0), before_start, 0.0)
            parts = []
            for hd in range(2):
                qh = jnp.where(low if hd == 0 else jnp.logical_not(low), q, 0.0).astype(BF16)
                s = lax.dot_general(qh, k, (((1,), (1,)), ((), ())), preferred_element_type=F32)
                s = s + bias_ref[g, hd] + extra
                m = jnp.max(s, axis=-1, keepdims=True)
                e = jnp.exp2(s - m)
                den = jnp.sum(e, axis=-1, keepdims=True)
                parts.append((_dot(e.astype(BF16), v), m, den))
            acc_s[g, qsl, :] = jnp.where(low, parts[0][0], parts[1][0])
            m_s[g, qsl, :] = jnp.where(low, parts[0][1], parts[1][1])
            d_s[g, qsl, :] = jnp.where(low, parts[0][2], parts[1][2])
            return carry

        lax.fori_loop(0, 16, unit, 0)

    mx = jnp.maximum(jnp.maximum(m_s[0], m_s[1]), m_s[2])
    num = jnp.zeros((sup, LANES), F32)
    den = jnp.zeros((sup, LANES), F32)
    for g in range(3):
        a = jnp.exp2(m_s[g] - mx)
        num = num + a * acc_s[g]
        den = den + a * d_s[g]
    o_ref[0] = (num / den).astype(BF16)


def _dil(q, k, v, bias):
    b, s, _ = q.shape
    sup = DIL_SUPER
    cur = pl.BlockSpec((1, sup, LANES), lambda bi, hp, n: (bi, n, hp))
    prev = pl.BlockSpec((1, sup, LANES), lambda bi, hp, n: (bi, jnp.maximum(n - 1, 0), hp))
    return pl.pallas_call(
        _dil_kernel,
        grid=(b, DIL_HEADS // 2, s // sup),
        in_specs=[cur, cur, prev, cur, prev,
                  pl.BlockSpec((3, 2, DIL_BLOCK, 2 * DIL_BLOCK), lambda bi, hp, n: (0, hp, 0, 0))],
        out_specs=cur,
        out_shape=jax.ShapeDtypeStruct((b, s, 512), BF16),
        scratch_shapes=[
            pltpu.VMEM((sup, LANES), F32),
            pltpu.VMEM((2 * sup, LANES), F32),
            pltpu.VMEM((2 * sup, LANES), F32),
            pltpu.VMEM((3, sup, LANES), F32),
            pltpu.VMEM((3, sup, LANES), F32),
            pltpu.VMEM((3, sup, LANES), F32),
        ],
        compiler_params=_cparams(("parallel", "parallel", "arbitrary")),
        name="dilated",
    )(q, k, k, v, v, bias)


def _diff_kernel(q_ref, kt_ref, v_ref, bias_ref, far_ref, lam_ref, g_ref, o_ref, *, lam_init):
    tq = q_ref.shape[1]
    tk = kt_ref.shape[4]
    nd = bias_ref.shape[2]
    qi = pl.program_id(2)
    lane = lax.broadcasted_iota(jnp.int32, (tq, LANES), 1)
    q = q_ref[0]
    zero = jnp.zeros_like(q)
    qm = (jnp.where(lane < 64, q, zero), jnp.where(lane >= 64, q, zero))

    def step(j, carry, bias_of):
        k = kt_ref[0, 0, j]
        v = v_ref[0, pl.ds(pl.multiple_of(j * tk, tk), tk), :]
        out = []
        for mi in range(2):
            m, l, acc = carry[mi]
            s = _dot(qm[mi], k) + bias_of(mi, j)
            m_new = jnp.maximum(m, jnp.max(s, axis=-1, keepdims=True))
            alpha = jnp.exp2(m - m_new)
            p = jnp.exp2(s - m_new)
            l = alpha * l + jnp.sum(p, axis=-1, keepdims=True)
            acc = alpha * acc + _dot(p.astype(BF16), v)
            out.append((m_new, l, acc))
        return tuple(out)

    one = (jnp.full((tq, 1), NEG, F32), jnp.zeros((tq, 1), F32), jnp.zeros((tq, LANES), F32))
    carry = (one, one)
    n_far = jnp.maximum(qi - nd + 1, 0)
    carry = lax.fori_loop(0, n_far, functools.partial(step, bias_of=lambda mi, j: far_ref[0, mi, 0:1, 0:1]), carry)
    carry = lax.fori_loop(n_far, qi, functools.partial(step, bias_of=lambda mi, j: bias_ref[0, mi, qi - j]), carry)
    carry = step(qi, carry, lambda mi, j: bias_ref[0, mi, 0])
    (_, l0, a0), (_, l1, a1) = carry
    o = a0 / l0 - lam_ref[...] * (a1 / l1)
    o_ref[0] = (_rms(o, g_ref[...]) * (1.0 - lam_init)).astype(BF16)


def _diff(q, kt, v, bias, far, lam, sub_g, lam_init):
    b, s, _ = q.shape
    tq = ATT_TILE
    nk, tk = kt.shape[2], kt.shape[4]
    nd = bias.shape[2]
    return pl.pallas_call(
        functools.partial(_diff_kernel, lam_init=lam_init),
        grid=(DIFF_HEADS, b, s // tq),
        in_specs=[
            pl.BlockSpec((1, tq, LANES), lambda h, bi, qi: (bi, qi, h)),
            pl.BlockSpec((1, 1, nk, LANES, tk), lambda h, bi, qi: (bi, h, 0, 0, 0)),
            pl.BlockSpec((1, s, LANES), lambda h, bi, qi: (bi, 0, h)),
            pl.BlockSpec((1, 2, nd, tq, tk), lambda h, bi, qi: (h, 0, 0, 0, 0)),
            pl.BlockSpec((1, 2, 8, LANES), lambda h, bi, qi: (h, 0, 0, 0)),
            pl.BlockSpec((1, LANES), lambda h, bi, qi: (0, 0)),
            pl.BlockSpec((1, LANES), lambda h, bi, qi: (0, 0)),
        ],
        out_specs=pl.BlockSpec((1, tq, LANES), lambda h, bi, qi: (bi, qi, h)),
        out_shape=jax.ShapeDtypeStruct((b, s, 512), BF16),
        compiler_params=_cparams(("parallel", "parallel", "arbitrary")),
        name="diff",
    )(q, kt, v, bias, far, lam, sub_g)


def _sb_kernel(q_ref, kt_ref, v_ref, o_ref):
    tq = q_ref.shape[1]
    tk = kt_ref.shape[4]
    qi = pl.program_id(2)
    lane = lax.broadcasted_iota(jnp.int32, (tq, LANES), 1)
    strict = (lax.broadcasted_iota(jnp.int32, (tq, tk), 1)
              < lax.broadcasted_iota(jnp.int32, (tq, tk), 0))
    later = (lax.broadcasted_iota(jnp.int32, (tk, tk), 0)
             > lax.broadcasted_iota(jnp.int32, (tk, tk), 1)).astype(F32)
    q = q_ref[0]
    zero = jnp.zeros_like(q)
    res = []
    for hd in range(2):
        qh = jnp.where(lane < 64 if hd == 0 else lane >= 64, q, zero)

        def block(j, c, acc, masked, qh=qh):
            z = _dot(qh, kt_ref[0, 0, j])
            sp = jnp.maximum(z, 0.0) + jnp.log(1.0 + jnp.exp(-jnp.abs(z)))
            log_1m = -sp
            if masked:
                log_1m = jnp.where(strict, log_1m, 0.0)
            after = _dot(log_1m, later) + c
            w = jnp.exp(z - sp + after)
            if masked:
                w = jnp.where(strict, w, 0.0)
            v = v_ref[0, pl.ds(pl.multiple_of(j * tk, tk), tk), :]
            acc = acc + _dot(w.astype(BF16), v)
            c = c + jnp.sum(log_1m, axis=-1, keepdims=True)
            return c, acc

        c, acc = block(qi, jnp.zeros((tq, 1), F32), jnp.zeros((tq, LANES), F32), True)

        def cond(st):
            j, c, _ = st
            return jnp.logical_and(j >= 0, jnp.max(c) > SB_LOG_FLOOR)

        def body(st, block=block):
            j, c, acc = st
            c, acc = block(j, c, acc, False)
            return j - 1, c, acc

        _, _, acc = lax.while_loop(cond, body, (qi - 1, c, acc))
        res.append(acc)
    o_ref[0] = jnp.where(lane < 64, res[0], res[1]).astype(BF16)


def _sb(q, kt, v):
    b, s, _ = q.shape
    tq = ATT_TILE
    nk, tk = kt.shape[2], kt.shape[4]
    return pl.pallas_call(
        _sb_kernel,
        grid=(b, SB_HEADS // 2, s // tq),
        in_specs=[
            pl.BlockSpec((1, tq, LANES), lambda bi, hp, qi: (bi, qi, hp)),
            pl.BlockSpec((1, 1, nk, LANES, tk), lambda bi, hp, qi: (bi, hp, 0, 0, 0)),
            pl.BlockSpec((1, s, LANES), lambda bi, hp, qi: (bi, 0, hp)),
        ],
        out_specs=pl.BlockSpec((1, tq, LANES), lambda bi, hp, qi: (bi, qi, hp)),
        out_shape=jax.ShapeDtypeStruct((b, s, 512), BF16),
        compiler_params=_cparams(("parallel", "parallel", "arbitrary")),
        name="stick_breaking",
    )(q, kt, v)


FF_CHUNKS = ((0, 768), (768, 1536), (1536, 2304), (2304, 2816))
EXPERT_CHUNKS = ((0, 1024), (1024, 2048), (2048, 3072), (3072, 3584))


def _swiglu(hb, wg_ref, wu_ref, wd_ref, chunks, lead):
    acc = None
    for c0, c1 in chunks:
        g = _dot(hb, wg_ref[lead + (slice(None), slice(c0, c1))])
        u = _dot(hb, wu_ref[lead + (slice(None), slice(c0, c1))])
        a = (g / (1.0 + jnp.exp(-g)) * u).astype(BF16)
        part = _dot(a, wd_ref[lead + (slice(c0, c1), slice(None))])
        acc = part if acc is None else acc + part
    return acc


def _mix_out(x_ref, oa_ref, ob_ref, wa_ref, wb_ref, gate_ref, pg_ref):
    y = _dot(oa_ref[0], wa_ref[...]) + _dot(ob_ref[0], wb_ref[...])
    return x_ref[0] + gate_ref[0] * _rms(y, pg_ref[...])


def _post_even_kernel(x_ref, oa_ref, ob_ref, wa_ref, wb_ref, gate_ref, pg_ref,
                      fg_ref, fsh_ref, fsc_ref, fgate_ref, fpg_ref, wg_ref, wu_ref, wd_ref, o_ref):
    x1 = _mix_out(x_ref, oa_ref, ob_ref, wa_ref, wb_ref, gate_ref, pg_ref)
    hb = _prenorm_mod(x1, fg_ref[...], fsh_ref[0], fsc_ref[0]).astype(BF16)
    y = _swiglu(hb, wg_ref, wu_ref, wd_ref, FF_CHUNKS, ())
    o_ref[0] = x1 + fgate_ref[0] * _rms(y, fpg_ref[...])


def _post_even(x, oa, ob, wa, wb, gate, pg, fg, fsh, fsc, fgate, fpg, wg, wu, wd):
    b, s, d = x.shape
    tm = TOK_TILE
    tok = lambda w: pl.BlockSpec((1, tm, w), lambda bi, i: (bi, i, 0))
    vec = pl.BlockSpec((1, d), lambda bi, i: (0, 0))
    mod = pl.BlockSpec((1, 1, d), lambda bi, i: (bi, 0, 0))
    full = lambda a: _resident(a.shape, lambda bi, i: (0,) * a.ndim)
    return pl.pallas_call(
        _post_even_kernel,
        grid=(b, s // tm),
        in_specs=[tok(d), tok(512), tok(512), full(wa), full(wb), mod, vec,
                  vec, mod, mod, mod, vec, full(wg), full(wu), full(wd)],
        out_specs=tok(d),
        out_shape=jax.ShapeDtypeStruct((b, s, d), F32),
        compiler_params=_cparams(("parallel", "parallel")),
        name="post_even",
    )(x, oa, ob, wa, wb, gate, pg, fg, fsh, fsc, fgate, fpg, wg, wu, wd)


def _post_odd_kernel(x_ref, oa_ref, ob_ref, wa_ref, wb_ref, gate_ref, pg_ref,
                     fg_ref, fsh_ref, fsc_ref, rw_ref, rb_ref, x_out, h_out, r_out):
    x1 = _mix_out(x_ref, oa_ref, ob_ref, wa_ref, wb_ref, gate_ref, pg_ref)
    x_out[0] = x1
    h = _prenorm_mod(x1, fg_ref[...], fsh_ref[0], fsc_ref[0])
    h_out[0] = h
    logits = _dot(h, rw_ref[...]) + rb_ref[...]
    lane = lax.broadcasted_iota(jnp.int32, logits.shape, 1)
    m1 = jnp.max(logits, axis=-1, keepdims=True)
    i1 = jnp.min(jnp.where(logits == m1, lane, LANES), axis=-1, keepdims=True)
    rest = jnp.where(lane == i1, NEG, logits)
    m2 = jnp.max(rest, axis=-1, keepdims=True)
    i2 = jnp.min(jnp.where(rest == m2, lane, LANES), axis=-1, keepdims=True)
    e2 = jnp.exp(m2 - m1)
    w1 = 1.0 / (1.0 + e2)
    w2 = e2 / (1.0 + e2)
    r = jnp.where(lane == 0, i1.astype(F32), 0.0)
    r = jnp.where(lane == 1, i2.astype(F32), r)
    r = jnp.where(lane == 2, w1, r)
    r = jnp.where(lane == 3, w2, r)
    r_out[0] = r


def _post_odd(x, oa, ob, wa, wb, gate, pg, fg, fsh, fsc, rw, rb):
    b, s, d = x.shape
    tm = TOK_TILE
    tok = lambda w: pl.BlockSpec((1, tm, w), lambda bi, i: (bi, i, 0))
    vec = pl.BlockSpec((1, d), lambda bi, i: (0, 0))
    mod = pl.BlockSpec((1, 1, d), lambda bi, i: (bi, 0, 0))
    full = lambda a: _resident(a.shape, lambda bi, i: (0,) * a.ndim)
    return pl.pallas_call(
        _post_odd_kernel,
        grid=(b, s // tm),
        in_specs=[tok(d), tok(512), tok(512), full(wa), full(wb), mod, vec,
                  vec, mod, mod, full(rw), pl.BlockSpec((1, LANES), lambda bi, i: (0, 0))],
        out_specs=(tok(d), tok(d), tok(LANES)),
        out_shape=(jax.ShapeDtypeStruct((b, s, d), F32), jax.ShapeDtypeStruct((b, s, d), F32),
                   jax.ShapeDtypeStruct((b, s, LANES), F32)),
        compiler_params=_cparams(("parallel", "parallel")),
        name="post_odd",
    )(x, oa, ob, wa, wb, gate, pg, fg, fsh, fsc, rw, rb)


def _gather_rows(idx_hbm, row, src_hbm, idx_smem, buf, isem, sem):
    n = buf.shape[0]
    cp = pltpu.make_async_copy(idx_hbm.at[row], idx_smem, isem)
    cp.start()
    cp.wait()

    def issue(r, carry):
        t = idx_smem[r]
        pltpu.make_async_copy(src_hbm.at[pl.ds(t, 1), :], buf.at[pl.ds(r, 1), :], sem).start()
        return carry

    lax.fori_loop(0, n, issue, 0)
    pltpu.make_async_copy(src_hbm.at[pl.ds(0, n), :], buf, sem).wait()


def _moe_kernel(te_ref, nu_ref, tok_hbm, h_hbm, wg_ref, wu_ref, wd_ref, y_ref, idx_smem, buf, isem, sem):
    i = pl.program_id(0)

    @pl.when(i < nu_ref[0])
    def _():
        _gather_rows(tok_hbm, i, h_hbm, idx_smem, buf, isem, sem)
        xs = buf[...].astype(BF16)
        y_ref[...] = _swiglu(xs, wg_ref, wu_ref, wd_ref, EXPERT_CHUNKS, (0,))

    @pl.when(i >= nu_ref[0])
    def _():
        y_ref[...] = jnp.zeros_like(y_ref)


def _moe(tile_expert, n_used, row_token, h, wg, wu, wd):
    t, d = h.shape
    nt, tm = row_token.shape
    dff = wg.shape[2]
    wspec = lambda shp: pl.BlockSpec(shp, lambda i, te, nu: (te[i], 0, 0), pipeline_mode=pl.Buffered(1))
    grid_spec = pltpu.PrefetchScalarGridSpec(
        num_scalar_prefetch=2,
        grid=(nt,),
        in_specs=[
            pl.BlockSpec(memory_space=pl.ANY),
            pl.BlockSpec(memory_space=pl.ANY),
            wspec((1, d, dff)), wspec((1, d, dff)), wspec((1, dff, d)),
        ],
        out_specs=pl.BlockSpec((tm, d), lambda i, te, nu: (i, 0)),
        scratch_shapes=[
            pltpu.SMEM((tm,), jnp.int32),
            pltpu.VMEM((tm, d), F32),
            pltpu.SemaphoreType.DMA,
            pltpu.SemaphoreType.DMA,
        ],
    )
    return pl.pallas_call(
        _moe_kernel,
        grid_spec=grid_spec,
        out_shape=jax.ShapeDtypeStruct((nt * tm, d), F32),
        compiler_params=_cparams(("arbitrary",)),
        name="moe_experts",
    )(tile_expert, n_used, row_token, h, wg, wu, wd)


def _combine_kernel(p0_hbm, p1_hbm, y_hbm, x_ref, r_ref, gate_ref, pg_ref, o_ref,
                    idx_smem, buf0, buf1, isem, sem):
    i = pl.program_id(0) * pl.num_programs(1) + pl.program_id(1)
    _gather_rows(p0_hbm, i, y_hbm, idx_smem, buf0, isem, sem)
    _gather_rows(p1_hbm, i, y_hbm, idx_smem, buf1, isem, sem)
    r = r_ref[0]
    y = r[:, 2:3] * buf0[...] + r[:, 3:4] * buf1[...]
    o_ref[0] = x_ref[0] + gate_ref[0] * _rms(y, pg_ref[...])


def _combine(pos0, pos1, y, x, r, gate, pg):
    b, s, d = x.shape
    tm = pos0.shape[1]
    tok = lambda w: pl.BlockSpec((1, tm, w), lambda bi, i: (bi, i, 0))
    return pl.pallas_call(
        _combine_kernel,
        grid=(b, s // tm),
        in_specs=[
            pl.BlockSpec(memory_space=pl.ANY),
            pl.BlockSpec(memory_space=pl.ANY),
            pl.BlockSpec(memory_space=pl.ANY),
            tok(d), tok(LANES),
            pl.BlockSpec((1, 1, d), lambda bi, i: (bi, 0, 0)),
            pl.BlockSpec((1, d), lambda bi, i: (0, 0)),
        ],
        out_specs=tok(d),
        out_shape=jax.ShapeDtypeStruct((b, s, d), F32),
        scratch_shapes=[
            pltpu.SMEM((tm,), jnp.int32),
            pltpu.VMEM((tm, d), F32),
            pltpu.VMEM((tm, d), F32),
            pltpu.SemaphoreType.DMA,
            pltpu.SemaphoreType.DMA,
        ],
        compiler_params=_cparams(("arbitrary", "arbitrary")),
        name="moe_combine",
    )(pos0, pos1, y, x, r, gate, pg)


def _t5_bucket(dist):
    max_exact = REL_BUCKETS // 2
    d = jnp.maximum(dist, 1).astype(F32)
    log_b = max_exact + (jnp.log(d / max_exact) / math.log(REL_MAX_DIST / max_exact)
                         * (REL_BUCKETS - max_exact)).astype(jnp.int32)
    log_b = jnp.minimum(log_b, REL_BUCKETS - 1)
    return jnp.where(dist < max_exact, dist, log_b)


def _rope_tables(s):
    half = MLA_ROPE // 2
    freqs = ROPE_THETA ** (-jnp.arange(half, dtype=F32) / half)
    ang = jnp.arange(s, dtype=F32)[:, None] * freqs[None, :]
    cos, sin = jnp.cos(ang), jnp.sin(ang)
    z64 = jnp.zeros((s, MLA_NOPE), F32)
    z32 = jnp.zeros((s, LANES - MLA_NOPE - MLA_ROPE), F32)
    ck = jnp.concatenate([z64, cos, cos, z32], axis=1)
    cq = jnp.concatenate([jnp.ones((s, MLA_NOPE), F32), cos, cos, z32], axis=1)
    sn = jnp.concatenate([z64, sin, sin, z32], axis=1)
    return cq, ck, sn


def _even_weights(w_in, w_uq, w_ukv):
    d = w_in.shape[0]
    half = MLA_ROPE // 2
    w_cq = w_in[:, :MLA_Q_RANK]
    w_ckv = w_in[:, MLA_Q_RANK:MLA_Q_RANK + MLA_KV_RANK]
    w_kr = w_in[:, MLA_Q_RANK + MLA_KV_RANK:MLA_Q_RANK + MLA_KV_RANK + MLA_ROPE]
    w_qkv = w_in[:, MLA_Q_RANK + MLA_KV_RANK + MLA_ROPE:]
    z = lambda n: jnp.zeros((d, n), F32)
    kr_a = jnp.concatenate([z(MLA_NOPE), w_kr, z(32)], axis=1)
    kr_b = jnp.concatenate([z(MLA_NOPE), -w_kr[:, half:], w_kr[:, :half], z(32)], axis=1)
    dil_scale = DIL_HD ** -0.5 * LOG2E
    w0 = jnp.concatenate([w_cq, w_ckv, kr_a, kr_b, w_qkv[:, :512] * dil_scale, w_qkv[:, 512:]], axis=1)

    r = w_uq.shape[0]
    wq = w_uq.reshape(r, MLA_HEADS, MLA_NOPE + MLA_ROPE) * ((MLA_NOPE + MLA_ROPE) ** -0.5 * LOG2E)
    zq = lambda n: jnp.zeros((r, MLA_HEADS, n), F32)
    nope, x1, x2 = wq[..., :MLA_NOPE], wq[..., MLA_NOPE:MLA_NOPE + half], wq[..., MLA_NOPE + half:]
    q_a = jnp.concatenate([nope, x1, x2, zq(32)], axis=-1).reshape(r, MLA_HEADS * LANES)
    q_b = jnp.concatenate([zq(MLA_NOPE), -x2, x1, zq(32)], axis=-1).reshape(r, MLA_HEADS * LANES)
    wq2 = jnp.concatenate([q_a, q_b], axis=1)

    rk = w_ukv.shape[0]
    wkv = w_ukv.reshape(rk, MLA_HEADS, MLA_NOPE + MLA_V)
    k_blk = jnp.concatenate([wkv[..., :MLA_NOPE], jnp.zeros((rk, MLA_HEADS, LANES - MLA_NOPE), F32)], axis=-1)
    wkv2 = jnp.concatenate([k_blk.reshape(rk, MLA_HEADS * LANES),
                            wkv[..., MLA_NOPE:].reshape(rk, MLA_HEADS * MLA_V)], axis=1)
    return w0.astype(BF16), wq2.astype(BF16), wkv2.astype(BF16)


def _dil_bias(rel_bias):
    blk = DIL_BLOCK
    qi = jnp.arange(blk)
    kj = jnp.arange(2 * blk)
    rel = blk + qi[:, None] - kj[None, :]
    out = []
    for window, dil in DIL_PATTERNS:
        band = (rel >= 0) & (rel <= window // dil)
        bias = rel_bias[_t5_bucket(jnp.maximum(rel, 0) * dil)]
        bias = jnp.transpose(bias, (2, 0, 1)).astype(F32) * LOG2E
        out.append(jnp.where(band[None], bias, NEG))
    return jnp.stack(out)


def _diff_bias(rel_bias, tile):
    nd = REL_MAX_DIST // tile + 1
    i = jnp.arange(tile)
    dist = (jnp.arange(nd)[:, None, None] * tile + i[None, :, None] - i[None, None, :])
    bias = rel_bias[_t5_bucket(jnp.maximum(dist, 0))] * LOG2E
    bias = jnp.where((dist >= 0)[..., None], bias, NEG)
    bias = jnp.transpose(bias, (3, 0, 1, 2)).reshape(DIFF_HEADS, 2, nd, tile, tile)
    far = rel_bias[_t5_bucket(jnp.array(REL_MAX_DIST))] * LOG2E
    far = jnp.broadcast_to(far.reshape(DIFF_HEADS, 2, 1, 1), (DIFF_HEADS, 2, 8, LANES))
    return bias.astype(F32), far.astype(F32)


def _routing(r, n_tok, tile):
    n_tiles = (2 * n_tok) // tile + N_EXPERTS
    e = jnp.concatenate([r[:, 0], r[:, 1]]).astype(jnp.int32)
    onehot = (e[:, None] == jnp.arange(N_EXPERTS)[None, :]).astype(jnp.int32)
    rank = jnp.take_along_axis(jnp.cumsum(onehot, axis=0), e[:, None], axis=1)[:, 0] - 1
    counts = jnp.sum(onehot, axis=0)
    padded = ((counts + tile - 1) // tile) * tile
    ends = jnp.cumsum(padded)
    starts = ends - padded
    pos = starts[e] + rank
    token = jnp.tile(jnp.arange(n_tok, dtype=jnp.int32), 2)
    row_token = jnp.zeros((n_tiles * tile,), jnp.int32).at[pos].set(token)
    tile_start = jnp.arange(n_tiles, dtype=jnp.int32) * tile
    tile_expert = jnp.sum((tile_start[:, None] >= ends[None, :]).astype(jnp.int32), axis=1)
    n_used = (ends[-1] // tile).astype(jnp.int32)
    last = jnp.sum((ends[-1] - 1 >= ends).astype(jnp.int32))
    tile_expert = jnp.minimum(tile_expert, last).astype(jnp.int32)
    return (tile_expert, n_used.reshape(1), row_token.reshape(n_tiles, tile),
            pos[:n_tok].astype(jnp.int32), pos[n_tok:].astype(jnp.int32))


def kernel(x, c, rel_bias, ada_mix_w, ada_mix_b, mix_pre_g, mix_post_g, ada_ffn_w, ada_ffn_b, ffn_pre_g, ffn_post_g, e_w_in, e_q_norm_g, e_w_uq, e_kv_norm_g, e_w_ukv, e_w_out, ffn_w_gate, ffn_w_up, ffn_w_down, o_w_in, diff_lq1, diff_lk1, diff_lq2, diff_lk2, diff_sub_g, o_w_out, router_w, router_b, moe_w_gate, moe_w_up, moe_w_down):
    b, s, d = x.shape
    assert d == D_MODEL and s % DIL_SUPER == 0 and s % TOK_TILE == 0
    row = lambda v: v.reshape(1, -1).astype(F32)

    mix_mod = _ada(c, ada_mix_w, ada_mix_b)
    ffn_mod = _ada(c, ada_ffn_w, ada_ffn_b)

    shift, scale, gate = _split_mod(mix_mod[0])
    w0, wq2, wkv2 = _even_weights(e_w_in[0], e_w_uq[0], e_w_ukv[0])
    cq, ck, sn = _rope_tables(s)
    qa, kta, va, qb, kb, vb = _even_in(x, row(mix_pre_g[0]), shift, scale, w0, row(e_q_norm_g[0]), wq2,
                                       row(e_kv_norm_g[0]), wkv2, cq, ck, sn)
    o_a = _mla(qa, kta, va)
    o_b = _dil(qb, kb, vb, _dil_bias(rel_bias))
    fshift, fscale, fgate = _split_mod(ffn_mod[0])
    w_out = e_w_out[0].astype(BF16)
    x = _post_even(x, o_a, o_b, w_out[:512], w_out[512:], gate, row(mix_post_g[0]),
                   row(ffn_pre_g[0]), fshift, fscale, fgate, row(ffn_post_g[0]),
                   ffn_w_gate[0].astype(BF16), ffn_w_up[0].astype(BF16), ffn_w_down[0].astype(BF16))

    layer = 1
    shift, scale, gate = _split_mod(mix_mod[1])
    w_in = o_w_in[0]
    att_scale = DIFF_HD ** -0.5
    w1 = jnp.concatenate([w_in[:, :512] * (att_scale * LOG2E), w_in[:, 512:1536],
                          w_in[:, 1536:2048] * (SB_HD ** -0.5), w_in[:, 2048:]], axis=1).astype(BF16)
    qd, kdt, vd, qs, kst, vs = _odd_in(x, row(mix_pre_g[1]), shift, scale, w1)
    lam_init = 0.8 - 0.6 * math.exp(-0.3 * layer)
    lam = (jnp.exp(jnp.sum(diff_lq1[0].astype(F32) * diff_lk1[0].astype(F32)))
           - jnp.exp(jnp.sum(diff_lq2[0].astype(F32) * diff_lk2[0].astype(F32))) + lam_init)
    bias, far = _diff_bias(rel_bias, ATT_TILE)
    o_c = _diff(qd, kdt, vd, bias, far, jnp.full((1, LANES), lam, F32), row(diff_sub_g[0]), lam_init)
    o_d = _sb(qs, kst, vs)
    fshift, fscale, fgate = _split_mod(ffn_mod[1])
    w_out = o_w_out[0].astype(BF16)
    rw = jnp.zeros((d, LANES), F32).at[:, :N_EXPERTS].set(router_w[0].astype(F32))
    rb = jnp.full((1, LANES), NEG, F32).at[0, :N_EXPERTS].set(router_b[0].astype(F32))
    x, h, r = _post_odd(x, o_c, o_d, w_out[:512], w_out[512:], gate, row(mix_post_g[1]),
                        row(ffn_pre_g[1]), fshift, fscale, rw, rb)

    n_tok = b * s
    tile_expert, n_used, row_token, pos0, pos1 = _routing(r.reshape(n_tok, LANES), n_tok, MOE_TILE)
    y = _moe(tile_expert, n_used, row_token, h.reshape(n_tok, d),
             moe_w_gate[0].astype(BF16), moe_w_up[0].astype(BF16), moe_w_down[0].astype(BF16))
    ct = TOK_TILE
    return _combine(pos0.reshape(n_tok // ct, ct), pos1.reshape(n_tok // ct, ct), y, x, r, fgate,
                    row(ffn_post_g[1]))
```

```python
import functools
import math

import jax
import jax.numpy as jnp
from jax import lax
from jax.experimental import pallas as pl
from jax.experimental.pallas import tpu as pltpu

F32 = jnp.float32
BF16 = jnp.bfloat16

D_MODEL = 1024
EPS = 1e-6

MLA_HEADS = 8
MLA_NOPE = 64
MLA_ROPE = 32
MLA_V = 64
MLA_Q_RANK = 256
MLA_KV_RANK = 128
ROPE_THETA = 10000.0

DIL_HEADS = 8
DIL_HD = 64
DIL_PATTERNS = ((128, 1), (512, 4), (2048, 16))
DIL_BLOCK = 128

DIFF_HEADS = 4
DIFF_HD = 64
SB_HEADS = 8
SB_HD = 64

REL_BUCKETS = 32
REL_MAX_DIST = 2048

D_FF = 2816
N_EXPERTS = 8
D_FF_EXPERT = 3584

LANES = 128
ROW_SUB = D_MODEL // LANES
LOG2E = math.log2(math.e)
NEG = -1e30

TOK_TILE = 512
MLA_TILE = 512
DIFF_TILE = 512
SB_TILE = 256
DIL_SUPER = DIL_BLOCK * 16
MOE_TILE = 512
SB_LOG_FLOOR = -104.0

VMEM_LIMIT = 56 * 1024 * 1024


def _cparams(sem):
    return pltpu.CompilerParams(dimension_semantics=sem, vmem_limit_bytes=VMEM_LIMIT)


def _resident(shape, index_map):
    return pl.BlockSpec(shape, index_map, pipeline_mode=pl.Buffered(1))


def _rms(x, g):
    return x * lax.rsqrt(jnp.mean(x * x, axis=-1, keepdims=True) + EPS) * g


def _dot(a, b):
    return jnp.dot(a, b, preferred_element_type=F32)


def _loop_pairs(lo, hi, step, carry):
    n = hi - lo
    carry = lax.fori_loop(0, n // 2, lambda i, c: step(lo + 2 * i + 1, step(lo + 2 * i, c)), carry)
    return lax.cond(n % 2 == 1, lambda c: step(hi - 1, c), lambda c: c, carry)


def _ada_kernel(c_ref, w_ref, b_ref, o_ref):
    c = c_ref[...]
    sc = c / (1.0 + jnp.exp(-c))
    o_ref[0] = _dot(sc.astype(BF16), w_ref[0].astype(BF16)) + b_ref[0]


def _ada(c, w, b):
    nl, d, d3 = w.shape
    bsz = c.shape[0]
    nb = d3 // d
    return pl.pallas_call(
        _ada_kernel,
        grid=(nl, nb),
        in_specs=[
            pl.BlockSpec((bsz, d), lambda l, j: (0, 0)),
            pl.BlockSpec((1, d, d), lambda l, j: (l, 0, j)),
            pl.BlockSpec((1, 1, d), lambda l, j: (l, 0, j)),
        ],
        out_specs=pl.BlockSpec((1, bsz, d), lambda l, j: (l, 0, j)),
        out_shape=jax.ShapeDtypeStruct((nl, bsz, d3), F32),
        compiler_params=_cparams(("arbitrary", "arbitrary")),
        name="ada",
    )(c, w, b.reshape(nl, 1, d3))


def _split_mod(m):
    b = m.shape[0]
    m = m.reshape(b, 3, 1, D_MODEL)
    return m[:, 0], m[:, 1], m[:, 2]


def _prenorm_mod(x, g, shift, scale):
    return _rms(x, g) * (1.0 + scale) + shift


def _even_in_kernel(x_ref, g_ref, sh_ref, sc_ref, w0_ref, qg_ref, wq_ref, kvg_ref, wkv_ref,
                    cq_ref, ck_ref, sn_ref,
                    qa_ref, kt_ref, va_ref, qb_ref, kb_ref, vb_ref):
    h = _prenorm_mod(x_ref[0], g_ref[...], sh_ref[0], sc_ref[0]).astype(BF16)
    proj = _dot(h, w0_ref[...])
    cqn = _rms(proj[:, 0:256], qg_ref[...]).astype(BF16)
    qq = _dot(cqn, wq_ref[...])
    ckvn = _rms(proj[:, 256:384], kvg_ref[...]).astype(BF16)
    kv = _dot(ckvn, wkv_ref[...])
    cq = cq_ref[...]
    ck = ck_ref[...]
    sn = sn_ref[...]
    krope = proj[:, 384:512] * ck + proj[:, 512:640] * sn
    nh = MLA_HEADS
    for hd in range(nh):
        lo = hd * LANES
        qh = qq[:, lo:lo + LANES] * cq + qq[:, nh * LANES + lo:nh * LANES + lo + LANES] * sn
        qa_ref[0, :, lo:lo + LANES] = qh.astype(BF16)
        _store_key_tiles(kt_ref, hd, kv[:, lo:lo + LANES] + krope)
    va_ref[0] = kv[:, nh * LANES:nh * LANES + 512].astype(BF16)
    qb_ref[0] = proj[:, 640:1152].astype(BF16)
    kb_ref[0] = proj[:, 1152:1664].astype(BF16)
    vb_ref[0] = proj[:, 1664:2176].astype(BF16)


def _even_in(x, g, shift, scale, w0, qg, wq, kvg, wkv, cq, ck, sn):
    b, s, d = x.shape
    tm = TOK_TILE
    tkb = MLA_TILE
    ns = s // tm
    tok = lambda w: pl.BlockSpec((1, tm, w), lambda bi, i: (bi, i, 0))
    vec = lambda w: pl.BlockSpec((1, w), lambda bi, i: (0, 0))
    mod = pl.BlockSpec((1, 1, d), lambda bi, i: (bi, 0, 0))
    tab = pl.BlockSpec((tm, LANES), lambda bi, i: (i, 0))
    full = lambda a: _resident(a.shape, lambda bi, i: (0,) * a.ndim)
    out_shapes = (
        jax.ShapeDtypeStruct((b, s, MLA_HEADS * LANES), BF16),
        jax.ShapeDtypeStruct((b, MLA_HEADS, s // tkb, LANES, tkb), BF16),
        jax.ShapeDtypeStruct((b, s, 512), BF16),
        jax.ShapeDtypeStruct((b, s, 512), BF16),
        jax.ShapeDtypeStruct((b, s, 512), BF16),
        jax.ShapeDtypeStruct((b, s, 512), BF16),
    )
    out_specs = (
        tok(MLA_HEADS * LANES),
        pl.BlockSpec((1, MLA_HEADS, tm // tkb, LANES, tkb), lambda bi, i: (bi, 0, i, 0, 0)),
        tok(512), tok(512), tok(512), tok(512),
    )
    return pl.pallas_call(
        _even_in_kernel,
        grid=(b, ns),
        in_specs=[tok(d), vec(d), mod, mod, full(w0), vec(MLA_Q_RANK), full(wq), vec(MLA_KV_RANK), full(wkv),
                  tab, tab, tab],
        out_specs=out_specs,
        out_shape=out_shapes,
        compiler_params=_cparams(("parallel", "parallel")),
        name="even_in",
    )(x, g, shift, scale, w0, qg, wq, kvg, wkv, cq, ck, sn)


def _store_key_tiles(kt_ref, hd, k):
    tkb = kt_ref.shape[4]
    for t in range(k.shape[0] // tkb):
        kt_ref[0, hd, t] = k[t * tkb:(t + 1) * tkb, :].T.astype(BF16)


def _odd_in_kernel(x_ref, g_ref, sh_ref, sc_ref, w_ref,
                   qd_ref, kdt_ref, vd_ref, qs_ref, kst_ref, vs_ref):
    h = _prenorm_mod(x_ref[0], g_ref[...], sh_ref[0], sc_ref[0]).astype(BF16)
    proj = _dot(h, w_ref[...])
    qd_ref[0] = proj[:, 0:512].astype(BF16)
    vd_ref[0] = proj[:, 1024:1536].astype(BF16)
    qs_ref[0] = proj[:, 1536:2048].astype(BF16)
    vs_ref[0] = proj[:, 2560:3072].astype(BF16)
    for hd in range(4):
        _store_key_tiles(kdt_ref, hd, proj[:, 512 + hd * LANES:512 + (hd + 1) * LANES])
        _store_key_tiles(kst_ref, hd, proj[:, 2048 + hd * LANES:2048 + (hd + 1) * LANES])


def _odd_in(x, g, shift, scale, w):
    b, s, d = x.shape
    tm = TOK_TILE
    ns = s // tm
    tok = lambda wd: pl.BlockSpec((1, tm, wd), lambda bi, i: (bi, i, 0))
    mod = pl.BlockSpec((1, 1, d), lambda bi, i: (bi, 0, 0))
    ktspec = lambda tkb: pl.BlockSpec((1, 4, tm // tkb, LANES, tkb), lambda bi, i: (bi, 0, i, 0, 0))
    act = jax.ShapeDtypeStruct((b, s, 512), BF16)
    kts = lambda tkb: jax.ShapeDtypeStruct((b, 4, s // tkb, LANES, tkb), BF16)
    return pl.pallas_call(
        _odd_in_kernel,
        grid=(b, ns),
        in_specs=[tok(d), pl.BlockSpec((1, d), lambda bi, i: (0, 0)), mod, mod,
                  _resident(w.shape, lambda bi, i: (0, 0))],
        out_specs=(tok(512), ktspec(DIFF_TILE), tok(512), tok(512), ktspec(SB_TILE), tok(512)),
        out_shape=(act, kts(DIFF_TILE), act, act, kts(SB_TILE), act),
        compiler_params=_cparams(("parallel", "parallel")),
        name="odd_in",
    )(x, g, shift, scale, w)


def _mla_kernel(q_ref, kt_ref, v_ref, o_ref):
    tq = q_ref.shape[1]
    tk = kt_ref.shape[4]
    qi = pl.program_id(2)
    lane = lax.broadcasted_iota(jnp.int32, (tq, LANES), 1)
    causal = (lax.broadcasted_iota(jnp.int32, (tq, tk), 1)
              <= lax.broadcasted_iota(jnp.int32, (tq, tk), 0))
    qs = (q_ref[0, :, 0:LANES], q_ref[0, :, LANES:2 * LANES])

    def step(j, carry, masked):
        v = v_ref[0, pl.ds(pl.multiple_of(j * tk, tk), tk), :]
        out = []
        for hd in range(2):
            m, l, acc = carry[hd]
            s = _dot(qs[hd], kt_ref[0, hd, j])
            if masked:
                s = jnp.where(causal, s, NEG)
            m_new = jnp.maximum(m, jnp.max(s, axis=-1, keepdims=True))
            alpha = jnp.exp2(m - m_new)
            p = jnp.exp2(s - m_new)
            l = alpha * l + jnp.sum(p, axis=-1, keepdims=True)
            acc = alpha * acc + _dot(p.astype(BF16), v)
            out.append((m_new, l, acc))
        return tuple(out)

    one = (jnp.full((tq, 1), NEG, F32), jnp.zeros((tq, 1), F32), jnp.zeros((tq, LANES), F32))
    carry = _loop_pairs(0, qi, functools.partial(step, masked=False), (one, one))
    (_, l0, a0), (_, l1, a1) = step(qi, carry, True)
    o_ref[0] = jnp.where(lane < 64, a0 / l0, a1 / l1).astype(BF16)


def _mla(q, kt, v):
    b, s, _ = q.shape
    tq = MLA_TILE
    nk = kt.shape[2]
    tk = kt.shape[4]
    return pl.pallas_call(
        _mla_kernel,
        grid=(b, MLA_HEADS // 2, s // tq),
        in_specs=[
            pl.BlockSpec((1, tq, 2 * LANES), lambda bi, hp, qi: (bi, qi, hp)),
            pl.BlockSpec((1, 2, nk, LANES, tk), lambda bi, hp, qi: (bi, hp, 0, 0, 0)),
            pl.BlockSpec((1, s, LANES), lambda bi, hp, qi: (bi, 0, hp)),
        ],
        out_specs=pl.BlockSpec((1, tq, LANES), lambda bi, hp, qi: (bi, qi, hp)),
        out_shape=jax.ShapeDtypeStruct((b, s, 512), BF16),
        compiler_params=_cparams(("parallel", "parallel", "arbitrary")),
        name="mla",
    )(q, kt, v)


def _dil_kernel(q_ref, kc_ref, kp_ref, vc_ref, vp_ref, bias_ref, o_ref,
                q32, k32, v32, acc_s, m_s, d_s):
    sup = DIL_SUPER
    blk = DIL_BLOCK
    n = pl.program_id(2)
    q32[...] = q_ref[0].astype(F32)
    k32[0:sup, :] = kp_ref[0].astype(F32)
    k32[sup:2 * sup, :] = kc_ref[0].astype(F32)
    v32[0:sup, :] = vp_ref[0].astype(F32)
    v32[sup:2 * sup, :] = vc_ref[0].astype(F32)
    low = lax.broadcasted_iota(jnp.int32, (blk, LANES), 1) < 64
    before_start = jnp.where(lax.broadcasted_iota(jnp.int32, (blk, 2 * blk), 1) < blk, NEG, 0.0)

    for g, (_, dil) in enumerate(DIL_PATTERNS):

        def unit(u, carry, g=g, dil=dil):
            n_loc = u // dil
            r = u % dil
            qs = n_loc * (blk * dil) + r
            ks = sup + (n_loc - 1) * (blk * dil) + r
            if dil == 1:
                qsl = pl.ds(pl.multiple_of(qs, blk), blk)
                ksl = pl.ds(pl.multiple_of(ks, blk), 2 * blk)
            else:
                qsl = pl.ds(qs, blk, stride=dil)
                ksl = pl.ds(ks, 2 * blk, stride=dil)
            q = q32[qsl, :]
            k = k32[ksl, :].astype(BF16)
            v = v32[ksl, :].astype(BF16)
            extra = jnp.where(jnp.logical_and(n == 0, n_loc == 0), before_start, 0.0)
            parts = []
            for hd in range(2):
                qh = jnp.where(low if hd == 0 else jnp.logical_not(low), q, 0.0).astype(BF16)
                s = lax.dot_general(qh, k, (((1,), (1,)), ((), ())), preferred_element_type=F32)
                s = s + bias_ref[g, hd] + extra
                m = jnp.max(s, axis=-1, keepdims=True)
                e = jnp.exp2(s - m)
                den = jnp.sum(e, axis=-1, keepdims=True)
                parts.append((_dot(e.astype(BF16), v), m, den))
            acc_s[g, qsl, :] = jnp.where(low, parts[0][0], parts[1][0])
            m_s[g, qsl, :] = jnp.where(low, parts[0][1], parts[1][1])
            d_s[g, qsl, :] = jnp.where(low, parts[0][2], parts[1][2])
            return carry

        lax.fori_loop(0, 16, unit, 0, unroll=4)

    mx = jnp.maximum(jnp.maximum(m_s[0], m_s[1]), m_s[2])
    num = jnp.zeros((sup, LANES), F32)
    den = jnp.zeros((sup, LANES), F32)
    for g in range(3):
        a = jnp.exp2(m_s[g] - mx)
        num = num + a * acc_s[g]
        den = den + a * d_s[g]
    o_ref[0] = (num / den).astype(BF16)


def _dil(q, k, v, bias):
    b, s, _ = q.shape
    sup = DIL_SUPER
    cur = pl.BlockSpec((1, sup, LANES), lambda bi, hp, n: (bi, n, hp))
    prev = pl.BlockSpec((1, sup, LANES), lambda bi, hp, n: (bi, jnp.maximum(n - 1, 0), hp))
    return pl.pallas_call(
        _dil_kernel,
        grid=(b, DIL_HEADS // 2, s // sup),
        in_specs=[cur, cur, prev, cur, prev,
                  pl.BlockSpec((3, 2, DIL_BLOCK, 2 * DIL_BLOCK), lambda bi, hp, n: (0, hp, 0, 0))],
        out_specs=cur,
        out_shape=jax.ShapeDtypeStruct((b, s, 512), BF16),
        scratch_shapes=[
            pltpu.VMEM((sup, LANES), F32),
            pltpu.VMEM((2 * sup, LANES), F32),
            pltpu.VMEM((2 * sup, LANES), F32),
            pltpu.VMEM((3, sup, LANES), F32),
            pltpu.VMEM((3, sup, LANES), F32),
            pltpu.VMEM((3, sup, LANES), F32),
        ],
        compiler_params=_cparams(("parallel", "parallel", "arbitrary")),
        name="dilated",
    )(q, k, k, v, v, bias)


def _diff_kernel(q_ref, kt_ref, v_ref, bias_ref, far_ref, lam_ref, g_ref, o_ref, *, lam_init):
    tq = q_ref.shape[1]
    tk = kt_ref.shape[4]
    nd = bias_ref.shape[2]
    qi = pl.program_id(2)
    lane = lax.broadcasted_iota(jnp.int32, (tq, LANES), 1)
    q = q_ref[0]
    zero = jnp.zeros_like(q)
    qm = (jnp.where(lane < 64, q, zero), jnp.where(lane >= 64, q, zero))

    def step(j, carry, bias_of):
        k = kt_ref[0, 0, j]
        v = v_ref[0, pl.ds(pl.multiple_of(j * tk, tk), tk), :]
        out = []
        for mi in range(2):
            m, l, acc = carry[mi]
            s = _dot(qm[mi], k) + bias_of(mi, j)
            m_new = jnp.maximum(m, jnp.max(s, axis=-1, keepdims=True))
            alpha = jnp.exp2(m - m_new)
            p = jnp.exp2(s - m_new)
            l = alpha * l + jnp.sum(p, axis=-1, keepdims=True)
            acc = alpha * acc + _dot(p.astype(BF16), v)
            out.append((m_new, l, acc))
        return tuple(out)

    one = (jnp.full((tq, 1), NEG, F32), jnp.zeros((tq, 1), F32), jnp.zeros((tq, LANES), F32))
    carry = (one, one)
    n_far = jnp.maximum(qi - nd + 1, 0)
    carry = _loop_pairs(0, n_far, functools.partial(step, bias_of=lambda mi, j: far_ref[0, mi, 0:1, 0:1]), carry)
    carry = lax.fori_loop(n_far, qi, functools.partial(step, bias_of=lambda mi, j: bias_ref[0, mi, qi - j]), carry)
    carry = step(qi, carry, lambda mi, j: bias_ref[0, mi, 0])
    (_, l0, a0), (_, l1, a1) = carry
    o = a0 / l0 - lam_ref[...] * (a1 / l1)
    o_ref[0] = (_rms(o, g_ref[...]) * (1.0 - lam_init)).astype(BF16)


def _diff(q, kt, v, bias, far, lam, sub_g, lam_init):
    b, s, _ = q.shape
    tq = DIFF_TILE
    nk, tk = kt.shape[2], kt.shape[4]
    nd = bias.shape[2]
    return pl.pallas_call(
        functools.partial(_diff_kernel, lam_init=lam_init),
        grid=(DIFF_HEADS, b, s // tq),
        in_specs=[
            pl.BlockSpec((1, tq, LANES), lambda h, bi, qi: (bi, qi, h)),
            pl.BlockSpec((1, 1, nk, LANES, tk), lambda h, bi, qi: (bi, h, 0, 0, 0)),
            pl.BlockSpec((1, s, LANES), lambda h, bi, qi: (bi, 0, h)),
            _resident((1, 2, nd, tq, tk), lambda h, bi, qi: (h, 0, 0, 0, 0)),
            pl.BlockSpec((1, 2, 8, LANES), lambda h, bi, qi: (h, 0, 0, 0)),
            pl.BlockSpec((1, LANES), lambda h, bi, qi: (0, 0)),
            pl.BlockSpec((1, LANES), lambda h, bi, qi: (0, 0)),
        ],
        out_specs=pl.BlockSpec((1, tq, LANES), lambda h, bi, qi: (bi, qi, h)),
        out_shape=jax.ShapeDtypeStruct((b, s, 512), BF16),
        compiler_params=_cparams(("parallel", "parallel", "arbitrary")),
        name="diff",
    )(q, kt, v, bias, far, lam, sub_g)


def _sb_kernel(q_ref, kt_ref, v_ref, o_ref):
    tq = q_ref.shape[1]
    tk = kt_ref.shape[4]
    qi = pl.program_id(2)
    lane = lax.broadcasted_iota(jnp.int32, (tq, LANES), 1)
    strict = (lax.broadcasted_iota(jnp.int32, (tq, tk), 1)
              < lax.broadcasted_iota(jnp.int32, (tq, tk), 0))
    later = (lax.broadcasted_iota(jnp.int32, (tk, tk), 0)
             > lax.broadcasted_iota(jnp.int32, (tk, tk), 1)).astype(BF16)
    q = q_ref[0]
    zero = jnp.zeros_like(q)
    qh = (jnp.where(lane < 64, q, zero), jnp.where(lane >= 64, q, zero))

    def block(j, state, masked):
        k = kt_ref[0, 0, j]
        v = v_ref[0, pl.ds(pl.multiple_of(j * tk, tk), tk), :]
        out = []
        for hd in range(2):
            c, acc = state[hd]
            z = _dot(qh[hd], k)
            sp = jnp.maximum(z, 0.0) + jnp.log(1.0 + jnp.exp(-jnp.abs(z)))
            log_1m = -sp
            if masked:
                log_1m = jnp.where(strict, log_1m, 0.0)
            hi = log_1m.astype(BF16)
            lo = (log_1m - hi.astype(F32)).astype(BF16)
            after = _dot(hi, later) + _dot(lo, later) + c
            w = jnp.exp(z - sp + after)
            if masked:
                w = jnp.where(strict, w, 0.0)
            acc = acc + _dot(w.astype(BF16), v)
            c = c + jnp.sum(log_1m, axis=-1, keepdims=True)
            out.append((c, acc))
        return tuple(out)

    one = (jnp.zeros((tq, 1), F32), jnp.zeros((tq, LANES), F32))
    state = block(qi, (one, one), True)

    def cond(st):
        j, ((c0, _), (c1, _)) = st
        return jnp.logical_and(j >= 0, jnp.max(jnp.maximum(c0, c1)) > SB_LOG_FLOOR)

    def body(st):
        j, state = st
        return j - 1, block(j, state, False)

    _, ((_, a0), (_, a1)) = lax.while_loop(cond, body, (qi - 1, state))
    o_ref[0] = jnp.where(lane < 64, a0, a1).astype(BF16)


def _sb(q, kt, v):
    b, s, _ = q.shape
    tq = SB_TILE
    nk, tk = kt.shape[2], kt.shape[4]
    return pl.pallas_call(
        _sb_kernel,
        grid=(b, SB_HEADS // 2, s // tq),
        in_specs=[
            pl.BlockSpec((1, tq, LANES), lambda bi, hp, qi: (bi, qi, hp)),
            pl.BlockSpec((1, 1, nk, LANES, tk), lambda bi, hp, qi: (bi, hp, 0, 0, 0)),
            pl.BlockSpec((1, s, LANES), lambda bi, hp, qi: (bi, 0, hp)),
        ],
        out_specs=pl.BlockSpec((1, tq, LANES), lambda bi, hp, qi: (bi, qi, hp)),
        out_shape=jax.ShapeDtypeStruct((b, s, 512), BF16),
        compiler_params=_cparams(("parallel", "parallel", "arbitrary")),
        name="stick_breaking",
    )(q, kt, v)


FF_CHUNKS = ((0, 768), (768, 1536), (1536, 2304), (2304, 2816))
EXPERT_CHUNKS = ((0, 1024), (1024, 2048), (2048, 3072), (3072, 3584))


def _swiglu(hb, wg_ref, wu_ref, wd_ref, chunks, lead):
    acc = None
    for c0, c1 in chunks:
        g = _dot(hb, wg_ref[lead + (slice(None), slice(c0, c1))])
        u = _dot(hb, wu_ref[lead + (slice(None), slice(c0, c1))])
        a = (g / (1.0 + jnp.exp(-g)) * u).astype(BF16)
        part = _dot(a, wd_ref[lead + (slice(c0, c1), slice(None))])
        acc = part if acc is None else acc + part
    return acc


def _mix_out(x_ref, oa_ref, ob_ref, wa_ref, wb_ref, gate_ref, pg_ref):
    y = _dot(oa_ref[0], wa_ref[...]) + _dot(ob_ref[0], wb_ref[...])
    return x_ref[0] + gate_ref[0] * _rms(y, pg_ref[...])


def _post_even_kernel(x_ref, oa_ref, ob_ref, wa_ref, wb_ref, gate_ref, pg_ref,
                      fg_ref, fsh_ref, fsc_ref, fgate_ref, fpg_ref, wg_ref, wu_ref, wd_ref, o_ref):
    x1 = _mix_out(x_ref, oa_ref, ob_ref, wa_ref, wb_ref, gate_ref, pg_ref)
    hb = _prenorm_mod(x1, fg_ref[...], fsh_ref[0], fsc_ref[0]).astype(BF16)
    y = _swiglu(hb, wg_ref, wu_ref, wd_ref, FF_CHUNKS, ())
    o_ref[0] = x1 + fgate_ref[0] * _rms(y, fpg_ref[...])


def _post_even(x, oa, ob, wa, wb, gate, pg, fg, fsh, fsc, fgate, fpg, wg, wu, wd):
    b, s, d = x.shape
    tm = TOK_TILE
    tok = lambda w: pl.BlockSpec((1, tm, w), lambda bi, i: (bi, i, 0))
    vec = pl.BlockSpec((1, d), lambda bi, i: (0, 0))
    mod = pl.BlockSpec((1, 1, d), lambda bi, i: (bi, 0, 0))
    full = lambda a: _resident(a.shape, lambda bi, i: (0,) * a.ndim)
    return pl.pallas_call(
        _post_even_kernel,
        grid=(b, s // tm),
        in_specs=[tok(d), tok(512), tok(512), full(wa), full(wb), mod, vec,
                  vec, mod, mod, mod, vec, full(wg), full(wu), full(wd)],
        out_specs=tok(d),
        out_shape=jax.ShapeDtypeStruct((b, s, d), F32),
        compiler_params=_cparams(("parallel", "parallel")),
        name="post_even",
    )(x, oa, ob, wa, wb, gate, pg, fg, fsh, fsc, fgate, fpg, wg, wu, wd)


def _post_odd_kernel(x_ref, oa_ref, ob_ref, wa_ref, wb_ref, gate_ref, pg_ref,
                     fg_ref, fsh_ref, fsc_ref, rw_ref, rb_ref, x_out, h_out, r_out):
    x1 = _mix_out(x_ref, oa_ref, ob_ref, wa_ref, wb_ref, gate_ref, pg_ref)
    x_out[0] = x1
    h = _prenorm_mod(x1, fg_ref[...], fsh_ref[0], fsc_ref[0])
    _store_rows(h_out, (0,), h)
    logits = _dot(h, rw_ref[...]) + rb_ref[...]
    lane = lax.broadcasted_iota(jnp.int32, logits.shape, 1)
    m1 = jnp.max(logits, axis=-1, keepdims=True)
    i1 = jnp.min(jnp.where(logits == m1, lane, LANES), axis=-1, keepdims=True)
    rest = jnp.where(lane == i1, NEG, logits)
    m2 = jnp.max(rest, axis=-1, keepdims=True)
    i2 = jnp.min(jnp.where(rest == m2, lane, LANES), axis=-1, keepdims=True)
    e2 = jnp.exp(m2 - m1)
    w1 = 1.0 / (1.0 + e2)
    w2 = e2 / (1.0 + e2)
    r = jnp.where(lane == 0, i1.astype(F32), 0.0)
    r = jnp.where(lane == 1, i2.astype(F32), r)
    r = jnp.where(lane == 2, w1, r)
    r = jnp.where(lane == 3, w2, r)
    r_out[0] = r


def _post_odd(x, oa, ob, wa, wb, gate, pg, fg, fsh, fsc, rw, rb):
    b, s, d = x.shape
    tm = TOK_TILE
    tok = lambda w: pl.BlockSpec((1, tm, w), lambda bi, i: (bi, i, 0))
    vec = pl.BlockSpec((1, d), lambda bi, i: (0, 0))
    mod = pl.BlockSpec((1, 1, d), lambda bi, i: (bi, 0, 0))
    full = lambda a: _resident(a.shape, lambda bi, i: (0,) * a.ndim)
    return pl.pallas_call(
        _post_odd_kernel,
        grid=(b, s // tm),
        in_specs=[tok(d), tok(512), tok(512), full(wa), full(wb), mod, vec,
                  vec, mod, mod, full(rw), pl.BlockSpec((1, LANES), lambda bi, i: (0, 0))],
        out_specs=(tok(d), pl.BlockSpec((1, tm, ROW_SUB, LANES), lambda bi, i: (bi, i, 0, 0)), tok(LANES)),
        out_shape=(jax.ShapeDtypeStruct((b, s, d), F32), jax.ShapeDtypeStruct((b, s, ROW_SUB, LANES), F32),
                   jax.ShapeDtypeStruct((b, s, LANES), F32)),
        compiler_params=_cparams(("parallel", "parallel")),
        name="post_odd",
    )(x, oa, ob, wa, wb, gate, pg, fg, fsh, fsc, rw, rb)


def _store_rows(ref, lead, val):
    for c in range(ROW_SUB):
        ref[lead + (slice(None), c, slice(None))] = val[:, c * LANES:(c + 1) * LANES]


def _load_rows(ref, lead, lo, hi):
    return jnp.concatenate([ref[lead, lo:hi, c, :] for c in range(ROW_SUB)], axis=1)


def _gather_ahead(i, n_steps, idx_hbm, src_hbm, idx_smem, buf, isem, sem):
    n = buf.shape[1]
    slot = i % 2
    nxt = 1 - slot

    def idx_copy(step, sl):
        return pltpu.make_async_copy(idx_hbm.at[step], idx_smem.at[sl], isem.at[sl])

    def issue_rows(sl):
        def issue(r, carry):
            t = idx_smem[sl, r]
            pltpu.make_async_copy(src_hbm.at[t], buf.at[sl, r], sem.at[sl]).start()
            return carry

        lax.fori_loop(0, n, issue, 0, unroll=8)

    @pl.when(i == 0)
    def _():
        first = idx_copy(0, 0)
        first.start()
        first.wait()
        issue_rows(0)

        @pl.when(n_steps > 1)
        def _():
            idx_copy(1, 1).start()

    @pl.when(i + 1 < n_steps)
    def _():
        idx_copy(i + 1, nxt).wait()
        issue_rows(nxt)

    @pl.when(i + 2 < n_steps)
    def _():
        idx_copy(i + 2, slot).start()

    pltpu.make_async_copy(src_hbm.at[pl.ds(0, n)], buf.at[slot], sem.at[slot]).wait()
    return slot


def _moe_kernel(te_ref, nu_ref, tok_hbm, h_hbm, wg_ref, wu_ref, wd_ref, y_ref, idx_smem, buf, isem, sem):
    i = pl.program_id(0)
    n_used = nu_ref[0]

    @pl.when(i < n_used)
    def _():
        slot = _gather_ahead(i, n_used, tok_hbm, h_hbm, idx_smem, buf, isem, sem)
        xs = _load_rows(buf, slot, 0, buf.shape[1]).astype(BF16)
        _store_rows(y_ref, (), _swiglu(xs, wg_ref, wu_ref, wd_ref, EXPERT_CHUNKS, (0,)))

    @pl.when(i >= n_used)
    def _():
        y_ref[...] = jnp.zeros_like(y_ref)


def _moe(tile_expert, n_used, row_token, h, wg, wu, wd):
    d = h.shape[1] * h.shape[2]
    nt, tm = row_token.shape
    dff = wg.shape[2]
    wspec = lambda shp: pl.BlockSpec(shp, lambda i, te, nu: (te[i], 0, 0), pipeline_mode=pl.Buffered(1))
    grid_spec = pltpu.PrefetchScalarGridSpec(
        num_scalar_prefetch=2,
        grid=(nt,),
        in_specs=[
            pl.BlockSpec(memory_space=pl.ANY),
            pl.BlockSpec(memory_space=pl.ANY),
            wspec((1, d, dff)), wspec((1, d, dff)), wspec((1, dff, d)),
        ],
        out_specs=pl.BlockSpec((tm, ROW_SUB, LANES), lambda i, te, nu: (i, 0, 0)),
        scratch_shapes=[
            pltpu.SMEM((2, tm), jnp.int32),
            pltpu.VMEM((2, tm, ROW_SUB, LANES), F32),
            pltpu.SemaphoreType.DMA((2,)),
            pltpu.SemaphoreType.DMA((2,)),
        ],
    )
    return pl.pallas_call(
        _moe_kernel,
        grid_spec=grid_spec,
        out_shape=jax.ShapeDtypeStruct((nt * tm, ROW_SUB, LANES), F32),
        compiler_params=_cparams(("arbitrary",)),
        name="moe_experts",
    )(tile_expert, n_used, row_token, h, wg, wu, wd)


def _combine_kernel(pos_hbm, y_hbm, x_ref, r_ref, gate_ref, pg_ref, o_ref, idx_smem, buf, isem, sem):
    tm = x_ref.shape[0]
    slot = _gather_ahead(pl.program_id(0), pl.num_programs(0), pos_hbm, y_hbm, idx_smem, buf, isem, sem)
    r = r_ref[...]
    y = r[:, 2:3] * _load_rows(buf, slot, 0, tm) + r[:, 3:4] * _load_rows(buf, slot, tm, 2 * tm)
    o_ref[...] = x_ref[...] + gate_ref[0] * _rms(y, pg_ref[...])


def _combine(pos, y, x, r, gate, pg, tokens_per_seq):
    n_tok, d = x.shape
    nt, tm2 = pos.shape
    tm = tm2 // 2
    per_seq = tokens_per_seq // tm
    tok = lambda w: pl.BlockSpec((tm, w), lambda i: (i, 0))
    return pl.pallas_call(
        _combine_kernel,
        grid=(nt,),
        in_specs=[
            pl.BlockSpec(memory_space=pl.ANY),
            pl.BlockSpec(memory_space=pl.ANY),
            tok(d), tok(LANES),
            pl.BlockSpec((1, 1, d), lambda i: (i // per_seq, 0, 0)),
            pl.BlockSpec((1, d), lambda i: (0, 0)),
        ],
        out_specs=tok(d),
        out_shape=jax.ShapeDtypeStruct((n_tok, d), F32),
        scratch_shapes=[
            pltpu.SMEM((2, tm2), jnp.int32),
            pltpu.VMEM((2, tm2, ROW_SUB, LANES), F32),
            pltpu.SemaphoreType.DMA((2,)),
            pltpu.SemaphoreType.DMA((2,)),
        ],
        compiler_params=_cparams(("arbitrary",)),
        name="moe_combine",
    )(pos, y, x, r, gate, pg)


def _t5_bucket(dist):
    max_exact = REL_BUCKETS // 2
    d = jnp.maximum(dist, 1).astype(F32)
    log_b = max_exact + (jnp.log(d / max_exact) / math.log(REL_MAX_DIST / max_exact)
                         * (REL_BUCKETS - max_exact)).astype(jnp.int32)
    log_b = jnp.minimum(log_b, REL_BUCKETS - 1)
    return jnp.where(dist < max_exact, dist, log_b)


def _rope_tables(s):
    half = MLA_ROPE // 2
    freqs = ROPE_THETA ** (-jnp.arange(half, dtype=F32) / half)
    ang = jnp.arange(s, dtype=F32)[:, None] * freqs[None, :]
    cos, sin = jnp.cos(ang), jnp.sin(ang)
    z64 = jnp.zeros((s, MLA_NOPE), F32)
    z32 = jnp.zeros((s, LANES - MLA_NOPE - MLA_ROPE), F32)
    ck = jnp.concatenate([z64, cos, cos, z32], axis=1)
    cq = jnp.concatenate([jnp.ones((s, MLA_NOPE), F32), cos, cos, z32], axis=1)
    sn = jnp.concatenate([z64, sin, sin, z32], axis=1)
    return cq, ck, sn


def _even_weights(w_in, w_uq, w_ukv):
    d = w_in.shape[0]
    half = MLA_ROPE // 2
    w_cq = w_in[:, :MLA_Q_RANK]
    w_ckv = w_in[:, MLA_Q_RANK:MLA_Q_RANK + MLA_KV_RANK]
    w_kr = w_in[:, MLA_Q_RANK + MLA_KV_RANK:MLA_Q_RANK + MLA_KV_RANK + MLA_ROPE]
    w_qkv = w_in[:, MLA_Q_RANK + MLA_KV_RANK + MLA_ROPE:]
    z = lambda n: jnp.zeros((d, n), F32)
    kr_a = jnp.concatenate([z(MLA_NOPE), w_kr, z(32)], axis=1)
    kr_b = jnp.concatenate([z(MLA_NOPE), -w_kr[:, half:], w_kr[:, :half], z(32)], axis=1)
    dil_scale = DIL_HD ** -0.5 * LOG2E
    w0 = jnp.concatenate([w_cq, w_ckv, kr_a, kr_b, w_qkv[:, :512] * dil_scale, w_qkv[:, 512:]], axis=1)

    r = w_uq.shape[0]
    wq = w_uq.reshape(r, MLA_HEADS, MLA_NOPE + MLA_ROPE) * ((MLA_NOPE + MLA_ROPE) ** -0.5 * LOG2E)
    zq = lambda n: jnp.zeros((r, MLA_HEADS, n), F32)
    nope, x1, x2 = wq[..., :MLA_NOPE], wq[..., MLA_NOPE:MLA_NOPE + half], wq[..., MLA_NOPE + half:]
    q_a = jnp.concatenate([nope, x1, x2, zq(32)], axis=-1).reshape(r, MLA_HEADS * LANES)
    q_b = jnp.concatenate([zq(MLA_NOPE), -x2, x1, zq(32)], axis=-1).reshape(r, MLA_HEADS * LANES)
    wq2 = jnp.concatenate([q_a, q_b], axis=1)

    rk = w_ukv.shape[0]
    wkv = w_ukv.reshape(rk, MLA_HEADS, MLA_NOPE + MLA_V)
    k_blk = jnp.concatenate([wkv[..., :MLA_NOPE], jnp.zeros((rk, MLA_HEADS, LANES - MLA_NOPE), F32)], axis=-1)
    wkv2 = jnp.concatenate([k_blk.reshape(rk, MLA_HEADS * LANES),
                            wkv[..., MLA_NOPE:].reshape(rk, MLA_HEADS * MLA_V)], axis=1)
    return w0.astype(BF16), wq2.astype(BF16), wkv2.astype(BF16)


def _dil_bias(rel_bias):
    blk = DIL_BLOCK
    qi = jnp.arange(blk)
    kj = jnp.arange(2 * blk)
    rel = blk + qi[:, None] - kj[None, :]
    out = []
    for window, dil in DIL_PATTERNS:
        band = (rel >= 0) & (rel <= window // dil)
        bias = rel_bias[_t5_bucket(jnp.maximum(rel, 0) * dil)]
        bias = jnp.transpose(bias, (2, 0, 1)).astype(F32) * LOG2E
        out.append(jnp.where(band[None], bias, NEG))
    return jnp.stack(out)


def _diff_bias(rel_bias, tile):
    nd = REL_MAX_DIST // tile + 1
    maps = rel_bias.shape[1]
    ncols = nd * tile
    m = ncols + tile
    k = jnp.arange(m)
    dist = jnp.where(k < ncols, (nd - 1) * tile - k, (nd - 1) * tile + m - k)
    vec = jnp.where((dist >= 0)[:, None], rel_bias[_t5_bucket(jnp.maximum(dist, 0))] * LOG2E, NEG).T
    big = jnp.tile(vec, (1, tile))[:, :tile * (m - 1)].reshape(maps, tile, m - 1)[:, :, :ncols]
    bias = jnp.flip(big.reshape(maps, tile, nd, tile), axis=2)
    bias = jnp.transpose(bias, (0, 2, 1, 3)).reshape(DIFF_HEADS, 2, nd, tile, tile)
    far = rel_bias[_t5_bucket(jnp.array(REL_MAX_DIST))] * LOG2E
    far = jnp.broadcast_to(far.reshape(DIFF_HEADS, 2, 1, 1), (DIFF_HEADS, 2, 8, LANES))
    return bias.astype(F32), far.astype(F32)


def _routing(r, n_tok, tile):
    n_tiles = (2 * n_tok) // tile + N_EXPERTS
    e = jnp.concatenate([r[:, 0], r[:, 1]]).astype(jnp.int32)
    onehot = (e[:, None] == jnp.arange(N_EXPERTS)[None, :]).astype(jnp.int32)
    rank = jnp.take_along_axis(jnp.cumsum(onehot, axis=0), e[:, None], axis=1)[:, 0] - 1
    counts = jnp.sum(onehot, axis=0)
    padded = ((counts + tile - 1) // tile) * tile
    ends = jnp.cumsum(padded)
    starts = ends - padded
    pos = starts[e] + rank
    token = jnp.tile(jnp.arange(n_tok, dtype=jnp.int32), 2)
    row_token = jnp.zeros((n_tiles * tile,), jnp.int32).at[pos].set(token)
    tile_start = jnp.arange(n_tiles, dtype=jnp.int32) * tile
    tile_expert = jnp.sum((tile_start[:, None] >= ends[None, :]).astype(jnp.int32), axis=1)
    n_used = (ends[-1] // tile).astype(jnp.int32)
    last = jnp.sum((ends[-1] - 1 >= ends).astype(jnp.int32))
    tile_expert = jnp.minimum(tile_expert, last).astype(jnp.int32)
    return (tile_expert, n_used.reshape(1), row_token.reshape(n_tiles, tile),
            pos[:n_tok].astype(jnp.int32), pos[n_tok:].astype(jnp.int32))


def kernel(x, c, rel_bias, ada_mix_w, ada_mix_b, mix_pre_g, mix_post_g, ada_ffn_w, ada_ffn_b, ffn_pre_g, ffn_post_g, e_w_in, e_q_norm_g, e_w_uq, e_kv_norm_g, e_w_ukv, e_w_out, ffn_w_gate, ffn_w_up, ffn_w_down, o_w_in, diff_lq1, diff_lk1, diff_lq2, diff_lk2, diff_sub_g, o_w_out, router_w, router_b, moe_w_gate, moe_w_up, moe_w_down):
    b, s, d = x.shape
    assert d == D_MODEL and s % DIL_SUPER == 0 and s % TOK_TILE == 0
    row = lambda v: v.reshape(1, -1).astype(F32)

    mix_mod = _ada(c, ada_mix_w, ada_mix_b)
    ffn_mod = _ada(c, ada_ffn_w, ada_ffn_b)

    shift, scale, gate = _split_mod(mix_mod[0])
    w0, wq2, wkv2 = _even_weights(e_w_in[0], e_w_uq[0], e_w_ukv[0])
    cq, ck, sn = _rope_tables(s)
    qa, kta, va, qb, kb, vb = _even_in(x, row(mix_pre_g[0]), shift, scale, w0, row(e_q_norm_g[0]), wq2,
                                       row(e_kv_norm_g[0]), wkv2, cq, ck, sn)
    o_a = _mla(qa, kta, va)
    o_b = _dil(qb, kb, vb, _dil_bias(rel_bias))
    fshift, fscale, fgate = _split_mod(ffn_mod[0])
    w_out = e_w_out[0].astype(BF16)
    x = _post_even(x, o_a, o_b, w_out[:512], w_out[512:], gate, row(mix_post_g[0]),
                   row(ffn_pre_g[0]), fshift, fscale, fgate, row(ffn_post_g[0]),
                   ffn_w_gate[0].astype(BF16), ffn_w_up[0].astype(BF16), ffn_w_down[0].astype(BF16))

    layer = 1
    shift, scale, gate = _split_mod(mix_mod[1])
    w_in = o_w_in[0]
    att_scale = DIFF_HD ** -0.5
    w1 = jnp.concatenate([w_in[:, :512] * (att_scale * LOG2E), w_in[:, 512:1536],
                          w_in[:, 1536:2048] * (SB_HD ** -0.5), w_in[:, 2048:]], axis=1).astype(BF16)
    qd, kdt, vd, qs, kst, vs = _odd_in(x, row(mix_pre_g[1]), shift, scale, w1)
    lam_init = 0.8 - 0.6 * math.exp(-0.3 * layer)
    lam = (jnp.exp(jnp.sum(diff_lq1[0].astype(F32) * diff_lk1[0].astype(F32)))
           - jnp.exp(jnp.sum(diff_lq2[0].astype(F32) * diff_lk2[0].astype(F32))) + lam_init)
    bias, far = _diff_bias(rel_bias, DIFF_TILE)
    o_c = _diff(qd, kdt, vd, bias, far, jnp.full((1, LANES), lam, F32), row(diff_sub_g[0]), lam_init)
    o_d = _sb(qs, kst, vs)
    fshift, fscale, fgate = _split_mod(ffn_mod[1])
    w_out = o_w_out[0].astype(BF16)
    rw = jnp.zeros((d, LANES), F32).at[:, :N_EXPERTS].set(router_w[0].astype(F32))
    rb = jnp.full((1, LANES), NEG, F32).at[0, :N_EXPERTS].set(router_b[0].astype(F32))
    x, h, r = _post_odd(x, o_c, o_d, w_out[:512], w_out[512:], gate, row(mix_post_g[1]),
                        row(ffn_pre_g[1]), fshift, fscale, rw, rb)

    n_tok = b * s
    tile_expert, n_used, row_token, pos0, pos1 = _routing(r.reshape(n_tok, LANES), n_tok, MOE_TILE)
    y = _moe(tile_expert, n_used, row_token, h.reshape(n_tok, ROW_SUB, LANES),
             moe_w_gate[0].astype(BF16), moe_w_up[0].astype(BF16), moe_w_down[0].astype(BF16))
    ct = TOK_TILE
    pos = jnp.concatenate([pos0.reshape(n_tok // ct, ct), pos1.reshape(n_tok // ct, ct)], axis=1)
    out = _combine(pos, y, x.reshape(n_tok, d), r.reshape(n_tok, LANES), fgate, row(ffn_post_g[1]), s)
    return out.reshape(b, s, d)
```

```python
import functools
import math

import jax
import jax.numpy as jnp
from jax import lax
from jax.experimental import pallas as pl
from jax.experimental.pallas import tpu as pltpu

F32 = jnp.float32
BF16 = jnp.bfloat16

D_MODEL = 1024
EPS = 1e-6

MLA_HEADS = 8
MLA_NOPE = 64
MLA_ROPE = 32
MLA_V = 64
MLA_Q_RANK = 256
MLA_KV_RANK = 128
ROPE_THETA = 10000.0

DIL_HEADS = 8
DIL_HD = 64
DIL_PATTERNS = ((128, 1), (512, 4), (2048, 16))
DIL_BLOCK = 128

DIFF_HEADS = 4
DIFF_HD = 64
SB_HEADS = 8
SB_HD = 64

REL_BUCKETS = 32
REL_MAX_DIST = 2048

D_FF = 2816
N_EXPERTS = 8
D_FF_EXPERT = 3584

LANES = 128
ROW_SUB = D_MODEL // LANES
LOG2E = math.log2(math.e)
NEG = -1e30

TOK_TILE = 512
MLA_TILE = 512
DIFF_TILE = 512
SB_TILE = 256
DIL_SUPER = DIL_BLOCK * 16
MOE_TILE = 512
SB_LOG_FLOOR = -104.0

VMEM_LIMIT = 56 * 1024 * 1024


def _cparams(sem):
    return pltpu.CompilerParams(dimension_semantics=sem, vmem_limit_bytes=VMEM_LIMIT)


def _resident(shape, index_map):
    return pl.BlockSpec(shape, index_map, pipeline_mode=pl.Buffered(1))


def _rms(x, g):
    return x * lax.rsqrt(jnp.mean(x * x, axis=-1, keepdims=True) + EPS) * g


def _dot(a, b):
    return jnp.dot(a, b, preferred_element_type=F32)


def _loop_pairs(lo, hi, step, carry):
    n = hi - lo
    carry = lax.fori_loop(0, n // 2, lambda i, c: step((lo + 2 * i, lo + 2 * i + 1), c), carry)
    return lax.cond(n % 2 == 1, lambda c: step((hi - 1,), c), lambda c: c, carry)


def _softmax_step(logits, values, carry):
    out = []
    for s_list, (m, l, acc) in zip(logits, carry):
        m_new = m
        for s in s_list:
            m_new = jnp.maximum(m_new, jnp.max(s, axis=-1, keepdims=True))
        alpha = jnp.exp2(m - m_new)
        l = alpha * l
        acc = alpha * acc
        for s, v in zip(s_list, values):
            p = jnp.exp2(s - m_new)
            l = l + jnp.sum(p, axis=-1, keepdims=True)
            acc = acc + _dot(p.astype(BF16), v)
        out.append((m_new, l, acc))
    return tuple(out)


def _ada_kernel(c_ref, w_ref, b_ref, o_ref):
    c = c_ref[...]
    sc = c / (1.0 + jnp.exp(-c))
    o_ref[0] = _dot(sc.astype(BF16), w_ref[0].astype(BF16)) + b_ref[0]


def _ada(c, w, b):
    nl, d, d3 = w.shape
    bsz = c.shape[0]
    nb = d3 // d
    return pl.pallas_call(
        _ada_kernel,
        grid=(nl, nb),
        in_specs=[
            pl.BlockSpec((bsz, d), lambda l, j: (0, 0)),
            pl.BlockSpec((1, d, d), lambda l, j: (l, 0, j)),
            pl.BlockSpec((1, 1, d), lambda l, j: (l, 0, j)),
        ],
        out_specs=pl.BlockSpec((1, bsz, d), lambda l, j: (l, 0, j)),
        out_shape=jax.ShapeDtypeStruct((nl, bsz, d3), F32),
        compiler_params=_cparams(("arbitrary", "arbitrary")),
        name="ada",
    )(c, w, b.reshape(nl, 1, d3))


def _split_mod(m):
    b = m.shape[0]
    m = m.reshape(b, 3, 1, D_MODEL)
    return m[:, 0], m[:, 1], m[:, 2]


def _prenorm_mod(x, g, shift, scale):
    return _rms(x, g) * (1.0 + scale) + shift


def _even_in_kernel(x_ref, g_ref, sh_ref, sc_ref, w0_ref, qg_ref, wq_ref, kvg_ref, wkv_ref,
                    cq_ref, ck_ref, sn_ref,
                    qa_ref, kt_ref, va_ref, qb_ref, kb_ref, vb_ref):
    h = _prenorm_mod(x_ref[0], g_ref[...], sh_ref[0], sc_ref[0]).astype(BF16)
    proj = _dot(h, w0_ref[...])
    cqn = _rms(proj[:, 0:256], qg_ref[...]).astype(BF16)
    qq = _dot(cqn, wq_ref[...])
    ckvn = _rms(proj[:, 256:384], kvg_ref[...]).astype(BF16)
    kv = _dot(ckvn, wkv_ref[...])
    cq = cq_ref[...]
    ck = ck_ref[...]
    sn = sn_ref[...]
    krope = proj[:, 384:512] * ck + proj[:, 512:640] * sn
    nh = MLA_HEADS
    for hd in range(nh):
        lo = hd * LANES
        qh = qq[:, lo:lo + LANES] * cq + qq[:, nh * LANES + lo:nh * LANES + lo + LANES] * sn
        qa_ref[0, :, lo:lo + LANES] = qh.astype(BF16)
        _store_key_tiles(kt_ref, hd, kv[:, lo:lo + LANES] + krope)
    va_ref[0] = kv[:, nh * LANES:nh * LANES + 512].astype(BF16)
    qb_ref[0] = proj[:, 640:1152].astype(BF16)
    kb_ref[0] = proj[:, 1152:1664].astype(BF16)
    vb_ref[0] = proj[:, 1664:2176].astype(BF16)


def _even_in(x, g, shift, scale, w0, qg, wq, kvg, wkv, cq, ck, sn):
    b, s, d = x.shape
    tm = TOK_TILE
    tkb = MLA_TILE
    ns = s // tm
    tok = lambda w: pl.BlockSpec((1, tm, w), lambda bi, i: (bi, i, 0))
    vec = lambda w: pl.BlockSpec((1, w), lambda bi, i: (0, 0))
    mod = pl.BlockSpec((1, 1, d), lambda bi, i: (bi, 0, 0))
    tab = pl.BlockSpec((tm, LANES), lambda bi, i: (i, 0))
    full = lambda a: _resident(a.shape, lambda bi, i: (0,) * a.ndim)
    out_shapes = (
        jax.ShapeDtypeStruct((b, s, MLA_HEADS * LANES), BF16),
        jax.ShapeDtypeStruct((b, MLA_HEADS, s // tkb, LANES, tkb), BF16),
        jax.ShapeDtypeStruct((b, s, 512), BF16),
        jax.ShapeDtypeStruct((b, s, 512), BF16),
        jax.ShapeDtypeStruct((b, s, 512), BF16),
        jax.ShapeDtypeStruct((b, s, 512), BF16),
    )
    out_specs = (
        tok(MLA_HEADS * LANES),
        pl.BlockSpec((1, MLA_HEADS, tm // tkb, LANES, tkb), lambda bi, i: (bi, 0, i, 0, 0)),
        tok(512), tok(512), tok(512), tok(512),
    )
    return pl.pallas_call(
        _even_in_kernel,
        grid=(b, ns),
        in_specs=[tok(d), vec(d), mod, mod, full(w0), vec(MLA_Q_RANK), full(wq), vec(MLA_KV_RANK), full(wkv),
                  tab, tab, tab],
        out_specs=out_specs,
        out_shape=out_shapes,
        compiler_params=_cparams(("parallel", "parallel")),
        name="even_in",
    )(x, g, shift, scale, w0, qg, wq, kvg, wkv, cq, ck, sn)


def _store_key_tiles(kt_ref, hd, k):
    tkb = kt_ref.shape[4]
    for t in range(k.shape[0] // tkb):
        kt_ref[0, hd, t] = k[t * tkb:(t + 1) * tkb, :].T.astype(BF16)


def _odd_in_kernel(x_ref, g_ref, sh_ref, sc_ref, w_ref,
                   qd_ref, kdt_ref, vd_ref, qs_ref, kst_ref, vs_ref):
    h = _prenorm_mod(x_ref[0], g_ref[...], sh_ref[0], sc_ref[0]).astype(BF16)
    proj = _dot(h, w_ref[...])
    qd_ref[0] = proj[:, 0:512].astype(BF16)
    vd_ref[0] = proj[:, 1024:1536].astype(BF16)
    qs_ref[0] = proj[:, 1536:2048].astype(BF16)
    vs_ref[0] = proj[:, 2560:3072].astype(BF16)
    for hd in range(4):
        _store_key_tiles(kdt_ref, hd, proj[:, 512 + hd * LANES:512 + (hd + 1) * LANES])
        _store_key_tiles(kst_ref, hd, proj[:, 2048 + hd * LANES:2048 + (hd + 1) * LANES])


def _odd_in(x, g, shift, scale, w):
    b, s, d = x.shape
    tm = TOK_TILE
    ns = s // tm
    tok = lambda wd: pl.BlockSpec((1, tm, wd), lambda bi, i: (bi, i, 0))
    mod = pl.BlockSpec((1, 1, d), lambda bi, i: (bi, 0, 0))
    ktspec = lambda tkb: pl.BlockSpec((1, 4, tm // tkb, LANES, tkb), lambda bi, i: (bi, 0, i, 0, 0))
    act = jax.ShapeDtypeStruct((b, s, 512), BF16)
    kts = lambda tkb: jax.ShapeDtypeStruct((b, 4, s // tkb, LANES, tkb), BF16)
    return pl.pallas_call(
        _odd_in_kernel,
        grid=(b, ns),
        in_specs=[tok(d), pl.BlockSpec((1, d), lambda bi, i: (0, 0)), mod, mod,
                  _resident(w.shape, lambda bi, i: (0, 0))],
        out_specs=(tok(512), ktspec(DIFF_TILE), tok(512), tok(512), ktspec(SB_TILE), tok(512)),
        out_shape=(act, kts(DIFF_TILE), act, act, kts(SB_TILE), act),
        compiler_params=_cparams(("parallel", "parallel")),
        name="odd_in",
    )(x, g, shift, scale, w)


def _mla_kernel(q_ref, kt_ref, v_ref, o_ref):
    tq = q_ref.shape[1]
    tk = kt_ref.shape[4]
    qi = pl.program_id(2)
    lane = lax.broadcasted_iota(jnp.int32, (tq, LANES), 1)
    causal = (lax.broadcasted_iota(jnp.int32, (tq, tk), 1)
              <= lax.broadcasted_iota(jnp.int32, (tq, tk), 0))
    qs = (q_ref[0, :, 0:LANES], q_ref[0, :, LANES:2 * LANES])

    def step(js, carry, masked):
        values = [v_ref[0, pl.ds(pl.multiple_of(j * tk, tk), tk), :] for j in js]
        logits = []
        for hd in range(2):
            s_list = [_dot(qs[hd], kt_ref[0, hd, j]) for j in js]
            if masked:
                s_list = [jnp.where(causal, s, NEG) for s in s_list]
            logits.append(s_list)
        return _softmax_step(logits, values, carry)

    one = (jnp.full((tq, 1), NEG, F32), jnp.zeros((tq, 1), F32), jnp.zeros((tq, LANES), F32))
    carry = _loop_pairs(0, qi, functools.partial(step, masked=False), (one, one))
    (_, l0, a0), (_, l1, a1) = step((qi,), carry, True)
    o_ref[0] = jnp.where(lane < 64, a0 / l0, a1 / l1).astype(BF16)


def _mla(q, kt, v):
    b, s, _ = q.shape
    tq = MLA_TILE
    nk = kt.shape[2]
    tk = kt.shape[4]
    return pl.pallas_call(
        _mla_kernel,
        grid=(b, MLA_HEADS // 2, s // tq),
        in_specs=[
            pl.BlockSpec((1, tq, 2 * LANES), lambda bi, hp, qi: (bi, qi, hp)),
            pl.BlockSpec((1, 2, nk, LANES, tk), lambda bi, hp, qi: (bi, hp, 0, 0, 0)),
            pl.BlockSpec((1, s, LANES), lambda bi, hp, qi: (bi, 0, hp)),
        ],
        out_specs=pl.BlockSpec((1, tq, LANES), lambda bi, hp, qi: (bi, qi, hp)),
        out_shape=jax.ShapeDtypeStruct((b, s, 512), BF16),
        compiler_params=_cparams(("parallel", "parallel", "arbitrary")),
        name="mla",
    )(q, kt, v)


def _dil_kernel(q_ref, kc_ref, kp_ref, vc_ref, vp_ref, bias_ref, o_ref,
                q32, k32, v32, acc_s, m_s, d_s):
    sup = DIL_SUPER
    blk = DIL_BLOCK
    n = pl.program_id(2)
    q32[...] = q_ref[0].astype(F32)
    k32[0:sup, :] = kp_ref[0].astype(F32)
    k32[sup:2 * sup, :] = kc_ref[0].astype(F32)
    v32[0:sup, :] = vp_ref[0].astype(F32)
    v32[sup:2 * sup, :] = vc_ref[0].astype(F32)
    low = lax.broadcasted_iota(jnp.int32, (blk, LANES), 1) < 64
    before_start = jnp.where(lax.broadcasted_iota(jnp.int32, (blk, 2 * blk), 1) < blk, NEG, 0.0)

    for g, (_, dil) in enumerate(DIL_PATTERNS):

        def unit(u, carry, g=g, dil=dil):
            n_loc = u // dil
            r = u % dil
            qs = n_loc * (blk * dil) + r
            ks = sup + (n_loc - 1) * (blk * dil) + r
            if dil == 1:
                qsl = pl.ds(pl.multiple_of(qs, blk), blk)
                ksl = pl.ds(pl.multiple_of(ks, blk), 2 * blk)
            else:
                qsl = pl.ds(qs, blk, stride=dil)
                ksl = pl.ds(ks, 2 * blk, stride=dil)
            q = q32[qsl, :]
            k = k32[ksl, :].astype(BF16)
            v = v32[ksl, :].astype(BF16)
            extra = jnp.where(jnp.logical_and(n == 0, n_loc == 0), before_start, 0.0)
            parts = []
            for hd in range(2):
                qh = jnp.where(low if hd == 0 else jnp.logical_not(low), q, 0.0).astype(BF16)
                s = lax.dot_general(qh, k, (((1,), (1,)), ((), ())), preferred_element_type=F32)
                s = s + bias_ref[g, hd] + extra
                m = jnp.max(s, axis=-1, keepdims=True)
                e = jnp.exp2(s - m)
                den = jnp.sum(e, axis=-1, keepdims=True)
                parts.append((_dot(e.astype(BF16), v), m, den))
            acc_s[g, qsl, :] = jnp.where(low, parts[0][0], parts[1][0])
            m_s[g, qsl, :] = jnp.where(low, parts[0][1], parts[1][1])
            d_s[g, qsl, :] = jnp.where(low, parts[0][2], parts[1][2])
            return carry

        lax.fori_loop(0, 16, unit, 0, unroll=4)

    mx = jnp.maximum(jnp.maximum(m_s[0], m_s[1]), m_s[2])
    num = jnp.zeros((sup, LANES), F32)
    den = jnp.zeros((sup, LANES), F32)
    for g in range(3):
        a = jnp.exp2(m_s[g] - mx)
        num = num + a * acc_s[g]
        den = den + a * d_s[g]
    o_ref[0] = (num / den).astype(BF16)


def _dil(q, k, v, bias):
    b, s, _ = q.shape
    sup = DIL_SUPER
    cur = pl.BlockSpec((1, sup, LANES), lambda bi, hp, n: (bi, n, hp))
    prev = pl.BlockSpec((1, sup, LANES), lambda bi, hp, n: (bi, jnp.maximum(n - 1, 0), hp))
    return pl.pallas_call(
        _dil_kernel,
        grid=(b, DIL_HEADS // 2, s // sup),
        in_specs=[cur, cur, prev, cur, prev,
                  pl.BlockSpec((3, 2, DIL_BLOCK, 2 * DIL_BLOCK), lambda bi, hp, n: (0, hp, 0, 0))],
        out_specs=cur,
        out_shape=jax.ShapeDtypeStruct((b, s, 512), BF16),
        scratch_shapes=[
            pltpu.VMEM((sup, LANES), F32),
            pltpu.VMEM((2 * sup, LANES), F32),
            pltpu.VMEM((2 * sup, LANES), F32),
            pltpu.VMEM((3, sup, LANES), F32),
            pltpu.VMEM((3, sup, LANES), F32),
            pltpu.VMEM((3, sup, LANES), F32),
        ],
        compiler_params=_cparams(("parallel", "parallel", "arbitrary")),
        name="dilated",
    )(q, k, k, v, v, bias)


def _diff_kernel(q_ref, kt_ref, v_ref, bias_ref, far_ref, lam_ref, g_ref, o_ref, *, lam_init):
    tq = q_ref.shape[1]
    tk = kt_ref.shape[4]
    nd = bias_ref.shape[2]
    qi = pl.program_id(2)
    lane = lax.broadcasted_iota(jnp.int32, (tq, LANES), 1)
    q = q_ref[0]
    zero = jnp.zeros_like(q)
    qm = (jnp.where(lane < 64, q, zero), jnp.where(lane >= 64, q, zero))

    def step(js, carry, bias_of):
        keys = [kt_ref[0, 0, j] for j in js]
        values = [v_ref[0, pl.ds(pl.multiple_of(j * tk, tk), tk), :] for j in js]
        logits = [[_dot(qm[mi], k) + bias_of(mi, j) for j, k in zip(js, keys)] for mi in range(2)]
        return _softmax_step(logits, values, carry)

    one = (jnp.full((tq, 1), NEG, F32), jnp.zeros((tq, 1), F32), jnp.zeros((tq, LANES), F32))
    carry = (one, one)
    n_far = jnp.maximum(qi - nd + 1, 0)
    carry = _loop_pairs(0, n_far, functools.partial(step, bias_of=lambda mi, j: far_ref[0, mi, 0:1, 0:1]), carry)
    carry = _loop_pairs(n_far, qi, functools.partial(step, bias_of=lambda mi, j: bias_ref[0, mi, qi - j]), carry)
    carry = step((qi,), carry, lambda mi, j: bias_ref[0, mi, 0])
    (_, l0, a0), (_, l1, a1) = carry
    o = a0 / l0 - lam_ref[...] * (a1 / l1)
    o_ref[0] = (_rms(o, g_ref[...]) * (1.0 - lam_init)).astype(BF16)


def _diff(q, kt, v, bias, far, lam, sub_g, lam_init):
    b, s, _ = q.shape
    tq = DIFF_TILE
    nk, tk = kt.shape[2], kt.shape[4]
    nd = bias.shape[2]
    return pl.pallas_call(
        functools.partial(_diff_kernel, lam_init=lam_init),
        grid=(DIFF_HEADS, b, s // tq),
        in_specs=[
            pl.BlockSpec((1, tq, LANES), lambda h, bi, qi: (bi, qi, h)),
            pl.BlockSpec((1, 1, nk, LANES, tk), lambda h, bi, qi: (bi, h, 0, 0, 0)),
            pl.BlockSpec((1, s, LANES), lambda h, bi, qi: (bi, 0, h)),
            _resident((1, 2, nd, tq, tk), lambda h, bi, qi: (h, 0, 0, 0, 0)),
            pl.BlockSpec((1, 2, 8, LANES), lambda h, bi, qi: (h, 0, 0, 0)),
            pl.BlockSpec((1, LANES), lambda h, bi, qi: (0, 0)),
            pl.BlockSpec((1, LANES), lambda h, bi, qi: (0, 0)),
        ],
        out_specs=pl.BlockSpec((1, tq, LANES), lambda h, bi, qi: (bi, qi, h)),
        out_shape=jax.ShapeDtypeStruct((b, s, 512), BF16),
        compiler_params=_cparams(("parallel", "parallel", "arbitrary")),
        name="diff",
    )(q, kt, v, bias, far, lam, sub_g)


def _sb_kernel(q_ref, kt_ref, v_ref, o_ref):
    tq = q_ref.shape[1]
    tk = kt_ref.shape[4]
    qi = pl.program_id(2)
    lane = lax.broadcasted_iota(jnp.int32, (tq, LANES), 1)
    strict = (lax.broadcasted_iota(jnp.int32, (tq, tk), 1)
              < lax.broadcasted_iota(jnp.int32, (tq, tk), 0))
    later = (lax.broadcasted_iota(jnp.int32, (tk, tk), 0)
             > lax.broadcasted_iota(jnp.int32, (tk, tk), 1)).astype(BF16)
    q = q_ref[0]
    zero = jnp.zeros_like(q)
    qh = (jnp.where(lane < 64, q, zero), jnp.where(lane >= 64, q, zero))

    def blocks(js, state, masked):
        items = [(bi, hd) for bi in range(len(js)) for hd in range(2)]
        z = {it: _dot(qh[it[1]], kt_ref[0, 0, js[it[0]]]) for it in items}
        sp, log_1m, inblock = {}, {}, {}
        for it in items:
            sp[it] = jnp.maximum(z[it], 0.0) + jnp.log2(1.0 + jnp.exp2(-jnp.abs(z[it])))
            l1m = -sp[it]
            if masked:
                l1m = jnp.where(strict, l1m, 0.0)
            log_1m[it] = l1m
            hi = l1m.astype(BF16)
            lo = (l1m - hi.astype(F32)).astype(BF16)
            inblock[it] = _dot(hi, later) + _dot(lo, later)
        state = list(state)
        for bi, j in enumerate(js):
            v = v_ref[0, pl.ds(pl.multiple_of(j * tk, tk), tk), :]
            for hd in range(2):
                it = (bi, hd)
                c, acc = state[hd]
                w = jnp.exp2(z[it] - sp[it] + (inblock[it] + c))
                if masked:
                    w = jnp.where(strict, w, 0.0)
                acc = acc + _dot(w.astype(BF16), v)
                c = c + jnp.sum(log_1m[it], axis=-1, keepdims=True)
                state[hd] = (c, acc)
        return tuple(state)

    one = (jnp.zeros((tq, 1), F32), jnp.zeros((tq, LANES), F32))
    state = blocks((qi,), (one, one), True)

    odd = qi % 2
    state = lax.cond(odd == 1, lambda st: blocks((qi - 1,), st, False), lambda st: st, state)
    floor = SB_LOG_FLOOR * LOG2E

    def cond(st):
        j, ((c0, _), (c1, _)) = st
        return jnp.logical_and(j >= 1, jnp.max(jnp.maximum(c0, c1)) > floor)

    def body(st):
        j, state = st
        return j - 2, blocks((j, j - 1), state, False)

    _, ((_, a0), (_, a1)) = lax.while_loop(cond, body, (qi - 1 - odd, state))
    o_ref[0] = jnp.where(lane < 64, a0, a1).astype(BF16)


def _sb(q, kt, v):
    b, s, _ = q.shape
    tq = SB_TILE
    nk, tk = kt.shape[2], kt.shape[4]
    return pl.pallas_call(
        _sb_kernel,
        grid=(b, SB_HEADS // 2, s // tq),
        in_specs=[
            pl.BlockSpec((1, tq, LANES), lambda bi, hp, qi: (bi, qi, hp)),
            pl.BlockSpec((1, 1, nk, LANES, tk), lambda bi, hp, qi: (bi, hp, 0, 0, 0)),
            pl.BlockSpec((1, s, LANES), lambda bi, hp, qi: (bi, 0, hp)),
        ],
        out_specs=pl.BlockSpec((1, tq, LANES), lambda bi, hp, qi: (bi, qi, hp)),
        out_shape=jax.ShapeDtypeStruct((b, s, 512), BF16),
        compiler_params=_cparams(("parallel", "parallel", "arbitrary")),
        name="stick_breaking",
    )(q, kt, v)


FF_CHUNKS = ((0, 768), (768, 1536), (1536, 2304), (2304, 2816))
EXPERT_CHUNKS = ((0, 1024), (1024, 2048), (2048, 3072), (3072, 3584))


def _swiglu(hb, wg_ref, wu_ref, wd_ref, chunks, lead, between=None):
    acc = None
    for ci, (c0, c1) in enumerate(chunks):
        g = _dot(hb, wg_ref[lead + (slice(None), slice(c0, c1))])
        u = _dot(hb, wu_ref[lead + (slice(None), slice(c0, c1))])
        a = (g / (1.0 + jnp.exp(-g)) * u).astype(BF16)
        part = _dot(a, wd_ref[lead + (slice(c0, c1), slice(None))])
        acc = part if acc is None else acc + part
        if between is not None:
            between(ci)
    return acc


def _mix_out(x_ref, oa_ref, ob_ref, wa_ref, wb_ref, gate_ref, pg_ref):
    y = _dot(oa_ref[0], wa_ref[...]) + _dot(ob_ref[0], wb_ref[...])
    return x_ref[0] + gate_ref[0] * _rms(y, pg_ref[...])


def _post_even_kernel(x_ref, oa_ref, ob_ref, wa_ref, wb_ref, gate_ref, pg_ref,
                      fg_ref, fsh_ref, fsc_ref, fgate_ref, fpg_ref, wg_ref, wu_ref, wd_ref, o_ref):
    x1 = _mix_out(x_ref, oa_ref, ob_ref, wa_ref, wb_ref, gate_ref, pg_ref)
    hb = _prenorm_mod(x1, fg_ref[...], fsh_ref[0], fsc_ref[0]).astype(BF16)
    y = _swiglu(hb, wg_ref, wu_ref, wd_ref, FF_CHUNKS, ())
    o_ref[0] = x1 + fgate_ref[0] * _rms(y, fpg_ref[...])


def _post_even(x, oa, ob, wa, wb, gate, pg, fg, fsh, fsc, fgate, fpg, wg, wu, wd):
    b, s, d = x.shape
    tm = TOK_TILE
    tok = lambda w: pl.BlockSpec((1, tm, w), lambda bi, i: (bi, i, 0))
    vec = pl.BlockSpec((1, d), lambda bi, i: (0, 0))
    mod = pl.BlockSpec((1, 1, d), lambda bi, i: (bi, 0, 0))
    full = lambda a: _resident(a.shape, lambda bi, i: (0,) * a.ndim)
    return pl.pallas_call(
        _post_even_kernel,
        grid=(b, s // tm),
        in_specs=[tok(d), tok(512), tok(512), full(wa), full(wb), mod, vec,
                  vec, mod, mod, mod, vec, full(wg), full(wu), full(wd)],
        out_specs=tok(d),
        out_shape=jax.ShapeDtypeStruct((b, s, d), F32),
        compiler_params=_cparams(("parallel", "parallel")),
        name="post_even",
    )(x, oa, ob, wa, wb, gate, pg, fg, fsh, fsc, fgate, fpg, wg, wu, wd)


def _post_odd_kernel(x_ref, oa_ref, ob_ref, wa_ref, wb_ref, gate_ref, pg_ref,
                     fg_ref, fsh_ref, fsc_ref, rw_ref, rb_ref, x_out, h_out, r_out, cnt_ref):
    x1 = _mix_out(x_ref, oa_ref, ob_ref, wa_ref, wb_ref, gate_ref, pg_ref)
    x_out[0] = x1
    h = _prenorm_mod(x1, fg_ref[...], fsh_ref[0], fsc_ref[0])
    h_out[0] = h
    logits = _dot(h, rw_ref[...]) + rb_ref[...]
    lane = lax.broadcasted_iota(jnp.int32, logits.shape, 1)
    m1 = jnp.max(logits, axis=-1, keepdims=True)
    i1 = jnp.min(jnp.where(logits == m1, lane, LANES), axis=-1, keepdims=True)
    rest = jnp.where(lane == i1, NEG, logits)
    m2 = jnp.max(rest, axis=-1, keepdims=True)
    i2 = jnp.min(jnp.where(rest == m2, lane, LANES), axis=-1, keepdims=True)
    e2 = jnp.exp(m2 - m1)
    w1 = 1.0 / (1.0 + e2)
    w2 = e2 / (1.0 + e2)
    @pl.when(jnp.logical_and(pl.program_id(0) == 0, pl.program_id(1) == 0))
    def _():
        cnt_ref[...] = jnp.zeros_like(cnt_ref)

    tm = logits.shape[0]
    sel = jnp.logical_or(lane == i1, lane == i2)
    earlier = (lax.broadcasted_iota(jnp.int32, (tm, tm), 1)
               < lax.broadcasted_iota(jnp.int32, (tm, tm), 0)).astype(BF16)
    prefix = _dot(earlier, sel.astype(BF16)) + cnt_ref[0:1, :]
    rank1 = jnp.sum(jnp.where(lane == i1, prefix, 0.0), axis=-1, keepdims=True)
    rank2 = jnp.sum(jnp.where(lane == i2, prefix, 0.0), axis=-1, keepdims=True)
    cnt_ref[...] = cnt_ref[...] + jnp.sum(sel.astype(F32), axis=0, keepdims=True)
    r = jnp.where(lane == 0, i1.astype(F32), 0.0)
    r = jnp.where(lane == 1, i2.astype(F32), r)
    r = jnp.where(lane == 2, w1, r)
    r = jnp.where(lane == 3, w2, r)
    r = jnp.where(lane == 4, rank1, r)
    r = jnp.where(lane == 5, rank2, r)
    r_out[0] = r


def _post_odd(x, oa, ob, wa, wb, gate, pg, fg, fsh, fsc, rw, rb):
    b, s, d = x.shape
    tm = TOK_TILE
    tok = lambda w: pl.BlockSpec((1, tm, w), lambda bi, i: (bi, i, 0))
    vec = pl.BlockSpec((1, d), lambda bi, i: (0, 0))
    mod = pl.BlockSpec((1, 1, d), lambda bi, i: (bi, 0, 0))
    full = lambda a: _resident(a.shape, lambda bi, i: (0,) * a.ndim)
    return pl.pallas_call(
        _post_odd_kernel,
        grid=(b, s // tm),
        in_specs=[tok(d), tok(512), tok(512), full(wa), full(wb), mod, vec,
                  vec, mod, mod, full(rw), pl.BlockSpec((1, LANES), lambda bi, i: (0, 0))],
        out_specs=(tok(d), tok(d), tok(LANES),
                   pl.BlockSpec((8, LANES), lambda bi, i: (0, 0))),
        out_shape=(jax.ShapeDtypeStruct((b, s, d), F32), jax.ShapeDtypeStruct((b, s, d), F32),
                   jax.ShapeDtypeStruct((b, s, LANES), F32), jax.ShapeDtypeStruct((8, LANES), F32)),
        compiler_params=_cparams(("arbitrary", "arbitrary")),
        name="post_odd",
    )(x, oa, ob, wa, wb, gate, pg, fg, fsh, fsc, rw, rb)


def _store_rows(ref, lead, val):
    for c in range(ROW_SUB):
        ref[lead + (slice(None), c, slice(None))] = val[:, c * LANES:(c + 1) * LANES]


def _load_rows(ref, lead, lo, hi):
    return jnp.concatenate([ref[lead, lo:hi, c, :] for c in range(ROW_SUB)], axis=1)


def _gather_ahead(i, n_steps, idx_hbm, src_hbm, idx_smem, buf, isem, sem):
    n = buf.shape[1]
    slot = i % 2
    nxt = 1 - slot

    def idx_copy(step, sl):
        return pltpu.make_async_copy(idx_hbm.at[step], idx_smem.at[sl], isem.at[sl])

    def issue_rows(sl):
        def issue(r, carry):
            t = idx_smem[sl, r]
            pltpu.make_async_copy(src_hbm.at[t], buf.at[sl, r], sem.at[sl]).start()
            return carry

        lax.fori_loop(0, n, issue, 0, unroll=8)

    @pl.when(i == 0)
    def _():
        first = idx_copy(0, 0)
        first.start()
        first.wait()
        issue_rows(0)

        @pl.when(n_steps > 1)
        def _():
            idx_copy(1, 1).start()

    @pl.when(i + 1 < n_steps)
    def _():
        idx_copy(i + 1, nxt).wait()
        issue_rows(nxt)

    @pl.when(i + 2 < n_steps)
    def _():
        idx_copy(i + 2, slot).start()

    pltpu.make_async_copy(src_hbm.at[pl.ds(0, n)], buf.at[slot], sem.at[slot]).wait()
    return slot


def _moe_kernel(te_ref, nu_ref, tok_hbm, h_hbm, wg_ref, wu_ref, wd_ref, y_ref, idx_smem, buf, isem, sem):
    i = pl.program_id(0)
    nt = pl.num_programs(0)
    n = buf.shape[1]
    slot = i % 2
    nxt = 1 - slot
    tile = lambda t: jnp.minimum(t, nt - 1)

    def idx_copy(t, sl):
        return pltpu.make_async_copy(tok_hbm.at[tile(t)], idx_smem.at[sl], isem.at[sl])

    def rows_wait(sl):
        pltpu.make_async_copy(h_hbm.at[pl.ds(0, n), :], buf.at[sl], sem.at[sl]).wait()

    def issue_row(sl, r):
        t = idx_smem[sl, r]
        pltpu.make_async_copy(h_hbm.at[pl.ds(t, 1), :], buf.at[sl, pl.ds(r, 1), :], sem.at[sl]).start()

    def issue_loop(sl):
        def body(r, carry):
            issue_row(sl, r)
            return carry

        lax.fori_loop(0, n, body, 0, unroll=8)

    @pl.when(i == 0)
    def _():
        first = idx_copy(0, 0)
        first.start()
        first.wait()
        issue_loop(0)
        idx_copy(1, 1).start()

    idx_copy(i + 1, nxt).wait()
    idx_copy(i + 2, slot).start()
    rows_wait(slot)

    @pl.when(i < nu_ref[0])
    def _():
        xs = buf[slot].astype(BF16)
        per_chunk = n // len(EXPERT_CHUNKS)

        def issue_part(ci):
            for r in range(ci * per_chunk, (ci + 1) * per_chunk):
                issue_row(nxt, r)

        y = _swiglu(xs, wg_ref, wu_ref, wd_ref, EXPERT_CHUNKS, (0,), between=issue_part)
        _store_rows(y_ref, (), y)

    @pl.when(i >= nu_ref[0])
    def _():
        issue_loop(nxt)
        y_ref[...] = jnp.zeros_like(y_ref)

    @pl.when(i == nt - 1)
    def _():
        rows_wait(nxt)
        idx_copy(i + 2, slot).wait()


def _moe(tile_expert, n_used, row_token, h, wg, wu, wd):
    d = h.shape[1]
    nt, tm = row_token.shape
    dff = wg.shape[2]
    wspec = lambda shp: pl.BlockSpec(shp, lambda i, te, nu: (te[i], 0, 0), pipeline_mode=pl.Buffered(1))
    grid_spec = pltpu.PrefetchScalarGridSpec(
        num_scalar_prefetch=2,
        grid=(nt,),
        in_specs=[
            pl.BlockSpec(memory_space=pl.ANY),
            pl.BlockSpec(memory_space=pl.ANY),
            wspec((1, d, dff)), wspec((1, d, dff)), wspec((1, dff, d)),
        ],
        out_specs=pl.BlockSpec((tm, ROW_SUB, LANES), lambda i, te, nu: (i, 0, 0)),
        scratch_shapes=[
            pltpu.SMEM((2, tm), jnp.int32),
            pltpu.VMEM((2, tm, d), F32),
            pltpu.SemaphoreType.DMA((2,)),
            pltpu.SemaphoreType.DMA((2,)),
        ],
    )
    return pl.pallas_call(
        _moe_kernel,
        grid_spec=grid_spec,
        out_shape=jax.ShapeDtypeStruct((nt * tm, ROW_SUB, LANES), F32),
        compiler_params=_cparams(("arbitrary",)),
        name="moe_experts",
    )(tile_expert, n_used, row_token, h, wg, wu, wd)


def _combine_kernel(pos_hbm, y_hbm, x_ref, r_ref, gate_ref, pg_ref, o_ref, idx_smem, buf, isem, sem):
    tm = x_ref.shape[0]
    slot = _gather_ahead(pl.program_id(0), pl.num_programs(0), pos_hbm, y_hbm, idx_smem, buf, isem, sem)
    r = r_ref[...]
    y = r[:, 2:3] * _load_rows(buf, slot, 0, tm) + r[:, 3:4] * _load_rows(buf, slot, tm, 2 * tm)
    o_ref[...] = x_ref[...] + gate_ref[0] * _rms(y, pg_ref[...])


def _combine(pos, y, x, r, gate, pg, tokens_per_seq):
    n_tok, d = x.shape
    nt, tm2 = pos.shape
    tm = tm2 // 2
    per_seq = tokens_per_seq // tm
    tok = lambda w: pl.BlockSpec((tm, w), lambda i: (i, 0))
    return pl.pallas_call(
        _combine_kernel,
        grid=(nt,),
        in_specs=[
            pl.BlockSpec(memory_space=pl.ANY),
            pl.BlockSpec(memory_space=pl.ANY),
            tok(d), tok(LANES),
            pl.BlockSpec((1, 1, d), lambda i: (i // per_seq, 0, 0)),
            pl.BlockSpec((1, d), lambda i: (0, 0)),
        ],
        out_specs=tok(d),
        out_shape=jax.ShapeDtypeStruct((n_tok, d), F32),
        scratch_shapes=[
            pltpu.SMEM((2, tm2), jnp.int32),
            pltpu.VMEM((2, tm2, ROW_SUB, LANES), F32),
            pltpu.SemaphoreType.DMA((2,)),
            pltpu.SemaphoreType.DMA((2,)),
        ],
        compiler_params=_cparams(("arbitrary",)),
        name="moe_combine",
    )(pos, y, x, r, gate, pg)


def _t5_bucket(dist):
    max_exact = REL_BUCKETS // 2
    d = jnp.maximum(dist, 1).astype(F32)
    log_b = max_exact + (jnp.log(d / max_exact) / math.log(REL_MAX_DIST / max_exact)
                         * (REL_BUCKETS - max_exact)).astype(jnp.int32)
    log_b = jnp.minimum(log_b, REL_BUCKETS - 1)
    return jnp.where(dist < max_exact, dist, log_b)


def _rope_tables(s):
    half = MLA_ROPE // 2
    freqs = ROPE_THETA ** (-jnp.arange(half, dtype=F32) / half)
    ang = jnp.arange(s, dtype=F32)[:, None] * freqs[None, :]
    cos, sin = jnp.cos(ang), jnp.sin(ang)
    z64 = jnp.zeros((s, MLA_NOPE), F32)
    z32 = jnp.zeros((s, LANES - MLA_NOPE - MLA_ROPE), F32)
    ck = jnp.concatenate([z64, cos, cos, z32], axis=1)
    cq = jnp.concatenate([jnp.ones((s, MLA_NOPE), F32), cos, cos, z32], axis=1)
    sn = jnp.concatenate([z64, sin, sin, z32], axis=1)
    return cq, ck, sn


def _even_weights(w_in, w_uq, w_ukv):
    d = w_in.shape[0]
    half = MLA_ROPE // 2
    w_cq = w_in[:, :MLA_Q_RANK]
    w_ckv = w_in[:, MLA_Q_RANK:MLA_Q_RANK + MLA_KV_RANK]
    w_kr = w_in[:, MLA_Q_RANK + MLA_KV_RANK:MLA_Q_RANK + MLA_KV_RANK + MLA_ROPE]
    w_qkv = w_in[:, MLA_Q_RANK + MLA_KV_RANK + MLA_ROPE:]
    z = lambda n: jnp.zeros((d, n), F32)
    kr_a = jnp.concatenate([z(MLA_NOPE), w_kr, z(32)], axis=1)
    kr_b = jnp.concatenate([z(MLA_NOPE), -w_kr[:, half:], w_kr[:, :half], z(32)], axis=1)
    dil_scale = DIL_HD ** -0.5 * LOG2E
    w0 = jnp.concatenate([w_cq, w_ckv, kr_a, kr_b, w_qkv[:, :512] * dil_scale, w_qkv[:, 512:]], axis=1)

    r = w_uq.shape[0]
    wq = w_uq.reshape(r, MLA_HEADS, MLA_NOPE + MLA_ROPE) * ((MLA_NOPE + MLA_ROPE) ** -0.5 * LOG2E)
    zq = lambda n: jnp.zeros((r, MLA_HEADS, n), F32)
    nope, x1, x2 = wq[..., :MLA_NOPE], wq[..., MLA_NOPE:MLA_NOPE + half], wq[..., MLA_NOPE + half:]
    q_a = jnp.concatenate([nope, x1, x2, zq(32)], axis=-1).reshape(r, MLA_HEADS * LANES)
    q_b = jnp.concatenate([zq(MLA_NOPE), -x2, x1, zq(32)], axis=-1).reshape(r, MLA_HEADS * LANES)
    wq2 = jnp.concatenate([q_a, q_b], axis=1)

    rk = w_ukv.shape[0]
    wkv = w_ukv.reshape(rk, MLA_HEADS, MLA_NOPE + MLA_V)
    k_blk = jnp.concatenate([wkv[..., :MLA_NOPE], jnp.zeros((rk, MLA_HEADS, LANES - MLA_NOPE), F32)], axis=-1)
    wkv2 = jnp.concatenate([k_blk.reshape(rk, MLA_HEADS * LANES),
                            wkv[..., MLA_NOPE:].reshape(rk, MLA_HEADS * MLA_V)], axis=1)
    return w0.astype(BF16), wq2.astype(BF16), wkv2.astype(BF16)


def _dil_bias(rel_bias):
    blk = DIL_BLOCK
    qi = jnp.arange(blk)
    kj = jnp.arange(2 * blk)
    rel = blk + qi[:, None] - kj[None, :]
    out = []
    for window, dil in DIL_PATTERNS:
        band = (rel >= 0) & (rel <= window // dil)
        bias = rel_bias[_t5_bucket(jnp.maximum(rel, 0) * dil)]
        bias = jnp.transpose(bias, (2, 0, 1)).astype(F32) * LOG2E
        out.append(jnp.where(band[None], bias, NEG))
    return jnp.stack(out)


def _diff_bias(rel_bias, tile):
    nd = REL_MAX_DIST // tile + 1
    maps = rel_bias.shape[1]
    ncols = nd * tile
    m = ncols + tile
    k = jnp.arange(m)
    dist = jnp.where(k < ncols, (nd - 1) * tile - k, (nd - 1) * tile + m - k)
    vec = jnp.where((dist >= 0)[:, None], rel_bias[_t5_bucket(jnp.maximum(dist, 0))] * LOG2E, NEG).T
    big = jnp.tile(vec, (1, tile))[:, :tile * (m - 1)].reshape(maps, tile, m - 1)[:, :, :ncols]
    bias = jnp.flip(big.reshape(maps, tile, nd, tile), axis=2)
    bias = jnp.transpose(bias, (0, 2, 1, 3)).reshape(DIFF_HEADS, 2, nd, tile, tile)
    far = rel_bias[_t5_bucket(jnp.array(REL_MAX_DIST))] * LOG2E
    far = jnp.broadcast_to(far.reshape(DIFF_HEADS, 2, 1, 1), (DIFF_HEADS, 2, 8, LANES))
    return bias.astype(F32), far.astype(F32)


def _routing(r, counts, n_tok, tile):
    n_tiles = (2 * n_tok) // tile + N_EXPERTS
    e = jnp.concatenate([r[:, 0], r[:, 1]]).astype(jnp.int32)
    rank = jnp.concatenate([r[:, 4], r[:, 5]]).astype(jnp.int32)
    counts = counts.astype(jnp.int32)
    padded = ((counts + tile - 1) // tile) * tile
    ends = jnp.cumsum(padded)
    starts = ends - padded
    onehot = (e[:, None] == jnp.arange(N_EXPERTS)[None, :]).astype(jnp.int32)
    pos = jnp.sum(onehot * starts[None, :], axis=1) + rank
    token = jnp.tile(jnp.arange(n_tok, dtype=jnp.int32), 2)
    row_token = jnp.zeros((n_tiles * tile,), jnp.int32).at[pos].set(token)
    tile_start = jnp.arange(n_tiles, dtype=jnp.int32) * tile
    tile_expert = jnp.sum((tile_start[:, None] >= ends[None, :]).astype(jnp.int32), axis=1)
    n_used = (ends[-1] // tile).astype(jnp.int32)
    last = jnp.sum((ends[-1] - 1 >= ends).astype(jnp.int32))
    tile_expert = jnp.minimum(tile_expert, last).astype(jnp.int32)
    return (tile_expert, n_used.reshape(1), row_token.reshape(n_tiles, tile),
            pos[:n_tok].astype(jnp.int32), pos[n_tok:].astype(jnp.int32))


def kernel(x, c, rel_bias, ada_mix_w, ada_mix_b, mix_pre_g, mix_post_g, ada_ffn_w, ada_ffn_b, ffn_pre_g, ffn_post_g, e_w_in, e_q_norm_g, e_w_uq, e_kv_norm_g, e_w_ukv, e_w_out, ffn_w_gate, ffn_w_up, ffn_w_down, o_w_in, diff_lq1, diff_lk1, diff_lq2, diff_lk2, diff_sub_g, o_w_out, router_w, router_b, moe_w_gate, moe_w_up, moe_w_down):
    b, s, d = x.shape
    assert d == D_MODEL and s % DIL_SUPER == 0 and s % TOK_TILE == 0
    row = lambda v: v.reshape(1, -1).astype(F32)

    mix_mod = _ada(c, ada_mix_w, ada_mix_b)
    ffn_mod = _ada(c, ada_ffn_w, ada_ffn_b)

    shift, scale, gate = _split_mod(mix_mod[0])
    w0, wq2, wkv2 = _even_weights(e_w_in[0], e_w_uq[0], e_w_ukv[0])
    cq, ck, sn = _rope_tables(s)
    qa, kta, va, qb, kb, vb = _even_in(x, row(mix_pre_g[0]), shift, scale, w0, row(e_q_norm_g[0]), wq2,
                                       row(e_kv_norm_g[0]), wkv2, cq, ck, sn)
    o_a = _mla(qa, kta, va)
    o_b = _dil(qb, kb, vb, _dil_bias(rel_bias))
    fshift, fscale, fgate = _split_mod(ffn_mod[0])
    w_out = e_w_out[0].astype(BF16)
    x = _post_even(x, o_a, o_b, w_out[:512], w_out[512:], gate, row(mix_post_g[0]),
                   row(ffn_pre_g[0]), fshift, fscale, fgate, row(ffn_post_g[0]),
                   ffn_w_gate[0].astype(BF16), ffn_w_up[0].astype(BF16), ffn_w_down[0].astype(BF16))

    layer = 1
    shift, scale, gate = _split_mod(mix_mod[1])
    w_in = o_w_in[0]
    att_scale = DIFF_HD ** -0.5
    w1 = jnp.concatenate([w_in[:, :512] * (att_scale * LOG2E), w_in[:, 512:1536],
                          w_in[:, 1536:2048] * (SB_HD ** -0.5 * LOG2E), w_in[:, 2048:]], axis=1).astype(BF16)
    qd, kdt, vd, qs, kst, vs = _odd_in(x, row(mix_pre_g[1]), shift, scale, w1)
    lam_init = 0.8 - 0.6 * math.exp(-0.3 * layer)
    lam = (jnp.exp(jnp.sum(diff_lq1[0].astype(F32) * diff_lk1[0].astype(F32)))
           - jnp.exp(jnp.sum(diff_lq2[0].astype(F32) * diff_lk2[0].astype(F32))) + lam_init)
    bias, far = _diff_bias(rel_bias, DIFF_TILE)
    o_c = _diff(qd, kdt, vd, bias, far, jnp.full((1, LANES), lam, F32), row(diff_sub_g[0]), lam_init)
    o_d = _sb(qs, kst, vs)
    fshift, fscale, fgate = _split_mod(ffn_mod[1])
    w_out = o_w_out[0].astype(BF16)
    rw = jnp.zeros((d, LANES), F32).at[:, :N_EXPERTS].set(router_w[0].astype(F32))
    rb = jnp.full((1, LANES), NEG, F32).at[0, :N_EXPERTS].set(router_b[0].astype(F32))
    x, h, r, counts = _post_odd(x, o_c, o_d, w_out[:512], w_out[512:], gate, row(mix_post_g[1]),
                                row(ffn_pre_g[1]), fshift, fscale, rw, rb)

    n_tok = b * s
    tile_expert, n_used, row_token, pos0, pos1 = _routing(r.reshape(n_tok, LANES), counts[0, :N_EXPERTS],
                                                          n_tok, MOE_TILE)
    y = _moe(tile_expert, n_used, row_token, h.reshape(n_tok, d),
             moe_w_gate[0].astype(BF16), moe_w_up[0].astype(BF16), moe_w_down[0].astype(BF16))
    ct = TOK_TILE
    pos = jnp.concatenate([pos0.reshape(n_tok // ct, ct), pos1.reshape(n_tok // ct, ct)], axis=1)
    out = _combine(pos, y, x.reshape(n_tok, d), r.reshape(n_tok, LANES), fgate, row(ffn_post_g[1]), s)
    return out.reshape(b, s, d)
```

```python
import functools
import math

import jax
import jax.numpy as jnp
from jax import lax
from jax.experimental import pallas as pl
from jax.experimental.pallas import tpu as pltpu

F32 = jnp.float32
BF16 = jnp.bfloat16

D_MODEL = 1024
EPS = 1e-6

MLA_HEADS = 8
MLA_NOPE = 64
MLA_ROPE = 32
MLA_V = 64
MLA_Q_RANK = 256
MLA_KV_RANK = 128
ROPE_THETA = 10000.0

DIL_HEADS = 8
DIL_HD = 64
DIL_PATTERNS = ((128, 1), (512, 4), (2048, 16))
DIL_BLOCK = 128

DIFF_HEADS = 4
DIFF_HD = 64
SB_HEADS = 8
SB_HD = 64

REL_BUCKETS = 32
REL_MAX_DIST = 2048

D_FF = 2816
N_EXPERTS = 8
D_FF_EXPERT = 3584

LANES = 128
ROW_SUB = D_MODEL // LANES
LOG2E = math.log2(math.e)
NEG = -1e30

TOK_TILE = 512
MLA_TILE = 512
DIFF_TILE = 512
SB_TILE = 256
DIL_SUPER = DIL_BLOCK * 16
MOE_TILE = 512
SB_LOG_FLOOR = -104.0

VMEM_LIMIT = 56 * 1024 * 1024


def _cparams(sem):
    return pltpu.CompilerParams(dimension_semantics=sem, vmem_limit_bytes=VMEM_LIMIT)


def _resident(shape, index_map):
    return pl.BlockSpec(shape, index_map, pipeline_mode=pl.Buffered(1))


def _rms(x, g):
    return x * lax.rsqrt(jnp.mean(x * x, axis=-1, keepdims=True) + EPS) * g


def _dot(a, b):
    return jnp.dot(a, b, preferred_element_type=F32)


def _loop_pairs(lo, hi, step, carry, group=2):
    n = hi - lo
    carry = lax.fori_loop(
        0, n // group, lambda i, c: step(tuple(lo + group * i + g for g in range(group)), c), carry)
    done = lo + (n // group) * group
    if group == 4:
        carry = lax.cond(hi - done >= 2, lambda c: step((done, done + 1), c), lambda c: c, carry)
    return lax.cond(n % 2 == 1, lambda c: step((hi - 1,), c), lambda c: c, carry)


def _softmax_step(logits, values, carry):
    out = []
    for s_list, (m, l, acc) in zip(logits, carry):
        m_new = m
        for s in s_list:
            m_new = jnp.maximum(m_new, jnp.max(s, axis=-1, keepdims=True))
        alpha = jnp.exp2(m - m_new)
        l = alpha * l
        acc = alpha * acc
        for s, v in zip(s_list, values):
            p = jnp.exp2(s - m_new)
            l = l + jnp.sum(p, axis=-1, keepdims=True)
            acc = acc + _dot(p.astype(BF16), v)
        out.append((m_new, l, acc))
    return tuple(out)


def _ada_kernel(c_ref, w_ref, b_ref, o_ref):
    c = c_ref[...]
    sc = c / (1.0 + jnp.exp(-c))
    o_ref[0] = _dot(sc.astype(BF16), w_ref[0].astype(BF16)) + b_ref[0]


def _ada(c, w, b):
    nl, d, d3 = w.shape
    bsz = c.shape[0]
    nb = d3 // d
    return pl.pallas_call(
        _ada_kernel,
        grid=(nl, nb),
        in_specs=[
            pl.BlockSpec((bsz, d), lambda l, j: (0, 0)),
            pl.BlockSpec((1, d, d), lambda l, j: (l, 0, j)),
            pl.BlockSpec((1, 1, d), lambda l, j: (l, 0, j)),
        ],
        out_specs=pl.BlockSpec((1, bsz, d), lambda l, j: (l, 0, j)),
        out_shape=jax.ShapeDtypeStruct((nl, bsz, d3), F32),
        compiler_params=_cparams(("arbitrary", "arbitrary")),
        name="ada",
    )(c, w, b.reshape(nl, 1, d3))


def _split_mod(m):
    b = m.shape[0]
    m = m.reshape(b, 3, 1, D_MODEL)
    return m[:, 0], m[:, 1], m[:, 2]


def _prenorm_mod(x, g, shift, scale):
    return _rms(x, g) * (1.0 + scale) + shift


def _even_in_kernel(x_ref, g_ref, sh_ref, sc_ref, w0_ref, qg_ref, wq_ref, kvg_ref, wkv_ref,
                    cq_ref, ck_ref, sn_ref,
                    qa_ref, kt_ref, va_ref, qb_ref, kb_ref, vb_ref):
    h = _prenorm_mod(x_ref[0], g_ref[...], sh_ref[0], sc_ref[0]).astype(BF16)
    proj = _dot(h, w0_ref[...])
    cqn = _rms(proj[:, 0:256], qg_ref[...]).astype(BF16)
    qq = _dot(cqn, wq_ref[...])
    ckvn = _rms(proj[:, 256:384], kvg_ref[...]).astype(BF16)
    kv = _dot(ckvn, wkv_ref[...])
    cq = cq_ref[...]
    ck = ck_ref[...]
    sn = sn_ref[...]
    krope = proj[:, 384:512] * ck + proj[:, 512:640] * sn
    nh = MLA_HEADS
    for hd in range(nh):
        lo = hd * LANES
        qh = qq[:, lo:lo + LANES] * cq + qq[:, nh * LANES + lo:nh * LANES + lo + LANES] * sn
        qa_ref[0, :, lo:lo + LANES] = qh.astype(BF16)
        _store_key_tiles(kt_ref, hd, kv[:, lo:lo + LANES] + krope)
    va_ref[0] = kv[:, nh * LANES:nh * LANES + 512].astype(BF16)
    qb_ref[0] = proj[:, 640:1152].astype(BF16)
    kb_ref[0] = proj[:, 1152:1664].astype(BF16)
    vb_ref[0] = proj[:, 1664:2176].astype(BF16)


def _even_in(x, g, shift, scale, w0, qg, wq, kvg, wkv, cq, ck, sn):
    b, s, d = x.shape
    tm = TOK_TILE
    tkb = MLA_TILE
    ns = s // tm
    tok = lambda w: pl.BlockSpec((1, tm, w), lambda bi, i: (bi, i, 0))
    vec = lambda w: pl.BlockSpec((1, w), lambda bi, i: (0, 0))
    mod = pl.BlockSpec((1, 1, d), lambda bi, i: (bi, 0, 0))
    tab = pl.BlockSpec((tm, LANES), lambda bi, i: (i, 0))
    full = lambda a: _resident(a.shape, lambda bi, i: (0,) * a.ndim)
    out_shapes = (
        jax.ShapeDtypeStruct((b, s, MLA_HEADS * LANES), BF16),
        jax.ShapeDtypeStruct((b, MLA_HEADS, s // tkb, LANES, tkb), BF16),
        jax.ShapeDtypeStruct((b, s, 512), BF16),
        jax.ShapeDtypeStruct((b, s, 512), BF16),
        jax.ShapeDtypeStruct((b, s, 512), BF16),
        jax.ShapeDtypeStruct((b, s, 512), BF16),
    )
    out_specs = (
        tok(MLA_HEADS * LANES),
        pl.BlockSpec((1, MLA_HEADS, tm // tkb, LANES, tkb), lambda bi, i: (bi, 0, i, 0, 0)),
        tok(512), tok(512), tok(512), tok(512),
    )
    return pl.pallas_call(
        _even_in_kernel,
        grid=(b, ns),
        in_specs=[tok(d), vec(d), mod, mod, full(w0), vec(MLA_Q_RANK), full(wq), vec(MLA_KV_RANK), full(wkv),
                  tab, tab, tab],
        out_specs=out_specs,
        out_shape=out_shapes,
        compiler_params=_cparams(("parallel", "parallel")),
        name="even_in",
    )(x, g, shift, scale, w0, qg, wq, kvg, wkv, cq, ck, sn)


def _store_key_tiles(kt_ref, hd, k):
    tkb = kt_ref.shape[4]
    for t in range(k.shape[0] // tkb):
        kt_ref[0, hd, t] = k[t * tkb:(t + 1) * tkb, :].T.astype(BF16)


def _odd_in_kernel(x_ref, g_ref, sh_ref, sc_ref, w_ref,
                   qd_ref, kdt_ref, vd_ref, qs_ref, kst_ref, vs_ref):
    h = _prenorm_mod(x_ref[0], g_ref[...], sh_ref[0], sc_ref[0]).astype(BF16)
    proj = _dot(h, w_ref[...])
    qd_ref[0] = proj[:, 0:512].astype(BF16)
    vd_ref[0] = proj[:, 1024:1536].astype(BF16)
    qs_ref[0] = proj[:, 1536:2048].astype(BF16)
    vs_ref[0] = proj[:, 2560:3072].astype(BF16)
    for hd in range(4):
        _store_key_tiles(kdt_ref, hd, proj[:, 512 + hd * LANES:512 + (hd + 1) * LANES])
        _store_key_tiles(kst_ref, hd, proj[:, 2048 + hd * LANES:2048 + (hd + 1) * LANES])


def _odd_in(x, g, shift, scale, w):
    b, s, d = x.shape
    tm = TOK_TILE
    ns = s // tm
    tok = lambda wd: pl.BlockSpec((1, tm, wd), lambda bi, i: (bi, i, 0))
    mod = pl.BlockSpec((1, 1, d), lambda bi, i: (bi, 0, 0))
    ktspec = lambda tkb: pl.BlockSpec((1, 4, tm // tkb, LANES, tkb), lambda bi, i: (bi, 0, i, 0, 0))
    act = jax.ShapeDtypeStruct((b, s, 512), BF16)
    kts = lambda tkb: jax.ShapeDtypeStruct((b, 4, s // tkb, LANES, tkb), BF16)
    return pl.pallas_call(
        _odd_in_kernel,
        grid=(b, ns),
        in_specs=[tok(d), pl.BlockSpec((1, d), lambda bi, i: (0, 0)), mod, mod,
                  _resident(w.shape, lambda bi, i: (0, 0))],
        out_specs=(tok(512), ktspec(DIFF_TILE), tok(512), tok(512), ktspec(SB_TILE), tok(512)),
        out_shape=(act, kts(DIFF_TILE), act, act, kts(SB_TILE), act),
        compiler_params=_cparams(("parallel", "parallel")),
        name="odd_in",
    )(x, g, shift, scale, w)


def _mla_kernel(q_ref, kt_ref, v_ref, o_ref):
    tq = q_ref.shape[1]
    tk = kt_ref.shape[4]
    qi = pl.program_id(2)
    lane = lax.broadcasted_iota(jnp.int32, (tq, LANES), 1)
    causal = (lax.broadcasted_iota(jnp.int32, (tq, tk), 1)
              <= lax.broadcasted_iota(jnp.int32, (tq, tk), 0))
    qs = (q_ref[0, :, 0:LANES], q_ref[0, :, LANES:2 * LANES])

    def step(js, carry, masked):
        values = [v_ref[0, pl.ds(pl.multiple_of(j * tk, tk), tk), :] for j in js]
        logits = []
        for hd in range(2):
            s_list = [_dot(qs[hd], kt_ref[0, hd, j]) for j in js]
            if masked:
                s_list = [jnp.where(causal, s, NEG) for s in s_list]
            logits.append(s_list)
        return _softmax_step(logits, values, carry)

    one = (jnp.full((tq, 1), NEG, F32), jnp.zeros((tq, 1), F32), jnp.zeros((tq, LANES), F32))
    carry = _loop_pairs(0, qi, functools.partial(step, masked=False), (one, one), group=4)
    (_, l0, a0), (_, l1, a1) = step((qi,), carry, True)
    o_ref[0] = jnp.where(lane < 64, a0 / l0, a1 / l1).astype(BF16)


def _mla(q, kt, v):
    b, s, _ = q.shape
    tq = MLA_TILE
    nk = kt.shape[2]
    tk = kt.shape[4]
    return pl.pallas_call(
        _mla_kernel,
        grid=(b, MLA_HEADS // 2, s // tq),
        in_specs=[
            pl.BlockSpec((1, tq, 2 * LANES), lambda bi, hp, qi: (bi, qi, hp)),
            pl.BlockSpec((1, 2, nk, LANES, tk), lambda bi, hp, qi: (bi, hp, 0, 0, 0)),
            pl.BlockSpec((1, s, LANES), lambda bi, hp, qi: (bi, 0, hp)),
        ],
        out_specs=pl.BlockSpec((1, tq, LANES), lambda bi, hp, qi: (bi, qi, hp)),
        out_shape=jax.ShapeDtypeStruct((b, s, 512), BF16),
        compiler_params=_cparams(("parallel", "parallel", "arbitrary")),
        name="mla",
    )(q, kt, v)


def _dil_kernel(q_ref, kc_ref, kp_ref, vc_ref, vp_ref, bias_ref, o_ref,
                q32, k32, v32, acc_s, m_s, d_s):
    sup = DIL_SUPER
    blk = DIL_BLOCK
    n = pl.program_id(2)
    q32[...] = q_ref[0].astype(F32)
    k32[0:sup, :] = kp_ref[0].astype(F32)
    k32[sup:2 * sup, :] = kc_ref[0].astype(F32)
    v32[0:sup, :] = vp_ref[0].astype(F32)
    v32[sup:2 * sup, :] = vc_ref[0].astype(F32)
    low = lax.broadcasted_iota(jnp.int32, (blk, LANES), 1) < 64
    before_start = jnp.where(lax.broadcasted_iota(jnp.int32, (blk, 2 * blk), 1) < blk, NEG, 0.0)

    for g, (_, dil) in enumerate(DIL_PATTERNS):

        def unit(u, carry, g=g, dil=dil):
            n_loc = u // dil
            r = u % dil
            qs = n_loc * (blk * dil) + r
            ks = sup + (n_loc - 1) * (blk * dil) + r
            if dil == 1:
                qsl = pl.ds(pl.multiple_of(qs, blk), blk)
                ksl = pl.ds(pl.multiple_of(ks, blk), 2 * blk)
            else:
                qsl = pl.ds(qs, blk, stride=dil)
                ksl = pl.ds(ks, 2 * blk, stride=dil)
            q = q32[qsl, :]
            k = k32[ksl, :].astype(BF16)
            v = v32[ksl, :].astype(BF16)
            extra = jnp.where(jnp.logical_and(n == 0, n_loc == 0), before_start, 0.0)
            parts = []
            for hd in range(2):
                qh = jnp.where(low if hd == 0 else jnp.logical_not(low), q, 0.0).astype(BF16)
                s = lax.dot_general(qh, k, (((1,), (1,)), ((), ())), preferred_element_type=F32)
                s = s + bias_ref[g, hd] + extra
                m = jnp.max(s, axis=-1, keepdims=True)
                e = jnp.exp2(s - m)
                den = jnp.sum(e, axis=-1, keepdims=True)
                parts.append((_dot(e.astype(BF16), v), m, den))
            acc_s[g, qsl, :] = jnp.where(low, parts[0][0], parts[1][0])
            m_s[g, qsl, :] = jnp.where(low, parts[0][1], parts[1][1])
            d_s[g, qsl, :] = jnp.where(low, parts[0][2], parts[1][2])
            return carry

        lax.fori_loop(0, 16, unit, 0, unroll=8)

    mx = jnp.maximum(jnp.maximum(m_s[0], m_s[1]), m_s[2])
    num = jnp.zeros((sup, LANES), F32)
    den = jnp.zeros((sup, LANES), F32)
    for g in range(3):
        a = jnp.exp2(m_s[g] - mx)
        num = num + a * acc_s[g]
        den = den + a * d_s[g]
    o_ref[0] = (num / den).astype(BF16)


def _dil(q, k, v, bias):
    b, s, _ = q.shape
    sup = DIL_SUPER
    cur = pl.BlockSpec((1, sup, LANES), lambda bi, hp, n: (bi, n, hp))
    prev = pl.BlockSpec((1, sup, LANES), lambda bi, hp, n: (bi, jnp.maximum(n - 1, 0), hp))
    return pl.pallas_call(
        _dil_kernel,
        grid=(b, DIL_HEADS // 2, s // sup),
        in_specs=[cur, cur, prev, cur, prev,
                  pl.BlockSpec((3, 2, DIL_BLOCK, 2 * DIL_BLOCK), lambda bi, hp, n: (0, hp, 0, 0))],
        out_specs=cur,
        out_shape=jax.ShapeDtypeStruct((b, s, 512), BF16),
        scratch_shapes=[
            pltpu.VMEM((sup, LANES), F32),
            pltpu.VMEM((2 * sup, LANES), F32),
            pltpu.VMEM((2 * sup, LANES), F32),
            pltpu.VMEM((3, sup, LANES), F32),
            pltpu.VMEM((3, sup, LANES), F32),
            pltpu.VMEM((3, sup, LANES), F32),
        ],
        compiler_params=_cparams(("parallel", "parallel", "arbitrary")),
        name="dilated",
    )(q, k, k, v, v, bias)


def _diff_kernel(q_ref, kt_ref, v_ref, bias_ref, far_ref, lam_ref, g_ref, o_ref, *, lam_init):
    tq = q_ref.shape[1]
    tk = kt_ref.shape[4]
    nd = bias_ref.shape[2]
    qi = pl.program_id(2)
    lane = lax.broadcasted_iota(jnp.int32, (tq, LANES), 1)
    q = q_ref[0]
    zero = jnp.zeros_like(q)
    qm = (jnp.where(lane < 64, q, zero), jnp.where(lane >= 64, q, zero))

    def step(js, carry, bias_of):
        keys = [kt_ref[0, 0, j] for j in js]
        values = [v_ref[0, pl.ds(pl.multiple_of(j * tk, tk), tk), :] for j in js]
        logits = [[_dot(qm[mi], k) + bias_of(mi, j) for j, k in zip(js, keys)] for mi in range(2)]
        return _softmax_step(logits, values, carry)

    one = (jnp.full((tq, 1), NEG, F32), jnp.zeros((tq, 1), F32), jnp.zeros((tq, LANES), F32))
    carry = (one, one)
    n_far = jnp.maximum(qi - nd + 1, 0)
    carry = _loop_pairs(0, n_far, functools.partial(step, bias_of=lambda mi, j: far_ref[0, mi, 0:1, 0:1]), carry,
                        group=4)
    carry = _loop_pairs(n_far, qi, functools.partial(step, bias_of=lambda mi, j: bias_ref[0, mi, qi - j]), carry,
                        group=4)
    carry = step((qi,), carry, lambda mi, j: bias_ref[0, mi, 0])
    (_, l0, a0), (_, l1, a1) = carry
    o = a0 / l0 - lam_ref[...] * (a1 / l1)
    o_ref[0] = (_rms(o, g_ref[...]) * (1.0 - lam_init)).astype(BF16)


def _diff(q, kt, v, bias, far, lam, sub_g, lam_init):
    b, s, _ = q.shape
    tq = DIFF_TILE
    nk, tk = kt.shape[2], kt.shape[4]
    nd = bias.shape[2]
    return pl.pallas_call(
        functools.partial(_diff_kernel, lam_init=lam_init),
        grid=(DIFF_HEADS, b, s // tq),
        in_specs=[
            pl.BlockSpec((1, tq, LANES), lambda h, bi, qi: (bi, qi, h)),
            pl.BlockSpec((1, 1, nk, LANES, tk), lambda h, bi, qi: (bi, h, 0, 0, 0)),
            pl.BlockSpec((1, s, LANES), lambda h, bi, qi: (bi, 0, h)),
            _resident((1, 2, nd, tq, tk), lambda h, bi, qi: (h, 0, 0, 0, 0)),
            pl.BlockSpec((1, 2, 8, LANES), lambda h, bi, qi: (h, 0, 0, 0)),
            pl.BlockSpec((1, LANES), lambda h, bi, qi: (0, 0)),
            pl.BlockSpec((1, LANES), lambda h, bi, qi: (0, 0)),
        ],
        out_specs=pl.BlockSpec((1, tq, LANES), lambda h, bi, qi: (bi, qi, h)),
        out_shape=jax.ShapeDtypeStruct((b, s, 512), BF16),
        compiler_params=_cparams(("parallel", "parallel", "arbitrary")),
        name="diff",
    )(q, kt, v, bias, far, lam, sub_g)


def _sb_kernel(q_ref, kt_ref, v_ref, o_ref):
    tq = q_ref.shape[1]
    tk = kt_ref.shape[4]
    qi = pl.program_id(2)
    lane = lax.broadcasted_iota(jnp.int32, (tq, LANES), 1)
    strict = (lax.broadcasted_iota(jnp.int32, (tq, tk), 1)
              < lax.broadcasted_iota(jnp.int32, (tq, tk), 0))
    later = (lax.broadcasted_iota(jnp.int32, (tk, tk), 0)
             > lax.broadcasted_iota(jnp.int32, (tk, tk), 1)).astype(BF16)
    q = q_ref[0]
    zero = jnp.zeros_like(q)
    qh = (jnp.where(lane < 64, q, zero), jnp.where(lane >= 64, q, zero))

    def blocks(js, state, masked):
        items = [(bi, hd) for bi in range(len(js)) for hd in range(2)]
        z = {it: _dot(qh[it[1]], kt_ref[0, 0, js[it[0]]]) for it in items}
        sp, log_1m, inblock = {}, {}, {}
        for it in items:
            sp[it] = jnp.maximum(z[it], 0.0) + jnp.log2(1.0 + jnp.exp2(-jnp.abs(z[it])))
            l1m = -sp[it]
            if masked:
                l1m = jnp.where(strict, l1m, 0.0)
            log_1m[it] = l1m
            hi = l1m.astype(BF16)
            lo = (l1m - hi.astype(F32)).astype(BF16)
            inblock[it] = _dot(hi, later) + _dot(lo, later)
        state = list(state)
        for bi, j in enumerate(js):
            v = v_ref[0, pl.ds(pl.multiple_of(j * tk, tk), tk), :]
            for hd in range(2):
                it = (bi, hd)
                c, acc = state[hd]
                w = jnp.exp2(z[it] - sp[it] + (inblock[it] + c))
                if masked:
                    w = jnp.where(strict, w, 0.0)
                acc = acc + _dot(w.astype(BF16), v)
                c = c + jnp.sum(log_1m[it], axis=-1, keepdims=True)
                state[hd] = (c, acc)
        return tuple(state)

    one = (jnp.zeros((tq, 1), F32), jnp.zeros((tq, LANES), F32))
    state = blocks((qi,), (one, one), True)

    odd = qi % 2
    state = lax.cond(odd == 1, lambda st: blocks((qi - 1,), st, False), lambda st: st, state)
    floor = SB_LOG_FLOOR * LOG2E

    def cond(st):
        j, ((c0, _), (c1, _)) = st
        return jnp.logical_and(j >= 1, jnp.max(jnp.maximum(c0, c1)) > floor)

    def body(st):
        j, state = st
        return j - 2, blocks((j, j - 1), state, False)

    _, ((_, a0), (_, a1)) = lax.while_loop(cond, body, (qi - 1 - odd, state))
    o_ref[0] = jnp.where(lane < 64, a0, a1).astype(BF16)


def _sb(q, kt, v):
    b, s, _ = q.shape
    tq = SB_TILE
    nk, tk = kt.shape[2], kt.shape[4]
    return pl.pallas_call(
        _sb_kernel,
        grid=(b, SB_HEADS // 2, s // tq),
        in_specs=[
            pl.BlockSpec((1, tq, LANES), lambda bi, hp, qi: (bi, qi, hp)),
            pl.BlockSpec((1, 1, nk, LANES, tk), lambda bi, hp, qi: (bi, hp, 0, 0, 0)),
            pl.BlockSpec((1, s, LANES), lambda bi, hp, qi: (bi, 0, hp)),
        ],
        out_specs=pl.BlockSpec((1, tq, LANES), lambda bi, hp, qi: (bi, qi, hp)),
        out_shape=jax.ShapeDtypeStruct((b, s, 512), BF16),
        compiler_params=_cparams(("parallel", "parallel", "arbitrary")),
        name="stick_breaking",
    )(q, kt, v)


FF_CHUNKS = ((0, 768), (768, 1536), (1536, 2304), (2304, 2816))
EXPERT_CHUNKS = ((0, 1024), (1024, 2048), (2048, 3072), (3072, 3584))


def _swiglu(hb, wg_ref, wu_ref, wd_ref, chunks, lead, between=None):
    acc = None
    for ci, (c0, c1) in enumerate(chunks):
        g = _dot(hb, wg_ref[lead + (slice(None), slice(c0, c1))])
        u = _dot(hb, wu_ref[lead + (slice(None), slice(c0, c1))])
        a = (g / (1.0 + jnp.exp(-g)) * u).astype(BF16)
        part = _dot(a, wd_ref[lead + (slice(c0, c1), slice(None))])
        acc = part if acc is None else acc + part
        if between is not None:
            between(ci)
    return acc


def _mix_out(x_ref, oa_ref, ob_ref, wa_ref, wb_ref, gate_ref, pg_ref):
    y = _dot(oa_ref[0], wa_ref[...]) + _dot(ob_ref[0], wb_ref[...])
    return x_ref[0] + gate_ref[0] * _rms(y, pg_ref[...])


def _post_even_kernel(x_ref, oa_ref, ob_ref, wa_ref, wb_ref, gate_ref, pg_ref,
                      fg_ref, fsh_ref, fsc_ref, fgate_ref, fpg_ref, wg_ref, wu_ref, wd_ref, o_ref):
    x1 = _mix_out(x_ref, oa_ref, ob_ref, wa_ref, wb_ref, gate_ref, pg_ref)
    hb = _prenorm_mod(x1, fg_ref[...], fsh_ref[0], fsc_ref[0]).astype(BF16)
    y = _swiglu(hb, wg_ref, wu_ref, wd_ref, FF_CHUNKS, ())
    o_ref[0] = x1 + fgate_ref[0] * _rms(y, fpg_ref[...])


def _post_even(x, oa, ob, wa, wb, gate, pg, fg, fsh, fsc, fgate, fpg, wg, wu, wd):
    b, s, d = x.shape
    tm = TOK_TILE
    tok = lambda w: pl.BlockSpec((1, tm, w), lambda bi, i: (bi, i, 0))
    vec = pl.BlockSpec((1, d), lambda bi, i: (0, 0))
    mod = pl.BlockSpec((1, 1, d), lambda bi, i: (bi, 0, 0))
    full = lambda a: _resident(a.shape, lambda bi, i: (0,) * a.ndim)
    return pl.pallas_call(
        _post_even_kernel,
        grid=(b, s // tm),
        in_specs=[tok(d), tok(512), tok(512), full(wa), full(wb), mod, vec,
                  vec, mod, mod, mod, vec, full(wg), full(wu), full(wd)],
        out_specs=tok(d),
        out_shape=jax.ShapeDtypeStruct((b, s, d), F32),
        compiler_params=_cparams(("parallel", "parallel")),
        name="post_even",
    )(x, oa, ob, wa, wb, gate, pg, fg, fsh, fsc, fgate, fpg, wg, wu, wd)


def _post_odd_kernel(x_ref, oa_ref, ob_ref, wa_ref, wb_ref, gate_ref, pg_ref,
                     fg_ref, fsh_ref, fsc_ref, rw_ref, rb_ref, x_out, h_out, r_out, cnt_ref):
    x1 = _mix_out(x_ref, oa_ref, ob_ref, wa_ref, wb_ref, gate_ref, pg_ref)
    x_out[0] = x1
    h = _prenorm_mod(x1, fg_ref[...], fsh_ref[0], fsc_ref[0])
    h_out[0] = h
    logits = _dot(h, rw_ref[...]) + rb_ref[...]
    lane = lax.broadcasted_iota(jnp.int32, logits.shape, 1)
    m1 = jnp.max(logits, axis=-1, keepdims=True)
    i1 = jnp.min(jnp.where(logits == m1, lane, LANES), axis=-1, keepdims=True)
    rest = jnp.where(lane == i1, NEG, logits)
    m2 = jnp.max(rest, axis=-1, keepdims=True)
    i2 = jnp.min(jnp.where(rest == m2, lane, LANES), axis=-1, keepdims=True)
    e2 = jnp.exp(m2 - m1)
    w1 = 1.0 / (1.0 + e2)
    w2 = e2 / (1.0 + e2)
    @pl.when(jnp.logical_and(pl.program_id(0) == 0, pl.program_id(1) == 0))
    def _():
        cnt_ref[...] = jnp.zeros_like(cnt_ref)

    tm = logits.shape[0]
    sel = jnp.logical_or(lane == i1, lane == i2)
    earlier = (lax.broadcasted_iota(jnp.int32, (tm, tm), 1)
               < lax.broadcasted_iota(jnp.int32, (tm, tm), 0)).astype(BF16)
    prefix = _dot(earlier, sel.astype(BF16)) + cnt_ref[0:1, :]
    rank1 = jnp.sum(jnp.where(lane == i1, prefix, 0.0), axis=-1, keepdims=True)
    rank2 = jnp.sum(jnp.where(lane == i2, prefix, 0.0), axis=-1, keepdims=True)
    cnt_ref[...] = cnt_ref[...] + jnp.sum(sel.astype(F32), axis=0, keepdims=True)
    r = jnp.where(lane == 0, i1.astype(F32), 0.0)
    r = jnp.where(lane == 1, i2.astype(F32), r)
    r = jnp.where(lane == 2, w1, r)
    r = jnp.where(lane == 3, w2, r)
    r = jnp.where(lane == 4, rank1, r)
    r = jnp.where(lane == 5, rank2, r)
    r_out[0] = r


def _post_odd(x, oa, ob, wa, wb, gate, pg, fg, fsh, fsc, rw, rb):
    b, s, d = x.shape
    tm = TOK_TILE
    tok = lambda w: pl.BlockSpec((1, tm, w), lambda bi, i: (bi, i, 0))
    vec = pl.BlockSpec((1, d), lambda bi, i: (0, 0))
    mod = pl.BlockSpec((1, 1, d), lambda bi, i: (bi, 0, 0))
    full = lambda a: _resident(a.shape, lambda bi, i: (0,) * a.ndim)
    return pl.pallas_call(
        _post_odd_kernel,
        grid=(b, s // tm),
        in_specs=[tok(d), tok(512), tok(512), full(wa), full(wb), mod, vec,
                  vec, mod, mod, full(rw), pl.BlockSpec((1, LANES), lambda bi, i: (0, 0))],
        out_specs=(tok(d), tok(d), tok(LANES),
                   pl.BlockSpec((8, LANES), lambda bi, i: (0, 0))),
        out_shape=(jax.ShapeDtypeStruct((b, s, d), F32), jax.ShapeDtypeStruct((b, s, d), F32),
                   jax.ShapeDtypeStruct((b, s, LANES), F32), jax.ShapeDtypeStruct((8, LANES), F32)),
        compiler_params=_cparams(("arbitrary", "arbitrary")),
        name="post_odd",
    )(x, oa, ob, wa, wb, gate, pg, fg, fsh, fsc, rw, rb)


def _store_rows(ref, lead, val):
    for c in range(ROW_SUB):
        ref[lead + (slice(None), c, slice(None))] = val[:, c * LANES:(c + 1) * LANES]


def _load_rows(ref, lead, lo, hi):
    return jnp.concatenate([ref[lead, lo:hi, c, :] for c in range(ROW_SUB)], axis=1)


def _gather_ahead(i, n_steps, idx_hbm, src_hbm, idx_smem, buf, isem, sem):
    n = buf.shape[1]
    slot = i % 2
    nxt = 1 - slot

    def idx_copy(step, sl):
        return pltpu.make_async_copy(idx_hbm.at[step], idx_smem.at[sl], isem.at[sl])

    def issue_rows(sl):
        def issue(r, carry):
            t = idx_smem[sl, r]
            pltpu.make_async_copy(src_hbm.at[t], buf.at[sl, r], sem.at[sl]).start()
            return carry

        lax.fori_loop(0, n, issue, 0, unroll=8)

    @pl.when(i == 0)
    def _():
        first = idx_copy(0, 0)
        first.start()
        first.wait()
        issue_rows(0)

        @pl.when(n_steps > 1)
        def _():
            idx_copy(1, 1).start()

    @pl.when(i + 1 < n_steps)
    def _():
        idx_copy(i + 1, nxt).wait()
        issue_rows(nxt)

    @pl.when(i + 2 < n_steps)
    def _():
        idx_copy(i + 2, slot).start()

    pltpu.make_async_copy(src_hbm.at[pl.ds(0, n)], buf.at[slot], sem.at[slot]).wait()
    return slot


def _moe_kernel(te_ref, nu_ref, tok_hbm, h_hbm, wg_ref, wu_ref, wd_ref, y_ref, idx_smem, buf, isem, sem):
    i = pl.program_id(0)
    nt = pl.num_programs(0)
    n = buf.shape[1]
    slot = i % 2
    nxt = 1 - slot
    tile = lambda t: jnp.minimum(t, nt - 1)

    def idx_copy(t, sl):
        return pltpu.make_async_copy(tok_hbm.at[tile(t)], idx_smem.at[sl], isem.at[sl])

    def rows_wait(sl):
        pltpu.make_async_copy(h_hbm.at[pl.ds(0, n), :], buf.at[sl], sem.at[sl]).wait()

    def issue_row(sl, r):
        t = idx_smem[sl, r]
        pltpu.make_async_copy(h_hbm.at[pl.ds(t, 1), :], buf.at[sl, pl.ds(r, 1), :], sem.at[sl]).start()

    def issue_loop(sl):
        def body(r, carry):
            issue_row(sl, r)
            return carry

        lax.fori_loop(0, n, body, 0, unroll=8)

    @pl.when(i == 0)
    def _():
        first = idx_copy(0, 0)
        first.start()
        first.wait()
        issue_loop(0)
        idx_copy(1, 1).start()

    idx_copy(i + 1, nxt).wait()
    idx_copy(i + 2, slot).start()
    rows_wait(slot)

    @pl.when(i < nu_ref[0])
    def _():
        xs = buf[slot].astype(BF16)
        per_chunk = n // len(EXPERT_CHUNKS)

        def issue_part(ci):
            for r in range(ci * per_chunk, (ci + 1) * per_chunk):
                issue_row(nxt, r)

        y = _swiglu(xs, wg_ref, wu_ref, wd_ref, EXPERT_CHUNKS, (0,), between=issue_part)
        _store_rows(y_ref, (), y)

    @pl.when(i >= nu_ref[0])
    def _():
        issue_loop(nxt)
        y_ref[...] = jnp.zeros_like(y_ref)

    @pl.when(i == nt - 1)
    def _():
        rows_wait(nxt)
        idx_copy(i + 2, slot).wait()


def _moe(tile_expert, n_used, row_token, h, wg, wu, wd):
    d = h.shape[1]
    nt, tm = row_token.shape
    dff = wg.shape[2]
    wspec = lambda shp: pl.BlockSpec(shp, lambda i, te, nu: (te[i], 0, 0), pipeline_mode=pl.Buffered(1))
    grid_spec = pltpu.PrefetchScalarGridSpec(
        num_scalar_prefetch=2,
        grid=(nt,),
        in_specs=[
            pl.BlockSpec(memory_space=pl.ANY),
            pl.BlockSpec(memory_space=pl.ANY),
            wspec((1, d, dff)), wspec((1, d, dff)), wspec((1, dff, d)),
        ],
        out_specs=pl.BlockSpec((tm, ROW_SUB, LANES), lambda i, te, nu: (i, 0, 0)),
        scratch_shapes=[
            pltpu.SMEM((2, tm), jnp.int32),
            pltpu.VMEM((2, tm, d), F32),
            pltpu.SemaphoreType.DMA((2,)),
            pltpu.SemaphoreType.DMA((2,)),
        ],
    )
    return pl.pallas_call(
        _moe_kernel,
        grid_spec=grid_spec,
        out_shape=jax.ShapeDtypeStruct((nt * tm, ROW_SUB, LANES), F32),
        compiler_params=_cparams(("arbitrary",)),
        name="moe_experts",
    )(tile_expert, n_used, row_token, h, wg, wu, wd)


def _combine_kernel(pos_hbm, y_hbm, x_ref, r_ref, gate_ref, pg_ref, o_ref, idx_smem, buf, isem, sem):
    tm = x_ref.shape[0]
    slot = _gather_ahead(pl.program_id(0), pl.num_programs(0), pos_hbm, y_hbm, idx_smem, buf, isem, sem)
    r = r_ref[...]
    y = r[:, 2:3] * _load_rows(buf, slot, 0, tm) + r[:, 3:4] * _load_rows(buf, slot, tm, 2 * tm)
    o_ref[...] = x_ref[...] + gate_ref[0] * _rms(y, pg_ref[...])


def _combine(pos, y, x, r, gate, pg, tokens_per_seq):
    n_tok, d = x.shape
    nt, tm2 = pos.shape
    tm = tm2 // 2
    per_seq = tokens_per_seq // tm
    tok = lambda w: pl.BlockSpec((tm, w), lambda i: (i, 0))
    return pl.pallas_call(
        _combine_kernel,
        grid=(nt,),
        in_specs=[
            pl.BlockSpec(memory_space=pl.ANY),
            pl.BlockSpec(memory_space=pl.ANY),
            tok(d), tok(LANES),
            pl.BlockSpec((1, 1, d), lambda i: (i // per_seq, 0, 0)),
            pl.BlockSpec((1, d), lambda i: (0, 0)),
        ],
        out_specs=tok(d),
        out_shape=jax.ShapeDtypeStruct((n_tok, d), F32),
        scratch_shapes=[
            pltpu.SMEM((2, tm2), jnp.int32),
            pltpu.VMEM((2, tm2, ROW_SUB, LANES), F32),
            pltpu.SemaphoreType.DMA((2,)),
            pltpu.SemaphoreType.DMA((2,)),
        ],
        compiler_params=_cparams(("arbitrary",)),
        name="moe_combine",
    )(pos, y, x, r, gate, pg)


def _t5_bucket(dist):
    max_exact = REL_BUCKETS // 2
    d = jnp.maximum(dist, 1).astype(F32)
    log_b = max_exact + (jnp.log(d / max_exact) / math.log(REL_MAX_DIST / max_exact)
                         * (REL_BUCKETS - max_exact)).astype(jnp.int32)
    log_b = jnp.minimum(log_b, REL_BUCKETS - 1)
    return jnp.where(dist < max_exact, dist, log_b)


def _rope_tables(s):
    half = MLA_ROPE // 2
    freqs = ROPE_THETA ** (-jnp.arange(half, dtype=F32) / half)
    ang = jnp.arange(s, dtype=F32)[:, None] * freqs[None, :]
    cos, sin = jnp.cos(ang), jnp.sin(ang)
    z64 = jnp.zeros((s, MLA_NOPE), F32)
    z32 = jnp.zeros((s, LANES - MLA_NOPE - MLA_ROPE), F32)
    ck = jnp.concatenate([z64, cos, cos, z32], axis=1)
    cq = jnp.concatenate([jnp.ones((s, MLA_NOPE), F32), cos, cos, z32], axis=1)
    sn = jnp.concatenate([z64, sin, sin, z32], axis=1)
    return cq, ck, sn


def _even_weights(w_in, w_uq, w_ukv):
    d = w_in.shape[0]
    half = MLA_ROPE // 2
    w_cq = w_in[:, :MLA_Q_RANK]
    w_ckv = w_in[:, MLA_Q_RANK:MLA_Q_RANK + MLA_KV_RANK]
    w_kr = w_in[:, MLA_Q_RANK + MLA_KV_RANK:MLA_Q_RANK + MLA_KV_RANK + MLA_ROPE]
    w_qkv = w_in[:, MLA_Q_RANK + MLA_KV_RANK + MLA_ROPE:]
    z = lambda n: jnp.zeros((d, n), F32)
    kr_a = jnp.concatenate([z(MLA_NOPE), w_kr, z(32)], axis=1)
    kr_b = jnp.concatenate([z(MLA_NOPE), -w_kr[:, half:], w_kr[:, :half], z(32)], axis=1)
    dil_scale = DIL_HD ** -0.5 * LOG2E
    w0 = jnp.concatenate([w_cq, w_ckv, kr_a, kr_b, w_qkv[:, :512] * dil_scale, w_qkv[:, 512:]], axis=1)

    r = w_uq.shape[0]
    wq = w_uq.reshape(r, MLA_HEADS, MLA_NOPE + MLA_ROPE) * ((MLA_NOPE + MLA_ROPE) ** -0.5 * LOG2E)
    zq = lambda n: jnp.zeros((r, MLA_HEADS, n), F32)
    nope, x1, x2 = wq[..., :MLA_NOPE], wq[..., MLA_NOPE:MLA_NOPE + half], wq[..., MLA_NOPE + half:]
    q_a = jnp.concatenate([nope, x1, x2, zq(32)], axis=-1).reshape(r, MLA_HEADS * LANES)
    q_b = jnp.concatenate([zq(MLA_NOPE), -x2, x1, zq(32)], axis=-1).reshape(r, MLA_HEADS * LANES)
    wq2 = jnp.concatenate([q_a, q_b], axis=1)

    rk = w_ukv.shape[0]
    wkv = w_ukv.reshape(rk, MLA_HEADS, MLA_NOPE + MLA_V)
    k_blk = jnp.concatenate([wkv[..., :MLA_NOPE], jnp.zeros((rk, MLA_HEADS, LANES - MLA_NOPE), F32)], axis=-1)
    wkv2 = jnp.concatenate([k_blk.reshape(rk, MLA_HEADS * LANES),
                            wkv[..., MLA_NOPE:].reshape(rk, MLA_HEADS * MLA_V)], axis=1)
    return w0.astype(BF16), wq2.astype(BF16), wkv2.astype(BF16)


def _toeplitz(vec, rows, cols):
    n, width = vec.shape

    def toeplitz_kernel(v_ref, o_ref):
        tiled = jnp.broadcast_to(v_ref[0], (rows, width))
        o_ref[0] = pltpu.roll(tiled, 0, 1, stride=1, stride_axis=0)[:, :cols]

    return pl.pallas_call(
        toeplitz_kernel,
        grid=(n,),
        in_specs=[pl.BlockSpec((1, 1, width), lambda t: (t, 0, 0))],
        out_specs=pl.BlockSpec((1, rows, cols), lambda t: (t, 0, 0)),
        out_shape=jax.ShapeDtypeStruct((n, rows, cols), F32),
        compiler_params=_cparams(("parallel",)),
        name="toeplitz_bias",
    )(vec.reshape(n, 1, width).astype(F32))


def _dil_bias(rel_bias):
    blk = DIL_BLOCK
    width = 4 * blk
    k = jnp.arange(width)
    rel = jnp.where(k < 2 * blk, blk - k, blk + width - k)
    out = []
    for window, dil in DIL_PATTERNS:
        band = (rel >= 0) & (rel <= window // dil)
        bias = rel_bias[_t5_bucket(jnp.maximum(rel, 0) * dil)] * LOG2E
        out.append(jnp.where(band[:, None], bias, NEG).T)
    vec = jnp.stack(out).reshape(3 * DIL_HEADS, width)
    return _toeplitz(vec, blk, 2 * blk).reshape(3, DIL_HEADS, blk, 2 * blk)


def _diff_bias(rel_bias, tile):
    nd = REL_MAX_DIST // tile + 1
    maps = rel_bias.shape[1]
    k = jnp.arange(2 * tile)[None, :]
    dist = jnp.arange(nd)[:, None] * tile + jnp.where(k < tile, -k, 2 * tile - k)
    vec = jnp.where((dist >= 0)[..., None], rel_bias[_t5_bucket(jnp.maximum(dist, 0))] * LOG2E, NEG)
    vec = jnp.transpose(vec, (2, 0, 1)).reshape(maps * nd, 2 * tile)
    bias = _toeplitz(vec, tile, tile).reshape(DIFF_HEADS, 2, nd, tile, tile)
    far = rel_bias[_t5_bucket(jnp.array(REL_MAX_DIST))] * LOG2E
    far = jnp.broadcast_to(far.reshape(DIFF_HEADS, 2, 1, 1), (DIFF_HEADS, 2, 8, LANES))
    return bias, far.astype(F32)


def _routing(r, counts, n_tok, tile):
    n_tiles = (2 * n_tok) // tile + N_EXPERTS
    e = jnp.concatenate([r[:, 0], r[:, 1]]).astype(jnp.int32)
    rank = jnp.concatenate([r[:, 4], r[:, 5]]).astype(jnp.int32)
    counts = counts.astype(jnp.int32)
    padded = ((counts + tile - 1) // tile) * tile
    ends = jnp.cumsum(padded)
    starts = ends - padded
    onehot = (e[:, None] == jnp.arange(N_EXPERTS)[None, :]).astype(jnp.int32)
    pos = jnp.sum(onehot * starts[None, :], axis=1) + rank
    token = jnp.tile(jnp.arange(n_tok, dtype=jnp.int32), 2)
    row_token = jnp.zeros((n_tiles * tile,), jnp.int32).at[pos].set(token)
    tile_start = jnp.arange(n_tiles, dtype=jnp.int32) * tile
    tile_expert = jnp.sum((tile_start[:, None] >= ends[None, :]).astype(jnp.int32), axis=1)
    n_used = (ends[-1] // tile).astype(jnp.int32)
    last = jnp.sum((ends[-1] - 1 >= ends).astype(jnp.int32))
    tile_expert = jnp.minimum(tile_expert, last).astype(jnp.int32)
    return (tile_expert, n_used.reshape(1), row_token.reshape(n_tiles, tile),
            pos[:n_tok].astype(jnp.int32), pos[n_tok:].astype(jnp.int32))


def kernel(x, c, rel_bias, ada_mix_w, ada_mix_b, mix_pre_g, mix_post_g, ada_ffn_w, ada_ffn_b, ffn_pre_g, ffn_post_g, e_w_in, e_q_norm_g, e_w_uq, e_kv_norm_g, e_w_ukv, e_w_out, ffn_w_gate, ffn_w_up, ffn_w_down, o_w_in, diff_lq1, diff_lk1, diff_lq2, diff_lk2, diff_sub_g, o_w_out, router_w, router_b, moe_w_gate, moe_w_up, moe_w_down):
    b, s, d = x.shape
    assert d == D_MODEL and s % DIL_SUPER == 0 and s % TOK_TILE == 0
    row = lambda v: v.reshape(1, -1).astype(F32)

    mix_mod = _ada(c, ada_mix_w, ada_mix_b)
    ffn_mod = _ada(c, ada_ffn_w, ada_ffn_b)

    shift, scale, gate = _split_mod(mix_mod[0])
    w0, wq2, wkv2 = _even_weights(e_w_in[0], e_w_uq[0], e_w_ukv[0])
    cq, ck, sn = _rope_tables(s)
    qa, kta, va, qb, kb, vb = _even_in(x, row(mix_pre_g[0]), shift, scale, w0, row(e_q_norm_g[0]), wq2,
                                       row(e_kv_norm_g[0]), wkv2, cq, ck, sn)
    o_a = _mla(qa, kta, va)
    o_b = _dil(qb, kb, vb, _dil_bias(rel_bias))
    fshift, fscale, fgate = _split_mod(ffn_mod[0])
    w_out = e_w_out[0].astype(BF16)
    x = _post_even(x, o_a, o_b, w_out[:512], w_out[512:], gate, row(mix_post_g[0]),
                   row(ffn_pre_g[0]), fshift, fscale, fgate, row(ffn_post_g[0]),
                   ffn_w_gate[0].astype(BF16), ffn_w_up[0].astype(BF16), ffn_w_down[0].astype(BF16))

    layer = 1
    shift, scale, gate = _split_mod(mix_mod[1])
    w_in = o_w_in[0]
    att_scale = DIFF_HD ** -0.5
    w1 = jnp.concatenate([w_in[:, :512] * (att_scale * LOG2E), w_in[:, 512:1536],
                          w_in[:, 1536:2048] * (SB_HD ** -0.5 * LOG2E), w_in[:, 2048:]], axis=1).astype(BF16)
    qd, kdt, vd, qs, kst, vs = _odd_in(x, row(mix_pre_g[1]), shift, scale, w1)
    lam_init = 0.8 - 0.6 * math.exp(-0.3 * layer)
    lam = (jnp.exp(jnp.sum(diff_lq1[0].astype(F32) * diff_lk1[0].astype(F32)))
           - jnp.exp(jnp.sum(diff_lq2[0].astype(F32) * diff_lk2[0].astype(F32))) + lam_init)
    bias, far = _diff_bias(rel_bias, DIFF_TILE)
    o_c = _diff(qd, kdt, vd, bias, far, jnp.full((1, LANES), lam, F32), row(diff_sub_g[0]), lam_init)
    o_d = _sb(qs, kst, vs)
    fshift, fscale, fgate = _split_mod(ffn_mod[1])
    w_out = o_w_out[0].astype(BF16)
    rw = jnp.zeros((d, LANES), F32).at[:, :N_EXPERTS].set(router_w[0].astype(F32))
    rb = jnp.full((1, LANES), NEG, F32).at[0, :N_EXPERTS].set(router_b[0].astype(F32))
    x, h, r, counts = _post_odd(x, o_c, o_d, w_out[:512], w_out[512:], gate, row(mix_post_g[1]),
                                row(ffn_pre_g[1]), fshift, fscale, rw, rb)

    n_tok = b * s
    tile_expert, n_used, row_token, pos0, pos1 = _routing(r.reshape(n_tok, LANES), counts[0, :N_EXPERTS],
                                                          n_tok, MOE_TILE)
    y = _moe(tile_expert, n_used, row_token, h.reshape(n_tok, d),
             moe_w_gate[0].astype(BF16), moe_w_up[0].astype(BF16), moe_w_down[0].astype(BF16))
    ct = TOK_TILE
    pos = jnp.concatenate([pos0.reshape(n_tok // ct, ct), pos1.reshape(n_tok // ct, ct)], axis=1)
    out = _combine(pos, y, x.reshape(n_tok, d), r.reshape(n_tok, LANES), fgate, row(ffn_post_g[1]), s)
    return out.reshape(b, s, d)
```

```python
import functools
import math

import jax
import jax.numpy as jnp
from jax import lax
from jax.experimental import pallas as pl
from jax.experimental.pallas import tpu as pltpu

F32 = jnp.float32
BF16 = jnp.bfloat16

D_MODEL = 1024
EPS = 1e-6

MLA_HEADS = 8
MLA_NOPE = 64
MLA_ROPE = 32
MLA_V = 64
MLA_Q_RANK = 256
MLA_KV_RANK = 128
ROPE_THETA = 10000.0

DIL_HEADS = 8
DIL_HD = 64
DIL_PATTERNS = ((128, 1), (512, 4), (2048, 16))
DIL_BLOCK = 128

DIFF_HEADS = 4
DIFF_HD = 64
SB_HEADS = 8
SB_HD = 64

REL_BUCKETS = 32
REL_MAX_DIST = 2048

D_FF = 2816
N_EXPERTS = 8
D_FF_EXPERT = 3584

LANES = 128
ROW_SUB = D_MODEL // LANES
LOG2E = math.log2(math.e)
NEG = -1e30

TOK_TILE = 512
MLA_TILE = 512
DIFF_TILE = 512
SB_TILE = 256
DIL_SUPER = DIL_BLOCK * 16
MOE_TILE = 512
SB_LOG_FLOOR = -104.0

VMEM_LIMIT = 56 * 1024 * 1024


def _cparams(sem):
    return pltpu.CompilerParams(dimension_semantics=sem, vmem_limit_bytes=VMEM_LIMIT)


def _resident(shape, index_map):
    return pl.BlockSpec(shape, index_map, pipeline_mode=pl.Buffered(1))


def _rms(x, g):
    return x * lax.rsqrt(jnp.mean(x * x, axis=-1, keepdims=True) + EPS) * g


def _dot(a, b):
    return jnp.dot(a, b, preferred_element_type=F32)


def _softmax_step_t(logits, values, carry):
    out = []
    for s_list, v_list, (m, l, acc) in zip(logits, values, carry):
        for s, vt in zip(s_list, v_list):
            m_new = jnp.maximum(m, jnp.max(s, axis=0, keepdims=True))
            alpha = jnp.exp2(m - m_new)
            p = jnp.exp2(s - m_new)
            l = alpha * l + jnp.sum(p, axis=0, keepdims=True)
            acc = alpha * acc + _dot(vt, p.astype(BF16))
            m = m_new
        out.append((m, l, acc))
    return tuple(out)


def _loop_pairs(lo, hi, step, carry, group=2):
    n = hi - lo
    carry = lax.fori_loop(
        0, n // group, lambda i, c: step(tuple(lo + group * i + g for g in range(group)), c), carry)
    done = lo + (n // group) * group
    if group == 4:
        carry = lax.cond(hi - done >= 2, lambda c: step((done, done + 1), c), lambda c: c, carry)
    return lax.cond(n % 2 == 1, lambda c: step((hi - 1,), c), lambda c: c, carry)


def _softmax_step(logits, values, carry):
    out = []
    for s_list, (m, l, acc) in zip(logits, carry):
        m_new = m
        for s in s_list:
            m_new = jnp.maximum(m_new, jnp.max(s, axis=-1, keepdims=True))
        alpha = jnp.exp2(m - m_new)
        l = alpha * l
        acc = alpha * acc
        for s, v in zip(s_list, values):
            p = jnp.exp2(s - m_new)
            l = l + jnp.sum(p, axis=-1, keepdims=True)
            acc = acc + _dot(p.astype(BF16), v)
        out.append((m_new, l, acc))
    return tuple(out)


def _ada_kernel(c_ref, w_ref, b_ref, o_ref):
    c = c_ref[...]
    sc = c / (1.0 + jnp.exp(-c))
    o_ref[0] = _dot(sc.astype(BF16), w_ref[0].astype(BF16)) + b_ref[0]


def _ada(c, w, b):
    nl, d, d3 = w.shape
    bsz = c.shape[0]
    nb = d3 // d
    return pl.pallas_call(
        _ada_kernel,
        grid=(nl, nb),
        in_specs=[
            pl.BlockSpec((bsz, d), lambda l, j: (0, 0)),
            pl.BlockSpec((1, d, d), lambda l, j: (l, 0, j)),
            pl.BlockSpec((1, 1, d), lambda l, j: (l, 0, j)),
        ],
        out_specs=pl.BlockSpec((1, bsz, d), lambda l, j: (l, 0, j)),
        out_shape=jax.ShapeDtypeStruct((nl, bsz, d3), F32),
        compiler_params=_cparams(("arbitrary", "arbitrary")),
        name="ada",
    )(c, w, b.reshape(nl, 1, d3))


def _split_mod(m):
    b = m.shape[0]
    m = m.reshape(b, 3, 1, D_MODEL)
    return m[:, 0], m[:, 1], m[:, 2]


def _prenorm_mod(x, g, shift, scale):
    return _rms(x, g) * (1.0 + scale) + shift


def _even_in_kernel(x_ref, g_ref, sh_ref, sc_ref, w0_ref, qg_ref, wq_ref, kvg_ref, wkv_ref,
                    cq_ref, ck_ref, sn_ref,
                    qa_ref, ka_ref, va_ref, qb_ref, kb_ref, vb_ref):
    h = _prenorm_mod(x_ref[0], g_ref[...], sh_ref[0], sc_ref[0]).astype(BF16)
    proj = _dot(h, w0_ref[...])
    cqn = _rms(proj[:, 0:256], qg_ref[...]).astype(BF16)
    qq = _dot(cqn, wq_ref[...])
    ckvn = _rms(proj[:, 256:384], kvg_ref[...]).astype(BF16)
    kv = _dot(ckvn, wkv_ref[...])
    cq = cq_ref[...]
    ck = ck_ref[...]
    sn = sn_ref[...]
    krope = proj[:, 384:512] * ck + proj[:, 512:640] * sn
    nh = MLA_HEADS
    for hd in range(nh):
        lo = hd * LANES
        qh = qq[:, lo:lo + LANES] * cq + qq[:, nh * LANES + lo:nh * LANES + lo + LANES] * sn
        _store_key_tiles(qa_ref, hd, qh)
        ka_ref[0, :, lo:lo + LANES] = (kv[:, lo:lo + LANES] + krope).astype(BF16)
    for pr in range(nh // 2):
        _store_key_tiles(va_ref, pr, kv[:, nh * LANES + pr * LANES:nh * LANES + (pr + 1) * LANES])
    qb_ref[0] = proj[:, 640:1152].astype(BF16)
    kb_ref[0] = proj[:, 1152:1664].astype(BF16)
    vb_ref[0] = proj[:, 1664:2176].astype(BF16)


def _even_in(x, g, shift, scale, w0, qg, wq, kvg, wkv, cq, ck, sn):
    b, s, d = x.shape
    tm = TOK_TILE
    tkb = MLA_TILE
    ns = s // tm
    tok = lambda w: pl.BlockSpec((1, tm, w), lambda bi, i: (bi, i, 0))
    vec = lambda w: pl.BlockSpec((1, w), lambda bi, i: (0, 0))
    mod = pl.BlockSpec((1, 1, d), lambda bi, i: (bi, 0, 0))
    tab = pl.BlockSpec((tm, LANES), lambda bi, i: (i, 0))
    full = lambda a: _resident(a.shape, lambda bi, i: (0,) * a.ndim)
    out_shapes = (
        jax.ShapeDtypeStruct((b, MLA_HEADS, s // tkb, LANES, tkb), BF16),
        jax.ShapeDtypeStruct((b, s, MLA_HEADS * LANES), BF16),
        jax.ShapeDtypeStruct((b, MLA_HEADS // 2, s // tkb, LANES, tkb), BF16),
        jax.ShapeDtypeStruct((b, s, 512), BF16),
        jax.ShapeDtypeStruct((b, s, 512), BF16),
        jax.ShapeDtypeStruct((b, s, 512), BF16),
    )
    out_specs = (
        pl.BlockSpec((1, MLA_HEADS, tm // tkb, LANES, tkb), lambda bi, i: (bi, 0, i, 0, 0)),
        tok(MLA_HEADS * LANES),
        pl.BlockSpec((1, MLA_HEADS // 2, tm // tkb, LANES, tkb), lambda bi, i: (bi, 0, i, 0, 0)),
        tok(512), tok(512), tok(512),
    )
    return pl.pallas_call(
        _even_in_kernel,
        grid=(b, ns),
        in_specs=[tok(d), vec(d), mod, mod, full(w0), vec(MLA_Q_RANK), full(wq), vec(MLA_KV_RANK), full(wkv),
                  tab, tab, tab],
        out_specs=out_specs,
        out_shape=out_shapes,
        compiler_params=_cparams(("parallel", "parallel")),
        name="even_in",
    )(x, g, shift, scale, w0, qg, wq, kvg, wkv, cq, ck, sn)


def _store_key_tiles(kt_ref, hd, k):
    tkb = kt_ref.shape[4]
    for t in range(k.shape[0] // tkb):
        kt_ref[0, hd, t] = k[t * tkb:(t + 1) * tkb, :].T.astype(BF16)


def _odd_in_kernel(x_ref, g_ref, sh_ref, sc_ref, w_ref,
                   qdt_ref, kd_ref, vdt_ref, qs_ref, kst_ref, vs_ref):
    h = _prenorm_mod(x_ref[0], g_ref[...], sh_ref[0], sc_ref[0]).astype(BF16)
    proj = _dot(h, w_ref[...])
    kd_ref[0] = proj[:, 512:1024].astype(BF16)
    qs_ref[0] = proj[:, 1536:2048].astype(BF16)
    vs_ref[0] = proj[:, 2560:3072].astype(BF16)
    for hd in range(4):
        _store_key_tiles(qdt_ref, hd, proj[:, hd * LANES:(hd + 1) * LANES])
        _store_key_tiles(vdt_ref, hd, proj[:, 1024 + hd * LANES:1024 + (hd + 1) * LANES])
        _store_key_tiles(kst_ref, hd, proj[:, 2048 + hd * LANES:2048 + (hd + 1) * LANES])


def _odd_in(x, g, shift, scale, w):
    b, s, d = x.shape
    tm = TOK_TILE
    ns = s // tm
    tok = lambda wd: pl.BlockSpec((1, tm, wd), lambda bi, i: (bi, i, 0))
    mod = pl.BlockSpec((1, 1, d), lambda bi, i: (bi, 0, 0))
    ktspec = lambda tkb: pl.BlockSpec((1, 4, tm // tkb, LANES, tkb), lambda bi, i: (bi, 0, i, 0, 0))
    act = jax.ShapeDtypeStruct((b, s, 512), BF16)
    kts = lambda tkb: jax.ShapeDtypeStruct((b, 4, s // tkb, LANES, tkb), BF16)
    return pl.pallas_call(
        _odd_in_kernel,
        grid=(b, ns),
        in_specs=[tok(d), pl.BlockSpec((1, d), lambda bi, i: (0, 0)), mod, mod,
                  _resident(w.shape, lambda bi, i: (0, 0))],
        out_specs=(ktspec(DIFF_TILE), tok(512), ktspec(DIFF_TILE), tok(512), ktspec(SB_TILE), tok(512)),
        out_shape=(kts(DIFF_TILE), act, kts(DIFF_TILE), act, kts(SB_TILE), act),
        compiler_params=_cparams(("parallel", "parallel")),
        name="odd_in",
    )(x, g, shift, scale, w)


def _mla_kernel(qt_ref, k_ref, vt_ref, o_ref):
    tq = qt_ref.shape[4]
    tk = vt_ref.shape[4]
    hv = MLA_V
    qi = pl.program_id(2)
    causal = (lax.broadcasted_iota(jnp.int32, (tk, tq), 0)
              <= lax.broadcasted_iota(jnp.int32, (tk, tq), 1))
    qts = (qt_ref[0, 0, 0], qt_ref[0, 1, 0])

    def step(js, carry, masked):
        logits, values = [], []
        for hd in range(2):
            s_list = [_dot(k_ref[0, pl.ds(pl.multiple_of(j * tk, tk), tk), hd * LANES:(hd + 1) * LANES], qts[hd])
                      for j in js]
            if masked:
                s_list = [jnp.where(causal, s, NEG) for s in s_list]
            logits.append(s_list)
            values.append([vt_ref[0, 0, j, hd * hv:(hd + 1) * hv, :] for j in js])
        return _softmax_step_t(logits, values, carry)

    one = (jnp.full((1, tq), NEG, F32), jnp.zeros((1, tq), F32), jnp.zeros((hv, tq), F32))
    carry = _loop_pairs(0, qi, functools.partial(step, masked=False), (one, one), group=4)
    (_, l0, a0), (_, l1, a1) = step((qi,), carry, True)
    o_ref[0] = jnp.concatenate([a0 / l0, a1 / l1], axis=0).T.astype(BF16)


def _mla(qt, k, vt):
    b, s, _ = k.shape
    tq = qt.shape[4]
    nk = vt.shape[2]
    tk = vt.shape[4]
    return pl.pallas_call(
        _mla_kernel,
        grid=(b, MLA_HEADS // 2, s // tq),
        in_specs=[
            pl.BlockSpec((1, 2, 1, LANES, tq), lambda bi, hp, qi: (bi, hp, qi, 0, 0)),
            pl.BlockSpec((1, s, 2 * LANES), lambda bi, hp, qi: (bi, 0, hp)),
            pl.BlockSpec((1, 1, nk, LANES, tk), lambda bi, hp, qi: (bi, hp, 0, 0, 0)),
        ],
        out_specs=pl.BlockSpec((1, tq, LANES), lambda bi, hp, qi: (bi, qi, hp)),
        out_shape=jax.ShapeDtypeStruct((b, s, 512), BF16),
        compiler_params=_cparams(("parallel", "parallel", "arbitrary")),
        name="mla",
    )(qt, k, vt)


def _dil_kernel(q_ref, kc_ref, kp_ref, vc_ref, vp_ref, bias_ref, o_ref,
                q32, k32, v32, acc_s, m_s, d_s):
    sup = DIL_SUPER
    blk = DIL_BLOCK
    n = pl.program_id(2)
    q32[...] = q_ref[0].astype(F32)
    k32[0:sup, :] = kp_ref[0].astype(F32)
    k32[sup:2 * sup, :] = kc_ref[0].astype(F32)
    v32[0:sup, :] = vp_ref[0].astype(F32)
    v32[sup:2 * sup, :] = vc_ref[0].astype(F32)
    low = lax.broadcasted_iota(jnp.int32, (blk, LANES), 1) < 64
    before_start = jnp.where(lax.broadcasted_iota(jnp.int32, (blk, 2 * blk), 1) < blk, NEG, 0.0)

    for g, (_, dil) in enumerate(DIL_PATTERNS):

        def unit(u, carry, g=g, dil=dil):
            n_loc = u // dil
            r = u % dil
            qs = n_loc * (blk * dil) + r
            ks = sup + (n_loc - 1) * (blk * dil) + r
            if dil == 1:
                qsl = pl.ds(pl.multiple_of(qs, blk), blk)
                ksl = pl.ds(pl.multiple_of(ks, blk), 2 * blk)
            else:
                qsl = pl.ds(qs, blk, stride=dil)
                ksl = pl.ds(ks, 2 * blk, stride=dil)
            q = q32[qsl, :]
            k = k32[ksl, :].astype(BF16)
            v = v32[ksl, :].astype(BF16)
            extra = jnp.where(jnp.logical_and(n == 0, n_loc == 0), before_start, 0.0)
            parts = []
            for hd in range(2):
                qh = jnp.where(low if hd == 0 else jnp.logical_not(low), q, 0.0).astype(BF16)
                s = lax.dot_general(qh, k, (((1,), (1,)), ((), ())), preferred_element_type=F32)
                s = s + bias_ref[g, hd] + extra
                m = jnp.max(s, axis=-1, keepdims=True)
                e = jnp.exp2(s - m)
                den = jnp.sum(e, axis=-1, keepdims=True)
                parts.append((_dot(e.astype(BF16), v), m, den))
            acc_s[g, qsl, :] = jnp.where(low, parts[0][0], parts[1][0])
            m_s[g, qsl, :] = jnp.where(low, parts[0][1], parts[1][1])
            d_s[g, qsl, :] = jnp.where(low, parts[0][2], parts[1][2])
            return carry

        lax.fori_loop(0, 16, unit, 0, unroll=8)

    mx = jnp.maximum(jnp.maximum(m_s[0], m_s[1]), m_s[2])
    num = jnp.zeros((sup, LANES), F32)
    den = jnp.zeros((sup, LANES), F32)
    for g in range(3):
        a = jnp.exp2(m_s[g] - mx)
        num = num + a * acc_s[g]
        den = den + a * d_s[g]
    o_ref[0] = (num / den).astype(BF16)


def _dil(q, k, v, bias):
    b, s, _ = q.shape
    sup = DIL_SUPER
    cur = pl.BlockSpec((1, sup, LANES), lambda bi, hp, n: (bi, n, hp))
    prev = pl.BlockSpec((1, sup, LANES), lambda bi, hp, n: (bi, jnp.maximum(n - 1, 0), hp))
    return pl.pallas_call(
        _dil_kernel,
        grid=(b, DIL_HEADS // 2, s // sup),
        in_specs=[cur, cur, prev, cur, prev,
                  pl.BlockSpec((3, 2, DIL_BLOCK, 2 * DIL_BLOCK), lambda bi, hp, n: (0, hp, 0, 0))],
        out_specs=cur,
        out_shape=jax.ShapeDtypeStruct((b, s, 512), BF16),
        scratch_shapes=[
            pltpu.VMEM((sup, LANES), F32),
            pltpu.VMEM((2 * sup, LANES), F32),
            pltpu.VMEM((2 * sup, LANES), F32),
            pltpu.VMEM((3, sup, LANES), F32),
            pltpu.VMEM((3, sup, LANES), F32),
            pltpu.VMEM((3, sup, LANES), F32),
        ],
        compiler_params=_cparams(("parallel", "parallel", "arbitrary")),
        name="dilated",
    )(q, k, k, v, v, bias)


def _diff_kernel(qt_ref, k_ref, vt_ref, bias_ref, far_ref, lam_ref, g_ref, o_ref, *, lam_init):
    tq = qt_ref.shape[4]
    tk = vt_ref.shape[4]
    nd = bias_ref.shape[2]
    qi = pl.program_id(2)
    qt = qt_ref[0, 0, 0]
    row = lax.broadcasted_iota(jnp.int32, (LANES, tq), 0)
    zero = jnp.zeros_like(qt)
    qm = (jnp.where(row < DIFF_HD, qt, zero), jnp.where(row >= DIFF_HD, qt, zero))

    def step(js, carry, bias_of):
        keys = [k_ref[0, pl.ds(pl.multiple_of(j * tk, tk), tk), :] for j in js]
        vts = [vt_ref[0, 0, j] for j in js]
        logits = [[_dot(k, qm[mi]) + bias_of(mi, j) for j, k in zip(js, keys)] for mi in range(2)]
        return _softmax_step_t(logits, [vts, vts], carry)

    one = (jnp.full((1, tq), NEG, F32), jnp.zeros((1, tq), F32), jnp.zeros((LANES, tq), F32))
    carry = (one, one)
    n_far = jnp.maximum(qi - nd + 1, 0)
    carry = _loop_pairs(0, n_far, functools.partial(step, bias_of=lambda mi, j: far_ref[0, mi, 0:1, 0:1]), carry,
                        group=4)
    carry = _loop_pairs(n_far, qi, functools.partial(step, bias_of=lambda mi, j: bias_ref[0, mi, qi - j]), carry,
                        group=4)
    carry = step((qi,), carry, lambda mi, j: bias_ref[0, mi, 0])
    (_, l0, a0), (_, l1, a1) = carry
    o = (a0 / l0 - lam_ref[0:1, 0:1] * (a1 / l1)).T
    o_ref[0] = (_rms(o, g_ref[...]) * (1.0 - lam_init)).astype(BF16)


def _diff(qt, k, vt, bias, far, lam, sub_g, lam_init):
    b, s, _ = k.shape
    tq = qt.shape[4]
    nk, tk = vt.shape[2], vt.shape[4]
    nd = bias.shape[2]
    return pl.pallas_call(
        functools.partial(_diff_kernel, lam_init=lam_init),
        grid=(DIFF_HEADS, b, s // tq),
        in_specs=[
            pl.BlockSpec((1, 1, 1, LANES, tq), lambda h, bi, qi: (bi, h, qi, 0, 0)),
            pl.BlockSpec((1, s, LANES), lambda h, bi, qi: (bi, 0, h)),
            pl.BlockSpec((1, 1, nk, LANES, tk), lambda h, bi, qi: (bi, h, 0, 0, 0)),
            _resident((1, 2, nd, tk, tq), lambda h, bi, qi: (h, 0, 0, 0, 0)),
            pl.BlockSpec((1, 2, 8, LANES), lambda h, bi, qi: (h, 0, 0, 0)),
            pl.BlockSpec((1, LANES), lambda h, bi, qi: (0, 0)),
            pl.BlockSpec((1, LANES), lambda h, bi, qi: (0, 0)),
        ],
        out_specs=pl.BlockSpec((1, tq, LANES), lambda h, bi, qi: (bi, qi, h)),
        out_shape=jax.ShapeDtypeStruct((b, s, 512), BF16),
        compiler_params=_cparams(("parallel", "parallel", "arbitrary")),
        name="diff",
    )(qt, k, vt, bias, far, lam, sub_g)


def _sb_kernel(q_ref, kt_ref, v_ref, o_ref):
    tq = q_ref.shape[1]
    tk = kt_ref.shape[4]
    qi = pl.program_id(2)
    lane = lax.broadcasted_iota(jnp.int32, (tq, LANES), 1)
    strict = (lax.broadcasted_iota(jnp.int32, (tq, tk), 1)
              < lax.broadcasted_iota(jnp.int32, (tq, tk), 0))
    later = (lax.broadcasted_iota(jnp.int32, (tk, tk), 0)
             > lax.broadcasted_iota(jnp.int32, (tk, tk), 1)).astype(BF16)
    q = q_ref[0]
    zero = jnp.zeros_like(q)
    qh = (jnp.where(lane < 64, q, zero), jnp.where(lane >= 64, q, zero))

    def blocks(js, state, masked):
        items = [(bi, hd) for bi in range(len(js)) for hd in range(2)]
        z = {it: _dot(qh[it[1]], kt_ref[0, 0, js[it[0]]]) for it in items}
        sp, log_1m, inblock = {}, {}, {}
        for it in items:
            sp[it] = jnp.maximum(z[it], 0.0) + jnp.log2(1.0 + jnp.exp2(-jnp.abs(z[it])))
            l1m = -sp[it]
            if masked:
                l1m = jnp.where(strict, l1m, 0.0)
            log_1m[it] = l1m
            hi = l1m.astype(BF16)
            lo = (l1m - hi.astype(F32)).astype(BF16)
            inblock[it] = _dot(hi, later) + _dot(lo, later)
        state = list(state)
        for bi, j in enumerate(js):
            v = v_ref[0, pl.ds(pl.multiple_of(j * tk, tk), tk), :]
            for hd in range(2):
                it = (bi, hd)
                c, acc = state[hd]
                w = jnp.exp2(z[it] - sp[it] + (inblock[it] + c))
                if masked:
                    w = jnp.where(strict, w, 0.0)
                acc = acc + _dot(w.astype(BF16), v)
                c = c + jnp.sum(log_1m[it], axis=-1, keepdims=True)
                state[hd] = (c, acc)
        return tuple(state)

    one = (jnp.zeros((tq, 1), F32), jnp.zeros((tq, LANES), F32))
    state = blocks((qi,), (one, one), True)

    odd = qi % 2
    state = lax.cond(odd == 1, lambda st: blocks((qi - 1,), st, False), lambda st: st, state)
    floor = SB_LOG_FLOOR * LOG2E

    def cond(st):
        j, ((c0, _), (c1, _)) = st
        return jnp.logical_and(j >= 1, jnp.max(jnp.maximum(c0, c1)) > floor)

    def body(st):
        j, state = st
        return j - 2, blocks((j, j - 1), state, False)

    _, ((_, a0), (_, a1)) = lax.while_loop(cond, body, (qi - 1 - odd, state))
    o_ref[0] = jnp.where(lane < 64, a0, a1).astype(BF16)


def _sb(q, kt, v):
    b, s, _ = q.shape
    tq = SB_TILE
    nk, tk = kt.shape[2], kt.shape[4]
    return pl.pallas_call(
        _sb_kernel,
        grid=(b, SB_HEADS // 2, s // tq),
        in_specs=[
            pl.BlockSpec((1, tq, LANES), lambda bi, hp, qi: (bi, qi, hp)),
            pl.BlockSpec((1, 1, nk, LANES, tk), lambda bi, hp, qi: (bi, hp, 0, 0, 0)),
            pl.BlockSpec((1, s, LANES), lambda bi, hp, qi: (bi, 0, hp)),
        ],
        out_specs=pl.BlockSpec((1, tq, LANES), lambda bi, hp, qi: (bi, qi, hp)),
        out_shape=jax.ShapeDtypeStruct((b, s, 512), BF16),
        compiler_params=_cparams(("parallel", "parallel", "arbitrary")),
        name="stick_breaking",
    )(q, kt, v)


FF_CHUNKS = ((0, 768), (768, 1536), (1536, 2304), (2304, 2816))
EXPERT_CHUNKS = ((0, 1024), (1024, 2048), (2048, 3072), (3072, 3584))


def _swiglu(hb, wg_ref, wu_ref, wd_ref, chunks, lead, between=None):
    acc = None
    for ci, (c0, c1) in enumerate(chunks):
        g = _dot(hb, wg_ref[lead + (slice(None), slice(c0, c1))])
        u = _dot(hb, wu_ref[lead + (slice(None), slice(c0, c1))])
        a = (g / (1.0 + jnp.exp(-g)) * u).astype(BF16)
        part = _dot(a, wd_ref[lead + (slice(c0, c1), slice(None))])
        acc = part if acc is None else acc + part
        if between is not None:
            between(ci)
    return acc


def _mix_out(x_ref, oa_ref, ob_ref, wa_ref, wb_ref, gate_ref, pg_ref):
    y = _dot(oa_ref[0], wa_ref[...]) + _dot(ob_ref[0], wb_ref[...])
    return x_ref[0] + gate_ref[0] * _rms(y, pg_ref[...])


def _post_even_kernel(x_ref, oa_ref, ob_ref, wa_ref, wb_ref, gate_ref, pg_ref,
                      fg_ref, fsh_ref, fsc_ref, fgate_ref, fpg_ref, wg_ref, wu_ref, wd_ref, o_ref):
    x1 = _mix_out(x_ref, oa_ref, ob_ref, wa_ref, wb_ref, gate_ref, pg_ref)
    hb = _prenorm_mod(x1, fg_ref[...], fsh_ref[0], fsc_ref[0]).astype(BF16)
    y = _swiglu(hb, wg_ref, wu_ref, wd_ref, FF_CHUNKS, ())
    o_ref[0] = x1 + fgate_ref[0] * _rms(y, fpg_ref[...])


def _post_even(x, oa, ob, wa, wb, gate, pg, fg, fsh, fsc, fgate, fpg, wg, wu, wd):
    b, s, d = x.shape
    tm = TOK_TILE
    tok = lambda w: pl.BlockSpec((1, tm, w), lambda bi, i: (bi, i, 0))
    vec = pl.BlockSpec((1, d), lambda bi, i: (0, 0))
    mod = pl.BlockSpec((1, 1, d), lambda bi, i: (bi, 0, 0))
    full = lambda a: _resident(a.shape, lambda bi, i: (0,) * a.ndim)
    return pl.pallas_call(
        _post_even_kernel,
        grid=(b, s // tm),
        in_specs=[tok(d), tok(512), tok(512), full(wa), full(wb), mod, vec,
                  vec, mod, mod, mod, vec, full(wg), full(wu), full(wd)],
        out_specs=tok(d),
        out_shape=jax.ShapeDtypeStruct((b, s, d), F32),
        compiler_params=_cparams(("parallel", "parallel")),
        name="post_even",
    )(x, oa, ob, wa, wb, gate, pg, fg, fsh, fsc, fgate, fpg, wg, wu, wd)


def _post_odd_kernel(x_ref, oa_ref, ob_ref, wa_ref, wb_ref, gate_ref, pg_ref,
                     fg_ref, fsh_ref, fsc_ref, rw_ref, rb_ref, x_out, h_out, r_out, cnt_ref):
    x1 = _mix_out(x_ref, oa_ref, ob_ref, wa_ref, wb_ref, gate_ref, pg_ref)
    x_out[0] = x1
    h = _prenorm_mod(x1, fg_ref[...], fsh_ref[0], fsc_ref[0])
    h_out[0] = h
    logits = _dot(h, rw_ref[...]) + rb_ref[...]
    lane = lax.broadcasted_iota(jnp.int32, logits.shape, 1)
    m1 = jnp.max(logits, axis=-1, keepdims=True)
    i1 = jnp.min(jnp.where(logits == m1, lane, LANES), axis=-1, keepdims=True)
    rest = jnp.where(lane == i1, NEG, logits)
    m2 = jnp.max(rest, axis=-1, keepdims=True)
    i2 = jnp.min(jnp.where(rest == m2, lane, LANES), axis=-1, keepdims=True)
    e2 = jnp.exp(m2 - m1)
    w1 = 1.0 / (1.0 + e2)
    w2 = e2 / (1.0 + e2)
    @pl.when(jnp.logical_and(pl.program_id(0) == 0, pl.program_id(1) == 0))
    def _():
        cnt_ref[...] = jnp.zeros_like(cnt_ref)

    tm = logits.shape[0]
    sel = jnp.logical_or(lane == i1, lane == i2)
    earlier = (lax.broadcasted_iota(jnp.int32, (tm, tm), 1)
               < lax.broadcasted_iota(jnp.int32, (tm, tm), 0)).astype(BF16)
    prefix = _dot(earlier, sel.astype(BF16)) + cnt_ref[0:1, :]
    rank1 = jnp.sum(jnp.where(lane == i1, prefix, 0.0), axis=-1, keepdims=True)
    rank2 = jnp.sum(jnp.where(lane == i2, prefix, 0.0), axis=-1, keepdims=True)
    cnt_ref[...] = cnt_ref[...] + jnp.sum(sel.astype(F32), axis=0, keepdims=True)
    r = jnp.where(lane == 0, i1.astype(F32), 0.0)
    r = jnp.where(lane == 1, i2.astype(F32), r)
    r = jnp.where(lane == 2, w1, r)
    r = jnp.where(lane == 3, w2, r)
    r = jnp.where(lane == 4, rank1, r)
    r = jnp.where(lane == 5, rank2, r)
    r_out[0] = r


def _post_odd(x, oa, ob, wa, wb, gate, pg, fg, fsh, fsc, rw, rb):
    b, s, d = x.shape
    tm = TOK_TILE
    tok = lambda w: pl.BlockSpec((1, tm, w), lambda bi, i: (bi, i, 0))
    vec = pl.BlockSpec((1, d), lambda bi, i: (0, 0))
    mod = pl.BlockSpec((1, 1, d), lambda bi, i: (bi, 0, 0))
    full = lambda a: _resident(a.shape, lambda bi, i: (0,) * a.ndim)
    return pl.pallas_call(
        _post_odd_kernel,
        grid=(b, s // tm),
        in_specs=[tok(d), tok(512), tok(512), full(wa), full(wb), mod, vec,
                  vec, mod, mod, full(rw), pl.BlockSpec((1, LANES), lambda bi, i: (0, 0))],
        out_specs=(tok(d), tok(d), tok(LANES),
                   pl.BlockSpec((8, LANES), lambda bi, i: (0, 0))),
        out_shape=(jax.ShapeDtypeStruct((b, s, d), F32), jax.ShapeDtypeStruct((b, s, d), F32),
                   jax.ShapeDtypeStruct((b, s, LANES), F32), jax.ShapeDtypeStruct((8, LANES), F32)),
        compiler_params=_cparams(("arbitrary", "arbitrary")),
        name="post_odd",
    )(x, oa, ob, wa, wb, gate, pg, fg, fsh, fsc, rw, rb)


def _store_rows(ref, lead, val):
    for c in range(ROW_SUB):
        ref[lead + (slice(None), c, slice(None))] = val[:, c * LANES:(c + 1) * LANES]


def _load_rows(ref, lead, lo, hi):
    return jnp.concatenate([ref[lead, lo:hi, c, :] for c in range(ROW_SUB)], axis=1)


def _gather_ahead(i, n_steps, idx_hbm, src_hbm, idx_smem, buf, isem, sem):
    n = buf.shape[1]
    slot = i % 2
    nxt = 1 - slot

    def idx_copy(step, sl):
        return pltpu.make_async_copy(idx_hbm.at[step], idx_smem.at[sl], isem.at[sl])

    def issue_rows(sl):
        def issue(r, carry):
            t = idx_smem[sl, r]
            pltpu.make_async_copy(src_hbm.at[t], buf.at[sl, r], sem.at[sl]).start()
            return carry

        lax.fori_loop(0, n, issue, 0, unroll=8)

    @pl.when(i == 0)
    def _():
        first = idx_copy(0, 0)
        first.start()
        first.wait()
        issue_rows(0)

        @pl.when(n_steps > 1)
        def _():
            idx_copy(1, 1).start()

    @pl.when(i + 1 < n_steps)
    def _():
        idx_copy(i + 1, nxt).wait()
        issue_rows(nxt)

    @pl.when(i + 2 < n_steps)
    def _():
        idx_copy(i + 2, slot).start()

    pltpu.make_async_copy(src_hbm.at[pl.ds(0, n)], buf.at[slot], sem.at[slot]).wait()
    return slot


def _moe_kernel(te_ref, nu_ref, tok_hbm, h_hbm, wg_ref, wu_ref, wd_ref, y_ref, idx_smem, buf, isem, sem):
    i = pl.program_id(0)
    nt = pl.num_programs(0)
    n = buf.shape[1]
    slot = i % 2
    nxt = 1 - slot
    tile = lambda t: jnp.minimum(t, nt - 1)

    def idx_copy(t, sl):
        return pltpu.make_async_copy(tok_hbm.at[tile(t)], idx_smem.at[sl], isem.at[sl])

    def rows_wait(sl):
        pltpu.make_async_copy(h_hbm.at[pl.ds(0, n), :], buf.at[sl], sem.at[sl]).wait()

    def issue_row(sl, r):
        t = idx_smem[sl, r]
        pltpu.make_async_copy(h_hbm.at[pl.ds(t, 1), :], buf.at[sl, pl.ds(r, 1), :], sem.at[sl]).start()

    def issue_loop(sl):
        def body(r, carry):
            issue_row(sl, r)
            return carry

        lax.fori_loop(0, n, body, 0, unroll=8)

    @pl.when(i == 0)
    def _():
        first = idx_copy(0, 0)
        first.start()
        first.wait()
        issue_loop(0)
        idx_copy(1, 1).start()

    idx_copy(i + 1, nxt).wait()
    idx_copy(i + 2, slot).start()
    rows_wait(slot)

    @pl.when(i < nu_ref[0])
    def _():
        xs = buf[slot].astype(BF16)
        per_chunk = n // len(EXPERT_CHUNKS)

        def issue_part(ci):
            for r in range(ci * per_chunk, (ci + 1) * per_chunk):
                issue_row(nxt, r)

        y = _swiglu(xs, wg_ref, wu_ref, wd_ref, EXPERT_CHUNKS, (0,), between=issue_part)
        _store_rows(y_ref, (), y)

    @pl.when(i >= nu_ref[0])
    def _():
        issue_loop(nxt)
        y_ref[...] = jnp.zeros_like(y_ref)

    @pl.when(i == nt - 1)
    def _():
        rows_wait(nxt)
        idx_copy(i + 2, slot).wait()


def _moe(tile_expert, n_used, row_token, h, wg, wu, wd):
    d = h.shape[1]
    nt, tm = row_token.shape
    dff = wg.shape[2]
    wspec = lambda shp: pl.BlockSpec(shp, lambda i, te, nu: (te[i], 0, 0), pipeline_mode=pl.Buffered(1))
    grid_spec = pltpu.PrefetchScalarGridSpec(
        num_scalar_prefetch=2,
        grid=(nt,),
        in_specs=[
            pl.BlockSpec(memory_space=pl.ANY),
            pl.BlockSpec(memory_space=pl.ANY),
            wspec((1, d, dff)), wspec((1, d, dff)), wspec((1, dff, d)),
        ],
        out_specs=pl.BlockSpec((tm, ROW_SUB, LANES), lambda i, te, nu: (i, 0, 0)),
        scratch_shapes=[
            pltpu.SMEM((2, tm), jnp.int32),
            pltpu.VMEM((2, tm, d), F32),
            pltpu.SemaphoreType.DMA((2,)),
            pltpu.SemaphoreType.DMA((2,)),
        ],
    )
    return pl.pallas_call(
        _moe_kernel,
        grid_spec=grid_spec,
        out_shape=jax.ShapeDtypeStruct((nt * tm, ROW_SUB, LANES), F32),
        compiler_params=_cparams(("arbitrary",)),
        name="moe_experts",
    )(tile_expert, n_used, row_token, h, wg, wu, wd)


def _combine_kernel(pos_hbm, y_hbm, x_ref, r_ref, gate_ref, pg_ref, o_ref, idx_smem, buf, isem, sem):
    tm = x_ref.shape[0]
    slot = _gather_ahead(pl.program_id(0), pl.num_programs(0), pos_hbm, y_hbm, idx_smem, buf, isem, sem)
    r = r_ref[...]
    y = r[:, 2:3] * _load_rows(buf, slot, 0, tm) + r[:, 3:4] * _load_rows(buf, slot, tm, 2 * tm)
    o_ref[...] = x_ref[...] + gate_ref[0] * _rms(y, pg_ref[...])


def _combine(pos, y, x, r, gate, pg, tokens_per_seq):
    n_tok, d = x.shape
    nt, tm2 = pos.shape
    tm = tm2 // 2
    per_seq = tokens_per_seq // tm
    tok = lambda w: pl.BlockSpec((tm, w), lambda i: (i, 0))
    return pl.pallas_call(
        _combine_kernel,
        grid=(nt,),
        in_specs=[
            pl.BlockSpec(memory_space=pl.ANY),
            pl.BlockSpec(memory_space=pl.ANY),
            tok(d), tok(LANES),
            pl.BlockSpec((1, 1, d), lambda i: (i // per_seq, 0, 0)),
            pl.BlockSpec((1, d), lambda i: (0, 0)),
        ],
        out_specs=tok(d),
        out_shape=jax.ShapeDtypeStruct((n_tok, d), F32),
        scratch_shapes=[
            pltpu.SMEM((2, tm2), jnp.int32),
            pltpu.VMEM((2, tm2, ROW_SUB, LANES), F32),
            pltpu.SemaphoreType.DMA((2,)),
            pltpu.SemaphoreType.DMA((2,)),
        ],
        compiler_params=_cparams(("arbitrary",)),
        name="moe_combine",
    )(pos, y, x, r, gate, pg)


def _t5_bucket(dist):
    max_exact = REL_BUCKETS // 2
    d = jnp.maximum(dist, 1).astype(F32)
    log_b = max_exact + (jnp.log(d / max_exact) / math.log(REL_MAX_DIST / max_exact)
                         * (REL_BUCKETS - max_exact)).astype(jnp.int32)
    log_b = jnp.minimum(log_b, REL_BUCKETS - 1)
    return jnp.where(dist < max_exact, dist, log_b)


def _rope_tables(s):
    half = MLA_ROPE // 2
    freqs = ROPE_THETA ** (-jnp.arange(half, dtype=F32) / half)
    ang = jnp.arange(s, dtype=F32)[:, None] * freqs[None, :]
    cos, sin = jnp.cos(ang), jnp.sin(ang)
    z64 = jnp.zeros((s, MLA_NOPE), F32)
    z32 = jnp.zeros((s, LANES - MLA_NOPE - MLA_ROPE), F32)
    ck = jnp.concatenate([z64, cos, cos, z32], axis=1)
    cq = jnp.concatenate([jnp.ones((s, MLA_NOPE), F32), cos, cos, z32], axis=1)
    sn = jnp.concatenate([z64, sin, sin, z32], axis=1)
    return cq, ck, sn


def _even_weights(w_in, w_uq, w_ukv):
    d = w_in.shape[0]
    half = MLA_ROPE // 2
    w_cq = w_in[:, :MLA_Q_RANK]
    w_ckv = w_in[:, MLA_Q_RANK:MLA_Q_RANK + MLA_KV_RANK]
    w_kr = w_in[:, MLA_Q_RANK + MLA_KV_RANK:MLA_Q_RANK + MLA_KV_RANK + MLA_ROPE]
    w_qkv = w_in[:, MLA_Q_RANK + MLA_KV_RANK + MLA_ROPE:]
    z = lambda n: jnp.zeros((d, n), F32)
    kr_a = jnp.concatenate([z(MLA_NOPE), w_kr, z(32)], axis=1)
    kr_b = jnp.concatenate([z(MLA_NOPE), -w_kr[:, half:], w_kr[:, :half], z(32)], axis=1)
    dil_scale = DIL_HD ** -0.5 * LOG2E
    w0 = jnp.concatenate([w_cq, w_ckv, kr_a, kr_b, w_qkv[:, :512] * dil_scale, w_qkv[:, 512:]], axis=1)

    r = w_uq.shape[0]
    wq = w_uq.reshape(r, MLA_HEADS, MLA_NOPE + MLA_ROPE) * ((MLA_NOPE + MLA_ROPE) ** -0.5 * LOG2E)
    zq = lambda n: jnp.zeros((r, MLA_HEADS, n), F32)
    nope, x1, x2 = wq[..., :MLA_NOPE], wq[..., MLA_NOPE:MLA_NOPE + half], wq[..., MLA_NOPE + half:]
    q_a = jnp.concatenate([nope, x1, x2, zq(32)], axis=-1).reshape(r, MLA_HEADS * LANES)
    q_b = jnp.concatenate([zq(MLA_NOPE), -x2, x1, zq(32)], axis=-1).reshape(r, MLA_HEADS * LANES)
    wq2 = jnp.concatenate([q_a, q_b], axis=1)

    rk = w_ukv.shape[0]
    wkv = w_ukv.reshape(rk, MLA_HEADS, MLA_NOPE + MLA_V)
    k_blk = jnp.concatenate([wkv[..., :MLA_NOPE], jnp.zeros((rk, MLA_HEADS, LANES - MLA_NOPE), F32)], axis=-1)
    wkv2 = jnp.concatenate([k_blk.reshape(rk, MLA_HEADS * LANES),
                            wkv[..., MLA_NOPE:].reshape(rk, MLA_HEADS * MLA_V)], axis=1)
    return w0.astype(BF16), wq2.astype(BF16), wkv2.astype(BF16)


def _toeplitz(vec, rows, cols):
    n, width = vec.shape

    def toeplitz_kernel(v_ref, o_ref):
        tiled = jnp.broadcast_to(v_ref[0], (rows, width))
        o_ref[0] = pltpu.roll(tiled, 0, 1, stride=1, stride_axis=0)[:, :cols]

    return pl.pallas_call(
        toeplitz_kernel,
        grid=(n,),
        in_specs=[pl.BlockSpec((1, 1, width), lambda t: (t, 0, 0))],
        out_specs=pl.BlockSpec((1, rows, cols), lambda t: (t, 0, 0)),
        out_shape=jax.ShapeDtypeStruct((n, rows, cols), F32),
        compiler_params=_cparams(("parallel",)),
        name="toeplitz_bias",
    )(vec.reshape(n, 1, width).astype(F32))


def _dil_bias(rel_bias):
    blk = DIL_BLOCK
    width = 4 * blk
    k = jnp.arange(width)
    rel = jnp.where(k < 2 * blk, blk - k, blk + width - k)
    out = []
    for window, dil in DIL_PATTERNS:
        band = (rel >= 0) & (rel <= window // dil)
        bias = rel_bias[_t5_bucket(jnp.maximum(rel, 0) * dil)] * LOG2E
        out.append(jnp.where(band[:, None], bias, NEG).T)
    vec = jnp.stack(out).reshape(3 * DIL_HEADS, width)
    return _toeplitz(vec, blk, 2 * blk).reshape(3, DIL_HEADS, blk, 2 * blk)


def _diff_bias(rel_bias, tile):
    nd = REL_MAX_DIST // tile + 1
    maps = rel_bias.shape[1]
    k = jnp.arange(2 * tile)[None, :]
    dist = jnp.arange(nd)[:, None] * tile + jnp.where(k < tile, k, k - 2 * tile)
    vec = jnp.where((dist >= 0)[..., None], rel_bias[_t5_bucket(jnp.maximum(dist, 0))] * LOG2E, NEG)
    vec = jnp.transpose(vec, (2, 0, 1)).reshape(maps * nd, 2 * tile)
    bias = _toeplitz(vec, tile, tile).reshape(DIFF_HEADS, 2, nd, tile, tile)
    far = rel_bias[_t5_bucket(jnp.array(REL_MAX_DIST))] * LOG2E
    far = jnp.broadcast_to(far.reshape(DIFF_HEADS, 2, 1, 1), (DIFF_HEADS, 2, 8, LANES))
    return bias, far.astype(F32)


def _routing(r, counts, n_tok, tile):
    n_tiles = (2 * n_tok) // tile + N_EXPERTS
    e = jnp.concatenate([r[:, 0], r[:, 1]]).astype(jnp.int32)
    rank = jnp.concatenate([r[:, 4], r[:, 5]]).astype(jnp.int32)
    counts = counts.astype(jnp.int32)
    padded = ((counts + tile - 1) // tile) * tile
    ends = jnp.cumsum(padded)
    starts = ends - padded
    onehot = (e[:, None] == jnp.arange(N_EXPERTS)[None, :]).astype(jnp.int32)
    pos = jnp.sum(onehot * starts[None, :], axis=1) + rank
    token = jnp.tile(jnp.arange(n_tok, dtype=jnp.int32), 2)
    row_token = jnp.zeros((n_tiles * tile,), jnp.int32).at[pos].set(token)
    tile_start = jnp.arange(n_tiles, dtype=jnp.int32) * tile
    tile_expert = jnp.sum((tile_start[:, None] >= ends[None, :]).astype(jnp.int32), axis=1)
    n_used = (ends[-1] // tile).astype(jnp.int32)
    last = jnp.sum((ends[-1] - 1 >= ends).astype(jnp.int32))
    tile_expert = jnp.minimum(tile_expert, last).astype(jnp.int32)
    return (tile_expert, n_used.reshape(1), row_token.reshape(n_tiles, tile),
            pos[:n_tok].astype(jnp.int32), pos[n_tok:].astype(jnp.int32))


def kernel(x, c, rel_bias, ada_mix_w, ada_mix_b, mix_pre_g, mix_post_g, ada_ffn_w, ada_ffn_b, ffn_pre_g, ffn_post_g, e_w_in, e_q_norm_g, e_w_uq, e_kv_norm_g, e_w_ukv, e_w_out, ffn_w_gate, ffn_w_up, ffn_w_down, o_w_in, diff_lq1, diff_lk1, diff_lq2, diff_lk2, diff_sub_g, o_w_out, router_w, router_b, moe_w_gate, moe_w_up, moe_w_down):
    b, s, d = x.shape
    assert d == D_MODEL and s % DIL_SUPER == 0 and s % TOK_TILE == 0
    row = lambda v: v.reshape(1, -1).astype(F32)

    mix_mod = _ada(c, ada_mix_w, ada_mix_b)
    ffn_mod = _ada(c, ada_ffn_w, ada_ffn_b)

    shift, scale, gate = _split_mod(mix_mod[0])
    w0, wq2, wkv2 = _even_weights(e_w_in[0], e_w_uq[0], e_w_ukv[0])
    cq, ck, sn = _rope_tables(s)
    qa, kta, va, qb, kb, vb = _even_in(x, row(mix_pre_g[0]), shift, scale, w0, row(e_q_norm_g[0]), wq2,
                                       row(e_kv_norm_g[0]), wkv2, cq, ck, sn)
    o_a = _mla(qa, kta, va)
    o_b = _dil(qb, kb, vb, _dil_bias(rel_bias))
    fshift, fscale, fgate = _split_mod(ffn_mod[0])
    w_out = e_w_out[0].astype(BF16)
    x = _post_even(x, o_a, o_b, w_out[:512], w_out[512:], gate, row(mix_post_g[0]),
                   row(ffn_pre_g[0]), fshift, fscale, fgate, row(ffn_post_g[0]),
                   ffn_w_gate[0].astype(BF16), ffn_w_up[0].astype(BF16), ffn_w_down[0].astype(BF16))

    layer = 1
    shift, scale, gate = _split_mod(mix_mod[1])
    w_in = o_w_in[0]
    att_scale = DIFF_HD ** -0.5
    w1 = jnp.concatenate([w_in[:, :512] * (att_scale * LOG2E), w_in[:, 512:1536],
                          w_in[:, 1536:2048] * (SB_HD ** -0.5 * LOG2E), w_in[:, 2048:]], axis=1).astype(BF16)
    qd, kdt, vd, qs, kst, vs = _odd_in(x, row(mix_pre_g[1]), shift, scale, w1)
    lam_init = 0.8 - 0.6 * math.exp(-0.3 * layer)
    lam = (jnp.exp(jnp.sum(diff_lq1[0].astype(F32) * diff_lk1[0].astype(F32)))
           - jnp.exp(jnp.sum(diff_lq2[0].astype(F32) * diff_lk2[0].astype(F32))) + lam_init)
    bias, far = _diff_bias(rel_bias, DIFF_TILE)
    o_c = _diff(qd, kdt, vd, bias, far, jnp.full((1, LANES), lam, F32), row(diff_sub_g[0]), lam_init)
    o_d = _sb(qs, kst, vs)
    fshift, fscale, fgate = _split_mod(ffn_mod[1])
    w_out = o_w_out[0].astype(BF16)
    rw = jnp.zeros((d, LANES), F32).at[:, :N_EXPERTS].set(router_w[0].astype(F32))
    rb = jnp.full((1, LANES), NEG, F32).at[0, :N_EXPERTS].set(router_b[0].astype(F32))
    x, h, r, counts = _post_odd(x, o_c, o_d, w_out[:512], w_out[512:], gate, row(mix_post_g[1]),
                                row(ffn_pre_g[1]), fshift, fscale, rw, rb)

    n_tok = b * s
    tile_expert, n_used, row_token, pos0, pos1 = _routing(r.reshape(n_tok, LANES), counts[0, :N_EXPERTS],
                                                          n_tok, MOE_TILE)
    y = _moe(tile_expert, n_used, row_token, h.reshape(n_tok, d),
             moe_w_gate[0].astype(BF16), moe_w_up[0].astype(BF16), moe_w_down[0].astype(BF16))
    ct = TOK_TILE
    pos = jnp.concatenate([pos0.reshape(n_tok // ct, ct), pos1.reshape(n_tok // ct, ct)], axis=1)
    out = _combine(pos, y, x.reshape(n_tok, d), r.reshape(n_tok, LANES), fgate, row(ffn_post_g[1]), s)
    return out.reshape(b, s, d)
```

```python
import functools
import math

import jax
import jax.numpy as jnp
from jax import lax
from jax.experimental import pallas as pl
from jax.experimental.pallas import tpu as pltpu

F32 = jnp.float32
BF16 = jnp.bfloat16

D_MODEL = 1024
EPS = 1e-6

MLA_HEADS = 8
MLA_NOPE = 64
MLA_ROPE = 32
MLA_V = 64
MLA_Q_RANK = 256
MLA_KV_RANK = 128
ROPE_THETA = 10000.0

DIL_HEADS = 8
DIL_HD = 64
DIL_PATTERNS = ((128, 1), (512, 4), (2048, 16))
DIL_BLOCK = 128

DIFF_HEADS = 4
DIFF_HD = 64
SB_HEADS = 8
SB_HD = 64

REL_BUCKETS = 32
REL_MAX_DIST = 2048

D_FF = 2816
N_EXPERTS = 8
D_FF_EXPERT = 3584

LANES = 128
ROW_SUB = D_MODEL // LANES
LOG2E = math.log2(math.e)
NEG = -1e30

TOK_TILE = 512
MLA_TILE = 512
DIFF_TILE = 512
SB_TILE = 256
DIL_SUPER = DIL_BLOCK * 16
MOE_TILE = 512
SB_LOG_FLOOR = -104.0

VMEM_LIMIT = 56 * 1024 * 1024


def _cparams(sem):
    return pltpu.CompilerParams(dimension_semantics=sem, vmem_limit_bytes=VMEM_LIMIT)


def _resident(shape, index_map):
    return pl.BlockSpec(shape, index_map, pipeline_mode=pl.Buffered(1))


def _rms(x, g):
    return x * lax.rsqrt(jnp.mean(x * x, axis=-1, keepdims=True) + EPS) * g


def _dot(a, b):
    return jnp.dot(a, b, preferred_element_type=F32)


def _softmax_step_t(logits, values, carry):
    out = []
    for s_list, v_list, (m, l, acc) in zip(logits, values, carry):
        for s, vt in zip(s_list, v_list):
            m_new = jnp.maximum(m, jnp.max(s, axis=0, keepdims=True))
            alpha = jnp.exp2(m - m_new)
            p = jnp.exp2(s - m_new)
            l = alpha * l + jnp.sum(p, axis=0, keepdims=True)
            acc = alpha * acc + _dot(vt, p.astype(BF16))
            m = m_new
        out.append((m, l, acc))
    return tuple(out)


def _loop_pairs(lo, hi, step, carry, group=2):
    n = hi - lo
    carry = lax.fori_loop(
        0, n // group, lambda i, c: step(tuple(lo + group * i + g for g in range(group)), c), carry)
    done = lo + (n // group) * group
    if group == 4:
        carry = lax.cond(hi - done >= 2, lambda c: step((done, done + 1), c), lambda c: c, carry)
    return lax.cond(n % 2 == 1, lambda c: step((hi - 1,), c), lambda c: c, carry)


def _softmax_step(logits, values, carry):
    out = []
    for s_list, (m, l, acc) in zip(logits, carry):
        m_new = m
        for s in s_list:
            m_new = jnp.maximum(m_new, jnp.max(s, axis=-1, keepdims=True))
        alpha = jnp.exp2(m - m_new)
        l = alpha * l
        acc = alpha * acc
        for s, v in zip(s_list, values):
            p = jnp.exp2(s - m_new)
            l = l + jnp.sum(p, axis=-1, keepdims=True)
            acc = acc + _dot(p.astype(BF16), v)
        out.append((m_new, l, acc))
    return tuple(out)


def _ada_kernel(c_ref, w_ref, b_ref, o_ref):
    c = c_ref[...]
    sc = c / (1.0 + jnp.exp(-c))
    o_ref[0] = _dot(sc.astype(BF16), w_ref[0].astype(BF16)) + b_ref[0]


def _ada(c, w, b):
    nl, d, d3 = w.shape
    bsz = c.shape[0]
    nb = d3 // d
    return pl.pallas_call(
        _ada_kernel,
        grid=(nl, nb),
        in_specs=[
            pl.BlockSpec((bsz, d), lambda l, j: (0, 0)),
            pl.BlockSpec((1, d, d), lambda l, j: (l, 0, j)),
            pl.BlockSpec((1, 1, d), lambda l, j: (l, 0, j)),
        ],
        out_specs=pl.BlockSpec((1, bsz, d), lambda l, j: (l, 0, j)),
        out_shape=jax.ShapeDtypeStruct((nl, bsz, d3), F32),
        compiler_params=_cparams(("arbitrary", "arbitrary")),
        name="ada",
    )(c, w, b.reshape(nl, 1, d3))


def _split_mod(m):
    b = m.shape[0]
    m = m.reshape(b, 3, 1, D_MODEL)
    return m[:, 0], m[:, 1], m[:, 2]


def _prenorm_mod(x, g, shift, scale):
    return _rms(x, g) * (1.0 + scale) + shift


def _even_in_kernel(x_ref, g_ref, sh_ref, sc_ref, w0_ref, qg_ref, wq_ref, kvg_ref, wkv_ref,
                    cq_ref, ck_ref, sn_ref,
                    qa_ref, ka_ref, va_ref, qb_ref, kb_ref, vb_ref):
    h = _prenorm_mod(x_ref[0], g_ref[...], sh_ref[0], sc_ref[0]).astype(BF16)
    proj = _dot(h, w0_ref[...])
    cqn = _rms(proj[:, 0:256], qg_ref[...]).astype(BF16)
    qq = _dot(cqn, wq_ref[...])
    ckvn = _rms(proj[:, 256:384], kvg_ref[...]).astype(BF16)
    kv = _dot(ckvn, wkv_ref[...])
    cq = cq_ref[...]
    ck = ck_ref[...]
    sn = sn_ref[...]
    krope = proj[:, 384:512] * ck + proj[:, 512:640] * sn
    nh = MLA_HEADS
    for hd in range(nh):
        lo = hd * LANES
        qh = qq[:, lo:lo + LANES] * cq + qq[:, nh * LANES + lo:nh * LANES + lo + LANES] * sn
        _store_key_tiles(qa_ref, hd, qh)
        ka_ref[0, :, lo:lo + LANES] = (kv[:, lo:lo + LANES] + krope).astype(BF16)
    for pr in range(nh // 2):
        _store_key_tiles(va_ref, pr, kv[:, nh * LANES + pr * LANES:nh * LANES + (pr + 1) * LANES])
    qb_ref[0] = proj[:, 640:1152].astype(BF16)
    kb_ref[0] = proj[:, 1152:1664].astype(BF16)
    vb_ref[0] = proj[:, 1664:2176].astype(BF16)


def _even_in(x, g, shift, scale, w0, qg, wq, kvg, wkv, cq, ck, sn):
    b, s, d = x.shape
    tm = TOK_TILE
    tkb = MLA_TILE
    ns = s // tm
    tok = lambda w: pl.BlockSpec((1, tm, w), lambda bi, i: (bi, i, 0))
    vec = lambda w: pl.BlockSpec((1, w), lambda bi, i: (0, 0))
    mod = pl.BlockSpec((1, 1, d), lambda bi, i: (bi, 0, 0))
    tab = pl.BlockSpec((tm, LANES), lambda bi, i: (i, 0))
    full = lambda a: _resident(a.shape, lambda bi, i: (0,) * a.ndim)
    out_shapes = (
        jax.ShapeDtypeStruct((b, MLA_HEADS, s // tkb, LANES, tkb), BF16),
        jax.ShapeDtypeStruct((b, s, MLA_HEADS * LANES), BF16),
        jax.ShapeDtypeStruct((b, MLA_HEADS // 2, s // tkb, LANES, tkb), BF16),
        jax.ShapeDtypeStruct((b, s, 512), BF16),
        jax.ShapeDtypeStruct((b, s, 512), BF16),
        jax.ShapeDtypeStruct((b, s, 512), BF16),
    )
    out_specs = (
        pl.BlockSpec((1, MLA_HEADS, tm // tkb, LANES, tkb), lambda bi, i: (bi, 0, i, 0, 0)),
        tok(MLA_HEADS * LANES),
        pl.BlockSpec((1, MLA_HEADS // 2, tm // tkb, LANES, tkb), lambda bi, i: (bi, 0, i, 0, 0)),
        tok(512), tok(512), tok(512),
    )
    return pl.pallas_call(
        _even_in_kernel,
        grid=(b, ns),
        in_specs=[tok(d), vec(d), mod, mod, full(w0), vec(MLA_Q_RANK), full(wq), vec(MLA_KV_RANK), full(wkv),
                  tab, tab, tab],
        out_specs=out_specs,
        out_shape=out_shapes,
        compiler_params=_cparams(("parallel", "parallel")),
        name="even_in",
    )(x, g, shift, scale, w0, qg, wq, kvg, wkv, cq, ck, sn)


def _store_key_tiles(kt_ref, hd, k):
    tkb = kt_ref.shape[4]
    for t in range(k.shape[0] // tkb):
        kt_ref[0, hd, t] = k[t * tkb:(t + 1) * tkb, :].T.astype(BF16)


def _odd_in_kernel(x_ref, g_ref, sh_ref, sc_ref, w_ref,
                   qdt_ref, kd_ref, vdt_ref, qs_ref, kst_ref, vs_ref):
    h = _prenorm_mod(x_ref[0], g_ref[...], sh_ref[0], sc_ref[0]).astype(BF16)
    proj = _dot(h, w_ref[...])
    kd_ref[0] = proj[:, 512:1024].astype(BF16)
    qs_ref[0] = proj[:, 1536:2048].astype(BF16)
    vs_ref[0] = proj[:, 2560:3072].astype(BF16)
    for hd in range(4):
        _store_key_tiles(qdt_ref, hd, proj[:, hd * LANES:(hd + 1) * LANES])
        _store_key_tiles(vdt_ref, hd, proj[:, 1024 + hd * LANES:1024 + (hd + 1) * LANES])
        _store_key_tiles(kst_ref, hd, proj[:, 2048 + hd * LANES:2048 + (hd + 1) * LANES])


def _odd_in(x, g, shift, scale, w):
    b, s, d = x.shape
    tm = TOK_TILE
    ns = s // tm
    tok = lambda wd: pl.BlockSpec((1, tm, wd), lambda bi, i: (bi, i, 0))
    mod = pl.BlockSpec((1, 1, d), lambda bi, i: (bi, 0, 0))
    ktspec = lambda tkb: pl.BlockSpec((1, 4, tm // tkb, LANES, tkb), lambda bi, i: (bi, 0, i, 0, 0))
    act = jax.ShapeDtypeStruct((b, s, 512), BF16)
    kts = lambda tkb: jax.ShapeDtypeStruct((b, 4, s // tkb, LANES, tkb), BF16)
    return pl.pallas_call(
        _odd_in_kernel,
        grid=(b, ns),
        in_specs=[tok(d), pl.BlockSpec((1, d), lambda bi, i: (0, 0)), mod, mod,
                  _resident(w.shape, lambda bi, i: (0, 0))],
        out_specs=(ktspec(DIFF_TILE), tok(512), ktspec(DIFF_TILE), tok(512), ktspec(SB_TILE), tok(512)),
        out_shape=(kts(DIFF_TILE), act, kts(DIFF_TILE), act, kts(SB_TILE), act),
        compiler_params=_cparams(("parallel", "parallel")),
        name="odd_in",
    )(x, g, shift, scale, w)


def _mla_kernel(qt_ref, k_ref, vt_ref, o_ref):
    tq = qt_ref.shape[4]
    tk = vt_ref.shape[4]
    hv = MLA_V
    qi = pl.program_id(2)
    causal = (lax.broadcasted_iota(jnp.int32, (tk, tq), 0)
              <= lax.broadcasted_iota(jnp.int32, (tk, tq), 1))
    qts = (qt_ref[0, 0, 0], qt_ref[0, 1, 0])

    def step(js, carry, masked):
        logits, values = [], []
        for hd in range(2):
            s_list = [_dot(k_ref[0, pl.ds(pl.multiple_of(j * tk, tk), tk), hd * LANES:(hd + 1) * LANES], qts[hd])
                      for j in js]
            if masked:
                s_list = [jnp.where(causal, s, NEG) for s in s_list]
            logits.append(s_list)
            values.append([vt_ref[0, 0, j, hd * hv:(hd + 1) * hv, :] for j in js])
        return _softmax_step_t(logits, values, carry)

    one = (jnp.full((1, tq), NEG, F32), jnp.zeros((1, tq), F32), jnp.zeros((hv, tq), F32))
    carry = _loop_pairs(0, qi, functools.partial(step, masked=False), (one, one), group=4)
    (_, l0, a0), (_, l1, a1) = step((qi,), carry, True)
    o_ref[0] = jnp.concatenate([a0 / l0, a1 / l1], axis=0).T.astype(BF16)


def _mla(qt, k, vt):
    b, s, _ = k.shape
    tq = qt.shape[4]
    nk = vt.shape[2]
    tk = vt.shape[4]
    return pl.pallas_call(
        _mla_kernel,
        grid=(b, MLA_HEADS // 2, s // tq),
        in_specs=[
            pl.BlockSpec((1, 2, 1, LANES, tq), lambda bi, hp, qi: (bi, hp, qi, 0, 0)),
            pl.BlockSpec((1, s, 2 * LANES), lambda bi, hp, qi: (bi, 0, hp)),
            pl.BlockSpec((1, 1, nk, LANES, tk), lambda bi, hp, qi: (bi, hp, 0, 0, 0)),
        ],
        out_specs=pl.BlockSpec((1, tq, LANES), lambda bi, hp, qi: (bi, qi, hp)),
        out_shape=jax.ShapeDtypeStruct((b, s, 512), BF16),
        compiler_params=_cparams(("parallel", "parallel", "arbitrary")),
        name="mla",
    )(qt, k, vt)


def _dil_kernel(q_ref, kc_ref, kp_ref, vc_ref, vp_ref, bias_ref, o_ref,
                q32, k32, v32, acc_s, m_s, d_s):
    sup = DIL_SUPER
    blk = DIL_BLOCK
    n = pl.program_id(2)
    q32[...] = q_ref[0].astype(F32)
    k32[0:sup, :] = kp_ref[0].astype(F32)
    k32[sup:2 * sup, :] = kc_ref[0].astype(F32)
    v32[0:sup, :] = vp_ref[0].astype(F32)
    v32[sup:2 * sup, :] = vc_ref[0].astype(F32)
    low = lax.broadcasted_iota(jnp.int32, (blk, LANES), 1) < 64
    before_start = jnp.where(lax.broadcasted_iota(jnp.int32, (blk, 2 * blk), 1) < blk, NEG, 0.0)

    for g, (_, dil) in enumerate(DIL_PATTERNS):

        def unit(u, carry, g=g, dil=dil):
            n_loc = u // dil
            r = u % dil
            qs = n_loc * (blk * dil) + r
            ks = sup + (n_loc - 1) * (blk * dil) + r
            if dil == 1:
                qsl = pl.ds(pl.multiple_of(qs, blk), blk)
                ksl = pl.ds(pl.multiple_of(ks, blk), 2 * blk)
            else:
                qsl = pl.ds(qs, blk, stride=dil)
                ksl = pl.ds(ks, 2 * blk, stride=dil)
            q = q32[qsl, :]
            k = k32[ksl, :].astype(BF16)
            v = v32[ksl, :].astype(BF16)
            extra = jnp.where(jnp.logical_and(n == 0, n_loc == 0), before_start, 0.0)
            parts = []
            for hd in range(2):
                qh = jnp.where(low if hd == 0 else jnp.logical_not(low), q, 0.0).astype(BF16)
                s = lax.dot_general(qh, k, (((1,), (1,)), ((), ())), preferred_element_type=F32)
                s = s + bias_ref[g, hd] + extra
                m = jnp.max(s, axis=-1, keepdims=True)
                e = jnp.exp2(s - m)
                den = jnp.sum(e, axis=-1, keepdims=True)
                parts.append((_dot(e.astype(BF16), v), m, den))
            acc_s[g, qsl, :] = jnp.where(low, parts[0][0], parts[1][0])
            m_s[g, qsl, :] = jnp.where(low, parts[0][1], parts[1][1])
            d_s[g, qsl, :] = jnp.where(low, parts[0][2], parts[1][2])
            return carry

        lax.fori_loop(0, 16, unit, 0, unroll=8)

    mx = jnp.maximum(jnp.maximum(m_s[0], m_s[1]), m_s[2])
    num = jnp.zeros((sup, LANES), F32)
    den = jnp.zeros((sup, LANES), F32)
    for g in range(3):
        a = jnp.exp2(m_s[g] - mx)
        num = num + a * acc_s[g]
        den = den + a * d_s[g]
    o_ref[0] = (num / den).astype(BF16)


def _dil(q, k, v, bias):
    b, s, _ = q.shape
    sup = DIL_SUPER
    cur = pl.BlockSpec((1, sup, LANES), lambda bi, hp, n: (bi, n, hp))
    prev = pl.BlockSpec((1, sup, LANES), lambda bi, hp, n: (bi, jnp.maximum(n - 1, 0), hp))
    return pl.pallas_call(
        _dil_kernel,
        grid=(b, DIL_HEADS // 2, s // sup),
        in_specs=[cur, cur, prev, cur, prev,
                  pl.BlockSpec((3, 2, DIL_BLOCK, 2 * DIL_BLOCK), lambda bi, hp, n: (0, hp, 0, 0))],
        out_specs=cur,
        out_shape=jax.ShapeDtypeStruct((b, s, 512), BF16),
        scratch_shapes=[
            pltpu.VMEM((sup, LANES), F32),
            pltpu.VMEM((2 * sup, LANES), F32),
            pltpu.VMEM((2 * sup, LANES), F32),
            pltpu.VMEM((3, sup, LANES), F32),
            pltpu.VMEM((3, sup, LANES), F32),
            pltpu.VMEM((3, sup, LANES), F32),
        ],
        compiler_params=_cparams(("parallel", "parallel", "arbitrary")),
        name="dilated",
    )(q, k, k, v, v, bias)


def _diff_kernel(qt_ref, k_ref, vt_ref, bias_ref, far_ref, lam_ref, g_ref, o_ref, *, lam_init):
    tq = qt_ref.shape[4]
    tk = vt_ref.shape[4]
    nd = bias_ref.shape[2]
    qi = pl.program_id(2)
    qt = qt_ref[0, 0, 0]
    row = lax.broadcasted_iota(jnp.int32, (LANES, tq), 0)
    zero = jnp.zeros_like(qt)
    qm = (jnp.where(row < DIFF_HD, qt, zero), jnp.where(row >= DIFF_HD, qt, zero))

    def step(js, carry, bias_of):
        keys = [k_ref[0, pl.ds(pl.multiple_of(j * tk, tk), tk), :] for j in js]
        vts = [vt_ref[0, 0, j] for j in js]
        logits = [[_dot(k, qm[mi]) + bias_of(mi, j) for j, k in zip(js, keys)] for mi in range(2)]
        return _softmax_step_t(logits, [vts, vts], carry)

    one = (jnp.full((1, tq), NEG, F32), jnp.zeros((1, tq), F32), jnp.zeros((LANES, tq), F32))
    carry = (one, one)
    n_far = jnp.maximum(qi - nd + 1, 0)
    carry = _loop_pairs(0, n_far, functools.partial(step, bias_of=lambda mi, j: far_ref[0, mi, 0:1, 0:1]), carry,
                        group=4)
    carry = _loop_pairs(n_far, qi, functools.partial(step, bias_of=lambda mi, j: bias_ref[0, mi, qi - j]), carry,
                        group=4)
    carry = step((qi,), carry, lambda mi, j: bias_ref[0, mi, 0])
    (_, l0, a0), (_, l1, a1) = carry
    o = (a0 / l0 - lam_ref[0:1, 0:1] * (a1 / l1)).T
    o_ref[0] = (_rms(o, g_ref[...]) * (1.0 - lam_init)).astype(BF16)


def _diff(qt, k, vt, bias, far, lam, sub_g, lam_init):
    b, s, _ = k.shape
    tq = qt.shape[4]
    nk, tk = vt.shape[2], vt.shape[4]
    nd = bias.shape[2]
    return pl.pallas_call(
        functools.partial(_diff_kernel, lam_init=lam_init),
        grid=(DIFF_HEADS, b, s // tq),
        in_specs=[
            pl.BlockSpec((1, 1, 1, LANES, tq), lambda h, bi, qi: (bi, h, qi, 0, 0)),
            pl.BlockSpec((1, s, LANES), lambda h, bi, qi: (bi, 0, h)),
            pl.BlockSpec((1, 1, nk, LANES, tk), lambda h, bi, qi: (bi, h, 0, 0, 0)),
            _resident((1, 2, nd, tk, tq), lambda h, bi, qi: (h, 0, 0, 0, 0)),
            pl.BlockSpec((1, 2, 8, LANES), lambda h, bi, qi: (h, 0, 0, 0)),
            pl.BlockSpec((1, LANES), lambda h, bi, qi: (0, 0)),
            pl.BlockSpec((1, LANES), lambda h, bi, qi: (0, 0)),
        ],
        out_specs=pl.BlockSpec((1, tq, LANES), lambda h, bi, qi: (bi, qi, h)),
        out_shape=jax.ShapeDtypeStruct((b, s, 512), BF16),
        compiler_params=_cparams(("parallel", "parallel", "arbitrary")),
        name="diff",
    )(qt, k, vt, bias, far, lam, sub_g)


def _sb_kernel(q_ref, kt_ref, v_ref, o_ref):
    tq = q_ref.shape[1]
    tk = kt_ref.shape[4]
    qi = pl.program_id(2)
    lane = lax.broadcasted_iota(jnp.int32, (tq, LANES), 1)
    strict = (lax.broadcasted_iota(jnp.int32, (tq, tk), 1)
              < lax.broadcasted_iota(jnp.int32, (tq, tk), 0))
    later = (lax.broadcasted_iota(jnp.int32, (tk, tk), 0)
             > lax.broadcasted_iota(jnp.int32, (tk, tk), 1)).astype(BF16)
    q = q_ref[0]
    zero = jnp.zeros_like(q)
    qh = (jnp.where(lane < 64, q, zero), jnp.where(lane >= 64, q, zero))

    def blocks(js, state, masked):
        items = [(bi, hd) for bi in range(len(js)) for hd in range(2)]
        z = {it: _dot(qh[it[1]], kt_ref[0, 0, js[it[0]]]) for it in items}
        sp, log_1m, inblock = {}, {}, {}
        for it in items:
            sp[it] = jnp.maximum(z[it], 0.0) + jnp.log2(1.0 + jnp.exp2(-jnp.abs(z[it])))
            l1m = -sp[it]
            if masked:
                l1m = jnp.where(strict, l1m, 0.0)
            log_1m[it] = l1m
            hi = l1m.astype(BF16)
            lo = (l1m - hi.astype(F32)).astype(BF16)
            inblock[it] = _dot(hi, later) + _dot(lo, later)
        state = list(state)
        for bi, j in enumerate(js):
            v = v_ref[0, pl.ds(pl.multiple_of(j * tk, tk), tk), :]
            for hd in range(2):
                it = (bi, hd)
                c, acc = state[hd]
                w = jnp.exp2(z[it] - sp[it] + (inblock[it] + c))
                if masked:
                    w = jnp.where(strict, w, 0.0)
                acc = acc + _dot(w.astype(BF16), v)
                c = c + jnp.sum(log_1m[it], axis=-1, keepdims=True)
                state[hd] = (c, acc)
        return tuple(state)

    one = (jnp.zeros((tq, 1), F32), jnp.zeros((tq, LANES), F32))
    state = blocks((qi,), (one, one), True)

    odd = qi % 2
    state = lax.cond(odd == 1, lambda st: blocks((qi - 1,), st, False), lambda st: st, state)
    floor = SB_LOG_FLOOR * LOG2E

    def cond(st):
        j, ((c0, _), (c1, _)) = st
        return jnp.logical_and(j >= 1, jnp.max(jnp.maximum(c0, c1)) > floor)

    def body(st):
        j, state = st
        return j - 2, blocks((j, j - 1), state, False)

    _, ((_, a0), (_, a1)) = lax.while_loop(cond, body, (qi - 1 - odd, state))
    o_ref[0] = jnp.where(lane < 64, a0, a1).astype(BF16)


def _sb(q, kt, v):
    b, s, _ = q.shape
    tq = SB_TILE
    nk, tk = kt.shape[2], kt.shape[4]
    return pl.pallas_call(
        _sb_kernel,
        grid=(b, SB_HEADS // 2, s // tq),
        in_specs=[
            pl.BlockSpec((1, tq, LANES), lambda bi, hp, qi: (bi, qi, hp)),
            pl.BlockSpec((1, 1, nk, LANES, tk), lambda bi, hp, qi: (bi, hp, 0, 0, 0)),
            pl.BlockSpec((1, s, LANES), lambda bi, hp, qi: (bi, 0, hp)),
        ],
        out_specs=pl.BlockSpec((1, tq, LANES), lambda bi, hp, qi: (bi, qi, hp)),
        out_shape=jax.ShapeDtypeStruct((b, s, 512), BF16),
        compiler_params=_cparams(("parallel", "parallel", "arbitrary")),
        name="stick_breaking",
    )(q, kt, v)


FF_CHUNKS = ((0, 768), (768, 1536), (1536, 2304), (2304, 2816))
EXPERT_CHUNKS = ((0, 1024), (1024, 2048), (2048, 3072), (3072, 3584))


def _swiglu(hb, wg_ref, wu_ref, wd_ref, chunks, lead, between=None):
    acc = None
    hb_of = hb if callable(hb) else (lambda ci: hb)
    for ci, (c0, c1) in enumerate(chunks):
        hb = hb_of(ci)
        g = _dot(hb, wg_ref[lead + (slice(None), slice(c0, c1))])
        u = _dot(hb, wu_ref[lead + (slice(None), slice(c0, c1))])
        a = (g / (1.0 + jnp.exp(-g)) * u).astype(BF16)
        part = _dot(a, wd_ref[lead + (slice(c0, c1), slice(None))])
        acc = part if acc is None else acc + part
        if between is not None:
            between(ci)
    return acc


def _mix_out(x_ref, oa_ref, ob_ref, wa_ref, wb_ref, gate_ref, pg_ref):
    y = _dot(oa_ref[0], wa_ref[...]) + _dot(ob_ref[0], wb_ref[...])
    return x_ref[0] + gate_ref[0] * _rms(y, pg_ref[...])


def _post_even_kernel(x_ref, oa_ref, ob_ref, wa_ref, wb_ref, gate_ref, pg_ref,
                      fg_ref, fsh_ref, fsc_ref, fgate_ref, fpg_ref, wg_ref, wu_ref, wd_ref, o_ref):
    x1 = _mix_out(x_ref, oa_ref, ob_ref, wa_ref, wb_ref, gate_ref, pg_ref)
    hb = _prenorm_mod(x1, fg_ref[...], fsh_ref[0], fsc_ref[0]).astype(BF16)
    y = _swiglu(hb, wg_ref, wu_ref, wd_ref, FF_CHUNKS, ())
    o_ref[0] = x1 + fgate_ref[0] * _rms(y, fpg_ref[...])


def _post_even(x, oa, ob, wa, wb, gate, pg, fg, fsh, fsc, fgate, fpg, wg, wu, wd):
    b, s, d = x.shape
    tm = TOK_TILE
    tok = lambda w: pl.BlockSpec((1, tm, w), lambda bi, i: (bi, i, 0))
    vec = pl.BlockSpec((1, d), lambda bi, i: (0, 0))
    mod = pl.BlockSpec((1, 1, d), lambda bi, i: (bi, 0, 0))
    full = lambda a: _resident(a.shape, lambda bi, i: (0,) * a.ndim)
    return pl.pallas_call(
        _post_even_kernel,
        grid=(b, s // tm),
        in_specs=[tok(d), tok(512), tok(512), full(wa), full(wb), mod, vec,
                  vec, mod, mod, mod, vec, full(wg), full(wu), full(wd)],
        out_specs=tok(d),
        out_shape=jax.ShapeDtypeStruct((b, s, d), F32),
        compiler_params=_cparams(("parallel", "parallel")),
        name="post_even",
    )(x, oa, ob, wa, wb, gate, pg, fg, fsh, fsc, fgate, fpg, wg, wu, wd)


def _post_odd_kernel(x_ref, oa_ref, ob_ref, wa_ref, wb_ref, gate_ref, pg_ref,
                     fg_ref, fsh_ref, fsc_ref, rw_ref, rb_ref, x_out, h_out, r_out, cnt_ref):
    x1 = _mix_out(x_ref, oa_ref, ob_ref, wa_ref, wb_ref, gate_ref, pg_ref)
    x_out[0] = x1
    h = _prenorm_mod(x1, fg_ref[...], fsh_ref[0], fsc_ref[0])
    h_out[0] = h
    logits = _dot(h, rw_ref[...]) + rb_ref[...]
    lane = lax.broadcasted_iota(jnp.int32, logits.shape, 1)
    m1 = jnp.max(logits, axis=-1, keepdims=True)
    i1 = jnp.min(jnp.where(logits == m1, lane, LANES), axis=-1, keepdims=True)
    rest = jnp.where(lane == i1, NEG, logits)
    m2 = jnp.max(rest, axis=-1, keepdims=True)
    i2 = jnp.min(jnp.where(rest == m2, lane, LANES), axis=-1, keepdims=True)
    e2 = jnp.exp(m2 - m1)
    w1 = 1.0 / (1.0 + e2)
    w2 = e2 / (1.0 + e2)
    @pl.when(jnp.logical_and(pl.program_id(0) == 0, pl.program_id(1) == 0))
    def _():
        cnt_ref[...] = jnp.zeros_like(cnt_ref)

    tm = logits.shape[0]
    sel = jnp.logical_or(lane == i1, lane == i2)
    earlier = (lax.broadcasted_iota(jnp.int32, (tm, tm), 1)
               < lax.broadcasted_iota(jnp.int32, (tm, tm), 0)).astype(BF16)
    prefix = _dot(earlier, sel.astype(BF16)) + cnt_ref[0:1, :]
    rank1 = jnp.sum(jnp.where(lane == i1, prefix, 0.0), axis=-1, keepdims=True)
    rank2 = jnp.sum(jnp.where(lane == i2, prefix, 0.0), axis=-1, keepdims=True)
    cnt_ref[...] = cnt_ref[...] + jnp.sum(sel.astype(F32), axis=0, keepdims=True)
    r = jnp.where(lane == 0, i1.astype(F32), 0.0)
    r = jnp.where(lane == 1, i2.astype(F32), r)
    r = jnp.where(lane == 2, w1, r)
    r = jnp.where(lane == 3, w2, r)
    r = jnp.where(lane == 4, rank1, r)
    r = jnp.where(lane == 5, rank2, r)
    r_out[0] = r


def _post_odd(x, oa, ob, wa, wb, gate, pg, fg, fsh, fsc, rw, rb):
    b, s, d = x.shape
    tm = TOK_TILE
    tok = lambda w: pl.BlockSpec((1, tm, w), lambda bi, i: (bi, i, 0))
    vec = pl.BlockSpec((1, d), lambda bi, i: (0, 0))
    mod = pl.BlockSpec((1, 1, d), lambda bi, i: (bi, 0, 0))
    full = lambda a: _resident(a.shape, lambda bi, i: (0,) * a.ndim)
    return pl.pallas_call(
        _post_odd_kernel,
        grid=(b, s // tm),
        in_specs=[tok(d), tok(512), tok(512), full(wa), full(wb), mod, vec,
                  vec, mod, mod, full(rw), pl.BlockSpec((1, LANES), lambda bi, i: (0, 0))],
        out_specs=(tok(d), tok(d), tok(LANES),
                   pl.BlockSpec((8, LANES), lambda bi, i: (0, 0))),
        out_shape=(jax.ShapeDtypeStruct((b, s, d), F32), jax.ShapeDtypeStruct((b, s, d), F32),
                   jax.ShapeDtypeStruct((b, s, LANES), F32), jax.ShapeDtypeStruct((8, LANES), F32)),
        compiler_params=_cparams(("arbitrary", "arbitrary")),
        name="post_odd",
    )(x, oa, ob, wa, wb, gate, pg, fg, fsh, fsc, rw, rb)


def _store_rows(ref, lead, val):
    for c in range(ROW_SUB):
        ref[lead + (slice(None), c, slice(None))] = val[:, c * LANES:(c + 1) * LANES]


def _load_rows(ref, lead, start, size):
    return jnp.concatenate([ref[lead, pl.ds(start, size), c, :] for c in range(ROW_SUB)], axis=1)


class _TileGather:
    def __init__(self, idx_hbm, src_hbm, idx_smem, buf, isem, sem, tile_rows):
        self.idx_hbm, self.src_hbm, self.idx_smem, self.buf = idx_hbm, src_hbm, idx_smem, buf
        self.isem, self.sem, self.tile_rows = isem, sem, tile_rows
        self.i = pl.program_id(0)
        self.nt = pl.num_programs(0)
        self.n = buf.shape[1]
        self.slot = self.i % 2
        self.nxt = 1 - self.slot

    def _idx_copy(self, t, sl):
        t = jnp.minimum(t, self.nt - 1)
        return pltpu.make_async_copy(self.idx_hbm.at[t], self.idx_smem.at[sl], self.isem.at[sl])

    def _rows_wait(self, sl):
        pltpu.make_async_copy(self.src_hbm.at[pl.ds(0, self.n)], self.buf.at[sl], self.sem.at[sl]).wait()

    def _issue_row(self, sl, r):
        t = self.idx_smem[sl, r]
        if self.tile_rows:
            src, dst = self.src_hbm.at[t], self.buf.at[sl, r]
        else:
            src, dst = self.src_hbm.at[pl.ds(t, 1)], self.buf.at[sl, pl.ds(r, 1)]
        pltpu.make_async_copy(src, dst, self.sem.at[sl]).start()

    def _issue_loop(self, sl):
        def body(r, carry):
            self._issue_row(sl, r)
            return carry

        lax.fori_loop(0, self.n, body, 0, unroll=8)

    def begin(self):
        @pl.when(self.i == 0)
        def _():
            first = self._idx_copy(0, 0)
            first.start()
            first.wait()
            self._issue_loop(0)
            self._idx_copy(1, 1).start()

        self._idx_copy(self.i + 1, self.nxt).wait()
        self._idx_copy(self.i + 2, self.slot).start()
        self._rows_wait(self.slot)

    def issue(self, part, parts):
        per = -(-self.n // parts)
        for r in range(part * per, min((part + 1) * per, self.n)):
            self._issue_row(self.nxt, r)

    def issue_all(self):
        self._issue_loop(self.nxt)

    def anchor(self, zero):
        return zero

    def finish(self):
        @pl.when(self.i == self.nt - 1)
        def _():
            self._rows_wait(self.nxt)
            self._idx_copy(self.i + 2, self.slot).wait()


def _moe_kernel(te_ref, nu_ref, zero_ref, tok_hbm, h_hbm, wg_ref, wu_ref, wd_ref, y_ref, idx_smem, buf, isem, sem):
    g = _TileGather(tok_hbm, h_hbm, idx_smem, buf, isem, sem, tile_rows=False)
    n = g.n
    g.begin()

    @pl.when(g.i < nu_ref[0])
    def _():
        anchor = [0]
        head = 16
        rest = buf[g.slot, head:n, :].astype(BF16)

        def rows_of(ci):
            first = buf[g.slot, pl.ds(pl.multiple_of(anchor[0], head), head), :].astype(BF16)
            return jnp.concatenate([first, rest], axis=0)

        def issue_part(ci):
            if ci < len(EXPERT_CHUNKS) - 1:
                g.issue(ci, len(EXPERT_CHUNKS) - 1)
                anchor[0] = g.anchor(zero_ref[0])

        y = _swiglu(rows_of, wg_ref, wu_ref, wd_ref, EXPERT_CHUNKS, (0,), between=issue_part)
        _store_rows(y_ref, (), y)

    @pl.when(g.i >= nu_ref[0])
    def _():
        g.issue_all()
        y_ref[...] = jnp.zeros_like(y_ref)

    g.finish()


def _moe(tile_expert, n_used, row_token, h, wg, wu, wd):
    d = h.shape[1]
    nt, tm = row_token.shape
    dff = wg.shape[2]
    wspec = lambda shp: pl.BlockSpec(shp, lambda i, te, nu, z: (te[i], 0, 0), pipeline_mode=pl.Buffered(1))
    grid_spec = pltpu.PrefetchScalarGridSpec(
        num_scalar_prefetch=3,
        grid=(nt,),
        in_specs=[
            pl.BlockSpec(memory_space=pl.ANY),
            pl.BlockSpec(memory_space=pl.ANY),
            wspec((1, d, dff)), wspec((1, d, dff)), wspec((1, dff, d)),
        ],
        out_specs=pl.BlockSpec((tm, ROW_SUB, LANES), lambda i, te, nu, z: (i, 0, 0)),
        scratch_shapes=[
            pltpu.SMEM((2, tm), jnp.int32),
            pltpu.VMEM((2, tm, d), F32),
            pltpu.SemaphoreType.DMA((2,)),
            pltpu.SemaphoreType.DMA((2,)),
        ],
    )
    return pl.pallas_call(
        _moe_kernel,
        grid_spec=grid_spec,
        out_shape=jax.ShapeDtypeStruct((nt * tm, ROW_SUB, LANES), F32),
        compiler_params=_cparams(("arbitrary",)),
        name="moe_experts",
    )(tile_expert, n_used, jnp.zeros((1,), jnp.int32), row_token, h, wg, wu, wd)


def _combine_kernel(zero_ref, pos_hbm, y_hbm, x_ref, r_ref, gate_ref, pg_ref, o_ref, idx_smem, buf, isem, sem):
    tm = x_ref.shape[0]
    g = _TileGather(pos_hbm, y_hbm, idx_smem, buf, isem, sem, tile_rows=True)
    g.begin()
    parts = 4
    rows = tm // parts
    off = 0
    for c in range(parts):
        lo = c * rows
        first = _load_rows(buf, g.slot, pl.multiple_of(off + lo, 8), rows)
        second = _load_rows(buf, g.slot, pl.multiple_of(off + tm + lo, 8), rows)
        r = r_ref[lo:lo + rows, :]
        y = r[:, 2:3] * first + r[:, 3:4] * second
        o_ref[lo:lo + rows, :] = x_ref[lo:lo + rows, :] + gate_ref[0] * _rms(y, pg_ref[...])
        if c < parts - 1:
            g.issue(c, parts - 1)
            off = g.anchor(zero_ref[0])
    g.finish()


def _combine(pos, y, x, r, gate, pg, tokens_per_seq):
    n_tok, d = x.shape
    nt, tm2 = pos.shape
    tm = tm2 // 2
    per_seq = tokens_per_seq // tm
    tok = lambda w: pl.BlockSpec((tm, w), lambda i, z: (i, 0))
    grid_spec = pltpu.PrefetchScalarGridSpec(
        num_scalar_prefetch=1,
        grid=(nt,),
        in_specs=[
            pl.BlockSpec(memory_space=pl.ANY),
            pl.BlockSpec(memory_space=pl.ANY),
            tok(d), tok(LANES),
            pl.BlockSpec((1, 1, d), lambda i, z: (i // per_seq, 0, 0)),
            pl.BlockSpec((1, d), lambda i, z: (0, 0)),
        ],
        out_specs=tok(d),
        scratch_shapes=[
            pltpu.SMEM((2, tm2), jnp.int32),
            pltpu.VMEM((2, tm2, ROW_SUB, LANES), F32),
            pltpu.SemaphoreType.DMA((2,)),
            pltpu.SemaphoreType.DMA((2,)),
        ],
    )
    return pl.pallas_call(
        _combine_kernel,
        grid_spec=grid_spec,
        out_shape=jax.ShapeDtypeStruct((n_tok, d), F32),
        compiler_params=_cparams(("arbitrary",)),
        name="moe_combine",
    )(jnp.zeros((1,), jnp.int32), pos, y, x, r, gate, pg)


def _t5_bucket(dist):
    max_exact = REL_BUCKETS // 2
    d = jnp.maximum(dist, 1).astype(F32)
    log_b = max_exact + (jnp.log(d / max_exact) / math.log(REL_MAX_DIST / max_exact)
                         * (REL_BUCKETS - max_exact)).astype(jnp.int32)
    log_b = jnp.minimum(log_b, REL_BUCKETS - 1)
    return jnp.where(dist < max_exact, dist, log_b)


def _rope_tables(s):
    half = MLA_ROPE // 2
    freqs = ROPE_THETA ** (-jnp.arange(half, dtype=F32) / half)
    ang = jnp.arange(s, dtype=F32)[:, None] * freqs[None, :]
    cos, sin = jnp.cos(ang), jnp.sin(ang)
    z64 = jnp.zeros((s, MLA_NOPE), F32)
    z32 = jnp.zeros((s, LANES - MLA_NOPE - MLA_ROPE), F32)
    ck = jnp.concatenate([z64, cos, cos, z32], axis=1)
    cq = jnp.concatenate([jnp.ones((s, MLA_NOPE), F32), cos, cos, z32], axis=1)
    sn = jnp.concatenate([z64, sin, sin, z32], axis=1)
    return cq, ck, sn


def _even_weights(w_in, w_uq, w_ukv):
    d = w_in.shape[0]
    half = MLA_ROPE // 2
    w_cq = w_in[:, :MLA_Q_RANK]
    w_ckv = w_in[:, MLA_Q_RANK:MLA_Q_RANK + MLA_KV_RANK]
    w_kr = w_in[:, MLA_Q_RANK + MLA_KV_RANK:MLA_Q_RANK + MLA_KV_RANK + MLA_ROPE]
    w_qkv = w_in[:, MLA_Q_RANK + MLA_KV_RANK + MLA_ROPE:]
    z = lambda n: jnp.zeros((d, n), F32)
    kr_a = jnp.concatenate([z(MLA_NOPE), w_kr, z(32)], axis=1)
    kr_b = jnp.concatenate([z(MLA_NOPE), -w_kr[:, half:], w_kr[:, :half], z(32)], axis=1)
    dil_scale = DIL_HD ** -0.5 * LOG2E
    w0 = jnp.concatenate([w_cq, w_ckv, kr_a, kr_b, w_qkv[:, :512] * dil_scale, w_qkv[:, 512:]], axis=1)

    r = w_uq.shape[0]
    wq = w_uq.reshape(r, MLA_HEADS, MLA_NOPE + MLA_ROPE) * ((MLA_NOPE + MLA_ROPE) ** -0.5 * LOG2E)
    zq = lambda n: jnp.zeros((r, MLA_HEADS, n), F32)
    nope, x1, x2 = wq[..., :MLA_NOPE], wq[..., MLA_NOPE:MLA_NOPE + half], wq[..., MLA_NOPE + half:]
    q_a = jnp.concatenate([nope, x1, x2, zq(32)], axis=-1).reshape(r, MLA_HEADS * LANES)
    q_b = jnp.concatenate([zq(MLA_NOPE), -x2, x1, zq(32)], axis=-1).reshape(r, MLA_HEADS * LANES)
    wq2 = jnp.concatenate([q_a, q_b], axis=1)

    rk = w_ukv.shape[0]
    wkv = w_ukv.reshape(rk, MLA_HEADS, MLA_NOPE + MLA_V)
    k_blk = jnp.concatenate([wkv[..., :MLA_NOPE], jnp.zeros((rk, MLA_HEADS, LANES - MLA_NOPE), F32)], axis=-1)
    wkv2 = jnp.concatenate([k_blk.reshape(rk, MLA_HEADS * LANES),
                            wkv[..., MLA_NOPE:].reshape(rk, MLA_HEADS * MLA_V)], axis=1)
    return w0.astype(BF16), wq2.astype(BF16), wkv2.astype(BF16)


def _toeplitz(vec, rows, cols):
    n, width = vec.shape

    def toeplitz_kernel(v_ref, o_ref):
        tiled = jnp.broadcast_to(v_ref[0], (rows, width))
        o_ref[0] = pltpu.roll(tiled, 0, 1, stride=1, stride_axis=0)[:, :cols]

    return pl.pallas_call(
        toeplitz_kernel,
        grid=(n,),
        in_specs=[pl.BlockSpec((1, 1, width), lambda t: (t, 0, 0))],
        out_specs=pl.BlockSpec((1, rows, cols), lambda t: (t, 0, 0)),
        out_shape=jax.ShapeDtypeStruct((n, rows, cols), F32),
        compiler_params=_cparams(("parallel",)),
        name="toeplitz_bias",
    )(vec.reshape(n, 1, width).astype(F32))


def _dil_bias(rel_bias):
    blk = DIL_BLOCK
    width = 4 * blk
    k = jnp.arange(width)
    rel = jnp.where(k < 2 * blk, blk - k, blk + width - k)
    out = []
    for window, dil in DIL_PATTERNS:
        band = (rel >= 0) & (rel <= window // dil)
        bias = rel_bias[_t5_bucket(jnp.maximum(rel, 0) * dil)] * LOG2E
        out.append(jnp.where(band[:, None], bias, NEG).T)
    vec = jnp.stack(out).reshape(3 * DIL_HEADS, width)
    return _toeplitz(vec, blk, 2 * blk).reshape(3, DIL_HEADS, blk, 2 * blk)


def _diff_bias(rel_bias, tile):
    nd = REL_MAX_DIST // tile + 1
    maps = rel_bias.shape[1]
    k = jnp.arange(2 * tile)[None, :]
    dist = jnp.arange(nd)[:, None] * tile + jnp.where(k < tile, k, k - 2 * tile)
    vec = jnp.where((dist >= 0)[..., None], rel_bias[_t5_bucket(jnp.maximum(dist, 0))] * LOG2E, NEG)
    vec = jnp.transpose(vec, (2, 0, 1)).reshape(maps * nd, 2 * tile)
    bias = _toeplitz(vec, tile, tile).reshape(DIFF_HEADS, 2, nd, tile, tile)
    far = rel_bias[_t5_bucket(jnp.array(REL_MAX_DIST))] * LOG2E
    far = jnp.broadcast_to(far.reshape(DIFF_HEADS, 2, 1, 1), (DIFF_HEADS, 2, 8, LANES))
    return bias, far.astype(F32)


def _routing(r, counts, n_tok, tile):
    n_tiles = (2 * n_tok) // tile + N_EXPERTS
    e = jnp.concatenate([r[:, 0], r[:, 1]]).astype(jnp.int32)
    rank = jnp.concatenate([r[:, 4], r[:, 5]]).astype(jnp.int32)
    counts = counts.astype(jnp.int32)
    padded = ((counts + tile - 1) // tile) * tile
    ends = jnp.cumsum(padded)
    starts = ends - padded
    onehot = (e[:, None] == jnp.arange(N_EXPERTS)[None, :]).astype(jnp.int32)
    pos = jnp.sum(onehot * starts[None, :], axis=1) + rank
    token = jnp.tile(jnp.arange(n_tok, dtype=jnp.int32), 2)
    row_token = jnp.zeros((n_tiles * tile,), jnp.int32).at[pos].set(token)
    tile_start = jnp.arange(n_tiles, dtype=jnp.int32) * tile
    tile_expert = jnp.sum((tile_start[:, None] >= ends[None, :]).astype(jnp.int32), axis=1)
    n_used = (ends[-1] // tile).astype(jnp.int32)
    last = jnp.sum((ends[-1] - 1 >= ends).astype(jnp.int32))
    tile_expert = jnp.minimum(tile_expert, last).astype(jnp.int32)
    return (tile_expert, n_used.reshape(1), row_token.reshape(n_tiles, tile),
            pos[:n_tok].astype(jnp.int32), pos[n_tok:].astype(jnp.int32))


def kernel(x, c, rel_bias, ada_mix_w, ada_mix_b, mix_pre_g, mix_post_g, ada_ffn_w, ada_ffn_b, ffn_pre_g, ffn_post_g, e_w_in, e_q_norm_g, e_w_uq, e_kv_norm_g, e_w_ukv, e_w_out, ffn_w_gate, ffn_w_up, ffn_w_down, o_w_in, diff_lq1, diff_lk1, diff_lq2, diff_lk2, diff_sub_g, o_w_out, router_w, router_b, moe_w_gate, moe_w_up, moe_w_down):
    b, s, d = x.shape
    assert d == D_MODEL and s % DIL_SUPER == 0 and s % TOK_TILE == 0
    row = lambda v: v.reshape(1, -1).astype(F32)

    mix_mod = _ada(c, ada_mix_w, ada_mix_b)
    ffn_mod = _ada(c, ada_ffn_w, ada_ffn_b)

    shift, scale, gate = _split_mod(mix_mod[0])
    w0, wq2, wkv2 = _even_weights(e_w_in[0], e_w_uq[0], e_w_ukv[0])
    cq, ck, sn = _rope_tables(s)
    qa, kta, va, qb, kb, vb = _even_in(x, row(mix_pre_g[0]), shift, scale, w0, row(e_q_norm_g[0]), wq2,
                                       row(e_kv_norm_g[0]), wkv2, cq, ck, sn)
    o_a = _mla(qa, kta, va)
    o_b = _dil(qb, kb, vb, _dil_bias(rel_bias))
    fshift, fscale, fgate = _split_mod(ffn_mod[0])
    w_out = e_w_out[0].astype(BF16)
    x = _post_even(x, o_a, o_b, w_out[:512], w_out[512:], gate, row(mix_post_g[0]),
                   row(ffn_pre_g[0]), fshift, fscale, fgate, row(ffn_post_g[0]),
                   ffn_w_gate[0].astype(BF16), ffn_w_up[0].astype(BF16), ffn_w_down[0].astype(BF16))

    layer = 1
    shift, scale, gate = _split_mod(mix_mod[1])
    w_in = o_w_in[0]
    att_scale = DIFF_HD ** -0.5
    w1 = jnp.concatenate([w_in[:, :512] * (att_scale * LOG2E), w_in[:, 512:1536],
                          w_in[:, 1536:2048] * (SB_HD ** -0.5 * LOG2E), w_in[:, 2048:]], axis=1).astype(BF16)
    qd, kdt, vd, qs, kst, vs = _odd_in(x, row(mix_pre_g[1]), shift, scale, w1)
    lam_init = 0.8 - 0.6 * math.exp(-0.3 * layer)
    lam = (jnp.exp(jnp.sum(diff_lq1[0].astype(F32) * diff_lk1[0].astype(F32)))
           - jnp.exp(jnp.sum(diff_lq2[0].astype(F32) * diff_lk2[0].astype(F32))) + lam_init)
    bias, far = _diff_bias(rel_bias, DIFF_TILE)
    o_c = _diff(qd, kdt, vd, bias, far, jnp.full((1, LANES), lam, F32), row(diff_sub_g[0]), lam_init)
    o_d = _sb(qs, kst, vs)
    fshift, fscale, fgate = _split_mod(ffn_mod[1])
    w_out = o_w_out[0].astype(BF16)
    rw = jnp.zeros((d, LANES), F32).at[:, :N_EXPERTS].set(router_w[0].astype(F32))
    rb = jnp.full((1, LANES), NEG, F32).at[0, :N_EXPERTS].set(router_b[0].astype(F32))
    x, h, r, counts = _post_odd(x, o_c, o_d, w_out[:512], w_out[512:], gate, row(mix_post_g[1]),
                                row(ffn_pre_g[1]), fshift, fscale, rw, rb)

    n_tok = b * s
    tile_expert, n_used, row_token, pos0, pos1 = _routing(r.reshape(n_tok, LANES), counts[0, :N_EXPERTS],
                                                          n_tok, MOE_TILE)
    y = _moe(tile_expert, n_used, row_token, h.reshape(n_tok, d),
             moe_w_gate[0].astype(BF16), moe_w_up[0].astype(BF16), moe_w_down[0].astype(BF16))
    ct = TOK_TILE
    pos = jnp.concatenate([pos0.reshape(n_tok // ct, ct), pos1.reshape(n_tok // ct, ct)], axis=1)
    out = _combine(pos, y, x.reshape(n_tok, d), r.reshape(n_tok, LANES), fgate, row(ffn_post_g[1]), s)
    return out.reshape(b, s, d)
```

```python
import functools
import math

import jax
import jax.numpy as jnp
from jax import lax
from jax.experimental import pallas as pl
from jax.experimental.pallas import tpu as pltpu

F32 = jnp.float32
BF16 = jnp.bfloat16

D_MODEL = 1024
EPS = 1e-6

MLA_HEADS = 8
MLA_NOPE = 64
MLA_ROPE = 32
MLA_V = 64
MLA_Q_RANK = 256
MLA_KV_RANK = 128
ROPE_THETA = 10000.0

DIL_HEADS = 8
DIL_HD = 64
DIL_PATTERNS = ((128, 1), (512, 4), (2048, 16))
DIL_BLOCK = 128

DIFF_HEADS = 4
DIFF_HD = 64
SB_HEADS = 8
SB_HD = 64

REL_BUCKETS = 32
REL_MAX_DIST = 2048

D_FF = 2816
N_EXPERTS = 8
D_FF_EXPERT = 3584

LANES = 128
ROW_SUB = D_MODEL // LANES
LOG2E = math.log2(math.e)
NEG = -1e30

TOK_TILE = 512
MLA_TILE = 512
DIFF_TILE = 512
SB_TILE = 256
DIL_SUPER = DIL_BLOCK * 16
MOE_TILE = 512
SB_LOG_FLOOR = -104.0

VMEM_LIMIT = 56 * 1024 * 1024


def _cparams(sem):
    return pltpu.CompilerParams(dimension_semantics=sem, vmem_limit_bytes=VMEM_LIMIT)


def _resident(shape, index_map):
    return pl.BlockSpec(shape, index_map, pipeline_mode=pl.Buffered(1))


def _rms(x, g):
    return x * lax.rsqrt(jnp.mean(x * x, axis=-1, keepdims=True) + EPS) * g


def _dot(a, b):
    return jnp.dot(a, b, preferred_element_type=F32)


def _softmax_step_t(logits, values, carry):
    out = []
    for s_list, v_list, (m, l, acc) in zip(logits, values, carry):
        for s, vt in zip(s_list, v_list):
            m_new = jnp.maximum(m, jnp.max(s, axis=0, keepdims=True))
            alpha = jnp.exp2(m - m_new)
            p = jnp.exp2(s - m_new)
            l = alpha * l + jnp.sum(p, axis=0, keepdims=True)
            acc = alpha * acc + _dot(vt, p.astype(BF16))
            m = m_new
        out.append((m, l, acc))
    return tuple(out)


def _loop_pairs(lo, hi, step, carry, group=2):
    n = hi - lo
    carry = lax.fori_loop(
        0, n // group, lambda i, c: step(tuple(lo + group * i + g for g in range(group)), c), carry)
    size = group // 2
    while size >= 1:
        start = lo + (n // (2 * size)) * (2 * size)
        carry = lax.cond((n // size) % 2 == 1,
                         lambda c, start=start, size=size: step(tuple(start + g for g in range(size)), c),
                         lambda c: c, carry)
        size //= 2
    return carry


def _softmax_step(logits, values, carry):
    out = []
    for s_list, (m, l, acc) in zip(logits, carry):
        m_new = m
        for s in s_list:
            m_new = jnp.maximum(m_new, jnp.max(s, axis=-1, keepdims=True))
        alpha = jnp.exp2(m - m_new)
        l = alpha * l
        acc = alpha * acc
        for s, v in zip(s_list, values):
            p = jnp.exp2(s - m_new)
            l = l + jnp.sum(p, axis=-1, keepdims=True)
            acc = acc + _dot(p.astype(BF16), v)
        out.append((m_new, l, acc))
    return tuple(out)


def _ada_kernel(c_ref, w_ref, b_ref, o_ref):
    c = c_ref[...]
    sc = c / (1.0 + jnp.exp(-c))
    o_ref[0] = _dot(sc.astype(BF16), w_ref[0].astype(BF16)) + b_ref[0]


def _ada(c, w, b):
    nl, d, d3 = w.shape
    bsz = c.shape[0]
    nb = d3 // d
    return pl.pallas_call(
        _ada_kernel,
        grid=(nl, nb),
        in_specs=[
            pl.BlockSpec((bsz, d), lambda l, j: (0, 0)),
            pl.BlockSpec((1, d, d), lambda l, j: (l, 0, j)),
            pl.BlockSpec((1, 1, d), lambda l, j: (l, 0, j)),
        ],
        out_specs=pl.BlockSpec((1, bsz, d), lambda l, j: (l, 0, j)),
        out_shape=jax.ShapeDtypeStruct((nl, bsz, d3), F32),
        compiler_params=_cparams(("arbitrary", "arbitrary")),
        name="ada",
    )(c, w, b.reshape(nl, 1, d3))


def _split_mod(m):
    b = m.shape[0]
    m = m.reshape(b, 3, 1, D_MODEL)
    return m[:, 0], m[:, 1], m[:, 2]


def _prenorm_mod(x, g, shift, scale):
    return _rms(x, g) * (1.0 + scale) + shift


def _even_in_kernel(x_ref, g_ref, sh_ref, sc_ref, w0_ref, qg_ref, wq_ref, kvg_ref, wkv_ref,
                    cq_ref, ck_ref, sn_ref,
                    qa_ref, ka_ref, va_ref, qb_ref, kb_ref, vb_ref):
    h = _prenorm_mod(x_ref[0], g_ref[...], sh_ref[0], sc_ref[0]).astype(BF16)
    proj = _dot(h, w0_ref[...])
    cqn = _rms(proj[:, 0:256], qg_ref[...]).astype(BF16)
    qq = _dot(cqn, wq_ref[...])
    ckvn = _rms(proj[:, 256:384], kvg_ref[...]).astype(BF16)
    kv = _dot(ckvn, wkv_ref[...])
    cq = cq_ref[...]
    ck = ck_ref[...]
    sn = sn_ref[...]
    krope = proj[:, 384:512] * ck + proj[:, 512:640] * sn
    nh = MLA_HEADS
    for hd in range(nh):
        lo = hd * LANES
        qh = qq[:, lo:lo + LANES] * cq + qq[:, nh * LANES + lo:nh * LANES + lo + LANES] * sn
        _store_key_tiles(qa_ref, hd, qh)
        ka_ref[0, :, lo:lo + LANES] = (kv[:, lo:lo + LANES] + krope).astype(BF16)
    for pr in range(nh // 2):
        _store_key_tiles(va_ref, pr, kv[:, nh * LANES + pr * LANES:nh * LANES + (pr + 1) * LANES])
    qb_ref[0] = proj[:, 640:1152].astype(BF16)
    kb_ref[0] = proj[:, 1152:1664].astype(BF16)
    vb_ref[0] = proj[:, 1664:2176].astype(BF16)


def _even_in(x, g, shift, scale, w0, qg, wq, kvg, wkv, cq, ck, sn):
    b, s, d = x.shape
    tm = TOK_TILE
    tkb = MLA_TILE
    ns = s // tm
    tok = lambda w: pl.BlockSpec((1, tm, w), lambda bi, i: (bi, i, 0))
    vec = lambda w: pl.BlockSpec((1, w), lambda bi, i: (0, 0))
    mod = pl.BlockSpec((1, 1, d), lambda bi, i: (bi, 0, 0))
    tab = pl.BlockSpec((tm, LANES), lambda bi, i: (i, 0))
    full = lambda a: _resident(a.shape, lambda bi, i: (0,) * a.ndim)
    out_shapes = (
        jax.ShapeDtypeStruct((b, MLA_HEADS, s // tkb, LANES, tkb), BF16),
        jax.ShapeDtypeStruct((b, s, MLA_HEADS * LANES), BF16),
        jax.ShapeDtypeStruct((b, MLA_HEADS // 2, s // tkb, LANES, tkb), BF16),
        jax.ShapeDtypeStruct((b, s, 512), BF16),
        jax.ShapeDtypeStruct((b, s, 512), BF16),
        jax.ShapeDtypeStruct((b, s, 512), BF16),
    )
    out_specs = (
        pl.BlockSpec((1, MLA_HEADS, tm // tkb, LANES, tkb), lambda bi, i: (bi, 0, i, 0, 0)),
        tok(MLA_HEADS * LANES),
        pl.BlockSpec((1, MLA_HEADS // 2, tm // tkb, LANES, tkb), lambda bi, i: (bi, 0, i, 0, 0)),
        tok(512), tok(512), tok(512),
    )
    return pl.pallas_call(
        _even_in_kernel,
        grid=(b, ns),
        in_specs=[tok(d), vec(d), mod, mod, full(w0), vec(MLA_Q_RANK), full(wq), vec(MLA_KV_RANK), full(wkv),
                  tab, tab, tab],
        out_specs=out_specs,
        out_shape=out_shapes,
        compiler_params=_cparams(("parallel", "parallel")),
        name="even_in",
    )(x, g, shift, scale, w0, qg, wq, kvg, wkv, cq, ck, sn)


def _store_key_tiles(kt_ref, hd, k):
    tkb = kt_ref.shape[4]
    for t in range(k.shape[0] // tkb):
        kt_ref[0, hd, t] = k[t * tkb:(t + 1) * tkb, :].T.astype(BF16)


def _odd_in_kernel(x_ref, g_ref, sh_ref, sc_ref, w_ref,
                   qdt_ref, kd_ref, vdt_ref, qs_ref, kst_ref, vs_ref):
    h = _prenorm_mod(x_ref[0], g_ref[...], sh_ref[0], sc_ref[0]).astype(BF16)
    proj = _dot(h, w_ref[...])
    kd_ref[0] = proj[:, 512:1024].astype(BF16)
    qs_ref[0] = proj[:, 1536:2048].astype(BF16)
    vs_ref[0] = proj[:, 2560:3072].astype(BF16)
    for hd in range(4):
        _store_key_tiles(qdt_ref, hd, proj[:, hd * LANES:(hd + 1) * LANES])
        _store_key_tiles(vdt_ref, hd, proj[:, 1024 + hd * LANES:1024 + (hd + 1) * LANES])
        _store_key_tiles(kst_ref, hd, proj[:, 2048 + hd * LANES:2048 + (hd + 1) * LANES])


def _odd_in(x, g, shift, scale, w):
    b, s, d = x.shape
    tm = TOK_TILE
    ns = s // tm
    tok = lambda wd: pl.BlockSpec((1, tm, wd), lambda bi, i: (bi, i, 0))
    mod = pl.BlockSpec((1, 1, d), lambda bi, i: (bi, 0, 0))
    ktspec = lambda tkb: pl.BlockSpec((1, 4, tm // tkb, LANES, tkb), lambda bi, i: (bi, 0, i, 0, 0))
    act = jax.ShapeDtypeStruct((b, s, 512), BF16)
    kts = lambda tkb: jax.ShapeDtypeStruct((b, 4, s // tkb, LANES, tkb), BF16)
    return pl.pallas_call(
        _odd_in_kernel,
        grid=(b, ns),
        in_specs=[tok(d), pl.BlockSpec((1, d), lambda bi, i: (0, 0)), mod, mod,
                  _resident(w.shape, lambda bi, i: (0, 0))],
        out_specs=(ktspec(DIFF_TILE), tok(512), ktspec(DIFF_TILE), tok(512), ktspec(SB_TILE), tok(512)),
        out_shape=(kts(DIFF_TILE), act, kts(DIFF_TILE), act, kts(SB_TILE), act),
        compiler_params=_cparams(("parallel", "parallel")),
        name="odd_in",
    )(x, g, shift, scale, w)


def _mla_kernel(qt_ref, k_ref, vt_ref, o_ref):
    tq = qt_ref.shape[4]
    tk = vt_ref.shape[4]
    hv = MLA_V
    qi = pl.program_id(2)
    causal = (lax.broadcasted_iota(jnp.int32, (tk, tq), 0)
              <= lax.broadcasted_iota(jnp.int32, (tk, tq), 1))
    qts = (qt_ref[0, 0, 0], qt_ref[0, 1, 0])

    def step(js, carry, masked):
        logits, values = [], []
        for hd in range(2):
            s_list = [_dot(k_ref[0, pl.ds(pl.multiple_of(j * tk, tk), tk), hd * LANES:(hd + 1) * LANES], qts[hd])
                      for j in js]
            if masked:
                s_list = [jnp.where(causal, s, NEG) for s in s_list]
            logits.append(s_list)
            values.append([vt_ref[0, 0, j, hd * hv:(hd + 1) * hv, :] for j in js])
        return _softmax_step_t(logits, values, carry)

    one = (jnp.full((1, tq), NEG, F32), jnp.zeros((1, tq), F32), jnp.zeros((hv, tq), F32))
    carry = _loop_pairs(0, qi, functools.partial(step, masked=False), (one, one), group=4)
    (_, l0, a0), (_, l1, a1) = step((qi,), carry, True)
    o_ref[0] = jnp.concatenate([a0 / l0, a1 / l1], axis=0).T.astype(BF16)


def _mla(qt, k, vt):
    b, s, _ = k.shape
    tq = qt.shape[4]
    nk = vt.shape[2]
    tk = vt.shape[4]
    return pl.pallas_call(
        _mla_kernel,
        grid=(b, MLA_HEADS // 2, s // tq),
        in_specs=[
            pl.BlockSpec((1, 2, 1, LANES, tq), lambda bi, hp, qi: (bi, hp, qi, 0, 0)),
            pl.BlockSpec((1, s, 2 * LANES), lambda bi, hp, qi: (bi, 0, hp)),
            pl.BlockSpec((1, 1, nk, LANES, tk), lambda bi, hp, qi: (bi, hp, 0, 0, 0)),
        ],
        out_specs=pl.BlockSpec((1, tq, LANES), lambda bi, hp, qi: (bi, qi, hp)),
        out_shape=jax.ShapeDtypeStruct((b, s, 512), BF16),
        compiler_params=_cparams(("parallel", "parallel", "arbitrary")),
        name="mla",
    )(qt, k, vt)


def _dil_kernel(q_ref, kc_ref, kp_ref, vc_ref, vp_ref, bias_ref, o_ref,
                q32, k32, v32, acc_s, m_s, d_s):
    sup = DIL_SUPER
    blk = DIL_BLOCK
    n = pl.program_id(2)
    q32[...] = q_ref[0].astype(F32)
    k32[0:sup, :] = kp_ref[0].astype(F32)
    k32[sup:2 * sup, :] = kc_ref[0].astype(F32)
    v32[0:sup, :] = vp_ref[0].astype(F32)
    v32[sup:2 * sup, :] = vc_ref[0].astype(F32)
    low = lax.broadcasted_iota(jnp.int32, (blk, LANES), 1) < 64
    before_start = jnp.where(lax.broadcasted_iota(jnp.int32, (blk, 2 * blk), 1) < blk, NEG, 0.0)

    for g, (_, dil) in enumerate(DIL_PATTERNS):

        def unit(u, carry, g=g, dil=dil):
            n_loc = u // dil
            r = u % dil
            qs = n_loc * (blk * dil) + r
            ks = sup + (n_loc - 1) * (blk * dil) + r
            if dil == 1:
                qsl = pl.ds(pl.multiple_of(qs, blk), blk)
                ksl = pl.ds(pl.multiple_of(ks, blk), 2 * blk)
            else:
                qsl = pl.ds(qs, blk, stride=dil)
                ksl = pl.ds(ks, 2 * blk, stride=dil)
            q = q32[qsl, :]
            k = k32[ksl, :].astype(BF16)
            v = v32[ksl, :].astype(BF16)
            extra = jnp.where(jnp.logical_and(n == 0, n_loc == 0), before_start, 0.0)
            parts = []
            for hd in range(2):
                qh = jnp.where(low if hd == 0 else jnp.logical_not(low), q, 0.0).astype(BF16)
                s = lax.dot_general(qh, k, (((1,), (1,)), ((), ())), preferred_element_type=F32)
                s = s + bias_ref[g, hd] + extra
                m = jnp.max(s, axis=-1, keepdims=True)
                e = jnp.exp2(s - m)
                den = jnp.sum(e, axis=-1, keepdims=True)
                parts.append((_dot(e.astype(BF16), v), m, den))
            acc_s[g, qsl, :] = jnp.where(low, parts[0][0], parts[1][0])
            m_s[g, qsl, :] = jnp.where(low, parts[0][1], parts[1][1])
            d_s[g, qsl, :] = jnp.where(low, parts[0][2], parts[1][2])
            return carry

        lax.fori_loop(0, 16, unit, 0, unroll=16)

    mx = jnp.maximum(jnp.maximum(m_s[0], m_s[1]), m_s[2])
    num = jnp.zeros((sup, LANES), F32)
    den = jnp.zeros((sup, LANES), F32)
    for g in range(3):
        a = jnp.exp2(m_s[g] - mx)
        num = num + a * acc_s[g]
        den = den + a * d_s[g]
    o_ref[0] = (num / den).astype(BF16)


def _dil(q, k, v, bias):
    b, s, _ = q.shape
    sup = DIL_SUPER
    cur = pl.BlockSpec((1, sup, LANES), lambda bi, hp, n: (bi, n, hp))
    prev = pl.BlockSpec((1, sup, LANES), lambda bi, hp, n: (bi, jnp.maximum(n - 1, 0), hp))
    return pl.pallas_call(
        _dil_kernel,
        grid=(b, DIL_HEADS // 2, s // sup),
        in_specs=[cur, cur, prev, cur, prev,
                  pl.BlockSpec((3, 2, DIL_BLOCK, 2 * DIL_BLOCK), lambda bi, hp, n: (0, hp, 0, 0))],
        out_specs=cur,
        out_shape=jax.ShapeDtypeStruct((b, s, 512), BF16),
        scratch_shapes=[
            pltpu.VMEM((sup, LANES), F32),
            pltpu.VMEM((2 * sup, LANES), F32),
            pltpu.VMEM((2 * sup, LANES), F32),
            pltpu.VMEM((3, sup, LANES), F32),
            pltpu.VMEM((3, sup, LANES), F32),
            pltpu.VMEM((3, sup, LANES), F32),
        ],
        compiler_params=_cparams(("parallel", "parallel", "arbitrary")),
        name="dilated",
    )(q, k, k, v, v, bias)


def _diff_kernel(qt_ref, k_ref, vt_ref, bias_ref, far_ref, lam_ref, g_ref, o_ref, *, lam_init):
    tq = qt_ref.shape[4]
    tk = vt_ref.shape[4]
    nd = bias_ref.shape[2]
    qi = pl.program_id(2)
    qt = qt_ref[0, 0, 0]
    row = lax.broadcasted_iota(jnp.int32, (LANES, tq), 0)
    zero = jnp.zeros_like(qt)
    qm = (jnp.where(row < DIFF_HD, qt, zero), jnp.where(row >= DIFF_HD, qt, zero))

    def step(js, carry, bias_of):
        keys = [k_ref[0, pl.ds(pl.multiple_of(j * tk, tk), tk), :] for j in js]
        vts = [vt_ref[0, 0, j] for j in js]
        logits = [[_dot(k, qm[mi]) + bias_of(mi, j) for j, k in zip(js, keys)] for mi in range(2)]
        return _softmax_step_t(logits, [vts, vts], carry)

    one = (jnp.full((1, tq), NEG, F32), jnp.zeros((1, tq), F32), jnp.zeros((LANES, tq), F32))
    carry = (one, one)
    n_far = jnp.maximum(qi - nd + 1, 0)
    carry = _loop_pairs(0, n_far, functools.partial(step, bias_of=lambda mi, j: far_ref[0, mi, 0:1, 0:1]), carry,
                        group=4)
    carry = _loop_pairs(n_far, qi, functools.partial(step, bias_of=lambda mi, j: bias_ref[0, mi, qi - j]), carry,
                        group=4)
    carry = step((qi,), carry, lambda mi, j: bias_ref[0, mi, 0])
    (_, l0, a0), (_, l1, a1) = carry
    o = (a0 / l0 - lam_ref[0:1, 0:1] * (a1 / l1)).T
    o_ref[0] = (_rms(o, g_ref[...]) * (1.0 - lam_init)).astype(BF16)


def _diff(qt, k, vt, bias, far, lam, sub_g, lam_init):
    b, s, _ = k.shape
    tq = qt.shape[4]
    nk, tk = vt.shape[2], vt.shape[4]
    nd = bias.shape[2]
    return pl.pallas_call(
        functools.partial(_diff_kernel, lam_init=lam_init),
        grid=(DIFF_HEADS, b, s // tq),
        in_specs=[
            pl.BlockSpec((1, 1, 1, LANES, tq), lambda h, bi, qi: (bi, h, qi, 0, 0)),
            pl.BlockSpec((1, s, LANES), lambda h, bi, qi: (bi, 0, h)),
            pl.BlockSpec((1, 1, nk, LANES, tk), lambda h, bi, qi: (bi, h, 0, 0, 0)),
            _resident((1, 2, nd, tk, tq), lambda h, bi, qi: (h, 0, 0, 0, 0)),
            pl.BlockSpec((1, 2, 8, LANES), lambda h, bi, qi: (h, 0, 0, 0)),
            pl.BlockSpec((1, LANES), lambda h, bi, qi: (0, 0)),
            pl.BlockSpec((1, LANES), lambda h, bi, qi: (0, 0)),
        ],
        out_specs=pl.BlockSpec((1, tq, LANES), lambda h, bi, qi: (bi, qi, h)),
        out_shape=jax.ShapeDtypeStruct((b, s, 512), BF16),
        compiler_params=_cparams(("parallel", "parallel", "arbitrary")),
        name="diff",
    )(qt, k, vt, bias, far, lam, sub_g)


def _sb_kernel(q_ref, kt_ref, v_ref, o_ref):
    tq = q_ref.shape[1]
    tk = kt_ref.shape[4]
    qi = pl.program_id(2)
    lane = lax.broadcasted_iota(jnp.int32, (tq, LANES), 1)
    strict = (lax.broadcasted_iota(jnp.int32, (tq, tk), 1)
              < lax.broadcasted_iota(jnp.int32, (tq, tk), 0))
    later = (lax.broadcasted_iota(jnp.int32, (tk, tk), 0)
             > lax.broadcasted_iota(jnp.int32, (tk, tk), 1)).astype(BF16)
    q = q_ref[0]
    zero = jnp.zeros_like(q)
    qh = (jnp.where(lane < 64, q, zero), jnp.where(lane >= 64, q, zero))

    def blocks(js, state, masked):
        items = [(bi, hd) for bi in range(len(js)) for hd in range(2)]
        z = {it: _dot(qh[it[1]], kt_ref[0, 0, js[it[0]]]) for it in items}
        log_sig, log_1m, inblock = {}, {}, {}
        for it in items:
            neg = -z[it]
            t = jnp.log2(1.0 + jnp.exp2(jnp.minimum(z[it], neg)))
            log_sig[it] = jnp.minimum(z[it], 0.0) - t
            l1m = jnp.minimum(neg, 0.0) - t
            if masked:
                l1m = jnp.where(strict, l1m, 0.0)
            log_1m[it] = l1m
            inblock[it] = _dot(l1m.astype(BF16), later)
        state = list(state)
        for bi, j in enumerate(js):
            v = v_ref[0, pl.ds(pl.multiple_of(j * tk, tk), tk), :]
            for hd in range(2):
                it = (bi, hd)
                c, acc = state[hd]
                w = jnp.exp2(log_sig[it] + (inblock[it] + c))
                if masked:
                    w = jnp.where(strict, w, 0.0)
                acc = acc + _dot(w.astype(BF16), v)
                c = c + jnp.sum(log_1m[it], axis=-1, keepdims=True)
                state[hd] = (c, acc)
        return tuple(state)

    one = (jnp.zeros((tq, 1), F32), jnp.zeros((tq, LANES), F32))
    state = blocks((qi,), (one, one), True)

    odd = qi % 2
    state = lax.cond(odd == 1, lambda st: blocks((qi - 1,), st, False), lambda st: st, state)
    floor = SB_LOG_FLOOR * LOG2E

    def cond(st):
        j, ((c0, _), (c1, _)) = st
        return jnp.logical_and(j >= 1, jnp.max(jnp.maximum(c0, c1)) > floor)

    def body(st):
        j, state = st
        return j - 2, blocks((j, j - 1), state, False)

    _, ((_, a0), (_, a1)) = lax.while_loop(cond, body, (qi - 1 - odd, state))
    o_ref[0] = jnp.where(lane < 64, a0, a1).astype(BF16)


def _sb(q, kt, v):
    b, s, _ = q.shape
    tq = SB_TILE
    nk, tk = kt.shape[2], kt.shape[4]
    return pl.pallas_call(
        _sb_kernel,
        grid=(b, SB_HEADS // 2, s // tq),
        in_specs=[
            pl.BlockSpec((1, tq, LANES), lambda bi, hp, qi: (bi, qi, hp)),
            pl.BlockSpec((1, 1, nk, LANES, tk), lambda bi, hp, qi: (bi, hp, 0, 0, 0)),
            pl.BlockSpec((1, s, LANES), lambda bi, hp, qi: (bi, 0, hp)),
        ],
        out_specs=pl.BlockSpec((1, tq, LANES), lambda bi, hp, qi: (bi, qi, hp)),
        out_shape=jax.ShapeDtypeStruct((b, s, 512), BF16),
        compiler_params=_cparams(("parallel", "parallel", "arbitrary")),
        name="stick_breaking",
    )(q, kt, v)


FF_CHUNKS = ((0, 768), (768, 1536), (1536, 2304), (2304, 2816))
EXPERT_CHUNKS = ((0, 1024), (1024, 2048), (2048, 3072), (3072, 3584))


def _swiglu(hb, wg_ref, wu_ref, wd_ref, chunks, lead, between=None):
    acc = None
    hb_of = hb if callable(hb) else (lambda ci: hb)
    for ci, (c0, c1) in enumerate(chunks):
        hb = hb_of(ci)
        g = _dot(hb, wg_ref[lead + (slice(None), slice(c0, c1))])
        u = _dot(hb, wu_ref[lead + (slice(None), slice(c0, c1))])
        a = (g / (1.0 + jnp.exp(-g)) * u).astype(BF16)
        part = _dot(a, wd_ref[lead + (slice(c0, c1), slice(None))])
        acc = part if acc is None else acc + part
        if between is not None:
            between(ci)
    return acc


def _mix_out(x_ref, oa_ref, ob_ref, wa_ref, wb_ref, gate_ref, pg_ref):
    y = _dot(oa_ref[0], wa_ref[...]) + _dot(ob_ref[0], wb_ref[...])
    return x_ref[0] + gate_ref[0] * _rms(y, pg_ref[...])


def _post_even_kernel(x_ref, oa_ref, ob_ref, wa_ref, wb_ref, gate_ref, pg_ref,
                      fg_ref, fsh_ref, fsc_ref, fgate_ref, fpg_ref, wg_ref, wu_ref, wd_ref, o_ref):
    x1 = _mix_out(x_ref, oa_ref, ob_ref, wa_ref, wb_ref, gate_ref, pg_ref)
    hb = _prenorm_mod(x1, fg_ref[...], fsh_ref[0], fsc_ref[0]).astype(BF16)
    y = _swiglu(hb, wg_ref, wu_ref, wd_ref, FF_CHUNKS, ())
    o_ref[0] = x1 + fgate_ref[0] * _rms(y, fpg_ref[...])


def _post_even(x, oa, ob, wa, wb, gate, pg, fg, fsh, fsc, fgate, fpg, wg, wu, wd):
    b, s, d = x.shape
    tm = TOK_TILE
    tok = lambda w: pl.BlockSpec((1, tm, w), lambda bi, i: (bi, i, 0))
    vec = pl.BlockSpec((1, d), lambda bi, i: (0, 0))
    mod = pl.BlockSpec((1, 1, d), lambda bi, i: (bi, 0, 0))
    full = lambda a: _resident(a.shape, lambda bi, i: (0,) * a.ndim)
    return pl.pallas_call(
        _post_even_kernel,
        grid=(b, s // tm),
        in_specs=[tok(d), tok(512), tok(512), full(wa), full(wb), mod, vec,
                  vec, mod, mod, mod, vec, full(wg), full(wu), full(wd)],
        out_specs=tok(d),
        out_shape=jax.ShapeDtypeStruct((b, s, d), F32),
        compiler_params=_cparams(("parallel", "parallel")),
        name="post_even",
    )(x, oa, ob, wa, wb, gate, pg, fg, fsh, fsc, fgate, fpg, wg, wu, wd)


def _post_odd_kernel(x_ref, oa_ref, ob_ref, wa_ref, wb_ref, gate_ref, pg_ref,
                     fg_ref, fsh_ref, fsc_ref, rw_ref, rb_ref, x_out, h_out, r_out, cnt_ref):
    x1 = _mix_out(x_ref, oa_ref, ob_ref, wa_ref, wb_ref, gate_ref, pg_ref)
    x_out[0] = x1
    h = _prenorm_mod(x1, fg_ref[...], fsh_ref[0], fsc_ref[0])
    h_out[0] = h
    logits = _dot(h, rw_ref[...]) + rb_ref[...]
    lane = lax.broadcasted_iota(jnp.int32, logits.shape, 1)
    m1 = jnp.max(logits, axis=-1, keepdims=True)
    i1 = jnp.min(jnp.where(logits == m1, lane, LANES), axis=-1, keepdims=True)
    rest = jnp.where(lane == i1, NEG, logits)
    m2 = jnp.max(rest, axis=-1, keepdims=True)
    i2 = jnp.min(jnp.where(rest == m2, lane, LANES), axis=-1, keepdims=True)
    e2 = jnp.exp(m2 - m1)
    w1 = 1.0 / (1.0 + e2)
    w2 = e2 / (1.0 + e2)
    @pl.when(jnp.logical_and(pl.program_id(0) == 0, pl.program_id(1) == 0))
    def _():
        cnt_ref[...] = jnp.zeros_like(cnt_ref)

    tm = logits.shape[0]
    sel = jnp.logical_or(lane == i1, lane == i2)
    earlier = (lax.broadcasted_iota(jnp.int32, (tm, tm), 1)
               < lax.broadcasted_iota(jnp.int32, (tm, tm), 0)).astype(BF16)
    prefix = _dot(earlier, sel.astype(BF16)) + cnt_ref[0:1, :]
    rank1 = jnp.sum(jnp.where(lane == i1, prefix, 0.0), axis=-1, keepdims=True)
    rank2 = jnp.sum(jnp.where(lane == i2, prefix, 0.0), axis=-1, keepdims=True)
    cnt_ref[...] = cnt_ref[...] + jnp.sum(sel.astype(F32), axis=0, keepdims=True)
    r = jnp.where(lane == 0, i1.astype(F32), 0.0)
    r = jnp.where(lane == 1, i2.astype(F32), r)
    r = jnp.where(lane == 2, w1, r)
    r = jnp.where(lane == 3, w2, r)
    r = jnp.where(lane == 4, rank1, r)
    r = jnp.where(lane == 5, rank2, r)
    r_out[0] = r


def _post_odd(x, oa, ob, wa, wb, gate, pg, fg, fsh, fsc, rw, rb):
    b, s, d = x.shape
    tm = TOK_TILE
    tok = lambda w: pl.BlockSpec((1, tm, w), lambda bi, i: (bi, i, 0))
    vec = pl.BlockSpec((1, d), lambda bi, i: (0, 0))
    mod = pl.BlockSpec((1, 1, d), lambda bi, i: (bi, 0, 0))
    full = lambda a: _resident(a.shape, lambda bi, i: (0,) * a.ndim)
    return pl.pallas_call(
        _post_odd_kernel,
        grid=(b, s // tm),
        in_specs=[tok(d), tok(512), tok(512), full(wa), full(wb), mod, vec,
                  vec, mod, mod, full(rw), pl.BlockSpec((1, LANES), lambda bi, i: (0, 0))],
        out_specs=(tok(d), tok(d), tok(LANES),
                   pl.BlockSpec((8, LANES), lambda bi, i: (0, 0))),
        out_shape=(jax.ShapeDtypeStruct((b, s, d), F32), jax.ShapeDtypeStruct((b, s, d), F32),
                   jax.ShapeDtypeStruct((b, s, LANES), F32), jax.ShapeDtypeStruct((8, LANES), F32)),
        compiler_params=_cparams(("arbitrary", "arbitrary")),
        name="post_odd",
    )(x, oa, ob, wa, wb, gate, pg, fg, fsh, fsc, rw, rb)


def _store_rows(ref, lead, val):
    for c in range(ROW_SUB):
        ref[lead + (slice(None), c, slice(None))] = val[:, c * LANES:(c + 1) * LANES]


def _load_rows(ref, lead, start, size):
    return jnp.concatenate([ref[lead, pl.ds(start, size), c, :] for c in range(ROW_SUB)], axis=1)


class _TileGather:
    def __init__(self, idx_hbm, src_hbm, idx_smem, buf, isem, sem, tile_rows):
        self.idx_hbm, self.src_hbm, self.idx_smem, self.buf = idx_hbm, src_hbm, idx_smem, buf
        self.isem, self.sem, self.tile_rows = isem, sem, tile_rows
        self.i = pl.program_id(0)
        self.nt = pl.num_programs(0)
        self.n = buf.shape[1]
        self.slot = self.i % 2
        self.nxt = 1 - self.slot

    def _idx_copy(self, t, sl):
        t = jnp.minimum(t, self.nt - 1)
        return pltpu.make_async_copy(self.idx_hbm.at[t], self.idx_smem.at[sl], self.isem.at[sl])

    def _rows_wait(self, sl):
        pltpu.make_async_copy(self.src_hbm.at[pl.ds(0, self.n)], self.buf.at[sl], self.sem.at[sl]).wait()

    def _issue_row(self, sl, r):
        t = self.idx_smem[sl, r]
        if self.tile_rows:
            src, dst = self.src_hbm.at[t], self.buf.at[sl, r]
        else:
            src, dst = self.src_hbm.at[pl.ds(t, 1)], self.buf.at[sl, pl.ds(r, 1)]
        pltpu.make_async_copy(src, dst, self.sem.at[sl]).start()

    def _issue_loop(self, sl):
        def body(r, carry):
            self._issue_row(sl, r)
            return carry

        lax.fori_loop(0, self.n, body, 0, unroll=8)

    def begin(self):
        @pl.when(self.i == 0)
        def _():
            first = self._idx_copy(0, 0)
            first.start()
            first.wait()
            self._issue_loop(0)
            self._idx_copy(1, 1).start()

        self._idx_copy(self.i + 1, self.nxt).wait()
        self._idx_copy(self.i + 2, self.slot).start()
        self._rows_wait(self.slot)

    def issue(self, part, parts):
        per = -(-self.n // parts)
        for r in range(part * per, min((part + 1) * per, self.n)):
            self._issue_row(self.nxt, r)

    def issue_all(self):
        self._issue_loop(self.nxt)

    def anchor(self, zero):
        return zero

    def finish(self):
        @pl.when(self.i == self.nt - 1)
        def _():
            self._rows_wait(self.nxt)
            self._idx_copy(self.i + 2, self.slot).wait()


def _moe_kernel(te_ref, nu_ref, zero_ref, tok_hbm, h_hbm, wg_ref, wu_ref, wd_ref, y_ref, idx_smem, buf, isem, sem):
    g = _TileGather(tok_hbm, h_hbm, idx_smem, buf, isem, sem, tile_rows=False)
    n = g.n
    g.begin()

    @pl.when(g.i < nu_ref[0])
    def _():
        anchor = [0]
        head = 16
        rest = buf[g.slot, head:n, :].astype(BF16)

        def rows_of(ci):
            first = buf[g.slot, pl.ds(pl.multiple_of(anchor[0], head), head), :].astype(BF16)
            return jnp.concatenate([first, rest], axis=0)

        def issue_part(ci):
            if ci < len(EXPERT_CHUNKS) - 1:
                g.issue(ci, len(EXPERT_CHUNKS) - 1)
                anchor[0] = g.anchor(zero_ref[0])

        y = _swiglu(rows_of, wg_ref, wu_ref, wd_ref, EXPERT_CHUNKS, (0,), between=issue_part)
        _store_rows(y_ref, (), y)

    @pl.when(g.i >= nu_ref[0])
    def _():
        g.issue_all()
        y_ref[...] = jnp.zeros_like(y_ref)

    g.finish()


def _moe(tile_expert, n_used, row_token, h, wg, wu, wd):
    d = h.shape[1]
    nt, tm = row_token.shape
    dff = wg.shape[2]
    wspec = lambda shp: pl.BlockSpec(shp, lambda i, te, nu, z: (te[i], 0, 0), pipeline_mode=pl.Buffered(1))
    grid_spec = pltpu.PrefetchScalarGridSpec(
        num_scalar_prefetch=3,
        grid=(nt,),
        in_specs=[
            pl.BlockSpec(memory_space=pl.ANY),
            pl.BlockSpec(memory_space=pl.ANY),
            wspec((1, d, dff)), wspec((1, d, dff)), wspec((1, dff, d)),
        ],
        out_specs=pl.BlockSpec((tm, ROW_SUB, LANES), lambda i, te, nu, z: (i, 0, 0)),
        scratch_shapes=[
            pltpu.SMEM((2, tm), jnp.int32),
            pltpu.VMEM((2, tm, d), F32),
            pltpu.SemaphoreType.DMA((2,)),
            pltpu.SemaphoreType.DMA((2,)),
        ],
    )
    return pl.pallas_call(
        _moe_kernel,
        grid_spec=grid_spec,
        out_shape=jax.ShapeDtypeStruct((nt * tm, ROW_SUB, LANES), F32),
        compiler_params=_cparams(("arbitrary",)),
        name="moe_experts",
    )(tile_expert, n_used, jnp.zeros((1,), jnp.int32), row_token, h, wg, wu, wd)


def _combine_kernel(zero_ref, pos_hbm, y_hbm, x_ref, r_ref, gate_ref, pg_ref, o_ref, idx_smem, buf, isem, sem):
    tm = x_ref.shape[0]
    g = _TileGather(pos_hbm, y_hbm, idx_smem, buf, isem, sem, tile_rows=True)
    g.begin()
    parts = 4
    rows = tm // parts
    off = 0
    for c in range(parts):
        lo = c * rows
        first = _load_rows(buf, g.slot, pl.multiple_of(off + lo, 8), rows)
        second = _load_rows(buf, g.slot, pl.multiple_of(off + tm + lo, 8), rows)
        r = r_ref[lo:lo + rows, :]
        y = r[:, 2:3] * first + r[:, 3:4] * second
        o_ref[lo:lo + rows, :] = x_ref[lo:lo + rows, :] + gate_ref[0] * _rms(y, pg_ref[...])
        if c < parts - 1:
            g.issue(c, parts - 1)
            off = g.anchor(zero_ref[0])
    g.finish()


def _combine(pos, y, x, r, gate, pg, tokens_per_seq):
    n_tok, d = x.shape
    nt, tm2 = pos.shape
    tm = tm2 // 2
    per_seq = tokens_per_seq // tm
    tok = lambda w: pl.BlockSpec((tm, w), lambda i, z: (i, 0))
    grid_spec = pltpu.PrefetchScalarGridSpec(
        num_scalar_prefetch=1,
        grid=(nt,),
        in_specs=[
            pl.BlockSpec(memory_space=pl.ANY),
            pl.BlockSpec(memory_space=pl.ANY),
            tok(d), tok(LANES),
            pl.BlockSpec((1, 1, d), lambda i, z: (i // per_seq, 0, 0)),
            pl.BlockSpec((1, d), lambda i, z: (0, 0)),
        ],
        out_specs=tok(d),
        scratch_shapes=[
            pltpu.SMEM((2, tm2), jnp.int32),
            pltpu.VMEM((2, tm2, ROW_SUB, LANES), F32),
            pltpu.SemaphoreType.DMA((2,)),
            pltpu.SemaphoreType.DMA((2,)),
        ],
    )
    return pl.pallas_call(
        _combine_kernel,
        grid_spec=grid_spec,
        out_shape=jax.ShapeDtypeStruct((n_tok, d), F32),
        compiler_params=_cparams(("arbitrary",)),
        name="moe_combine",
    )(jnp.zeros((1,), jnp.int32), pos, y, x, r, gate, pg)


def _t5_bucket(dist):
    max_exact = REL_BUCKETS // 2
    d = jnp.maximum(dist, 1).astype(F32)
    log_b = max_exact + (jnp.log(d / max_exact) / math.log(REL_MAX_DIST / max_exact)
                         * (REL_BUCKETS - max_exact)).astype(jnp.int32)
    log_b = jnp.minimum(log_b, REL_BUCKETS - 1)
    return jnp.where(dist < max_exact, dist, log_b)


def _rope_tables(s):
    half = MLA_ROPE // 2
    freqs = ROPE_THETA ** (-jnp.arange(half, dtype=F32) / half)
    ang = jnp.arange(s, dtype=F32)[:, None] * freqs[None, :]
    cos, sin = jnp.cos(ang), jnp.sin(ang)
    z64 = jnp.zeros((s, MLA_NOPE), F32)
    z32 = jnp.zeros((s, LANES - MLA_NOPE - MLA_ROPE), F32)
    ck = jnp.concatenate([z64, cos, cos, z32], axis=1)
    cq = jnp.concatenate([jnp.ones((s, MLA_NOPE), F32), cos, cos, z32], axis=1)
    sn = jnp.concatenate([z64, sin, sin, z32], axis=1)
    return cq, ck, sn


def _even_weights(w_in, w_uq, w_ukv):
    d = w_in.shape[0]
    half = MLA_ROPE // 2
    w_cq = w_in[:, :MLA_Q_RANK]
    w_ckv = w_in[:, MLA_Q_RANK:MLA_Q_RANK + MLA_KV_RANK]
    w_kr = w_in[:, MLA_Q_RANK + MLA_KV_RANK:MLA_Q_RANK + MLA_KV_RANK + MLA_ROPE]
    w_qkv = w_in[:, MLA_Q_RANK + MLA_KV_RANK + MLA_ROPE:]
    z = lambda n: jnp.zeros((d, n), F32)
    kr_a = jnp.concatenate([z(MLA_NOPE), w_kr, z(32)], axis=1)
    kr_b = jnp.concatenate([z(MLA_NOPE), -w_kr[:, half:], w_kr[:, :half], z(32)], axis=1)
    dil_scale = DIL_HD ** -0.5 * LOG2E
    w0 = jnp.concatenate([w_cq, w_ckv, kr_a, kr_b, w_qkv[:, :512] * dil_scale, w_qkv[:, 512:]], axis=1)

    r = w_uq.shape[0]
    wq = w_uq.reshape(r, MLA_HEADS, MLA_NOPE + MLA_ROPE) * ((MLA_NOPE + MLA_ROPE) ** -0.5 * LOG2E)
    zq = lambda n: jnp.zeros((r, MLA_HEADS, n), F32)
    nope, x1, x2 = wq[..., :MLA_NOPE], wq[..., MLA_NOPE:MLA_NOPE + half], wq[..., MLA_NOPE + half:]
    q_a = jnp.concatenate([nope, x1, x2, zq(32)], axis=-1).reshape(r, MLA_HEADS * LANES)
    q_b = jnp.concatenate([zq(MLA_NOPE), -x2, x1, zq(32)], axis=-1).reshape(r, MLA_HEADS * LANES)
    wq2 = jnp.concatenate([q_a, q_b], axis=1)

    rk = w_ukv.shape[0]
    wkv = w_ukv.reshape(rk, MLA_HEADS, MLA_NOPE + MLA_V)
    k_blk = jnp.concatenate([wkv[..., :MLA_NOPE], jnp.zeros((rk, MLA_HEADS, LANES - MLA_NOPE), F32)], axis=-1)
    wkv2 = jnp.concatenate([k_blk.reshape(rk, MLA_HEADS * LANES),
                            wkv[..., MLA_NOPE:].reshape(rk, MLA_HEADS * MLA_V)], axis=1)
    return w0.astype(BF16), wq2.astype(BF16), wkv2.astype(BF16)


def _toeplitz(vec, rows, cols):
    n, width = vec.shape

    def toeplitz_kernel(v_ref, o_ref):
        tiled = jnp.broadcast_to(v_ref[0], (rows, width))
        o_ref[0] = pltpu.roll(tiled, 0, 1, stride=1, stride_axis=0)[:, :cols]

    return pl.pallas_call(
        toeplitz_kernel,
        grid=(n,),
        in_specs=[pl.BlockSpec((1, 1, width), lambda t: (t, 0, 0))],
        out_specs=pl.BlockSpec((1, rows, cols), lambda t: (t, 0, 0)),
        out_shape=jax.ShapeDtypeStruct((n, rows, cols), F32),
        compiler_params=_cparams(("parallel",)),
        name="toeplitz_bias",
    )(vec.reshape(n, 1, width).astype(F32))


def _dil_bias(rel_bias):
    blk = DIL_BLOCK
    width = 4 * blk
    k = jnp.arange(width)
    rel = jnp.where(k < 2 * blk, blk - k, blk + width - k)
    out = []
    for window, dil in DIL_PATTERNS:
        band = (rel >= 0) & (rel <= window // dil)
        bias = rel_bias[_t5_bucket(jnp.maximum(rel, 0) * dil)] * LOG2E
        out.append(jnp.where(band[:, None], bias, NEG).T)
    vec = jnp.stack(out).reshape(3 * DIL_HEADS, width)
    return _toeplitz(vec, blk, 2 * blk).reshape(3, DIL_HEADS, blk, 2 * blk)


def _diff_bias(rel_bias, tile):
    nd = REL_MAX_DIST // tile + 1
    maps = rel_bias.shape[1]
    k = jnp.arange(2 * tile)[None, :]
    dist = jnp.arange(nd)[:, None] * tile + jnp.where(k < tile, k, k - 2 * tile)
    vec = jnp.where((dist >= 0)[..., None], rel_bias[_t5_bucket(jnp.maximum(dist, 0))] * LOG2E, NEG)
    vec = jnp.transpose(vec, (2, 0, 1)).reshape(maps * nd, 2 * tile)
    bias = _toeplitz(vec, tile, tile).reshape(DIFF_HEADS, 2, nd, tile, tile)
    far = rel_bias[_t5_bucket(jnp.array(REL_MAX_DIST))] * LOG2E
    far = jnp.broadcast_to(far.reshape(DIFF_HEADS, 2, 1, 1), (DIFF_HEADS, 2, 8, LANES))
    return bias, far.astype(F32)


def _routing(r, counts, n_tok, tile):
    n_tiles = (2 * n_tok) // tile + N_EXPERTS
    e = jnp.concatenate([r[:, 0], r[:, 1]]).astype(jnp.int32)
    rank = jnp.concatenate([r[:, 4], r[:, 5]]).astype(jnp.int32)
    counts = counts.astype(jnp.int32)
    padded = ((counts + tile - 1) // tile) * tile
    ends = jnp.cumsum(padded)
    starts = ends - padded
    onehot = (e[:, None] == jnp.arange(N_EXPERTS)[None, :]).astype(jnp.int32)
    pos = jnp.sum(onehot * starts[None, :], axis=1) + rank
    token = jnp.tile(jnp.arange(n_tok, dtype=jnp.int32), 2)
    row_token = jnp.zeros((n_tiles * tile,), jnp.int32).at[pos].set(token)
    tile_start = jnp.arange(n_tiles, dtype=jnp.int32) * tile
    tile_expert = jnp.sum((tile_start[:, None] >= ends[None, :]).astype(jnp.int32), axis=1)
    n_used = (ends[-1] // tile).astype(jnp.int32)
    last = jnp.sum((ends[-1] - 1 >= ends).astype(jnp.int32))
    tile_expert = jnp.minimum(tile_expert, last).astype(jnp.int32)
    return (tile_expert, n_used.reshape(1), row_token.reshape(n_tiles, tile),
            pos[:n_tok].astype(jnp.int32), pos[n_tok:].astype(jnp.int32))


def kernel(x, c, rel_bias, ada_mix_w, ada_mix_b, mix_pre_g, mix_post_g, ada_ffn_w, ada_ffn_b, ffn_pre_g, ffn_post_g, e_w_in, e_q_norm_g, e_w_uq, e_kv_norm_g, e_w_ukv, e_w_out, ffn_w_gate, ffn_w_up, ffn_w_down, o_w_in, diff_lq1, diff_lk1, diff_lq2, diff_lk2, diff_sub_g, o_w_out, router_w, router_b, moe_w_gate, moe_w_up, moe_w_down):
    b, s, d = x.shape
    assert d == D_MODEL and s % DIL_SUPER == 0 and s % TOK_TILE == 0
    row = lambda v: v.reshape(1, -1).astype(F32)

    mix_mod = _ada(c, ada_mix_w, ada_mix_b)
    ffn_mod = _ada(c, ada_ffn_w, ada_ffn_b)

    shift, scale, gate = _split_mod(mix_mod[0])
    w0, wq2, wkv2 = _even_weights(e_w_in[0], e_w_uq[0], e_w_ukv[0])
    cq, ck, sn = _rope_tables(s)
    qa, kta, va, qb, kb, vb = _even_in(x, row(mix_pre_g[0]), shift, scale, w0, row(e_q_norm_g[0]), wq2,
                                       row(e_kv_norm_g[0]), wkv2, cq, ck, sn)
    o_a = _mla(qa, kta, va)
    o_b = _dil(qb, kb, vb, _dil_bias(rel_bias))
    fshift, fscale, fgate = _split_mod(ffn_mod[0])
    w_out = e_w_out[0].astype(BF16)
    x = _post_even(x, o_a, o_b, w_out[:512], w_out[512:], gate, row(mix_post_g[0]),
                   row(ffn_pre_g[0]), fshift, fscale, fgate, row(ffn_post_g[0]),
                   ffn_w_gate[0].astype(BF16), ffn_w_up[0].astype(BF16), ffn_w_down[0].astype(BF16))

    layer = 1
    shift, scale, gate = _split_mod(mix_mod[1])
    w_in = o_w_in[0]
    att_scale = DIFF_HD ** -0.5
    w1 = jnp.concatenate([w_in[:, :512] * (att_scale * LOG2E), w_in[:, 512:1536],
                          w_in[:, 1536:2048] * (SB_HD ** -0.5 * LOG2E), w_in[:, 2048:]], axis=1).astype(BF16)
    qd, kdt, vd, qs, kst, vs = _odd_in(x, row(mix_pre_g[1]), shift, scale, w1)
    lam_init = 0.8 - 0.6 * math.exp(-0.3 * layer)
    lam = (jnp.exp(jnp.sum(diff_lq1[0].astype(F32) * diff_lk1[0].astype(F32)))
           - jnp.exp(jnp.sum(diff_lq2[0].astype(F32) * diff_lk2[0].astype(F32))) + lam_init)
    bias, far = _diff_bias(rel_bias, DIFF_TILE)
    o_c = _diff(qd, kdt, vd, bias, far, jnp.full((1, LANES), lam, F32), row(diff_sub_g[0]), lam_init)
    o_d = _sb(qs, kst, vs)
    fshift, fscale, fgate = _split_mod(ffn_mod[1])
    w_out = o_w_out[0].astype(BF16)
    rw = jnp.zeros((d, LANES), F32).at[:, :N_EXPERTS].set(router_w[0].astype(F32))
    rb = jnp.full((1, LANES), NEG, F32).at[0, :N_EXPERTS].set(router_b[0].astype(F32))
    x, h, r, counts = _post_odd(x, o_c, o_d, w_out[:512], w_out[512:], gate, row(mix_post_g[1]),
                                row(ffn_pre_g[1]), fshift, fscale, rw, rb)

    n_tok = b * s
    tile_expert, n_used, row_token, pos0, pos1 = _routing(r.reshape(n_tok, LANES), counts[0, :N_EXPERTS],
                                                          n_tok, MOE_TILE)
    y = _moe(tile_expert, n_used, row_token, h.reshape(n_tok, d),
             moe_w_gate[0].astype(BF16), moe_w_up[0].astype(BF16), moe_w_down[0].astype(BF16))
    ct = TOK_TILE
    pos = jnp.concatenate([pos0.reshape(n_tok // ct, ct), pos1.reshape(n_tok // ct, ct)], axis=1)
    out = _combine(pos, y, x.reshape(n_tok, d), r.reshape(n_tok, LANES), fgate, row(ffn_post_g[1]), s)
    return out.reshape(b, s, d)
```

```python
import functools
import math

import jax
import jax.numpy as jnp
from jax import lax
from jax.experimental import pallas as pl
from jax.experimental.pallas import tpu as pltpu

F32 = jnp.float32
BF16 = jnp.bfloat16

D_MODEL = 1024
EPS = 1e-6

MLA_HEADS = 8
MLA_NOPE = 64
MLA_ROPE = 32
MLA_V = 64
MLA_Q_RANK = 256
MLA_KV_RANK = 128
ROPE_THETA = 10000.0

DIL_HEADS = 8
DIL_HD = 64
DIL_PATTERNS = ((128, 1), (512, 4), (2048, 16))
DIL_BLOCK = 128

DIFF_HEADS = 4
DIFF_HD = 64
SB_HEADS = 8
SB_HD = 64

REL_BUCKETS = 32
REL_MAX_DIST = 2048

D_FF = 2816
N_EXPERTS = 8
D_FF_EXPERT = 3584

LANES = 128
ROW_SUB = D_MODEL // LANES
LOG2E = math.log2(math.e)
NEG = -1e30

TOK_TILE = 512
MLA_TILE = 512
DIFF_TILE = 512
SB_TILE = 256
DIL_SUPER = DIL_BLOCK * 16
MOE_TILE = 512
SB_LOG_FLOOR = -104.0

VMEM_LIMIT = 56 * 1024 * 1024


def _cparams(sem):
    return pltpu.CompilerParams(dimension_semantics=sem, vmem_limit_bytes=VMEM_LIMIT)


def _resident(shape, index_map):
    return pl.BlockSpec(shape, index_map, pipeline_mode=pl.Buffered(1))


def _rms(x, g):
    return x * lax.rsqrt(jnp.mean(x * x, axis=-1, keepdims=True) + EPS) * g


def _dot(a, b):
    return jnp.dot(a, b, preferred_element_type=F32)


def _softmax_step_t(logits, values, carry):
    out = []
    for s_list, v_list, (m, l, acc) in zip(logits, values, carry):
        for s, vt in zip(s_list, v_list):
            m_new = jnp.maximum(m, jnp.max(s, axis=0, keepdims=True))
            alpha = jnp.exp2(m - m_new)
            p = jnp.exp2(s - m_new)
            l = alpha * l + jnp.sum(p, axis=0, keepdims=True)
            acc = alpha * acc + _dot(vt, p.astype(BF16))
            m = m_new
        out.append((m, l, acc))
    return tuple(out)


def _loop_pairs(lo, hi, step, carry, group=2):
    n = hi - lo
    carry = lax.fori_loop(
        0, n // group, lambda i, c: step(tuple(lo + group * i + g for g in range(group)), c), carry)
    size = group // 2
    while size >= 1:
        start = lo + (n // (2 * size)) * (2 * size)
        carry = lax.cond((n // size) % 2 == 1,
                         lambda c, start=start, size=size: step(tuple(start + g for g in range(size)), c),
                         lambda c: c, carry)
        size //= 2
    return carry


def _ada_kernel(c_ref, w_ref, b_ref, o_ref):
    c = c_ref[...]
    sc = c / (1.0 + jnp.exp(-c))
    o_ref[0] = _dot(sc.astype(BF16), w_ref[0].astype(BF16)) + b_ref[0]


def _ada(c, w, b):
    nl, d, d3 = w.shape
    bsz = c.shape[0]
    nb = d3 // d
    return pl.pallas_call(
        _ada_kernel,
        grid=(nl, nb),
        in_specs=[
            pl.BlockSpec((bsz, d), lambda l, j: (0, 0)),
            pl.BlockSpec((1, d, d), lambda l, j: (l, 0, j)),
            pl.BlockSpec((1, 1, d), lambda l, j: (l, 0, j)),
        ],
        out_specs=pl.BlockSpec((1, bsz, d), lambda l, j: (l, 0, j)),
        out_shape=jax.ShapeDtypeStruct((nl, bsz, d3), F32),
        compiler_params=_cparams(("arbitrary", "arbitrary")),
        name="ada",
    )(c, w, b.reshape(nl, 1, d3))


def _split_mod(m):
    b = m.shape[0]
    m = m.reshape(b, 3, 1, D_MODEL)
    return m[:, 0], m[:, 1], m[:, 2]


def _prenorm_mod(x, g, shift, scale):
    return _rms(x, g) * (1.0 + scale) + shift


def _even_in_kernel(x_ref, g_ref, sh_ref, sc_ref, w0_ref, qg_ref, wq_ref, kvg_ref, wkv_ref,
                    cq_ref, ck_ref, sn_ref,
                    qa_ref, ka_ref, va_ref, qb_ref, kb_ref, vb_ref):
    h = _prenorm_mod(x_ref[0], g_ref[...], sh_ref[0], sc_ref[0]).astype(BF16)
    proj = _dot(h, w0_ref[...])
    cqn = _rms(proj[:, 0:256], qg_ref[...]).astype(BF16)
    qq = _dot(cqn, wq_ref[...])
    ckvn = _rms(proj[:, 256:384], kvg_ref[...]).astype(BF16)
    kv = _dot(ckvn, wkv_ref[...])
    cq = cq_ref[...]
    ck = ck_ref[...]
    sn = sn_ref[...]
    krope = proj[:, 384:512] * ck + proj[:, 512:640] * sn
    nh = MLA_HEADS
    for hd in range(nh):
        lo = hd * LANES
        qh = qq[:, lo:lo + LANES] * cq + qq[:, nh * LANES + lo:nh * LANES + lo + LANES] * sn
        _store_key_tiles(qa_ref, hd, qh)
        ka_ref[0, :, lo:lo + LANES] = (kv[:, lo:lo + LANES] + krope).astype(BF16)
    for pr in range(nh // 2):
        _store_key_tiles(va_ref, pr, kv[:, nh * LANES + pr * LANES:nh * LANES + (pr + 1) * LANES])
    qb_ref[0] = proj[:, 640:1152].astype(BF16)
    kb_ref[0] = proj[:, 1152:1664].astype(BF16)
    vb_ref[0] = proj[:, 1664:2176].astype(BF16)


def _even_in(x, g, shift, scale, w0, qg, wq, kvg, wkv, cq, ck, sn):
    b, s, d = x.shape
    tm = TOK_TILE
    tkb = MLA_TILE
    ns = s // tm
    tok = lambda w: pl.BlockSpec((1, tm, w), lambda bi, i: (bi, i, 0))
    vec = lambda w: pl.BlockSpec((1, w), lambda bi, i: (0, 0))
    mod = pl.BlockSpec((1, 1, d), lambda bi, i: (bi, 0, 0))
    tab = pl.BlockSpec((tm, LANES), lambda bi, i: (i, 0))
    full = lambda a: _resident(a.shape, lambda bi, i: (0,) * a.ndim)
    out_shapes = (
        jax.ShapeDtypeStruct((b, MLA_HEADS, s // tkb, LANES, tkb), BF16),
        jax.ShapeDtypeStruct((b, s, MLA_HEADS * LANES), BF16),
        jax.ShapeDtypeStruct((b, MLA_HEADS // 2, s // tkb, LANES, tkb), BF16),
        jax.ShapeDtypeStruct((b, s, 512), BF16),
        jax.ShapeDtypeStruct((b, s, 512), BF16),
        jax.ShapeDtypeStruct((b, s, 512), BF16),
    )
    out_specs = (
        pl.BlockSpec((1, MLA_HEADS, tm // tkb, LANES, tkb), lambda bi, i: (bi, 0, i, 0, 0)),
        tok(MLA_HEADS * LANES),
        pl.BlockSpec((1, MLA_HEADS // 2, tm // tkb, LANES, tkb), lambda bi, i: (bi, 0, i, 0, 0)),
        tok(512), tok(512), tok(512),
    )
    return pl.pallas_call(
        _even_in_kernel,
        grid=(b, ns),
        in_specs=[tok(d), vec(d), mod, mod, full(w0), vec(MLA_Q_RANK), full(wq), vec(MLA_KV_RANK), full(wkv),
                  tab, tab, tab],
        out_specs=out_specs,
        out_shape=out_shapes,
        compiler_params=_cparams(("parallel", "parallel")),
        name="even_in",
    )(x, g, shift, scale, w0, qg, wq, kvg, wkv, cq, ck, sn)


def _store_key_tiles(kt_ref, hd, k):
    tkb = kt_ref.shape[4]
    for t in range(k.shape[0] // tkb):
        kt_ref[0, hd, t] = k[t * tkb:(t + 1) * tkb, :].T.astype(BF16)


def _odd_in_kernel(x_ref, g_ref, sh_ref, sc_ref, w_ref,
                   qdt_ref, kd_ref, vdt_ref, qs_ref, kst_ref, vs_ref):
    h = _prenorm_mod(x_ref[0], g_ref[...], sh_ref[0], sc_ref[0]).astype(BF16)
    proj = _dot(h, w_ref[...])
    kd_ref[0] = proj[:, 512:1024].astype(BF16)
    qs_ref[0] = proj[:, 1536:2048].astype(BF16)
    vs_ref[0] = proj[:, 2560:3072].astype(BF16)
    for hd in range(4):
        _store_key_tiles(qdt_ref, hd, proj[:, hd * LANES:(hd + 1) * LANES])
        _store_key_tiles(vdt_ref, hd, proj[:, 1024 + hd * LANES:1024 + (hd + 1) * LANES])
        _store_key_tiles(kst_ref, hd, proj[:, 2048 + hd * LANES:2048 + (hd + 1) * LANES])


def _odd_in(x, g, shift, scale, w):
    b, s, d = x.shape
    tm = TOK_TILE
    ns = s // tm
    tok = lambda wd: pl.BlockSpec((1, tm, wd), lambda bi, i: (bi, i, 0))
    mod = pl.BlockSpec((1, 1, d), lambda bi, i: (bi, 0, 0))
    ktspec = lambda tkb: pl.BlockSpec((1, 4, tm // tkb, LANES, tkb), lambda bi, i: (bi, 0, i, 0, 0))
    act = jax.ShapeDtypeStruct((b, s, 512), BF16)
    kts = lambda tkb: jax.ShapeDtypeStruct((b, 4, s // tkb, LANES, tkb), BF16)
    return pl.pallas_call(
        _odd_in_kernel,
        grid=(b, ns),
        in_specs=[tok(d), pl.BlockSpec((1, d), lambda bi, i: (0, 0)), mod, mod,
                  _resident(w.shape, lambda bi, i: (0, 0))],
        out_specs=(ktspec(DIFF_TILE), tok(512), ktspec(DIFF_TILE), tok(512), ktspec(SB_TILE), tok(512)),
        out_shape=(kts(DIFF_TILE), act, kts(DIFF_TILE), act, kts(SB_TILE), act),
        compiler_params=_cparams(("parallel", "parallel")),
        name="odd_in",
    )(x, g, shift, scale, w)


def _mla_kernel(qt_ref, k_ref, vt_ref, o_ref):
    tq = qt_ref.shape[4]
    tk = vt_ref.shape[4]
    hv = MLA_V
    qi = pl.program_id(2)
    causal = (lax.broadcasted_iota(jnp.int32, (tk, tq), 0)
              <= lax.broadcasted_iota(jnp.int32, (tk, tq), 1))
    qts = (qt_ref[0, 0, 0], qt_ref[0, 1, 0])

    def step(js, carry, masked):
        logits, values = [], []
        for hd in range(2):
            s_list = [_dot(k_ref[0, pl.ds(pl.multiple_of(j * tk, tk), tk), hd * LANES:(hd + 1) * LANES], qts[hd])
                      for j in js]
            if masked:
                s_list = [jnp.where(causal, s, NEG) for s in s_list]
            logits.append(s_list)
            values.append([vt_ref[0, 0, j, hd * hv:(hd + 1) * hv, :] for j in js])
        return _softmax_step_t(logits, values, carry)

    one = (jnp.full((1, tq), NEG, F32), jnp.zeros((1, tq), F32), jnp.zeros((hv, tq), F32))
    carry = _loop_pairs(0, qi, functools.partial(step, masked=False), (one, one), group=4)
    (_, l0, a0), (_, l1, a1) = step((qi,), carry, True)
    o_ref[0] = jnp.concatenate([a0 / l0, a1 / l1], axis=0).T.astype(BF16)


def _mla(qt, k, vt):
    b, s, _ = k.shape
    tq = qt.shape[4]
    nk = vt.shape[2]
    tk = vt.shape[4]
    return pl.pallas_call(
        _mla_kernel,
        grid=(b, MLA_HEADS // 2, s // tq),
        in_specs=[
            pl.BlockSpec((1, 2, 1, LANES, tq), lambda bi, hp, qi: (bi, hp, qi, 0, 0)),
            pl.BlockSpec((1, s, 2 * LANES), lambda bi, hp, qi: (bi, 0, hp)),
            pl.BlockSpec((1, 1, nk, LANES, tk), lambda bi, hp, qi: (bi, hp, 0, 0, 0)),
        ],
        out_specs=pl.BlockSpec((1, tq, LANES), lambda bi, hp, qi: (bi, qi, hp)),
        out_shape=jax.ShapeDtypeStruct((b, s, 512), BF16),
        compiler_params=_cparams(("parallel", "parallel", "arbitrary")),
        name="mla",
    )(qt, k, vt)


def _dil_kernel(q_ref, kc_ref, kp_ref, vc_ref, vp_ref, bias_ref, o_ref,
                q32, k32, v32, acc_s, m_s, d_s):
    sup = DIL_SUPER
    blk = DIL_BLOCK
    n = pl.program_id(2)
    q32[...] = q_ref[0].astype(F32)
    k32[0:sup, :] = kp_ref[0].astype(F32)
    k32[sup:2 * sup, :] = kc_ref[0].astype(F32)
    v32[0:sup, :] = vp_ref[0].astype(F32)
    v32[sup:2 * sup, :] = vc_ref[0].astype(F32)
    low = lax.broadcasted_iota(jnp.int32, (blk, LANES), 1) < 64
    before_start = jnp.where(lax.broadcasted_iota(jnp.int32, (blk, 2 * blk), 1) < blk, NEG, 0.0)

    for g, (_, dil) in enumerate(DIL_PATTERNS):

        def unit(u, carry, g=g, dil=dil):
            n_loc = u // dil
            r = u % dil
            qs = n_loc * (blk * dil) + r
            ks = sup + (n_loc - 1) * (blk * dil) + r
            if dil == 1:
                qsl = pl.ds(pl.multiple_of(qs, blk), blk)
                ksl = pl.ds(pl.multiple_of(ks, blk), 2 * blk)
            else:
                qsl = pl.ds(qs, blk, stride=dil)
                ksl = pl.ds(ks, 2 * blk, stride=dil)
            q = q32[qsl, :]
            k = k32[ksl, :].astype(BF16)
            v = v32[ksl, :].astype(BF16)
            extra = jnp.where(jnp.logical_and(n == 0, n_loc == 0), before_start, 0.0)
            parts = []
            for hd in range(2):
                qh = jnp.where(low if hd == 0 else jnp.logical_not(low), q, 0.0).astype(BF16)
                s = lax.dot_general(qh, k, (((1,), (1,)), ((), ())), preferred_element_type=F32)
                s = s + bias_ref[g, hd] + extra
                m = jnp.max(s, axis=-1, keepdims=True)
                e = jnp.exp2(s - m)
                den = jnp.sum(e, axis=-1, keepdims=True)
                parts.append((_dot(e.astype(BF16), v), m, den))
            acc_s[g, qsl, :] = jnp.where(low, parts[0][0], parts[1][0])
            m_s[g, qsl, :] = jnp.where(low, parts[0][1], parts[1][1])
            d_s[g, qsl, :] = jnp.where(low, parts[0][2], parts[1][2])
            return carry

        lax.fori_loop(0, 16, unit, 0, unroll=16)

    mx = jnp.maximum(jnp.maximum(m_s[0], m_s[1]), m_s[2])
    num = jnp.zeros((sup, LANES), F32)
    den = jnp.zeros((sup, LANES), F32)
    for g in range(3):
        a = jnp.exp2(m_s[g] - mx)
        num = num + a * acc_s[g]
        den = den + a * d_s[g]
    o_ref[0] = (num / den).astype(BF16)


def _dil(q, k, v, bias):
    b, s, _ = q.shape
    sup = DIL_SUPER
    cur = pl.BlockSpec((1, sup, LANES), lambda bi, hp, n: (bi, n, hp))
    prev = pl.BlockSpec((1, sup, LANES), lambda bi, hp, n: (bi, jnp.maximum(n - 1, 0), hp))
    return pl.pallas_call(
        _dil_kernel,
        grid=(b, DIL_HEADS // 2, s // sup),
        in_specs=[cur, cur, prev, cur, prev,
                  pl.BlockSpec((3, 2, DIL_BLOCK, 2 * DIL_BLOCK), lambda bi, hp, n: (0, hp, 0, 0))],
        out_specs=cur,
        out_shape=jax.ShapeDtypeStruct((b, s, 512), BF16),
        scratch_shapes=[
            pltpu.VMEM((sup, LANES), F32),
            pltpu.VMEM((2 * sup, LANES), F32),
            pltpu.VMEM((2 * sup, LANES), F32),
            pltpu.VMEM((3, sup, LANES), F32),
            pltpu.VMEM((3, sup, LANES), F32),
            pltpu.VMEM((3, sup, LANES), F32),
        ],
        compiler_params=_cparams(("parallel", "parallel", "arbitrary")),
        name="dilated",
    )(q, k, k, v, v, bias)


def _diff_kernel(qt_ref, k_ref, vt_ref, bias_ref, far_ref, lam_ref, g_ref, o_ref, *, lam_init):
    tq = qt_ref.shape[4]
    tk = vt_ref.shape[4]
    nd = bias_ref.shape[2]
    qi = pl.program_id(2)
    qt = qt_ref[0, 0, 0]
    row = lax.broadcasted_iota(jnp.int32, (LANES, tq), 0)
    zero = jnp.zeros_like(qt)
    qm = (jnp.where(row < DIFF_HD, qt, zero), jnp.where(row >= DIFF_HD, qt, zero))

    def step(js, carry, bias_of):
        keys = [k_ref[0, pl.ds(pl.multiple_of(j * tk, tk), tk), :] for j in js]
        vts = [vt_ref[0, 0, j] for j in js]
        logits = [[_dot(k, qm[mi]) for k in keys] for mi in range(2)]
        if bias_of is not None:
            logits = [[s + bias_of(mi, j) for j, s in zip(js, logits[mi])] for mi in range(2)]
        return _softmax_step_t(logits, [vts, vts], carry)

    one = (jnp.full((1, tq), NEG, F32), jnp.zeros((1, tq), F32), jnp.zeros((LANES, tq), F32))
    carry = (one, one)
    n_far = jnp.maximum(qi - nd + 1, 0)
    def shift_max(carry, sign):
        return tuple((m + sign * far_ref[0, mi, 0:1, 0:1], l, acc) for mi, (m, l, acc) in enumerate(carry))

    carry = shift_max(carry, -1.0)
    carry = _loop_pairs(0, n_far, functools.partial(step, bias_of=None), carry, group=4)
    carry = shift_max(carry, 1.0)
    carry = _loop_pairs(n_far, qi, functools.partial(step, bias_of=lambda mi, j: bias_ref[0, mi, qi - j]), carry,
                        group=4)
    carry = step((qi,), carry, lambda mi, j: bias_ref[0, mi, 0])
    (_, l0, a0), (_, l1, a1) = carry
    o = (a0 / l0 - lam_ref[0:1, 0:1] * (a1 / l1)).T
    o_ref[0] = (_rms(o, g_ref[...]) * (1.0 - lam_init)).astype(BF16)


def _diff(qt, k, vt, bias, far, lam, sub_g, lam_init):
    b, s, _ = k.shape
    tq = qt.shape[4]
    nk, tk = vt.shape[2], vt.shape[4]
    nd = bias.shape[2]
    return pl.pallas_call(
        functools.partial(_diff_kernel, lam_init=lam_init),
        grid=(DIFF_HEADS, b, s // tq),
        in_specs=[
            pl.BlockSpec((1, 1, 1, LANES, tq), lambda h, bi, qi: (bi, h, qi, 0, 0)),
            pl.BlockSpec((1, s, LANES), lambda h, bi, qi: (bi, 0, h)),
            pl.BlockSpec((1, 1, nk, LANES, tk), lambda h, bi, qi: (bi, h, 0, 0, 0)),
            _resident((1, 2, nd, tk, tq), lambda h, bi, qi: (h, 0, 0, 0, 0)),
            pl.BlockSpec((1, 2, 8, LANES), lambda h, bi, qi: (h, 0, 0, 0)),
            pl.BlockSpec((1, LANES), lambda h, bi, qi: (0, 0)),
            pl.BlockSpec((1, LANES), lambda h, bi, qi: (0, 0)),
        ],
        out_specs=pl.BlockSpec((1, tq, LANES), lambda h, bi, qi: (bi, qi, h)),
        out_shape=jax.ShapeDtypeStruct((b, s, 512), BF16),
        compiler_params=_cparams(("parallel", "parallel", "arbitrary")),
        name="diff",
    )(qt, k, vt, bias, far, lam, sub_g)


def _sb_kernel(q_ref, kt_ref, v_ref, o_ref):
    tq = q_ref.shape[1]
    tk = kt_ref.shape[4]
    qi = pl.program_id(2)
    lane = lax.broadcasted_iota(jnp.int32, (tq, LANES), 1)
    strict = (lax.broadcasted_iota(jnp.int32, (tq, tk), 1)
              < lax.broadcasted_iota(jnp.int32, (tq, tk), 0))
    later = (lax.broadcasted_iota(jnp.int32, (tk, tk), 0)
             > lax.broadcasted_iota(jnp.int32, (tk, tk), 1)).astype(BF16)
    q = q_ref[0]
    zero = jnp.zeros_like(q)
    qh = (jnp.where(lane < 64, q, zero), jnp.where(lane >= 64, q, zero))

    def blocks(js, state, masked):
        items = [(bi, hd) for bi in range(len(js)) for hd in range(2)]
        z = {it: _dot(qh[it[1]], kt_ref[0, 0, js[it[0]]]) for it in items}
        log_sig, log_1m, inblock = {}, {}, {}
        for it in items:
            neg = -z[it]
            t = jnp.log2(1.0 + jnp.exp2(jnp.minimum(z[it], neg)))
            log_sig[it] = jnp.minimum(z[it], 0.0) - t
            l1m = jnp.minimum(neg, 0.0) - t
            if masked:
                l1m = jnp.where(strict, l1m, 0.0)
            log_1m[it] = l1m
            inblock[it] = _dot(l1m.astype(BF16), later)
        state = list(state)
        for bi, j in enumerate(js):
            v = v_ref[0, pl.ds(pl.multiple_of(j * tk, tk), tk), :]
            for hd in range(2):
                it = (bi, hd)
                c, acc = state[hd]
                w = jnp.exp2(log_sig[it] + (inblock[it] + c))
                if masked:
                    w = jnp.where(strict, w, 0.0)
                acc = acc + _dot(w.astype(BF16), v)
                c = c + jnp.sum(log_1m[it], axis=-1, keepdims=True)
                state[hd] = (c, acc)
        return tuple(state)

    one = (jnp.zeros((tq, 1), F32), jnp.zeros((tq, LANES), F32))
    state = blocks((qi,), (one, one), True)

    odd = qi % 2
    state = lax.cond(odd == 1, lambda st: blocks((qi - 1,), st, False), lambda st: st, state)
    floor = SB_LOG_FLOOR * LOG2E

    def cond(st):
        j, ((c0, _), (c1, _)) = st
        return jnp.logical_and(j >= 1, jnp.max(jnp.maximum(c0, c1)) > floor)

    def body(st):
        j, state = st
        return j - 2, blocks((j, j - 1), state, False)

    _, ((_, a0), (_, a1)) = lax.while_loop(cond, body, (qi - 1 - odd, state))
    o_ref[0] = jnp.where(lane < 64, a0, a1).astype(BF16)


def _sb(q, kt, v):
    b, s, _ = q.shape
    tq = SB_TILE
    nk, tk = kt.shape[2], kt.shape[4]
    return pl.pallas_call(
        _sb_kernel,
        grid=(b, SB_HEADS // 2, s // tq),
        in_specs=[
            pl.BlockSpec((1, tq, LANES), lambda bi, hp, qi: (bi, qi, hp)),
            pl.BlockSpec((1, 1, nk, LANES, tk), lambda bi, hp, qi: (bi, hp, 0, 0, 0)),
            pl.BlockSpec((1, s, LANES), lambda bi, hp, qi: (bi, 0, hp)),
        ],
        out_specs=pl.BlockSpec((1, tq, LANES), lambda bi, hp, qi: (bi, qi, hp)),
        out_shape=jax.ShapeDtypeStruct((b, s, 512), BF16),
        compiler_params=_cparams(("parallel", "parallel", "arbitrary")),
        name="stick_breaking",
    )(q, kt, v)


FF_CHUNKS = ((0, 768), (768, 1536), (1536, 2304), (2304, 2816))
EXPERT_CHUNKS = ((0, 1024), (1024, 2048), (2048, 3072), (3072, 3584))


def _swiglu(hb, wg_ref, wu_ref, wd_ref, chunks, lead, between=None):
    acc = None
    hb_of = hb if callable(hb) else (lambda ci: hb)
    for ci, (c0, c1) in enumerate(chunks):
        hb = hb_of(ci)
        g = _dot(hb, wg_ref[lead + (slice(None), slice(c0, c1))])
        u = _dot(hb, wu_ref[lead + (slice(None), slice(c0, c1))])
        a = (g / (1.0 + jnp.exp(-g)) * u).astype(BF16)
        part = _dot(a, wd_ref[lead + (slice(c0, c1), slice(None))])
        acc = part if acc is None else acc + part
        if between is not None:
            between(ci)
    return acc


def _mix_out(x_ref, oa_ref, ob_ref, wa_ref, wb_ref, gate_ref, pg_ref):
    y = _dot(oa_ref[0], wa_ref[...]) + _dot(ob_ref[0], wb_ref[...])
    return x_ref[0] + gate_ref[0] * _rms(y, pg_ref[...])


def _post_even_kernel(x_ref, oa_ref, ob_ref, wa_ref, wb_ref, gate_ref, pg_ref,
                      fg_ref, fsh_ref, fsc_ref, fgate_ref, fpg_ref, wg_ref, wu_ref, wd_ref, o_ref):
    x1 = _mix_out(x_ref, oa_ref, ob_ref, wa_ref, wb_ref, gate_ref, pg_ref)
    hb = _prenorm_mod(x1, fg_ref[...], fsh_ref[0], fsc_ref[0]).astype(BF16)
    y = _swiglu(hb, wg_ref, wu_ref, wd_ref, FF_CHUNKS, ())
    o_ref[0] = x1 + fgate_ref[0] * _rms(y, fpg_ref[...])


def _post_even(x, oa, ob, wa, wb, gate, pg, fg, fsh, fsc, fgate, fpg, wg, wu, wd):
    b, s, d = x.shape
    tm = TOK_TILE
    tok = lambda w: pl.BlockSpec((1, tm, w), lambda bi, i: (bi, i, 0))
    vec = pl.BlockSpec((1, d), lambda bi, i: (0, 0))
    mod = pl.BlockSpec((1, 1, d), lambda bi, i: (bi, 0, 0))
    full = lambda a: _resident(a.shape, lambda bi, i: (0,) * a.ndim)
    return pl.pallas_call(
        _post_even_kernel,
        grid=(b, s // tm),
        in_specs=[tok(d), tok(512), tok(512), full(wa), full(wb), mod, vec,
                  vec, mod, mod, mod, vec, full(wg), full(wu), full(wd)],
        out_specs=tok(d),
        out_shape=jax.ShapeDtypeStruct((b, s, d), F32),
        compiler_params=_cparams(("parallel", "parallel")),
        name="post_even",
    )(x, oa, ob, wa, wb, gate, pg, fg, fsh, fsc, fgate, fpg, wg, wu, wd)


def _post_odd_kernel(x_ref, oa_ref, ob_ref, wa_ref, wb_ref, gate_ref, pg_ref,
                     fg_ref, fsh_ref, fsc_ref, rw_ref, rb_ref, x_out, h_out, r_out, cnt_ref):
    x1 = _mix_out(x_ref, oa_ref, ob_ref, wa_ref, wb_ref, gate_ref, pg_ref)
    x_out[0] = x1
    h = _prenorm_mod(x1, fg_ref[...], fsh_ref[0], fsc_ref[0])
    h_out[0] = h
    logits = _dot(h, rw_ref[...]) + rb_ref[...]
    lane = lax.broadcasted_iota(jnp.int32, logits.shape, 1)
    m1 = jnp.max(logits, axis=-1, keepdims=True)
    i1 = jnp.min(jnp.where(logits == m1, lane, LANES), axis=-1, keepdims=True)
    rest = jnp.where(lane == i1, NEG, logits)
    m2 = jnp.max(rest, axis=-1, keepdims=True)
    i2 = jnp.min(jnp.where(rest == m2, lane, LANES), axis=-1, keepdims=True)
    e2 = jnp.exp(m2 - m1)
    w1 = 1.0 / (1.0 + e2)
    w2 = e2 / (1.0 + e2)
    @pl.when(jnp.logical_and(pl.program_id(0) == 0, pl.program_id(1) == 0))
    def _():
        cnt_ref[...] = jnp.zeros_like(cnt_ref)

    tm = logits.shape[0]
    sel = jnp.logical_or(lane == i1, lane == i2)
    earlier = (lax.broadcasted_iota(jnp.int32, (tm, tm), 1)
               < lax.broadcasted_iota(jnp.int32, (tm, tm), 0)).astype(BF16)
    prefix = _dot(earlier, sel.astype(BF16)) + cnt_ref[0:1, :]
    rank1 = jnp.sum(jnp.where(lane == i1, prefix, 0.0), axis=-1, keepdims=True)
    rank2 = jnp.sum(jnp.where(lane == i2, prefix, 0.0), axis=-1, keepdims=True)
    cnt_ref[...] = cnt_ref[...] + jnp.sum(sel.astype(F32), axis=0, keepdims=True)
    r = jnp.where(lane == 0, i1.astype(F32), 0.0)
    r = jnp.where(lane == 1, i2.astype(F32), r)
    r = jnp.where(lane == 2, w1, r)
    r = jnp.where(lane == 3, w2, r)
    r = jnp.where(lane == 4, rank1, r)
    r = jnp.where(lane == 5, rank2, r)
    r_out[0] = r


def _post_odd(x, oa, ob, wa, wb, gate, pg, fg, fsh, fsc, rw, rb):
    b, s, d = x.shape
    tm = TOK_TILE
    tok = lambda w: pl.BlockSpec((1, tm, w), lambda bi, i: (bi, i, 0))
    vec = pl.BlockSpec((1, d), lambda bi, i: (0, 0))
    mod = pl.BlockSpec((1, 1, d), lambda bi, i: (bi, 0, 0))
    full = lambda a: _resident(a.shape, lambda bi, i: (0,) * a.ndim)
    return pl.pallas_call(
        _post_odd_kernel,
        grid=(b, s // tm),
        in_specs=[tok(d), tok(512), tok(512), full(wa), full(wb), mod, vec,
                  vec, mod, mod, full(rw), pl.BlockSpec((1, LANES), lambda bi, i: (0, 0))],
        out_specs=(tok(d), tok(d), tok(LANES),
                   pl.BlockSpec((8, LANES), lambda bi, i: (0, 0))),
        out_shape=(jax.ShapeDtypeStruct((b, s, d), F32), jax.ShapeDtypeStruct((b, s, d), F32),
                   jax.ShapeDtypeStruct((b, s, LANES), F32), jax.ShapeDtypeStruct((8, LANES), F32)),
        compiler_params=_cparams(("arbitrary", "arbitrary")),
        name="post_odd",
    )(x, oa, ob, wa, wb, gate, pg, fg, fsh, fsc, rw, rb)


def _store_rows(ref, lead, val):
    for c in range(ROW_SUB):
        ref[lead + (slice(None), c, slice(None))] = val[:, c * LANES:(c + 1) * LANES]


def _load_rows(ref, lead, start, size):
    return jnp.concatenate([ref[lead, pl.ds(start, size), c, :] for c in range(ROW_SUB)], axis=1)


class _TileGather:
    def __init__(self, idx_hbm, src_hbm, idx_smem, buf, isem, sem, tile_rows):
        self.idx_hbm, self.src_hbm, self.idx_smem, self.buf = idx_hbm, src_hbm, idx_smem, buf
        self.isem, self.sem, self.tile_rows = isem, sem, tile_rows
        self.i = pl.program_id(0)
        self.nt = pl.num_programs(0)
        self.n = buf.shape[1]
        self.slot = self.i % 2
        self.nxt = 1 - self.slot

    def _idx_copy(self, t, sl):
        t = jnp.minimum(t, self.nt - 1)
        return pltpu.make_async_copy(self.idx_hbm.at[t], self.idx_smem.at[sl], self.isem.at[sl])

    def _rows_wait(self, sl):
        pltpu.make_async_copy(self.src_hbm.at[pl.ds(0, self.n)], self.buf.at[sl], self.sem.at[sl]).wait()

    def _issue_row(self, sl, r):
        t = self.idx_smem[sl, r]
        if self.tile_rows:
            src, dst = self.src_hbm.at[t], self.buf.at[sl, r]
        else:
            src, dst = self.src_hbm.at[pl.ds(t, 1)], self.buf.at[sl, pl.ds(r, 1)]
        pltpu.make_async_copy(src, dst, self.sem.at[sl]).start()

    def _issue_loop(self, sl):
        def body(r, carry):
            self._issue_row(sl, r)
            return carry

        lax.fori_loop(0, self.n, body, 0, unroll=8)

    def begin(self):
        @pl.when(self.i == 0)
        def _():
            first = self._idx_copy(0, 0)
            first.start()
            first.wait()
            self._issue_loop(0)
            self._idx_copy(1, 1).start()

        self._idx_copy(self.i + 1, self.nxt).wait()
        self._idx_copy(self.i + 2, self.slot).start()
        self._rows_wait(self.slot)

    def issue(self, part, parts):
        per = -(-self.n // parts)
        for r in range(part * per, min((part + 1) * per, self.n)):
            self._issue_row(self.nxt, r)

    def issue_all(self):
        self._issue_loop(self.nxt)

    def anchor(self, zero):
        return zero

    def finish(self):
        @pl.when(self.i == self.nt - 1)
        def _():
            self._rows_wait(self.nxt)
            self._idx_copy(self.i + 2, self.slot).wait()


def _moe_kernel(te_ref, nu_ref, zero_ref, tok_hbm, h_hbm, wg_ref, wu_ref, wd_ref, y_ref, idx_smem, buf, isem, sem):
    g = _TileGather(tok_hbm, h_hbm, idx_smem, buf, isem, sem, tile_rows=False)
    n = g.n
    g.begin()

    @pl.when(g.i < nu_ref[0])
    def _():
        anchor = [0]
        head = 16
        rest = buf[g.slot, head:n, :].astype(BF16)

        def rows_of(ci):
            first = buf[g.slot, pl.ds(pl.multiple_of(anchor[0], head), head), :].astype(BF16)
            return jnp.concatenate([first, rest], axis=0)

        def issue_part(ci):
            if ci < len(EXPERT_CHUNKS) - 1:
                g.issue(ci, len(EXPERT_CHUNKS) - 1)
                anchor[0] = g.anchor(zero_ref[0])

        y = _swiglu(rows_of, wg_ref, wu_ref, wd_ref, EXPERT_CHUNKS, (0,), between=issue_part)
        _store_rows(y_ref, (), y)

    @pl.when(g.i >= nu_ref[0])
    def _():
        g.issue_all()
        y_ref[...] = jnp.zeros_like(y_ref)

    g.finish()


def _moe(tile_expert, n_used, row_token, h, wg, wu, wd):
    d = h.shape[1]
    nt, tm = row_token.shape
    dff = wg.shape[2]
    wspec = lambda shp: pl.BlockSpec(shp, lambda i, te, nu, z: (te[i], 0, 0), pipeline_mode=pl.Buffered(1))
    grid_spec = pltpu.PrefetchScalarGridSpec(
        num_scalar_prefetch=3,
        grid=(nt,),
        in_specs=[
            pl.BlockSpec(memory_space=pl.ANY),
            pl.BlockSpec(memory_space=pl.ANY),
            wspec((1, d, dff)), wspec((1, d, dff)), wspec((1, dff, d)),
        ],
        out_specs=pl.BlockSpec((tm, ROW_SUB, LANES), lambda i, te, nu, z: (i, 0, 0)),
        scratch_shapes=[
            pltpu.SMEM((2, tm), jnp.int32),
            pltpu.VMEM((2, tm, d), F32),
            pltpu.SemaphoreType.DMA((2,)),
            pltpu.SemaphoreType.DMA((2,)),
        ],
    )
    return pl.pallas_call(
        _moe_kernel,
        grid_spec=grid_spec,
        out_shape=jax.ShapeDtypeStruct((nt * tm, ROW_SUB, LANES), F32),
        compiler_params=_cparams(("arbitrary",)),
        name="moe_experts",
    )(tile_expert, n_used, jnp.zeros((1,), jnp.int32), row_token, h, wg, wu, wd)


def _combine_kernel(zero_ref, pos_hbm, y_hbm, x_ref, r_ref, gate_ref, pg_ref, o_ref, idx_smem, buf, isem, sem):
    tm = x_ref.shape[0]
    g = _TileGather(pos_hbm, y_hbm, idx_smem, buf, isem, sem, tile_rows=True)
    g.begin()
    parts = 4
    rows = tm // parts
    off = 0
    for c in range(parts):
        lo = c * rows
        first = _load_rows(buf, g.slot, pl.multiple_of(off + lo, 8), rows)
        second = _load_rows(buf, g.slot, pl.multiple_of(off + tm + lo, 8), rows)
        r = r_ref[lo:lo + rows, :]
        y = r[:, 2:3] * first + r[:, 3:4] * second
        o_ref[lo:lo + rows, :] = x_ref[lo:lo + rows, :] + gate_ref[0] * _rms(y, pg_ref[...])
        if c < parts - 1:
            g.issue(c, parts - 1)
            off = g.anchor(zero_ref[0])
    g.finish()


def _combine(pos, y, x, r, gate, pg, tokens_per_seq):
    n_tok, d = x.shape
    nt, tm2 = pos.shape
    tm = tm2 // 2
    per_seq = tokens_per_seq // tm
    tok = lambda w: pl.BlockSpec((tm, w), lambda i, z: (i, 0))
    grid_spec = pltpu.PrefetchScalarGridSpec(
        num_scalar_prefetch=1,
        grid=(nt,),
        in_specs=[
            pl.BlockSpec(memory_space=pl.ANY),
            pl.BlockSpec(memory_space=pl.ANY),
            tok(d), tok(LANES),
            pl.BlockSpec((1, 1, d), lambda i, z: (i // per_seq, 0, 0)),
            pl.BlockSpec((1, d), lambda i, z: (0, 0)),
        ],
        out_specs=tok(d),
        scratch_shapes=[
            pltpu.SMEM((2, tm2), jnp.int32),
            pltpu.VMEM((2, tm2, ROW_SUB, LANES), F32),
            pltpu.SemaphoreType.DMA((2,)),
            pltpu.SemaphoreType.DMA((2,)),
        ],
    )
    return pl.pallas_call(
        _combine_kernel,
        grid_spec=grid_spec,
        out_shape=jax.ShapeDtypeStruct((n_tok, d), F32),
        compiler_params=_cparams(("arbitrary",)),
        name="moe_combine",
    )(jnp.zeros((1,), jnp.int32), pos, y, x, r, gate, pg)


def _t5_bucket(dist):
    max_exact = REL_BUCKETS // 2
    d = jnp.maximum(dist, 1).astype(F32)
    log_b = max_exact + (jnp.log(d / max_exact) / math.log(REL_MAX_DIST / max_exact)
                         * (REL_BUCKETS - max_exact)).astype(jnp.int32)
    log_b = jnp.minimum(log_b, REL_BUCKETS - 1)
    return jnp.where(dist < max_exact, dist, log_b)


def _rope_tables(s):
    half = MLA_ROPE // 2
    freqs = ROPE_THETA ** (-jnp.arange(half, dtype=F32) / half)
    ang = jnp.arange(s, dtype=F32)[:, None] * freqs[None, :]
    cos, sin = jnp.cos(ang), jnp.sin(ang)
    z64 = jnp.zeros((s, MLA_NOPE), F32)
    z32 = jnp.zeros((s, LANES - MLA_NOPE - MLA_ROPE), F32)
    ck = jnp.concatenate([z64, cos, cos, z32], axis=1)
    cq = jnp.concatenate([jnp.ones((s, MLA_NOPE), F32), cos, cos, z32], axis=1)
    sn = jnp.concatenate([z64, sin, sin, z32], axis=1)
    return cq, ck, sn


def _even_weights(w_in, w_uq, w_ukv):
    d = w_in.shape[0]
    half = MLA_ROPE // 2
    w_cq = w_in[:, :MLA_Q_RANK]
    w_ckv = w_in[:, MLA_Q_RANK:MLA_Q_RANK + MLA_KV_RANK]
    w_kr = w_in[:, MLA_Q_RANK + MLA_KV_RANK:MLA_Q_RANK + MLA_KV_RANK + MLA_ROPE]
    w_qkv = w_in[:, MLA_Q_RANK + MLA_KV_RANK + MLA_ROPE:]
    z = lambda n: jnp.zeros((d, n), F32)
    kr_a = jnp.concatenate([z(MLA_NOPE), w_kr, z(32)], axis=1)
    kr_b = jnp.concatenate([z(MLA_NOPE), -w_kr[:, half:], w_kr[:, :half], z(32)], axis=1)
    dil_scale = DIL_HD ** -0.5 * LOG2E
    w0 = jnp.concatenate([w_cq, w_ckv, kr_a, kr_b, w_qkv[:, :512] * dil_scale, w_qkv[:, 512:]], axis=1)

    r = w_uq.shape[0]
    wq = w_uq.reshape(r, MLA_HEADS, MLA_NOPE + MLA_ROPE) * ((MLA_NOPE + MLA_ROPE) ** -0.5 * LOG2E)
    zq = lambda n: jnp.zeros((r, MLA_HEADS, n), F32)
    nope, x1, x2 = wq[..., :MLA_NOPE], wq[..., MLA_NOPE:MLA_NOPE + half], wq[..., MLA_NOPE + half:]
    q_a = jnp.concatenate([nope, x1, x2, zq(32)], axis=-1).reshape(r, MLA_HEADS * LANES)
    q_b = jnp.concatenate([zq(MLA_NOPE), -x2, x1, zq(32)], axis=-1).reshape(r, MLA_HEADS * LANES)
    wq2 = jnp.concatenate([q_a, q_b], axis=1)

    rk = w_ukv.shape[0]
    wkv = w_ukv.reshape(rk, MLA_HEADS, MLA_NOPE + MLA_V)
    k_blk = jnp.concatenate([wkv[..., :MLA_NOPE], jnp.zeros((rk, MLA_HEADS, LANES - MLA_NOPE), F32)], axis=-1)
    wkv2 = jnp.concatenate([k_blk.reshape(rk, MLA_HEADS * LANES),
                            wkv[..., MLA_NOPE:].reshape(rk, MLA_HEADS * MLA_V)], axis=1)
    return w0.astype(BF16), wq2.astype(BF16), wkv2.astype(BF16)


def _toeplitz(vec, rows, cols):
    n, width = vec.shape

    def toeplitz_kernel(v_ref, o_ref):
        tiled = jnp.broadcast_to(v_ref[0], (rows, width))
        o_ref[0] = pltpu.roll(tiled, 0, 1, stride=1, stride_axis=0)[:, :cols]

    return pl.pallas_call(
        toeplitz_kernel,
        grid=(n,),
        in_specs=[pl.BlockSpec((1, 1, width), lambda t: (t, 0, 0))],
        out_specs=pl.BlockSpec((1, rows, cols), lambda t: (t, 0, 0)),
        out_shape=jax.ShapeDtypeStruct((n, rows, cols), F32),
        compiler_params=_cparams(("parallel",)),
        name="toeplitz_bias",
    )(vec.reshape(n, 1, width).astype(F32))


def _dil_bias(rel_bias):
    blk = DIL_BLOCK
    width = 4 * blk
    k = jnp.arange(width)
    rel = jnp.where(k < 2 * blk, blk - k, blk + width - k)
    out = []
    for window, dil in DIL_PATTERNS:
        band = (rel >= 0) & (rel <= window // dil)
        bias = rel_bias[_t5_bucket(jnp.maximum(rel, 0) * dil)] * LOG2E
        out.append(jnp.where(band[:, None], bias, NEG).T)
    vec = jnp.stack(out).reshape(3 * DIL_HEADS, width)
    return _toeplitz(vec, blk, 2 * blk).reshape(3, DIL_HEADS, blk, 2 * blk)


def _diff_bias(rel_bias, tile):
    nd = REL_MAX_DIST // tile + 1
    maps = rel_bias.shape[1]
    k = jnp.arange(2 * tile)[None, :]
    dist = jnp.arange(nd)[:, None] * tile + jnp.where(k < tile, k, k - 2 * tile)
    vec = jnp.where((dist >= 0)[..., None], rel_bias[_t5_bucket(jnp.maximum(dist, 0))] * LOG2E, NEG)
    vec = jnp.transpose(vec, (2, 0, 1)).reshape(maps * nd, 2 * tile)
    bias = _toeplitz(vec, tile, tile).reshape(DIFF_HEADS, 2, nd, tile, tile)
    far = rel_bias[_t5_bucket(jnp.array(REL_MAX_DIST))] * LOG2E
    far = jnp.broadcast_to(far.reshape(DIFF_HEADS, 2, 1, 1), (DIFF_HEADS, 2, 8, LANES))
    return bias, far.astype(F32)


def _routing(r, counts, n_tok, tile):
    n_tiles = (2 * n_tok) // tile + N_EXPERTS
    e = jnp.concatenate([r[:, 0], r[:, 1]]).astype(jnp.int32)
    rank = jnp.concatenate([r[:, 4], r[:, 5]]).astype(jnp.int32)
    counts = counts.astype(jnp.int32)
    padded = ((counts + tile - 1) // tile) * tile
    ends = jnp.cumsum(padded)
    starts = ends - padded
    onehot = (e[:, None] == jnp.arange(N_EXPERTS)[None, :]).astype(jnp.int32)
    pos = jnp.sum(onehot * starts[None, :], axis=1) + rank
    token = jnp.tile(jnp.arange(n_tok, dtype=jnp.int32), 2)
    row_token = jnp.zeros((n_tiles * tile,), jnp.int32).at[pos].set(token)
    tile_start = jnp.arange(n_tiles, dtype=jnp.int32) * tile
    tile_expert = jnp.sum((tile_start[:, None] >= ends[None, :]).astype(jnp.int32), axis=1)
    n_used = (ends[-1] // tile).astype(jnp.int32)
    last = jnp.sum((ends[-1] - 1 >= ends).astype(jnp.int32))
    tile_expert = jnp.minimum(tile_expert, last).astype(jnp.int32)
    return (tile_expert, n_used.reshape(1), row_token.reshape(n_tiles, tile),
            pos[:n_tok].astype(jnp.int32), pos[n_tok:].astype(jnp.int32))


def kernel(x, c, rel_bias, ada_mix_w, ada_mix_b, mix_pre_g, mix_post_g, ada_ffn_w, ada_ffn_b, ffn_pre_g, ffn_post_g, e_w_in, e_q_norm_g, e_w_uq, e_kv_norm_g, e_w_ukv, e_w_out, ffn_w_gate, ffn_w_up, ffn_w_down, o_w_in, diff_lq1, diff_lk1, diff_lq2, diff_lk2, diff_sub_g, o_w_out, router_w, router_b, moe_w_gate, moe_w_up, moe_w_down):
    b, s, d = x.shape
    assert d == D_MODEL and s % DIL_SUPER == 0 and s % TOK_TILE == 0
    row = lambda v: v.reshape(1, -1).astype(F32)

    mix_mod = _ada(c, ada_mix_w, ada_mix_b)
    ffn_mod = _ada(c, ada_ffn_w, ada_ffn_b)

    shift, scale, gate = _split_mod(mix_mod[0])
    w0, wq2, wkv2 = _even_weights(e_w_in[0], e_w_uq[0], e_w_ukv[0])
    cq, ck, sn = _rope_tables(s)
    qat, ka, vat, qb, kb, vb = _even_in(x, row(mix_pre_g[0]), shift, scale, w0, row(e_q_norm_g[0]), wq2,
                                        row(e_kv_norm_g[0]), wkv2, cq, ck, sn)
    o_a = _mla(qat, ka, vat)
    o_b = _dil(qb, kb, vb, _dil_bias(rel_bias))
    fshift, fscale, fgate = _split_mod(ffn_mod[0])
    w_out = e_w_out[0].astype(BF16)
    x = _post_even(x, o_a, o_b, w_out[:512], w_out[512:], gate, row(mix_post_g[0]),
                   row(ffn_pre_g[0]), fshift, fscale, fgate, row(ffn_post_g[0]),
                   ffn_w_gate[0].astype(BF16), ffn_w_up[0].astype(BF16), ffn_w_down[0].astype(BF16))

    layer = 1
    shift, scale, gate = _split_mod(mix_mod[1])
    w_in = o_w_in[0]
    att_scale = DIFF_HD ** -0.5
    w1 = jnp.concatenate([w_in[:, :512] * (att_scale * LOG2E), w_in[:, 512:1536],
                          w_in[:, 1536:2048] * (SB_HD ** -0.5 * LOG2E), w_in[:, 2048:]], axis=1).astype(BF16)
    qdt, kd, vdt, qs, kst, vs = _odd_in(x, row(mix_pre_g[1]), shift, scale, w1)
    lam_init = 0.8 - 0.6 * math.exp(-0.3 * layer)
    lam = (jnp.exp(jnp.sum(diff_lq1[0].astype(F32) * diff_lk1[0].astype(F32)))
           - jnp.exp(jnp.sum(diff_lq2[0].astype(F32) * diff_lk2[0].astype(F32))) + lam_init)
    bias, far = _diff_bias(rel_bias, DIFF_TILE)
    o_c = _diff(qdt, kd, vdt, bias, far, jnp.full((1, LANES), lam, F32), row(diff_sub_g[0]), lam_init)
    o_d = _sb(qs, kst, vs)
    fshift, fscale, fgate = _split_mod(ffn_mod[1])
    w_out = o_w_out[0].astype(BF16)
    rw = jnp.zeros((d, LANES), F32).at[:, :N_EXPERTS].set(router_w[0].astype(F32))
    rb = jnp.full((1, LANES), NEG, F32).at[0, :N_EXPERTS].set(router_b[0].astype(F32))
    x, h, r, counts = _post_odd(x, o_c, o_d, w_out[:512], w_out[512:], gate, row(mix_post_g[1]),
                                row(ffn_pre_g[1]), fshift, fscale, rw, rb)

    n_tok = b * s
    tile_expert, n_used, row_token, pos0, pos1 = _routing(r.reshape(n_tok, LANES), counts[0, :N_EXPERTS],
                                                          n_tok, MOE_TILE)
    y = _moe(tile_expert, n_used, row_token, h.reshape(n_tok, d),
             moe_w_gate[0].astype(BF16), moe_w_up[0].astype(BF16), moe_w_down[0].astype(BF16))
    ct = TOK_TILE
    pos = jnp.concatenate([pos0.reshape(n_tok // ct, ct), pos1.reshape(n_tok // ct, ct)], axis=1)
    out = _combine(pos, y, x.reshape(n_tok, d), r.reshape(n_tok, LANES), fgate, row(ffn_post_g[1]), s)
    return out.reshape(b, s, d)
```

```python
import functools
import math

import jax
import jax.numpy as jnp
from jax import lax
from jax.experimental import pallas as pl
from jax.experimental.pallas import tpu as pltpu

F32 = jnp.float32
BF16 = jnp.bfloat16

D_MODEL = 1024
EPS = 1e-6

MLA_HEADS = 8
MLA_NOPE = 64
MLA_ROPE = 32
MLA_V = 64
MLA_Q_RANK = 256
MLA_KV_RANK = 128
ROPE_THETA = 10000.0

DIL_HEADS = 8
DIL_HD = 64
DIL_PATTERNS = ((128, 1), (512, 4), (2048, 16))
DIL_BLOCK = 128

DIFF_HEADS = 4
DIFF_HD = 64
SB_HEADS = 8
SB_HD = 64

REL_BUCKETS = 32
REL_MAX_DIST = 2048

D_FF = 2816
N_EXPERTS = 8
D_FF_EXPERT = 3584

LANES = 128
ROW_SUB = D_MODEL // LANES
LOG2E = math.log2(math.e)
NEG = -1e30

TOK_TILE = 512
MLA_TILE = 512
DIFF_TILE = 512
SB_TILE = 256
DIL_SUPER = DIL_BLOCK * 16
MOE_TILE = 512
SB_LOG_FLOOR = -104.0

VMEM_LIMIT = 56 * 1024 * 1024


def _cparams(sem):
    return pltpu.CompilerParams(dimension_semantics=sem, vmem_limit_bytes=VMEM_LIMIT)


def _resident(shape, index_map):
    return pl.BlockSpec(shape, index_map, pipeline_mode=pl.Buffered(1))


def _rms(x, g):
    return x * lax.rsqrt(jnp.mean(x * x, axis=-1, keepdims=True) + EPS) * g


def _dot(a, b):
    return jnp.dot(a, b, preferred_element_type=F32)


def _softmax_step_t(logits, values, carry):
    out = []
    for s_list, v_list, (m, l, acc) in zip(logits, values, carry):
        for s, vt in zip(s_list, v_list):
            m_new = jnp.maximum(m, jnp.max(s, axis=0, keepdims=True))
            alpha = jnp.exp2(m - m_new)
            p = jnp.exp2(s - m_new)
            l = alpha * l + jnp.sum(p, axis=0, keepdims=True)
            acc = alpha * acc + _dot(vt, p.astype(BF16))
            m = m_new
        out.append((m, l, acc))
    return tuple(out)


def _loop_pairs(lo, hi, step, carry, group=2):
    n = hi - lo
    carry = lax.fori_loop(
        0, n // group, lambda i, c: step(tuple(lo + group * i + g for g in range(group)), c), carry)
    size = group // 2
    while size >= 1:
        start = lo + (n // (2 * size)) * (2 * size)
        carry = lax.cond((n // size) % 2 == 1,
                         lambda c, start=start, size=size: step(tuple(start + g for g in range(size)), c),
                         lambda c: c, carry)
        size //= 2
    return carry


def _ada_kernel(c_ref, w_ref, b_ref, o_ref):
    c = c_ref[...]
    sc = c / (1.0 + jnp.exp(-c))
    o_ref[0] = _dot(sc.astype(BF16), w_ref[0].astype(BF16)) + b_ref[0]


def _ada(c, w, b):
    nl, d, d3 = w.shape
    bsz = c.shape[0]
    nb = d3 // d
    return pl.pallas_call(
        _ada_kernel,
        grid=(nl, nb),
        in_specs=[
            pl.BlockSpec((bsz, d), lambda l, j: (0, 0)),
            pl.BlockSpec((1, d, d), lambda l, j: (l, 0, j)),
            pl.BlockSpec((1, 1, d), lambda l, j: (l, 0, j)),
        ],
        out_specs=pl.BlockSpec((1, bsz, d), lambda l, j: (l, 0, j)),
        out_shape=jax.ShapeDtypeStruct((nl, bsz, d3), F32),
        compiler_params=_cparams(("arbitrary", "arbitrary")),
        name="ada",
    )(c, w, b.reshape(nl, 1, d3))


def _split_mod(m):
    b = m.shape[0]
    m = m.reshape(b, 3, 1, D_MODEL)
    return m[:, 0], m[:, 1], m[:, 2]


def _prenorm_mod(x, g, shift, scale):
    return _rms(x, g) * (1.0 + scale) + shift


def _even_in_kernel(x_ref, g_ref, sh_ref, sc_ref, w0_ref, qg_ref, wq_ref, kvg_ref, wkv_ref,
                    cq_ref, ck_ref, sn_ref,
                    qa_ref, ka_ref, va_ref, qb_ref, kb_ref, vb_ref):
    h = _prenorm_mod(x_ref[0], g_ref[...], sh_ref[0], sc_ref[0]).astype(BF16)
    proj = _dot(h, w0_ref[...])
    cqn = _rms(proj[:, 0:256], qg_ref[...]).astype(BF16)
    qq = _dot(cqn, wq_ref[...])
    ckvn = _rms(proj[:, 256:384], kvg_ref[...]).astype(BF16)
    kv = _dot(ckvn, wkv_ref[...])
    cq = cq_ref[...]
    ck = ck_ref[...]
    sn = sn_ref[...]
    krope = proj[:, 384:512] * ck + proj[:, 512:640] * sn
    nh = MLA_HEADS
    for hd in range(nh):
        lo = hd * LANES
        qh = qq[:, lo:lo + LANES] * cq + qq[:, nh * LANES + lo:nh * LANES + lo + LANES] * sn
        _store_key_tiles(qa_ref, hd, qh)
        ka_ref[0, :, lo:lo + LANES] = (kv[:, lo:lo + LANES] + krope).astype(BF16)
    for pr in range(nh // 2):
        _store_key_tiles(va_ref, pr, kv[:, nh * LANES + pr * LANES:nh * LANES + (pr + 1) * LANES])
    qb_ref[0] = proj[:, 640:1152].astype(BF16)
    kb_ref[0] = proj[:, 1152:1664].astype(BF16)
    vb_ref[0] = proj[:, 1664:2176].astype(BF16)


def _even_in(x, g, shift, scale, w0, qg, wq, kvg, wkv, cq, ck, sn):
    b, s, d = x.shape
    tm = TOK_TILE
    tkb = MLA_TILE
    ns = s // tm
    tok = lambda w: pl.BlockSpec((1, tm, w), lambda bi, i: (bi, i, 0))
    vec = lambda w: pl.BlockSpec((1, w), lambda bi, i: (0, 0))
    mod = pl.BlockSpec((1, 1, d), lambda bi, i: (bi, 0, 0))
    tab = pl.BlockSpec((tm, LANES), lambda bi, i: (i, 0))
    full = lambda a: _resident(a.shape, lambda bi, i: (0,) * a.ndim)
    out_shapes = (
        jax.ShapeDtypeStruct((b, MLA_HEADS, s // tkb, LANES, tkb), BF16),
        jax.ShapeDtypeStruct((b, s, MLA_HEADS * LANES), BF16),
        jax.ShapeDtypeStruct((b, MLA_HEADS // 2, s // tkb, LANES, tkb), BF16),
        jax.ShapeDtypeStruct((b, s, 512), BF16),
        jax.ShapeDtypeStruct((b, s, 512), BF16),
        jax.ShapeDtypeStruct((b, s, 512), BF16),
    )
    out_specs = (
        pl.BlockSpec((1, MLA_HEADS, tm // tkb, LANES, tkb), lambda bi, i: (bi, 0, i, 0, 0)),
        tok(MLA_HEADS * LANES),
        pl.BlockSpec((1, MLA_HEADS // 2, tm // tkb, LANES, tkb), lambda bi, i: (bi, 0, i, 0, 0)),
        tok(512), tok(512), tok(512),
    )
    return pl.pallas_call(
        _even_in_kernel,
        grid=(b, ns),
        in_specs=[tok(d), vec(d), mod, mod, full(w0), vec(MLA_Q_RANK), full(wq), vec(MLA_KV_RANK), full(wkv),
                  tab, tab, tab],
        out_specs=out_specs,
        out_shape=out_shapes,
        compiler_params=_cparams(("parallel", "parallel")),
        name="even_in",
    )(x, g, shift, scale, w0, qg, wq, kvg, wkv, cq, ck, sn)


def _store_key_tiles(kt_ref, hd, k):
    tkb = kt_ref.shape[4]
    for t in range(k.shape[0] // tkb):
        kt_ref[0, hd, t] = k[t * tkb:(t + 1) * tkb, :].T.astype(BF16)


def _odd_in_kernel(x_ref, g_ref, sh_ref, sc_ref, w_ref,
                   qdt_ref, kd_ref, vdt_ref, qs_ref, kst_ref, vs_ref):
    h = _prenorm_mod(x_ref[0], g_ref[...], sh_ref[0], sc_ref[0]).astype(BF16)
    proj = _dot(h, w_ref[...])
    kd_ref[0] = proj[:, 512:1024].astype(BF16)
    qs_ref[0] = proj[:, 1536:2048].astype(BF16)
    vs_ref[0] = proj[:, 2560:3072].astype(BF16)
    for hd in range(4):
        _store_key_tiles(qdt_ref, hd, proj[:, hd * LANES:(hd + 1) * LANES])
        _store_key_tiles(vdt_ref, hd, proj[:, 1024 + hd * LANES:1024 + (hd + 1) * LANES])
        _store_key_tiles(kst_ref, hd, proj[:, 2048 + hd * LANES:2048 + (hd + 1) * LANES])


def _odd_in(x, g, shift, scale, w):
    b, s, d = x.shape
    tm = TOK_TILE
    ns = s // tm
    tok = lambda wd: pl.BlockSpec((1, tm, wd), lambda bi, i: (bi, i, 0))
    mod = pl.BlockSpec((1, 1, d), lambda bi, i: (bi, 0, 0))
    ktspec = lambda tkb: pl.BlockSpec((1, 4, tm // tkb, LANES, tkb), lambda bi, i: (bi, 0, i, 0, 0))
    act = jax.ShapeDtypeStruct((b, s, 512), BF16)
    kts = lambda tkb: jax.ShapeDtypeStruct((b, 4, s // tkb, LANES, tkb), BF16)
    return pl.pallas_call(
        _odd_in_kernel,
        grid=(b, ns),
        in_specs=[tok(d), pl.BlockSpec((1, d), lambda bi, i: (0, 0)), mod, mod,
                  _resident(w.shape, lambda bi, i: (0, 0))],
        out_specs=(ktspec(DIFF_TILE), tok(512), ktspec(DIFF_TILE), tok(512), ktspec(SB_TILE), tok(512)),
        out_shape=(kts(DIFF_TILE), act, kts(DIFF_TILE), act, kts(SB_TILE), act),
        compiler_params=_cparams(("parallel", "parallel")),
        name="odd_in",
    )(x, g, shift, scale, w)


def _mla_kernel(qt_ref, k_ref, vt_ref, o_ref):
    tq = qt_ref.shape[4]
    tk = vt_ref.shape[4]
    hv = MLA_V
    qi = pl.program_id(2)
    causal = (lax.broadcasted_iota(jnp.int32, (tk, tq), 0)
              <= lax.broadcasted_iota(jnp.int32, (tk, tq), 1))
    qts = (qt_ref[0, 0, 0], qt_ref[0, 1, 0])

    def step(js, carry, masked):
        logits, values = [], []
        for hd in range(2):
            s_list = [_dot(k_ref[0, pl.ds(pl.multiple_of(j * tk, tk), tk), hd * LANES:(hd + 1) * LANES], qts[hd])
                      for j in js]
            if masked:
                s_list = [jnp.where(causal, s, NEG) for s in s_list]
            logits.append(s_list)
            values.append([vt_ref[0, 0, j, hd * hv:(hd + 1) * hv, :] for j in js])
        return _softmax_step_t(logits, values, carry)

    one = (jnp.full((1, tq), NEG, F32), jnp.zeros((1, tq), F32), jnp.zeros((hv, tq), F32))
    carry = _loop_pairs(0, qi, functools.partial(step, masked=False), (one, one), group=4)
    (_, l0, a0), (_, l1, a1) = step((qi,), carry, True)
    o_ref[0] = jnp.concatenate([a0 / l0, a1 / l1], axis=0).T.astype(BF16)


def _mla(qt, k, vt):
    b, s, _ = k.shape
    tq = qt.shape[4]
    nk = vt.shape[2]
    tk = vt.shape[4]
    return pl.pallas_call(
        _mla_kernel,
        grid=(b, MLA_HEADS // 2, s // tq),
        in_specs=[
            pl.BlockSpec((1, 2, 1, LANES, tq), lambda bi, hp, qi: (bi, hp, qi, 0, 0)),
            pl.BlockSpec((1, s, 2 * LANES), lambda bi, hp, qi: (bi, 0, hp)),
            pl.BlockSpec((1, 1, nk, LANES, tk), lambda bi, hp, qi: (bi, hp, 0, 0, 0)),
        ],
        out_specs=pl.BlockSpec((1, tq, LANES), lambda bi, hp, qi: (bi, qi, hp)),
        out_shape=jax.ShapeDtypeStruct((b, s, 512), BF16),
        compiler_params=_cparams(("parallel", "parallel", "arbitrary")),
        name="mla",
    )(qt, k, vt)


def _dil_kernel(q_ref, kc_ref, kp_ref, vc_ref, vp_ref, bias_ref, o_ref,
                q32, k32, v32, acc_s, m_s, d_s):
    sup = DIL_SUPER
    blk = DIL_BLOCK
    n = pl.program_id(2)
    q32[...] = q_ref[0].astype(F32)
    k32[0:sup, :] = kp_ref[0].astype(F32)
    k32[sup:2 * sup, :] = kc_ref[0].astype(F32)
    v32[0:sup, :] = vp_ref[0].astype(F32)
    v32[sup:2 * sup, :] = vc_ref[0].astype(F32)
    low = lax.broadcasted_iota(jnp.int32, (blk, LANES), 1) < 64
    before_start = jnp.where(lax.broadcasted_iota(jnp.int32, (blk, 2 * blk), 1) < blk, NEG, 0.0)

    for g, (_, dil) in enumerate(DIL_PATTERNS):

        def unit(u, carry, g=g, dil=dil):
            n_loc = u // dil
            r = u % dil
            qs = n_loc * (blk * dil) + r
            ks = sup + (n_loc - 1) * (blk * dil) + r
            if dil == 1:
                qsl = pl.ds(pl.multiple_of(qs, blk), blk)
                ksl = pl.ds(pl.multiple_of(ks, blk), 2 * blk)
            else:
                qsl = pl.ds(qs, blk, stride=dil)
                ksl = pl.ds(ks, 2 * blk, stride=dil)
            q = q32[qsl, :]
            k = k32[ksl, :].astype(BF16)
            v = v32[ksl, :].astype(BF16)
            extra = jnp.where(jnp.logical_and(n == 0, n_loc == 0), before_start, 0.0)
            parts = []
            for hd in range(2):
                qh = jnp.where(low if hd == 0 else jnp.logical_not(low), q, 0.0).astype(BF16)
                s = lax.dot_general(qh, k, (((1,), (1,)), ((), ())), preferred_element_type=F32)
                s = s + bias_ref[g, hd] + extra
                m = jnp.max(s, axis=-1, keepdims=True)
                e = jnp.exp2(s - m)
                den = jnp.sum(e, axis=-1, keepdims=True)
                parts.append((_dot(e.astype(BF16), v), m, den))
            acc_s[g, qsl, :] = jnp.where(low, parts[0][0], parts[1][0])
            m_s[g, qsl, :] = jnp.where(low, parts[0][1], parts[1][1])
            d_s[g, qsl, :] = jnp.where(low, parts[0][2], parts[1][2])
            return carry

        lax.fori_loop(0, 16, unit, 0, unroll=16)

    mx = jnp.maximum(jnp.maximum(m_s[0], m_s[1]), m_s[2])
    num = jnp.zeros((sup, LANES), F32)
    den = jnp.zeros((sup, LANES), F32)
    for g in range(3):
        a = jnp.exp2(m_s[g] - mx)
        num = num + a * acc_s[g]
        den = den + a * d_s[g]
    o_ref[0] = (num / den).astype(BF16)


def _dil(q, k, v, bias):
    b, s, _ = q.shape
    sup = DIL_SUPER
    cur = pl.BlockSpec((1, sup, LANES), lambda bi, hp, n: (bi, n, hp))
    prev = pl.BlockSpec((1, sup, LANES), lambda bi, hp, n: (bi, jnp.maximum(n - 1, 0), hp))
    return pl.pallas_call(
        _dil_kernel,
        grid=(b, DIL_HEADS // 2, s // sup),
        in_specs=[cur, cur, prev, cur, prev,
                  pl.BlockSpec((3, 2, DIL_BLOCK, 2 * DIL_BLOCK), lambda bi, hp, n: (0, hp, 0, 0))],
        out_specs=cur,
        out_shape=jax.ShapeDtypeStruct((b, s, 512), BF16),
        scratch_shapes=[
            pltpu.VMEM((sup, LANES), F32),
            pltpu.VMEM((2 * sup, LANES), F32),
            pltpu.VMEM((2 * sup, LANES), F32),
            pltpu.VMEM((3, sup, LANES), F32),
            pltpu.VMEM((3, sup, LANES), F32),
            pltpu.VMEM((3, sup, LANES), F32),
        ],
        compiler_params=_cparams(("parallel", "parallel", "arbitrary")),
        name="dilated",
    )(q, k, k, v, v, bias)


def _diff_kernel(qt_ref, k_ref, vt_ref, bias_ref, far_ref, lam_ref, g_ref, o_ref, *, lam_init):
    tq = qt_ref.shape[4]
    tk = vt_ref.shape[4]
    nd = bias_ref.shape[2]
    qi = pl.program_id(2)
    qt = qt_ref[0, 0, 0]
    row = lax.broadcasted_iota(jnp.int32, (LANES, tq), 0)
    zero = jnp.zeros_like(qt)
    qm = (jnp.where(row < DIFF_HD, qt, zero), jnp.where(row >= DIFF_HD, qt, zero))

    def step(js, carry, bias_of):
        keys = [k_ref[0, pl.ds(pl.multiple_of(j * tk, tk), tk), :] for j in js]
        vts = [vt_ref[0, 0, j] for j in js]
        logits = [[_dot(k, qm[mi]) + bias_of(mi, j) for j, k in zip(js, keys)] for mi in range(2)]
        return _softmax_step_t(logits, [vts, vts], carry)

    one = (jnp.full((1, tq), NEG, F32), jnp.zeros((1, tq), F32), jnp.zeros((LANES, tq), F32))
    carry = (one, one)
    n_far = jnp.maximum(qi - nd + 1, 0)
    carry = _loop_pairs(0, n_far, functools.partial(step, bias_of=lambda mi, j: far_ref[0, mi, 0:1, 0:1]), carry,
                        group=4)
    carry = _loop_pairs(n_far, qi, functools.partial(step, bias_of=lambda mi, j: bias_ref[0, mi, qi - j]), carry,
                        group=4)
    carry = step((qi,), carry, lambda mi, j: bias_ref[0, mi, 0])
    (_, l0, a0), (_, l1, a1) = carry
    o = (a0 / l0 - lam_ref[0:1, 0:1] * (a1 / l1)).T
    o_ref[0] = (_rms(o, g_ref[...]) * (1.0 - lam_init)).astype(BF16)


def _diff(qt, k, vt, bias, far, lam, sub_g, lam_init):
    b, s, _ = k.shape
    tq = qt.shape[4]
    nk, tk = vt.shape[2], vt.shape[4]
    nd = bias.shape[2]
    return pl.pallas_call(
        functools.partial(_diff_kernel, lam_init=lam_init),
        grid=(DIFF_HEADS, b, s // tq),
        in_specs=[
            pl.BlockSpec((1, 1, 1, LANES, tq), lambda h, bi, qi: (bi, h, qi, 0, 0)),
            pl.BlockSpec((1, s, LANES), lambda h, bi, qi: (bi, 0, h)),
            pl.BlockSpec((1, 1, nk, LANES, tk), lambda h, bi, qi: (bi, h, 0, 0, 0)),
            _resident((1, 2, nd, tk, tq), lambda h, bi, qi: (h, 0, 0, 0, 0)),
            pl.BlockSpec((1, 2, 8, LANES), lambda h, bi, qi: (h, 0, 0, 0)),
            pl.BlockSpec((1, LANES), lambda h, bi, qi: (0, 0)),
            pl.BlockSpec((1, LANES), lambda h, bi, qi: (0, 0)),
        ],
        out_specs=pl.BlockSpec((1, tq, LANES), lambda h, bi, qi: (bi, qi, h)),
        out_shape=jax.ShapeDtypeStruct((b, s, 512), BF16),
        compiler_params=_cparams(("parallel", "parallel", "arbitrary")),
        name="diff",
    )(qt, k, vt, bias, far, lam, sub_g)


def _sb_kernel(q_ref, kt_ref, v_ref, o_ref):
    tq = q_ref.shape[1]
    tk = kt_ref.shape[4]
    qi = pl.program_id(2)
    lane = lax.broadcasted_iota(jnp.int32, (tq, LANES), 1)
    strict = (lax.broadcasted_iota(jnp.int32, (tq, tk), 1)
              < lax.broadcasted_iota(jnp.int32, (tq, tk), 0))
    later = (lax.broadcasted_iota(jnp.int32, (tk, tk), 0)
             > lax.broadcasted_iota(jnp.int32, (tk, tk), 1)).astype(BF16)
    q = q_ref[0]
    zero = jnp.zeros_like(q)
    qh = (jnp.where(lane < 64, q, zero), jnp.where(lane >= 64, q, zero))

    def blocks(js, state, masked):
        items = [(bi, hd) for bi in range(len(js)) for hd in range(2)]
        z = {it: _dot(qh[it[1]], kt_ref[0, 0, js[it[0]]]) for it in items}
        log_sig, log_1m, inblock = {}, {}, {}
        for it in items:
            neg = -z[it]
            t = jnp.log2(1.0 + jnp.exp2(jnp.minimum(z[it], neg)))
            log_sig[it] = jnp.minimum(z[it], 0.0) - t
            l1m = jnp.minimum(neg, 0.0) - t
            if masked:
                l1m = jnp.where(strict, l1m, 0.0)
            log_1m[it] = l1m
            inblock[it] = _dot(l1m.astype(BF16), later)
        state = list(state)
        for bi, j in enumerate(js):
            v = v_ref[0, pl.ds(pl.multiple_of(j * tk, tk), tk), :]
            for hd in range(2):
                it = (bi, hd)
                c, acc = state[hd]
                w = jnp.exp2(log_sig[it] + (inblock[it] + c))
                if masked:
                    w = jnp.where(strict, w, 0.0)
                acc = acc + _dot(w.astype(BF16), v)
                c = c + jnp.sum(log_1m[it], axis=-1, keepdims=True)
                state[hd] = (c, acc)
        return tuple(state)

    one = (jnp.zeros((tq, 1), F32), jnp.zeros((tq, LANES), F32))
    state = blocks((qi,), (one, one), True)

    odd = qi % 2
    state = lax.cond(odd == 1, lambda st: blocks((qi - 1,), st, False), lambda st: st, state)
    floor = SB_LOG_FLOOR * LOG2E

    def cond(st):
        j, ((c0, _), (c1, _)) = st
        return jnp.logical_and(j >= 1, jnp.max(jnp.maximum(c0, c1)) > floor)

    def body(st):
        j, state = st
        return j - 2, blocks((j, j - 1), state, False)

    _, ((_, a0), (_, a1)) = lax.while_loop(cond, body, (qi - 1 - odd, state))
    o_ref[0] = jnp.where(lane < 64, a0, a1).astype(BF16)


def _sb(q, kt, v):
    b, s, _ = q.shape
    tq = SB_TILE
    nk, tk = kt.shape[2], kt.shape[4]
    return pl.pallas_call(
        _sb_kernel,
        grid=(b, SB_HEADS // 2, s // tq),
        in_specs=[
            pl.BlockSpec((1, tq, LANES), lambda bi, hp, qi: (bi, qi, hp)),
            pl.BlockSpec((1, 1, nk, LANES, tk), lambda bi, hp, qi: (bi, hp, 0, 0, 0)),
            pl.BlockSpec((1, s, LANES), lambda bi, hp, qi: (bi, 0, hp)),
        ],
        out_specs=pl.BlockSpec((1, tq, LANES), lambda bi, hp, qi: (bi, qi, hp)),
        out_shape=jax.ShapeDtypeStruct((b, s, 512), BF16),
        compiler_params=_cparams(("parallel", "parallel", "arbitrary")),
        name="stick_breaking",
    )(q, kt, v)


FF_CHUNKS = ((0, 768), (768, 1536), (1536, 2304), (2304, 2816))
EXPERT_CHUNKS = ((0, 1024), (1024, 2048), (2048, 3072), (3072, 3584))


def _swiglu(hb, wg_ref, wu_ref, wd_ref, chunks, lead, between=None):
    acc = None
    hb_of = hb if callable(hb) else (lambda ci: hb)
    for ci, (c0, c1) in enumerate(chunks):
        hb = hb_of(ci)
        g = _dot(hb, wg_ref[lead + (slice(None), slice(c0, c1))])
        u = _dot(hb, wu_ref[lead + (slice(None), slice(c0, c1))])
        a = (g / (1.0 + jnp.exp(-g)) * u).astype(BF16)
        part = _dot(a, wd_ref[lead + (slice(c0, c1), slice(None))])
        acc = part if acc is None else acc + part
        if between is not None:
            between(ci)
    return acc


def _mix_out(x_ref, oa_ref, ob_ref, wa_ref, wb_ref, gate_ref, pg_ref):
    y = _dot(oa_ref[0], wa_ref[...]) + _dot(ob_ref[0], wb_ref[...])
    return x_ref[0] + gate_ref[0] * _rms(y, pg_ref[...])


def _post_even_kernel(x_ref, oa_ref, ob_ref, wa_ref, wb_ref, gate_ref, pg_ref,
                      fg_ref, fsh_ref, fsc_ref, fgate_ref, fpg_ref, wg_ref, wu_ref, wd_ref, o_ref):
    x1 = _mix_out(x_ref, oa_ref, ob_ref, wa_ref, wb_ref, gate_ref, pg_ref)
    hb = _prenorm_mod(x1, fg_ref[...], fsh_ref[0], fsc_ref[0]).astype(BF16)
    y = _swiglu(hb, wg_ref, wu_ref, wd_ref, FF_CHUNKS, ())
    o_ref[0] = x1 + fgate_ref[0] * _rms(y, fpg_ref[...])


def _post_even(x, oa, ob, wa, wb, gate, pg, fg, fsh, fsc, fgate, fpg, wg, wu, wd):
    b, s, d = x.shape
    tm = TOK_TILE
    tok = lambda w: pl.BlockSpec((1, tm, w), lambda bi, i: (bi, i, 0))
    vec = pl.BlockSpec((1, d), lambda bi, i: (0, 0))
    mod = pl.BlockSpec((1, 1, d), lambda bi, i: (bi, 0, 0))
    full = lambda a: _resident(a.shape, lambda bi, i: (0,) * a.ndim)
    return pl.pallas_call(
        _post_even_kernel,
        grid=(b, s // tm),
        in_specs=[tok(d), tok(512), tok(512), full(wa), full(wb), mod, vec,
                  vec, mod, mod, mod, vec, full(wg), full(wu), full(wd)],
        out_specs=tok(d),
        out_shape=jax.ShapeDtypeStruct((b, s, d), F32),
        compiler_params=_cparams(("parallel", "parallel")),
        name="post_even",
    )(x, oa, ob, wa, wb, gate, pg, fg, fsh, fsc, fgate, fpg, wg, wu, wd)


def _post_odd_kernel(x_ref, oa_ref, ob_ref, wa_ref, wb_ref, gate_ref, pg_ref,
                     fg_ref, fsh_ref, fsc_ref, rw_ref, rb_ref, x_out, h_out, r_out, cnt_ref):
    x1 = _mix_out(x_ref, oa_ref, ob_ref, wa_ref, wb_ref, gate_ref, pg_ref)
    x_out[0] = x1
    h = _prenorm_mod(x1, fg_ref[...], fsh_ref[0], fsc_ref[0])
    h_out[0] = h
    logits = _dot(h, rw_ref[...]) + rb_ref[...]
    lane = lax.broadcasted_iota(jnp.int32, logits.shape, 1)
    m1 = jnp.max(logits, axis=-1, keepdims=True)
    i1 = jnp.min(jnp.where(logits == m1, lane, LANES), axis=-1, keepdims=True)
    rest = jnp.where(lane == i1, NEG, logits)
    m2 = jnp.max(rest, axis=-1, keepdims=True)
    i2 = jnp.min(jnp.where(rest == m2, lane, LANES), axis=-1, keepdims=True)
    e2 = jnp.exp(m2 - m1)
    w1 = 1.0 / (1.0 + e2)
    w2 = e2 / (1.0 + e2)
    @pl.when(jnp.logical_and(pl.program_id(0) == 0, pl.program_id(1) == 0))
    def _():
        cnt_ref[...] = jnp.zeros_like(cnt_ref)

    tm = logits.shape[0]
    sel = jnp.logical_or(lane == i1, lane == i2)
    earlier = (lax.broadcasted_iota(jnp.int32, (tm, tm), 1)
               < lax.broadcasted_iota(jnp.int32, (tm, tm), 0)).astype(BF16)
    prefix = _dot(earlier, sel.astype(BF16)) + cnt_ref[0:1, :]
    rank1 = jnp.sum(jnp.where(lane == i1, prefix, 0.0), axis=-1, keepdims=True)
    rank2 = jnp.sum(jnp.where(lane == i2, prefix, 0.0), axis=-1, keepdims=True)
    cnt_ref[...] = cnt_ref[...] + jnp.sum(sel.astype(F32), axis=0, keepdims=True)
    r = jnp.where(lane == 0, i1.astype(F32), 0.0)
    r = jnp.where(lane == 1, i2.astype(F32), r)
    r = jnp.where(lane == 2, w1, r)
    r = jnp.where(lane == 3, w2, r)
    r = jnp.where(lane == 4, rank1, r)
    r = jnp.where(lane == 5, rank2, r)
    r_out[0] = r


def _post_odd(x, oa, ob, wa, wb, gate, pg, fg, fsh, fsc, rw, rb):
    b, s, d = x.shape
    tm = TOK_TILE
    tok = lambda w: pl.BlockSpec((1, tm, w), lambda bi, i: (bi, i, 0))
    vec = pl.BlockSpec((1, d), lambda bi, i: (0, 0))
    mod = pl.BlockSpec((1, 1, d), lambda bi, i: (bi, 0, 0))
    full = lambda a: _resident(a.shape, lambda bi, i: (0,) * a.ndim)
    return pl.pallas_call(
        _post_odd_kernel,
        grid=(b, s // tm),
        in_specs=[tok(d), tok(512), tok(512), full(wa), full(wb), mod, vec,
                  vec, mod, mod, full(rw), pl.BlockSpec((1, LANES), lambda bi, i: (0, 0))],
        out_specs=(tok(d), tok(d), tok(LANES),
                   pl.BlockSpec((8, LANES), lambda bi, i: (0, 0))),
        out_shape=(jax.ShapeDtypeStruct((b, s, d), F32), jax.ShapeDtypeStruct((b, s, d), F32),
                   jax.ShapeDtypeStruct((b, s, LANES), F32), jax.ShapeDtypeStruct((8, LANES), F32)),
        compiler_params=_cparams(("arbitrary", "arbitrary")),
        name="post_odd",
    )(x, oa, ob, wa, wb, gate, pg, fg, fsh, fsc, rw, rb)


def _store_rows(ref, lead, val):
    for c in range(ROW_SUB):
        ref[lead + (slice(None), c, slice(None))] = val[:, c * LANES:(c + 1) * LANES]


def _load_rows(ref, lead, start, size):
    return jnp.concatenate([ref[lead, pl.ds(start, size), c, :] for c in range(ROW_SUB)], axis=1)


class _TileGather:
    def __init__(self, idx_hbm, src_hbm, idx_smem, buf, isem, sem, tile_rows):
        self.idx_hbm, self.src_hbm, self.idx_smem, self.buf = idx_hbm, src_hbm, idx_smem, buf
        self.isem, self.sem, self.tile_rows = isem, sem, tile_rows
        self.i = pl.program_id(0)
        self.nt = pl.num_programs(0)
        self.n = buf.shape[1]
        self.slot = self.i % 2
        self.nxt = 1 - self.slot

    def _idx_copy(self, t, sl):
        t = jnp.minimum(t, self.nt - 1)
        return pltpu.make_async_copy(self.idx_hbm.at[t], self.idx_smem.at[sl], self.isem.at[sl])

    def _rows_wait(self, sl):
        pltpu.make_async_copy(self.src_hbm.at[pl.ds(0, self.n)], self.buf.at[sl], self.sem.at[sl]).wait()

    def _issue_row(self, sl, r):
        t = self.idx_smem[sl, r]
        if self.tile_rows:
            src, dst = self.src_hbm.at[t], self.buf.at[sl, r]
        else:
            src, dst = self.src_hbm.at[pl.ds(t, 1)], self.buf.at[sl, pl.ds(r, 1)]
        pltpu.make_async_copy(src, dst, self.sem.at[sl]).start()

    def _issue_loop(self, sl):
        def body(r, carry):
            self._issue_row(sl, r)
            return carry

        lax.fori_loop(0, self.n, body, 0, unroll=8)

    def begin(self):
        @pl.when(self.i == 0)
        def _():
            first = self._idx_copy(0, 0)
            first.start()
            first.wait()
            self._issue_loop(0)
            self._idx_copy(1, 1).start()

        self._idx_copy(self.i + 1, self.nxt).wait()
        self._idx_copy(self.i + 2, self.slot).start()
        self._rows_wait(self.slot)

    def issue(self, part, parts):
        per = -(-self.n // parts)
        for r in range(part * per, min((part + 1) * per, self.n)):
            self._issue_row(self.nxt, r)

    def issue_all(self):
        self._issue_loop(self.nxt)

    def anchor(self, zero):
        return zero

    def finish(self):
        @pl.when(self.i == self.nt - 1)
        def _():
            self._rows_wait(self.nxt)
            self._idx_copy(self.i + 2, self.slot).wait()


def _moe_kernel(te_ref, nu_ref, zero_ref, tok_hbm, h_hbm, wg_ref, wu_ref, wd_ref, y_ref, idx_smem, buf, isem, sem):
    g = _TileGather(tok_hbm, h_hbm, idx_smem, buf, isem, sem, tile_rows=False)
    n = g.n
    g.begin()

    @pl.when(g.i < nu_ref[0])
    def _():
        anchor = [0]
        head = 16
        rest = buf[g.slot, head:n, :].astype(BF16)

        def rows_of(ci):
            first = buf[g.slot, pl.ds(pl.multiple_of(anchor[0], head), head), :].astype(BF16)
            return jnp.concatenate([first, rest], axis=0)

        def issue_part(ci):
            if ci < len(EXPERT_CHUNKS) - 1:
                g.issue(ci, len(EXPERT_CHUNKS) - 1)
                anchor[0] = g.anchor(zero_ref[0])

        y = _swiglu(rows_of, wg_ref, wu_ref, wd_ref, EXPERT_CHUNKS, (0,), between=issue_part)
        _store_rows(y_ref, (), y)

    @pl.when(g.i >= nu_ref[0])
    def _():
        g.issue_all()
        y_ref[...] = jnp.zeros_like(y_ref)

    g.finish()


def _moe(tile_expert, n_used, row_token, h, wg, wu, wd):
    d = h.shape[1]
    nt, tm = row_token.shape
    dff = wg.shape[2]
    wspec = lambda shp: pl.BlockSpec(shp, lambda i, te, nu, z: (te[i], 0, 0), pipeline_mode=pl.Buffered(1))
    grid_spec = pltpu.PrefetchScalarGridSpec(
        num_scalar_prefetch=3,
        grid=(nt,),
        in_specs=[
            pl.BlockSpec(memory_space=pl.ANY),
            pl.BlockSpec(memory_space=pl.ANY),
            wspec((1, d, dff)), wspec((1, d, dff)), wspec((1, dff, d)),
        ],
        out_specs=pl.BlockSpec((tm, ROW_SUB, LANES), lambda i, te, nu, z: (i, 0, 0)),
        scratch_shapes=[
            pltpu.SMEM((2, tm), jnp.int32),
            pltpu.VMEM((2, tm, d), F32),
            pltpu.SemaphoreType.DMA((2,)),
            pltpu.SemaphoreType.DMA((2,)),
        ],
    )
    return pl.pallas_call(
        _moe_kernel,
        grid_spec=grid_spec,
        out_shape=jax.ShapeDtypeStruct((nt * tm, ROW_SUB, LANES), F32),
        compiler_params=_cparams(("arbitrary",)),
        name="moe_experts",
    )(tile_expert, n_used, jnp.zeros((1,), jnp.int32), row_token, h, wg, wu, wd)


def _combine_kernel(zero_ref, pos_hbm, y_hbm, x_ref, r_ref, gate_ref, pg_ref, o_ref, idx_smem, buf, isem, sem):
    tm = x_ref.shape[0]
    g = _TileGather(pos_hbm, y_hbm, idx_smem, buf, isem, sem, tile_rows=True)
    g.begin()
    parts = 4
    rows = tm // parts
    off = 0
    for c in range(parts):
        lo = c * rows
        first = _load_rows(buf, g.slot, pl.multiple_of(off + lo, 8), rows)
        second = _load_rows(buf, g.slot, pl.multiple_of(off + tm + lo, 8), rows)
        r = r_ref[lo:lo + rows, :]
        y = r[:, 2:3] * first + r[:, 3:4] * second
        o_ref[lo:lo + rows, :] = x_ref[lo:lo + rows, :] + gate_ref[0] * _rms(y, pg_ref[...])
        if c < parts - 1:
            g.issue(c, parts - 1)
            off = g.anchor(zero_ref[0])
    g.finish()


def _combine(pos, y, x, r, gate, pg, tokens_per_seq):
    n_tok, d = x.shape
    nt, tm2 = pos.shape
    tm = tm2 // 2
    per_seq = tokens_per_seq // tm
    tok = lambda w: pl.BlockSpec((tm, w), lambda i, z: (i, 0))
    grid_spec = pltpu.PrefetchScalarGridSpec(
        num_scalar_prefetch=1,
        grid=(nt,),
        in_specs=[
            pl.BlockSpec(memory_space=pl.ANY),
            pl.BlockSpec(memory_space=pl.ANY),
            tok(d), tok(LANES),
            pl.BlockSpec((1, 1, d), lambda i, z: (i // per_seq, 0, 0)),
            pl.BlockSpec((1, d), lambda i, z: (0, 0)),
        ],
        out_specs=tok(d),
        scratch_shapes=[
            pltpu.SMEM((2, tm2), jnp.int32),
            pltpu.VMEM((2, tm2, ROW_SUB, LANES), F32),
            pltpu.SemaphoreType.DMA((2,)),
            pltpu.SemaphoreType.DMA((2,)),
        ],
    )
    return pl.pallas_call(
        _combine_kernel,
        grid_spec=grid_spec,
        out_shape=jax.ShapeDtypeStruct((n_tok, d), F32),
        compiler_params=_cparams(("arbitrary",)),
        name="moe_combine",
    )(jnp.zeros((1,), jnp.int32), pos, y, x, r, gate, pg)


def _t5_bucket(dist):
    max_exact = REL_BUCKETS // 2
    d = jnp.maximum(dist, 1).astype(F32)
    log_b = max_exact + (jnp.log(d / max_exact) / math.log(REL_MAX_DIST / max_exact)
                         * (REL_BUCKETS - max_exact)).astype(jnp.int32)
    log_b = jnp.minimum(log_b, REL_BUCKETS - 1)
    return jnp.where(dist < max_exact, dist, log_b)


def _rope_tables(s):
    half = MLA_ROPE // 2
    freqs = ROPE_THETA ** (-jnp.arange(half, dtype=F32) / half)
    ang = jnp.arange(s, dtype=F32)[:, None] * freqs[None, :]
    cos, sin = jnp.cos(ang), jnp.sin(ang)
    z64 = jnp.zeros((s, MLA_NOPE), F32)
    z32 = jnp.zeros((s, LANES - MLA_NOPE - MLA_ROPE), F32)
    ck = jnp.concatenate([z64, cos, cos, z32], axis=1)
    cq = jnp.concatenate([jnp.ones((s, MLA_NOPE), F32), cos, cos, z32], axis=1)
    sn = jnp.concatenate([z64, sin, sin, z32], axis=1)
    return cq, ck, sn


def _even_weights(w_in, w_uq, w_ukv):
    d = w_in.shape[0]
    half = MLA_ROPE // 2
    w_cq = w_in[:, :MLA_Q_RANK]
    w_ckv = w_in[:, MLA_Q_RANK:MLA_Q_RANK + MLA_KV_RANK]
    w_kr = w_in[:, MLA_Q_RANK + MLA_KV_RANK:MLA_Q_RANK + MLA_KV_RANK + MLA_ROPE]
    w_qkv = w_in[:, MLA_Q_RANK + MLA_KV_RANK + MLA_ROPE:]
    z = lambda n: jnp.zeros((d, n), F32)
    kr_a = jnp.concatenate([z(MLA_NOPE), w_kr, z(32)], axis=1)
    kr_b = jnp.concatenate([z(MLA_NOPE), -w_kr[:, half:], w_kr[:, :half], z(32)], axis=1)
    dil_scale = DIL_HD ** -0.5 * LOG2E
    w0 = jnp.concatenate([w_cq, w_ckv, kr_a, kr_b, w_qkv[:, :512] * dil_scale, w_qkv[:, 512:]], axis=1)

    r = w_uq.shape[0]
    wq = w_uq.reshape(r, MLA_HEADS, MLA_NOPE + MLA_ROPE) * ((MLA_NOPE + MLA_ROPE) ** -0.5 * LOG2E)
    zq = lambda n: jnp.zeros((r, MLA_HEADS, n), F32)
    nope, x1, x2 = wq[..., :MLA_NOPE], wq[..., MLA_NOPE:MLA_NOPE + half], wq[..., MLA_NOPE + half:]
    q_a = jnp.concatenate([nope, x1, x2, zq(32)], axis=-1).reshape(r, MLA_HEADS * LANES)
    q_b = jnp.concatenate([zq(MLA_NOPE), -x2, x1, zq(32)], axis=-1).reshape(r, MLA_HEADS * LANES)
    wq2 = jnp.concatenate([q_a, q_b], axis=1)

    rk = w_ukv.shape[0]
    wkv = w_ukv.reshape(rk, MLA_HEADS, MLA_NOPE + MLA_V)
    k_blk = jnp.concatenate([wkv[..., :MLA_NOPE], jnp.zeros((rk, MLA_HEADS, LANES - MLA_NOPE), F32)], axis=-1)
    wkv2 = jnp.concatenate([k_blk.reshape(rk, MLA_HEADS * LANES),
                            wkv[..., MLA_NOPE:].reshape(rk, MLA_HEADS * MLA_V)], axis=1)
    return w0.astype(BF16), wq2.astype(BF16), wkv2.astype(BF16)


def _toeplitz(vec, rows, cols):
    n, width = vec.shape

    def toeplitz_kernel(v_ref, o_ref):
        tiled = jnp.broadcast_to(v_ref[0], (rows, width))
        o_ref[0] = pltpu.roll(tiled, 0, 1, stride=1, stride_axis=0)[:, :cols]

    return pl.pallas_call(
        toeplitz_kernel,
        grid=(n,),
        in_specs=[pl.BlockSpec((1, 1, width), lambda t: (t, 0, 0))],
        out_specs=pl.BlockSpec((1, rows, cols), lambda t: (t, 0, 0)),
        out_shape=jax.ShapeDtypeStruct((n, rows, cols), F32),
        compiler_params=_cparams(("parallel",)),
        name="toeplitz_bias",
    )(vec.reshape(n, 1, width).astype(F32))


def _dil_bias(rel_bias):
    blk = DIL_BLOCK
    width = 4 * blk
    k = jnp.arange(width)
    rel = jnp.where(k < 2 * blk, blk - k, blk + width - k)
    out = []
    for window, dil in DIL_PATTERNS:
        band = (rel >= 0) & (rel <= window // dil)
        bias = rel_bias[_t5_bucket(jnp.maximum(rel, 0) * dil)] * LOG2E
        out.append(jnp.where(band[:, None], bias, NEG).T)
    vec = jnp.stack(out).reshape(3 * DIL_HEADS, width)
    return _toeplitz(vec, blk, 2 * blk).reshape(3, DIL_HEADS, blk, 2 * blk)


def _diff_bias(rel_bias, tile):
    nd = REL_MAX_DIST // tile + 1
    maps = rel_bias.shape[1]
    k = jnp.arange(2 * tile)[None, :]
    dist = jnp.arange(nd)[:, None] * tile + jnp.where(k < tile, k, k - 2 * tile)
    vec = jnp.where((dist >= 0)[..., None], rel_bias[_t5_bucket(jnp.maximum(dist, 0))] * LOG2E, NEG)
    vec = jnp.transpose(vec, (2, 0, 1)).reshape(maps * nd, 2 * tile)
    bias = _toeplitz(vec, tile, tile).reshape(DIFF_HEADS, 2, nd, tile, tile)
    far = rel_bias[_t5_bucket(jnp.array(REL_MAX_DIST))] * LOG2E
    far = jnp.broadcast_to(far.reshape(DIFF_HEADS, 2, 1, 1), (DIFF_HEADS, 2, 8, LANES))
    return bias, far.astype(F32)


def _routing(r, counts, n_tok, tile):
    n_tiles = (2 * n_tok) // tile + N_EXPERTS
    e = jnp.concatenate([r[:, 0], r[:, 1]]).astype(jnp.int32)
    rank = jnp.concatenate([r[:, 4], r[:, 5]]).astype(jnp.int32)
    counts = counts.astype(jnp.int32)
    padded = ((counts + tile - 1) // tile) * tile
    ends = jnp.cumsum(padded)
    starts = ends - padded
    onehot = (e[:, None] == jnp.arange(N_EXPERTS)[None, :]).astype(jnp.int32)
    pos = jnp.sum(onehot * starts[None, :], axis=1) + rank
    token = jnp.tile(jnp.arange(n_tok, dtype=jnp.int32), 2)
    row_token = jnp.zeros((n_tiles * tile,), jnp.int32).at[pos].set(token)
    tile_start = jnp.arange(n_tiles, dtype=jnp.int32) * tile
    tile_expert = jnp.sum((tile_start[:, None] >= ends[None, :]).astype(jnp.int32), axis=1)
    n_used = (ends[-1] // tile).astype(jnp.int32)
    last = jnp.sum((ends[-1] - 1 >= ends).astype(jnp.int32))
    tile_expert = jnp.minimum(tile_expert, last).astype(jnp.int32)
    return (tile_expert, n_used.reshape(1), row_token.reshape(n_tiles, tile),
            pos[:n_tok].astype(jnp.int32), pos[n_tok:].astype(jnp.int32))


def kernel(x, c, rel_bias, ada_mix_w, ada_mix_b, mix_pre_g, mix_post_g, ada_ffn_w, ada_ffn_b, ffn_pre_g, ffn_post_g, e_w_in, e_q_norm_g, e_w_uq, e_kv_norm_g, e_w_ukv, e_w_out, ffn_w_gate, ffn_w_up, ffn_w_down, o_w_in, diff_lq1, diff_lk1, diff_lq2, diff_lk2, diff_sub_g, o_w_out, router_w, router_b, moe_w_gate, moe_w_up, moe_w_down):
    b, s, d = x.shape
    assert d == D_MODEL and s % DIL_SUPER == 0 and s % TOK_TILE == 0
    row = lambda v: v.reshape(1, -1).astype(F32)

    mix_mod = _ada(c, ada_mix_w, ada_mix_b)
    ffn_mod = _ada(c, ada_ffn_w, ada_ffn_b)

    shift, scale, gate = _split_mod(mix_mod[0])
    w0, wq2, wkv2 = _even_weights(e_w_in[0], e_w_uq[0], e_w_ukv[0])
    cq, ck, sn = _rope_tables(s)
    qat, ka, vat, qb, kb, vb = _even_in(x, row(mix_pre_g[0]), shift, scale, w0, row(e_q_norm_g[0]), wq2,
                                        row(e_kv_norm_g[0]), wkv2, cq, ck, sn)
    o_a = _mla(qat, ka, vat)
    o_b = _dil(qb, kb, vb, _dil_bias(rel_bias))
    fshift, fscale, fgate = _split_mod(ffn_mod[0])
    w_out = e_w_out[0].astype(BF16)
    x = _post_even(x, o_a, o_b, w_out[:512], w_out[512:], gate, row(mix_post_g[0]),
                   row(ffn_pre_g[0]), fshift, fscale, fgate, row(ffn_post_g[0]),
                   ffn_w_gate[0].astype(BF16), ffn_w_up[0].astype(BF16), ffn_w_down[0].astype(BF16))

    layer = 1
    shift, scale, gate = _split_mod(mix_mod[1])
    w_in = o_w_in[0]
    att_scale = DIFF_HD ** -0.5
    w1 = jnp.concatenate([w_in[:, :512] * (att_scale * LOG2E), w_in[:, 512:1536],
                          w_in[:, 1536:2048] * (SB_HD ** -0.5 * LOG2E), w_in[:, 2048:]], axis=1).astype(BF16)
    qdt, kd, vdt, qs, kst, vs = _odd_in(x, row(mix_pre_g[1]), shift, scale, w1)
    lam_init = 0.8 - 0.6 * math.exp(-0.3 * layer)
    lam = (jnp.exp(jnp.sum(diff_lq1[0].astype(F32) * diff_lk1[0].astype(F32)))
           - jnp.exp(jnp.sum(diff_lq2[0].astype(F32) * diff_lk2[0].astype(F32))) + lam_init)
    bias, far = _diff_bias(rel_bias, DIFF_TILE)
    o_c = _diff(qdt, kd, vdt, bias, far, jnp.full((1, LANES), lam, F32), row(diff_sub_g[0]), lam_init)
    o_d = _sb(qs, kst, vs)
    fshift, fscale, fgate = _split_mod(ffn_mod[1])
    w_out = o_w_out[0].astype(BF16)
    rw = jnp.zeros((d, LANES), F32).at[:, :N_EXPERTS].set(router_w[0].astype(F32))
    rb = jnp.full((1, LANES), NEG, F32).at[0, :N_EXPERTS].set(router_b[0].astype(F32))
    x, h, r, counts = _post_odd(x, o_c, o_d, w_out[:512], w_out[512:], gate, row(mix_post_g[1]),
                                row(ffn_pre_g[1]), fshift, fscale, rw, rb)

    n_tok = b * s
    tile_expert, n_used, row_token, pos0, pos1 = _routing(r.reshape(n_tok, LANES), counts[0, :N_EXPERTS],
                                                          n_tok, MOE_TILE)
    y = _moe(tile_expert, n_used, row_token, h.reshape(n_tok, d),
             moe_w_gate[0].astype(BF16), moe_w_up[0].astype(BF16), moe_w_down[0].astype(BF16))
    ct = TOK_TILE
    pos = jnp.concatenate([pos0.reshape(n_tok // ct, ct), pos1.reshape(n_tok // ct, ct)], axis=1)
    out = _combine(pos, y, x.reshape(n_tok, d), r.reshape(n_tok, LANES), fgate, row(ffn_post_g[1]), s)
    return out.reshape(b, s, d)
```

```python
import functools
import math

import jax
import jax.numpy as jnp
from jax import lax
from jax.experimental import pallas as pl
from jax.experimental.pallas import tpu as pltpu

F32 = jnp.float32
BF16 = jnp.bfloat16

D_MODEL = 1024
EPS = 1e-6

MLA_HEADS = 8
MLA_NOPE = 64
MLA_ROPE = 32
MLA_V = 64
MLA_Q_RANK = 256
MLA_KV_RANK = 128
ROPE_THETA = 10000.0

DIL_HEADS = 8
DIL_HD = 64
DIL_PATTERNS = ((128, 1), (512, 4), (2048, 16))
DIL_BLOCK = 128

DIFF_HEADS = 4
DIFF_HD = 64
SB_HEADS = 8
SB_HD = 64

REL_BUCKETS = 32
REL_MAX_DIST = 2048

D_FF = 2816
N_EXPERTS = 8
D_FF_EXPERT = 3584

LANES = 128
ROW_SUB = D_MODEL // LANES
LOG2E = math.log2(math.e)
NEG = -1e30

TOK_TILE = 512
MLA_TILE = 512
DIFF_TILE = 512
SB_TILE = 256
DIL_SUPER = DIL_BLOCK * 16
MOE_TILE = 512
SB_LOG_FLOOR = -104.0

VMEM_LIMIT = 56 * 1024 * 1024


def _cparams(sem):
    return pltpu.CompilerParams(dimension_semantics=sem, vmem_limit_bytes=VMEM_LIMIT)


def _resident(shape, index_map):
    return pl.BlockSpec(shape, index_map, pipeline_mode=pl.Buffered(1))


def _rms(x, g):
    return x * lax.rsqrt(jnp.mean(x * x, axis=-1, keepdims=True) + EPS) * g


def _dot(a, b):
    return jnp.dot(a, b, preferred_element_type=F32)


def _softmax_step_t(logits, values, carry):
    out = []
    for s_list, v_list, (m, l, acc) in zip(logits, values, carry):
        for s, vt in zip(s_list, v_list):
            m_new = jnp.maximum(m, jnp.max(s, axis=0, keepdims=True))
            alpha = jnp.exp2(m - m_new)
            p = jnp.exp2(s - m_new)
            l = alpha * l + jnp.sum(p, axis=0, keepdims=True)
            acc = alpha * acc + _dot(vt, p.astype(BF16))
            m = m_new
        out.append((m, l, acc))
    return tuple(out)


def _loop_pairs(lo, hi, step, carry, group=2):
    n = hi - lo
    carry = lax.fori_loop(
        0, n // group, lambda i, c: step(tuple(lo + group * i + g for g in range(group)), c), carry)
    size = group // 2
    while size >= 1:
        start = lo + (n // (2 * size)) * (2 * size)
        carry = lax.cond((n // size) % 2 == 1,
                         lambda c, start=start, size=size: step(tuple(start + g for g in range(size)), c),
                         lambda c: c, carry)
        size //= 2
    return carry


def _ada_kernel(c_ref, w_ref, b_ref, o_ref):
    c = c_ref[...]
    sc = c / (1.0 + jnp.exp(-c))
    o_ref[0] = _dot(sc.astype(BF16), w_ref[0].astype(BF16)) + b_ref[0]


def _ada(c, w, b):
    nl, d, d3 = w.shape
    bsz = c.shape[0]
    nb = d3 // d
    return pl.pallas_call(
        _ada_kernel,
        grid=(nl, nb),
        in_specs=[
            pl.BlockSpec((bsz, d), lambda l, j: (0, 0)),
            pl.BlockSpec((1, d, d), lambda l, j: (l, 0, j)),
            pl.BlockSpec((1, 1, d), lambda l, j: (l, 0, j)),
        ],
        out_specs=pl.BlockSpec((1, bsz, d), lambda l, j: (l, 0, j)),
        out_shape=jax.ShapeDtypeStruct((nl, bsz, d3), F32),
        compiler_params=_cparams(("arbitrary", "arbitrary")),
        name="ada",
    )(c, w, b.reshape(nl, 1, d3))


def _split_mod(m):
    b = m.shape[0]
    m = m.reshape(b, 3, 1, D_MODEL)
    return m[:, 0], m[:, 1], m[:, 2]


def _prenorm_mod(x, g, shift, scale):
    return _rms(x, g) * (1.0 + scale) + shift


def _even_in_kernel(x_ref, g_ref, sh_ref, sc_ref, w0_ref, qg_ref, wq_ref, kvg_ref, wkv_ref,
                    cq_ref, ck_ref, sn_ref,
                    qa_ref, ka_ref, va_ref, qb_ref, kb_ref, vb_ref):
    h = _prenorm_mod(x_ref[0], g_ref[...], sh_ref[0], sc_ref[0]).astype(BF16)
    proj = _dot(h, w0_ref[...])
    cqn = _rms(proj[:, 0:256], qg_ref[...]).astype(BF16)
    qq = _dot(cqn, wq_ref[...])
    ckvn = _rms(proj[:, 256:384], kvg_ref[...]).astype(BF16)
    kv = _dot(ckvn, wkv_ref[...])
    cq = cq_ref[...]
    ck = ck_ref[...]
    sn = sn_ref[...]
    krope = proj[:, 384:512] * ck + proj[:, 512:640] * sn
    nh = MLA_HEADS
    for hd in range(nh):
        lo = hd * LANES
        qh = qq[:, lo:lo + LANES] * cq + qq[:, nh * LANES + lo:nh * LANES + lo + LANES] * sn
        _store_key_tiles(qa_ref, hd, qh)
        ka_ref[0, :, lo:lo + LANES] = (kv[:, lo:lo + LANES] + krope).astype(BF16)
    for pr in range(nh // 2):
        _store_key_tiles(va_ref, pr, kv[:, nh * LANES + pr * LANES:nh * LANES + (pr + 1) * LANES])
    qb_ref[0] = proj[:, 640:1152].astype(BF16)
    kb_ref[0] = proj[:, 1152:1664].astype(BF16)
    vb_ref[0] = proj[:, 1664:2176].astype(BF16)


def _even_in(x, g, shift, scale, w0, qg, wq, kvg, wkv, cq, ck, sn):
    b, s, d = x.shape
    tm = TOK_TILE
    tkb = MLA_TILE
    ns = s // tm
    tok = lambda w: pl.BlockSpec((1, tm, w), lambda bi, i: (bi, i, 0))
    vec = lambda w: pl.BlockSpec((1, w), lambda bi, i: (0, 0))
    mod = pl.BlockSpec((1, 1, d), lambda bi, i: (bi, 0, 0))
    tab = pl.BlockSpec((tm, LANES), lambda bi, i: (i, 0))
    full = lambda a: _resident(a.shape, lambda bi, i: (0,) * a.ndim)
    out_shapes = (
        jax.ShapeDtypeStruct((b, MLA_HEADS, s // tkb, LANES, tkb), BF16),
        jax.ShapeDtypeStruct((b, s, MLA_HEADS * LANES), BF16),
        jax.ShapeDtypeStruct((b, MLA_HEADS // 2, s // tkb, LANES, tkb), BF16),
        jax.ShapeDtypeStruct((b, s, 512), BF16),
        jax.ShapeDtypeStruct((b, s, 512), BF16),
        jax.ShapeDtypeStruct((b, s, 512), BF16),
    )
    out_specs = (
        pl.BlockSpec((1, MLA_HEADS, tm // tkb, LANES, tkb), lambda bi, i: (bi, 0, i, 0, 0)),
        tok(MLA_HEADS * LANES),
        pl.BlockSpec((1, MLA_HEADS // 2, tm // tkb, LANES, tkb), lambda bi, i: (bi, 0, i, 0, 0)),
        tok(512), tok(512), tok(512),
    )
    return pl.pallas_call(
        _even_in_kernel,
        grid=(b, ns),
        in_specs=[tok(d), vec(d), mod, mod, full(w0), vec(MLA_Q_RANK), full(wq), vec(MLA_KV_RANK), full(wkv),
                  tab, tab, tab],
        out_specs=out_specs,
        out_shape=out_shapes,
        compiler_params=_cparams(("parallel", "parallel")),
        name="even_in",
    )(x, g, shift, scale, w0, qg, wq, kvg, wkv, cq, ck, sn)


def _store_key_tiles(kt_ref, hd, k):
    tkb = kt_ref.shape[4]
    for t in range(k.shape[0] // tkb):
        kt_ref[0, hd, t] = k[t * tkb:(t + 1) * tkb, :].T.astype(BF16)


def _odd_in_kernel(x_ref, g_ref, sh_ref, sc_ref, w_ref,
                   qdt_ref, kd_ref, vdt_ref, qs_ref, kst_ref, vs_ref):
    h = _prenorm_mod(x_ref[0], g_ref[...], sh_ref[0], sc_ref[0]).astype(BF16)
    proj = _dot(h, w_ref[...])
    kd_ref[0] = proj[:, 512:1024].astype(BF16)
    qs_ref[0] = proj[:, 1536:2048].astype(BF16)
    vs_ref[0] = proj[:, 2560:3072].astype(BF16)
    for hd in range(4):
        _store_key_tiles(qdt_ref, hd, proj[:, hd * LANES:(hd + 1) * LANES])
        _store_key_tiles(vdt_ref, hd, proj[:, 1024 + hd * LANES:1024 + (hd + 1) * LANES])
        _store_key_tiles(kst_ref, hd, proj[:, 2048 + hd * LANES:2048 + (hd + 1) * LANES])


def _odd_in(x, g, shift, scale, w):
    b, s, d = x.shape
    tm = TOK_TILE
    ns = s // tm
    tok = lambda wd: pl.BlockSpec((1, tm, wd), lambda bi, i: (bi, i, 0))
    mod = pl.BlockSpec((1, 1, d), lambda bi, i: (bi, 0, 0))
    ktspec = lambda tkb: pl.BlockSpec((1, 4, tm // tkb, LANES, tkb), lambda bi, i: (bi, 0, i, 0, 0))
    act = jax.ShapeDtypeStruct((b, s, 512), BF16)
    kts = lambda tkb: jax.ShapeDtypeStruct((b, 4, s // tkb, LANES, tkb), BF16)
    return pl.pallas_call(
        _odd_in_kernel,
        grid=(b, ns),
        in_specs=[tok(d), pl.BlockSpec((1, d), lambda bi, i: (0, 0)), mod, mod,
                  _resident(w.shape, lambda bi, i: (0, 0))],
        out_specs=(ktspec(DIFF_TILE), tok(512), ktspec(DIFF_TILE), tok(512), ktspec(SB_TILE), tok(512)),
        out_shape=(kts(DIFF_TILE), act, kts(DIFF_TILE), act, kts(SB_TILE), act),
        compiler_params=_cparams(("parallel", "parallel")),
        name="odd_in",
    )(x, g, shift, scale, w)


def _mla_kernel(qt_ref, k_ref, vt_ref, o_ref):
    tq = qt_ref.shape[4]
    tk = vt_ref.shape[4]
    hv = MLA_V
    qi = pl.program_id(2)
    causal = (lax.broadcasted_iota(jnp.int32, (tk, tq), 0)
              <= lax.broadcasted_iota(jnp.int32, (tk, tq), 1))
    qts = (qt_ref[0, 0, 0], qt_ref[0, 1, 0])

    def step(js, carry, masked):
        logits, values = [], []
        for hd in range(2):
            s_list = [_dot(k_ref[0, pl.ds(pl.multiple_of(j * tk, tk), tk), hd * LANES:(hd + 1) * LANES], qts[hd])
                      for j in js]
            if masked:
                s_list = [jnp.where(causal, s, NEG) for s in s_list]
            logits.append(s_list)
            values.append([vt_ref[0, 0, j, hd * hv:(hd + 1) * hv, :] for j in js])
        return _softmax_step_t(logits, values, carry)

    one = (jnp.full((1, tq), NEG, F32), jnp.zeros((1, tq), F32), jnp.zeros((hv, tq), F32))
    carry = _loop_pairs(0, qi, functools.partial(step, masked=False), (one, one), group=4)
    (_, l0, a0), (_, l1, a1) = step((qi,), carry, True)
    o_ref[0] = jnp.concatenate([a0 / l0, a1 / l1], axis=0).T.astype(BF16)


def _mla(qt, k, vt):
    b, s, _ = k.shape
    tq = qt.shape[4]
    nk = vt.shape[2]
    tk = vt.shape[4]
    return pl.pallas_call(
        _mla_kernel,
        grid=(b, MLA_HEADS // 2, s // tq),
        in_specs=[
            pl.BlockSpec((1, 2, 1, LANES, tq), lambda bi, hp, qi: (bi, hp, qi, 0, 0)),
            pl.BlockSpec((1, s, 2 * LANES), lambda bi, hp, qi: (bi, 0, hp)),
            pl.BlockSpec((1, 1, nk, LANES, tk), lambda bi, hp, qi: (bi, hp, 0, 0, 0)),
        ],
        out_specs=pl.BlockSpec((1, tq, LANES), lambda bi, hp, qi: (bi, qi, hp)),
        out_shape=jax.ShapeDtypeStruct((b, s, 512), BF16),
        compiler_params=_cparams(("parallel", "parallel", "arbitrary")),
        name="mla",
    )(qt, k, vt)


def _dil_kernel(q_ref, kc_ref, kp_ref, vc_ref, vp_ref, bias_ref, o_ref,
                q32, k32, v32, acc_s, m_s, d_s):
    sup = DIL_SUPER
    blk = DIL_BLOCK
    n = pl.program_id(2)
    q32[...] = q_ref[0].astype(F32)
    k32[0:sup, :] = kp_ref[0].astype(F32)
    k32[sup:2 * sup, :] = kc_ref[0].astype(F32)
    v32[0:sup, :] = vp_ref[0].astype(F32)
    v32[sup:2 * sup, :] = vc_ref[0].astype(F32)
    low = lax.broadcasted_iota(jnp.int32, (blk, LANES), 1) < 64
    before_start = jnp.where(lax.broadcasted_iota(jnp.int32, (blk, 2 * blk), 1) < blk, NEG, 0.0)

    for g, (_, dil) in enumerate(DIL_PATTERNS):

        def unit(u, carry, g=g, dil=dil):
            n_loc = u // dil
            r = u % dil
            qs = n_loc * (blk * dil) + r
            ks = sup + (n_loc - 1) * (blk * dil) + r
            if dil == 1:
                qsl = pl.ds(pl.multiple_of(qs, blk), blk)
                ksl = pl.ds(pl.multiple_of(ks, blk), 2 * blk)
            else:
                qsl = pl.ds(qs, blk, stride=dil)
                ksl = pl.ds(ks, 2 * blk, stride=dil)
            q = q32[qsl, :]
            k = k32[ksl, :].astype(BF16)
            v = v32[ksl, :].astype(BF16)
            extra = jnp.where(jnp.logical_and(n == 0, n_loc == 0), before_start, 0.0)
            parts = []
            for hd in range(2):
                qh = jnp.where(low if hd == 0 else jnp.logical_not(low), q, 0.0).astype(BF16)
                s = lax.dot_general(qh, k, (((1,), (1,)), ((), ())), preferred_element_type=F32)
                s = s + bias_ref[g, hd] + extra
                m = jnp.max(s, axis=-1, keepdims=True)
                e = jnp.exp2(s - m)
                den = jnp.sum(e, axis=-1, keepdims=True)
                parts.append((_dot(e.astype(BF16), v), m, den))
            acc_s[g, qsl, :] = jnp.where(low, parts[0][0], parts[1][0])
            m_s[g, qsl, :] = jnp.where(low, parts[0][1], parts[1][1])
            d_s[g, qsl, :] = jnp.where(low, parts[0][2], parts[1][2])
            return carry

        lax.fori_loop(0, 16, unit, 0, unroll=16)

    mx = jnp.maximum(jnp.maximum(m_s[0], m_s[1]), m_s[2])
    num = jnp.zeros((sup, LANES), F32)
    den = jnp.zeros((sup, LANES), F32)
    for g in range(3):
        a = jnp.exp2(m_s[g] - mx)
        num = num + a * acc_s[g]
        den = den + a * d_s[g]
    o_ref[0] = (num / den).astype(BF16)


def _dil(q, k, v, bias):
    b, s, _ = q.shape
    sup = DIL_SUPER
    cur = pl.BlockSpec((1, sup, LANES), lambda bi, hp, n: (bi, n, hp))
    prev = pl.BlockSpec((1, sup, LANES), lambda bi, hp, n: (bi, jnp.maximum(n - 1, 0), hp))
    return pl.pallas_call(
        _dil_kernel,
        grid=(b, DIL_HEADS // 2, s // sup),
        in_specs=[cur, cur, prev, cur, prev,
                  pl.BlockSpec((3, 2, DIL_BLOCK, 2 * DIL_BLOCK), lambda bi, hp, n: (0, hp, 0, 0))],
        out_specs=cur,
        out_shape=jax.ShapeDtypeStruct((b, s, 512), BF16),
        scratch_shapes=[
            pltpu.VMEM((sup, LANES), F32),
            pltpu.VMEM((2 * sup, LANES), F32),
            pltpu.VMEM((2 * sup, LANES), F32),
            pltpu.VMEM((3, sup, LANES), F32),
            pltpu.VMEM((3, sup, LANES), F32),
            pltpu.VMEM((3, sup, LANES), F32),
        ],
        compiler_params=_cparams(("parallel", "parallel", "arbitrary")),
        name="dilated",
    )(q, k, k, v, v, bias)


def _diff_kernel(qt_ref, k_ref, vt_ref, bias_ref, far_ref, lam_ref, g_ref, o_ref, *, lam_init):
    tq = qt_ref.shape[4]
    tk = vt_ref.shape[4]
    nd = bias_ref.shape[2]
    qi = pl.program_id(2)
    qt = qt_ref[0, 0, 0]
    row = lax.broadcasted_iota(jnp.int32, (LANES, tq), 0)
    zero = jnp.zeros_like(qt)
    qm = (jnp.where(row < DIFF_HD, qt, zero), jnp.where(row >= DIFF_HD, qt, zero))

    def step(js, carry, bias_of):
        keys = [k_ref[0, pl.ds(pl.multiple_of(j * tk, tk), tk), :] for j in js]
        vts = [vt_ref[0, 0, j] for j in js]
        logits = [[_dot(k, qm[mi]) + bias_of(mi, j) for j, k in zip(js, keys)] for mi in range(2)]
        return _softmax_step_t(logits, [vts, vts], carry)

    one = (jnp.full((1, tq), NEG, F32), jnp.zeros((1, tq), F32), jnp.zeros((LANES, tq), F32))
    carry = (one, one)
    n_far = jnp.maximum(qi - nd + 1, 0)
    carry = _loop_pairs(0, n_far, functools.partial(step, bias_of=lambda mi, j: far_ref[0, mi, 0:1, 0:1]), carry,
                        group=4)
    carry = _loop_pairs(n_far, qi, functools.partial(step, bias_of=lambda mi, j: bias_ref[0, mi, qi - j]), carry,
                        group=4)
    carry = step((qi,), carry, lambda mi, j: bias_ref[0, mi, 0])
    (_, l0, a0), (_, l1, a1) = carry
    o = (a0 / l0 - lam_ref[0:1, 0:1] * (a1 / l1)).T
    o_ref[0] = (_rms(o, g_ref[...]) * (1.0 - lam_init)).astype(BF16)


def _diff(qt, k, vt, bias, far, lam, sub_g, lam_init):
    b, s, _ = k.shape
    tq = qt.shape[4]
    nk, tk = vt.shape[2], vt.shape[4]
    nd = bias.shape[2]
    return pl.pallas_call(
        functools.partial(_diff_kernel, lam_init=lam_init),
        grid=(DIFF_HEADS, b, s // tq),
        in_specs=[
            pl.BlockSpec((1, 1, 1, LANES, tq), lambda h, bi, qi: (bi, h, qi, 0, 0)),
            pl.BlockSpec((1, s, LANES), lambda h, bi, qi: (bi, 0, h)),
            pl.BlockSpec((1, 1, nk, LANES, tk), lambda h, bi, qi: (bi, h, 0, 0, 0)),
            _resident((1, 2, nd, tk, tq), lambda h, bi, qi: (h, 0, 0, 0, 0)),
            pl.BlockSpec((1, 2, 8, LANES), lambda h, bi, qi: (h, 0, 0, 0)),
            pl.BlockSpec((1, LANES), lambda h, bi, qi: (0, 0)),
            pl.BlockSpec((1, LANES), lambda h, bi, qi: (0, 0)),
        ],
        out_specs=pl.BlockSpec((1, tq, LANES), lambda h, bi, qi: (bi, qi, h)),
        out_shape=jax.ShapeDtypeStruct((b, s, 512), BF16),
        compiler_params=_cparams(("parallel", "parallel", "arbitrary")),
        name="diff",
    )(qt, k, vt, bias, far, lam, sub_g)


def _sb_kernel(q_ref, kt_ref, v_ref, o_ref):
    tq = q_ref.shape[1]
    tk = kt_ref.shape[4]
    qi = pl.program_id(2)
    lane = lax.broadcasted_iota(jnp.int32, (tq, LANES), 1)
    strict = (lax.broadcasted_iota(jnp.int32, (tq, tk), 1)
              < lax.broadcasted_iota(jnp.int32, (tq, tk), 0))
    later = (lax.broadcasted_iota(jnp.int32, (tk, tk), 0)
             > lax.broadcasted_iota(jnp.int32, (tk, tk), 1)).astype(BF16)
    q = q_ref[0]
    zero = jnp.zeros_like(q)
    qh = (jnp.where(lane < 64, q, zero), jnp.where(lane >= 64, q, zero))

    def blocks(js, state, masked):
        items = [(bi, hd) for bi in range(len(js)) for hd in range(2)]
        z = {it: _dot(qh[it[1]], kt_ref[0, 0, js[it[0]]]) for it in items}
        log_sig, log_1m, inblock = {}, {}, {}
        for it in items:
            neg = -z[it]
            t = jnp.log2(1.0 + jnp.exp2(jnp.minimum(z[it], neg)))
            log_sig[it] = jnp.minimum(z[it], 0.0) - t
            l1m = jnp.minimum(neg, 0.0) - t
            if masked:
                l1m = jnp.where(strict, l1m, 0.0)
            log_1m[it] = l1m
            inblock[it] = _dot(l1m.astype(BF16), later)
        state = list(state)
        for bi, j in enumerate(js):
            v = v_ref[0, pl.ds(pl.multiple_of(j * tk, tk), tk), :]
            for hd in range(2):
                it = (bi, hd)
                c, acc = state[hd]
                w = jnp.exp2(log_sig[it] + (inblock[it] + c))
                if masked:
                    w = jnp.where(strict, w, 0.0)
                acc = acc + _dot(w.astype(BF16), v)
                c = c + jnp.sum(log_1m[it], axis=-1, keepdims=True)
                state[hd] = (c, acc)
        return tuple(state)

    one = (jnp.zeros((tq, 1), F32), jnp.zeros((tq, LANES), F32))
    state = blocks((qi,), (one, one), True)

    odd = qi % 2
    state = lax.cond(odd == 1, lambda st: blocks((qi - 1,), st, False), lambda st: st, state)
    floor = SB_LOG_FLOOR * LOG2E

    def cond(st):
        j, ((c0, _), (c1, _)) = st
        return jnp.logical_and(j >= 1, jnp.max(jnp.maximum(c0, c1)) > floor)

    def body(st):
        j, state = st
        return j - 2, blocks((j, j - 1), state, False)

    _, ((_, a0), (_, a1)) = lax.while_loop(cond, body, (qi - 1 - odd, state))
    o_ref[0] = jnp.where(lane < 64, a0, a1).astype(BF16)


def _sb(q, kt, v):
    b, s, _ = q.shape
    tq = SB_TILE
    nk, tk = kt.shape[2], kt.shape[4]
    return pl.pallas_call(
        _sb_kernel,
        grid=(b, SB_HEADS // 2, s // tq),
        in_specs=[
            pl.BlockSpec((1, tq, LANES), lambda bi, hp, qi: (bi, qi, hp)),
            pl.BlockSpec((1, 1, nk, LANES, tk), lambda bi, hp, qi: (bi, hp, 0, 0, 0)),
            pl.BlockSpec((1, s, LANES), lambda bi, hp, qi: (bi, 0, hp)),
        ],
        out_specs=pl.BlockSpec((1, tq, LANES), lambda bi, hp, qi: (bi, qi, hp)),
        out_shape=jax.ShapeDtypeStruct((b, s, 512), BF16),
        compiler_params=_cparams(("parallel", "parallel", "arbitrary")),
        name="stick_breaking",
    )(q, kt, v)


FF_CHUNKS = ((0, 768), (768, 1536), (1536, 2304), (2304, 2816))
EXPERT_CHUNKS = ((0, 1024), (1024, 2048), (2048, 3072), (3072, 3584))


def _swiglu(hb, wg_ref, wu_ref, wd_ref, chunks, lead, between=None):
    acc = None
    hb_of = hb if callable(hb) else (lambda ci: hb)
    for ci, (c0, c1) in enumerate(chunks):
        hb = hb_of(ci)
        g = _dot(hb, wg_ref[lead + (slice(None), slice(c0, c1))])
        u = _dot(hb, wu_ref[lead + (slice(None), slice(c0, c1))])
        a = (g / (1.0 + jnp.exp(-g)) * u).astype(BF16)
        part = _dot(a, wd_ref[lead + (slice(c0, c1), slice(None))])
        acc = part if acc is None else acc + part
        if between is not None:
            between(ci)
    return acc


def _mix_out(x_ref, oa_ref, ob_ref, wa_ref, wb_ref, gate_ref, pg_ref):
    y = _dot(oa_ref[0], wa_ref[...]) + _dot(ob_ref[0], wb_ref[...])
    return x_ref[0] + gate_ref[0] * _rms(y, pg_ref[...])


def _post_even_kernel(x_ref, oa_ref, ob_ref, wa_ref, wb_ref, gate_ref, pg_ref,
                      fg_ref, fsh_ref, fsc_ref, fgate_ref, fpg_ref, wg_ref, wu_ref, wd_ref, o_ref):
    x1 = _mix_out(x_ref, oa_ref, ob_ref, wa_ref, wb_ref, gate_ref, pg_ref)
    hb = _prenorm_mod(x1, fg_ref[...], fsh_ref[0], fsc_ref[0]).astype(BF16)
    y = _swiglu(hb, wg_ref, wu_ref, wd_ref, FF_CHUNKS, ())
    o_ref[0] = x1 + fgate_ref[0] * _rms(y, fpg_ref[...])


def _post_even(x, oa, ob, wa, wb, gate, pg, fg, fsh, fsc, fgate, fpg, wg, wu, wd):
    b, s, d = x.shape
    tm = TOK_TILE
    tok = lambda w: pl.BlockSpec((1, tm, w), lambda bi, i: (bi, i, 0))
    vec = pl.BlockSpec((1, d), lambda bi, i: (0, 0))
    mod = pl.BlockSpec((1, 1, d), lambda bi, i: (bi, 0, 0))
    full = lambda a: _resident(a.shape, lambda bi, i: (0,) * a.ndim)
    return pl.pallas_call(
        _post_even_kernel,
        grid=(b, s // tm),
        in_specs=[tok(d), tok(512), tok(512), full(wa), full(wb), mod, vec,
                  vec, mod, mod, mod, vec, full(wg), full(wu), full(wd)],
        out_specs=tok(d),
        out_shape=jax.ShapeDtypeStruct((b, s, d), F32),
        compiler_params=_cparams(("parallel", "parallel")),
        name="post_even",
    )(x, oa, ob, wa, wb, gate, pg, fg, fsh, fsc, fgate, fpg, wg, wu, wd)


def _post_odd_kernel(x_ref, oa_ref, ob_ref, wa_ref, wb_ref, gate_ref, pg_ref,
                     fg_ref, fsh_ref, fsc_ref, rw_ref, rb_ref, x_out, h_out, r_out, cnt_ref):
    x1 = _mix_out(x_ref, oa_ref, ob_ref, wa_ref, wb_ref, gate_ref, pg_ref)
    x_out[0] = x1
    h = _prenorm_mod(x1, fg_ref[...], fsh_ref[0], fsc_ref[0])
    h_out[0] = h
    logits = _dot(h, rw_ref[...]) + rb_ref[...]
    lane = lax.broadcasted_iota(jnp.int32, logits.shape, 1)
    m1 = jnp.max(logits, axis=-1, keepdims=True)
    i1 = jnp.min(jnp.where(logits == m1, lane, LANES), axis=-1, keepdims=True)
    rest = jnp.where(lane == i1, NEG, logits)
    m2 = jnp.max(rest, axis=-1, keepdims=True)
    i2 = jnp.min(jnp.where(rest == m2, lane, LANES), axis=-1, keepdims=True)
    e2 = jnp.exp(m2 - m1)
    w1 = 1.0 / (1.0 + e2)
    w2 = e2 / (1.0 + e2)
    @pl.when(jnp.logical_and(pl.program_id(0) == 0, pl.program_id(1) == 0))
    def _():
        cnt_ref[...] = jnp.zeros_like(cnt_ref)

    tm = logits.shape[0]
    sel = jnp.logical_or(lane == i1, lane == i2)
    earlier = (lax.broadcasted_iota(jnp.int32, (tm, tm), 1)
               < lax.broadcasted_iota(jnp.int32, (tm, tm), 0)).astype(BF16)
    prefix = _dot(earlier, sel.astype(BF16)) + cnt_ref[0:1, :]
    rank1 = jnp.sum(jnp.where(lane == i1, prefix, 0.0), axis=-1, keepdims=True)
    rank2 = jnp.sum(jnp.where(lane == i2, prefix, 0.0), axis=-1, keepdims=True)
    cnt_ref[...] = cnt_ref[...] + jnp.sum(sel.astype(F32), axis=0, keepdims=True)
    r = jnp.where(lane == 0, i1.astype(F32), 0.0)
    r = jnp.where(lane == 1, i2.astype(F32), r)
    r = jnp.where(lane == 2, w1, r)
    r = jnp.where(lane == 3, w2, r)
    r = jnp.where(lane == 4, rank1, r)
    r = jnp.where(lane == 5, rank2, r)
    r_out[0] = r


def _post_odd(x, oa, ob, wa, wb, gate, pg, fg, fsh, fsc, rw, rb):
    b, s, d = x.shape
    tm = TOK_TILE
    tok = lambda w: pl.BlockSpec((1, tm, w), lambda bi, i: (bi, i, 0))
    vec = pl.BlockSpec((1, d), lambda bi, i: (0, 0))
    mod = pl.BlockSpec((1, 1, d), lambda bi, i: (bi, 0, 0))
    full = lambda a: _resident(a.shape, lambda bi, i: (0,) * a.ndim)
    return pl.pallas_call(
        _post_odd_kernel,
        grid=(b, s // tm),
        in_specs=[tok(d), tok(512), tok(512), full(wa), full(wb), mod, vec,
                  vec, mod, mod, full(rw), pl.BlockSpec((1, LANES), lambda bi, i: (0, 0))],
        out_specs=(tok(d), tok(d), tok(LANES),
                   pl.BlockSpec((8, LANES), lambda bi, i: (0, 0))),
        out_shape=(jax.ShapeDtypeStruct((b, s, d), F32), jax.ShapeDtypeStruct((b, s, d), F32),
                   jax.ShapeDtypeStruct((b, s, LANES), F32), jax.ShapeDtypeStruct((8, LANES), F32)),
        compiler_params=_cparams(("arbitrary", "arbitrary")),
        name="post_odd",
    )(x, oa, ob, wa, wb, gate, pg, fg, fsh, fsc, rw, rb)


def _store_rows(ref, lead, val):
    for c in range(ROW_SUB):
        ref[lead + (slice(None), c, slice(None))] = val[:, c * LANES:(c + 1) * LANES]


def _load_rows(ref, lead, start, size):
    return jnp.concatenate([ref[lead, pl.ds(start, size), c, :] for c in range(ROW_SUB)], axis=1)


class _TileGather:
    def __init__(self, idx_hbm, src_hbm, idx_smem, buf, isem, sem, tile_rows):
        self.idx_hbm, self.src_hbm, self.idx_smem, self.buf = idx_hbm, src_hbm, idx_smem, buf
        self.isem, self.sem, self.tile_rows = isem, sem, tile_rows
        self.i = pl.program_id(0)
        self.nt = pl.num_programs(0)
        self.n = buf.shape[1]
        self.slot = self.i % 2
        self.nxt = 1 - self.slot

    def _idx_copy(self, t, sl):
        t = jnp.minimum(t, self.nt - 1)
        return pltpu.make_async_copy(self.idx_hbm.at[t], self.idx_smem.at[sl], self.isem.at[sl])

    def _rows_wait(self, sl):
        pltpu.make_async_copy(self.src_hbm.at[pl.ds(0, self.n)], self.buf.at[sl], self.sem.at[sl]).wait()

    def _issue_row(self, sl, r):
        t = self.idx_smem[sl, r]
        if self.tile_rows:
            src, dst = self.src_hbm.at[t], self.buf.at[sl, r]
        else:
            src, dst = self.src_hbm.at[pl.ds(t, 1)], self.buf.at[sl, pl.ds(r, 1)]
        pltpu.make_async_copy(src, dst, self.sem.at[sl]).start()

    def _issue_loop(self, sl):
        def body(r, carry):
            self._issue_row(sl, r)
            return carry

        lax.fori_loop(0, self.n, body, 0, unroll=8)

    def begin(self):
        @pl.when(self.i == 0)
        def _():
            first = self._idx_copy(0, 0)
            first.start()
            first.wait()
            self._issue_loop(0)
            self._idx_copy(1, 1).start()

        self._idx_copy(self.i + 1, self.nxt).wait()
        self._idx_copy(self.i + 2, self.slot).start()
        self._rows_wait(self.slot)

    def issue(self, part, parts):
        per = -(-self.n // parts)
        for r in range(part * per, min((part + 1) * per, self.n)):
            self._issue_row(self.nxt, r)

    def issue_all(self):
        self._issue_loop(self.nxt)

    def anchor(self, zero):
        return zero

    def finish(self):
        @pl.when(self.i == self.nt - 1)
        def _():
            self._rows_wait(self.nxt)
            self._idx_copy(self.i + 2, self.slot).wait()


def _moe_kernel(te_ref, nu_ref, zero_ref, tok_hbm, h_hbm, wg_ref, wu_ref, wd_ref, y_ref, idx_smem, buf, isem, sem):
    g = _TileGather(tok_hbm, h_hbm, idx_smem, buf, isem, sem, tile_rows=False)
    n = g.n
    g.begin()

    @pl.when(g.i < nu_ref[0])
    def _():
        anchor = [0]
        head = 16
        rest = buf[g.slot, head:n, :].astype(BF16)

        def rows_of(ci):
            first = buf[g.slot, pl.ds(pl.multiple_of(anchor[0], head), head), :].astype(BF16)
            return jnp.concatenate([first, rest], axis=0)

        def issue_part(ci):
            if ci < len(EXPERT_CHUNKS) - 1:
                g.issue(ci, len(EXPERT_CHUNKS) - 1)
                anchor[0] = g.anchor(zero_ref[0])

        y = _swiglu(rows_of, wg_ref, wu_ref, wd_ref, EXPERT_CHUNKS, (0,), between=issue_part)
        _store_rows(y_ref, (), y)

    @pl.when(g.i >= nu_ref[0])
    def _():
        g.issue_all()
        y_ref[...] = jnp.zeros_like(y_ref)

    g.finish()


def _moe(tile_expert, n_used, row_token, h, wg, wu, wd):
    d = h.shape[1]
    nt, tm = row_token.shape
    dff = wg.shape[2]
    wspec = lambda shp: pl.BlockSpec(shp, lambda i, te, nu, z: (te[i], 0, 0), pipeline_mode=pl.Buffered(1))
    grid_spec = pltpu.PrefetchScalarGridSpec(
        num_scalar_prefetch=3,
        grid=(nt,),
        in_specs=[
            pl.BlockSpec(memory_space=pl.ANY),
            pl.BlockSpec(memory_space=pl.ANY),
            wspec((1, d, dff)), wspec((1, d, dff)), wspec((1, dff, d)),
        ],
        out_specs=pl.BlockSpec((tm, ROW_SUB, LANES), lambda i, te, nu, z: (i, 0, 0)),
        scratch_shapes=[
            pltpu.SMEM((2, tm), jnp.int32),
            pltpu.VMEM((2, tm, d), F32),
            pltpu.SemaphoreType.DMA((2,)),
            pltpu.SemaphoreType.DMA((2,)),
        ],
    )
    return pl.pallas_call(
        _moe_kernel,
        grid_spec=grid_spec,
        out_shape=jax.ShapeDtypeStruct((nt * tm, ROW_SUB, LANES), F32),
        compiler_params=_cparams(("arbitrary",)),
        name="moe_experts",
    )(tile_expert, n_used, jnp.zeros((1,), jnp.int32), row_token, h, wg, wu, wd)


def _combine_kernel(zero_ref, pos_hbm, y_hbm, x_ref, r_ref, gate_ref, pg_ref, o_ref, idx_smem, buf, isem, sem):
    tm = x_ref.shape[0]
    g = _TileGather(pos_hbm, y_hbm, idx_smem, buf, isem, sem, tile_rows=True)
    g.begin()
    parts = 4
    rows = tm // parts
    off = 0
    for c in range(parts):
        lo = c * rows
        first = _load_rows(buf, g.slot, pl.multiple_of(off + lo, 8), rows)
        second = _load_rows(buf, g.slot, pl.multiple_of(off + tm + lo, 8), rows)
        r = r_ref[lo:lo + rows, :]
        y = r[:, 2:3] * first + r[:, 3:4] * second
        o_ref[lo:lo + rows, :] = x_ref[lo:lo + rows, :] + gate_ref[0] * _rms(y, pg_ref[...])
        if c < parts - 1:
            g.issue(c, parts - 1)
            off = g.anchor(zero_ref[0])
    g.finish()


def _combine(pos, y, x, r, gate, pg, tokens_per_seq):
    n_tok, d = x.shape
    nt, tm2 = pos.shape
    tm = tm2 // 2
    per_seq = tokens_per_seq // tm
    tok = lambda w: pl.BlockSpec((tm, w), lambda i, z: (i, 0))
    grid_spec = pltpu.PrefetchScalarGridSpec(
        num_scalar_prefetch=1,
        grid=(nt,),
        in_specs=[
            pl.BlockSpec(memory_space=pl.ANY),
            pl.BlockSpec(memory_space=pl.ANY),
            tok(d), tok(LANES),
            pl.BlockSpec((1, 1, d), lambda i, z: (i // per_seq, 0, 0)),
            pl.BlockSpec((1, d), lambda i, z: (0, 0)),
        ],
        out_specs=tok(d),
        scratch_shapes=[
            pltpu.SMEM((2, tm2), jnp.int32),
            pltpu.VMEM((2, tm2, ROW_SUB, LANES), F32),
            pltpu.SemaphoreType.DMA((2,)),
            pltpu.SemaphoreType.DMA((2,)),
        ],
    )
    return pl.pallas_call(
        _combine_kernel,
        grid_spec=grid_spec,
        out_shape=jax.ShapeDtypeStruct((n_tok, d), F32),
        compiler_params=_cparams(("arbitrary",)),
        name="moe_combine",
    )(jnp.zeros((1,), jnp.int32), pos, y, x, r, gate, pg)


def _t5_bucket(dist):
    max_exact = REL_BUCKETS // 2
    d = jnp.maximum(dist, 1).astype(F32)
    log_b = max_exact + (jnp.log(d / max_exact) / math.log(REL_MAX_DIST / max_exact)
                         * (REL_BUCKETS - max_exact)).astype(jnp.int32)
    log_b = jnp.minimum(log_b, REL_BUCKETS - 1)
    return jnp.where(dist < max_exact, dist, log_b)


def _rope_tables(s):
    half = MLA_ROPE // 2
    freqs = ROPE_THETA ** (-jnp.arange(half, dtype=F32) / half)
    ang = jnp.arange(s, dtype=F32)[:, None] * freqs[None, :]
    cos, sin = jnp.cos(ang), jnp.sin(ang)
    z64 = jnp.zeros((s, MLA_NOPE), F32)
    z32 = jnp.zeros((s, LANES - MLA_NOPE - MLA_ROPE), F32)
    ck = jnp.concatenate([z64, cos, cos, z32], axis=1)
    cq = jnp.concatenate([jnp.ones((s, MLA_NOPE), F32), cos, cos, z32], axis=1)
    sn = jnp.concatenate([z64, sin, sin, z32], axis=1)
    return cq, ck, sn


def _even_weights(w_in, w_uq, w_ukv):
    d = w_in.shape[0]
    half = MLA_ROPE // 2
    w_cq = w_in[:, :MLA_Q_RANK]
    w_ckv = w_in[:, MLA_Q_RANK:MLA_Q_RANK + MLA_KV_RANK]
    w_kr = w_in[:, MLA_Q_RANK + MLA_KV_RANK:MLA_Q_RANK + MLA_KV_RANK + MLA_ROPE]
    w_qkv = w_in[:, MLA_Q_RANK + MLA_KV_RANK + MLA_ROPE:]
    z = lambda n: jnp.zeros((d, n), F32)
    kr_a = jnp.concatenate([z(MLA_NOPE), w_kr, z(32)], axis=1)
    kr_b = jnp.concatenate([z(MLA_NOPE), -w_kr[:, half:], w_kr[:, :half], z(32)], axis=1)
    dil_scale = DIL_HD ** -0.5 * LOG2E
    w0 = jnp.concatenate([w_cq, w_ckv, kr_a, kr_b, w_qkv[:, :512] * dil_scale, w_qkv[:, 512:]], axis=1)

    r = w_uq.shape[0]
    wq = w_uq.reshape(r, MLA_HEADS, MLA_NOPE + MLA_ROPE) * ((MLA_NOPE + MLA_ROPE) ** -0.5 * LOG2E)
    zq = lambda n: jnp.zeros((r, MLA_HEADS, n), F32)
    nope, x1, x2 = wq[..., :MLA_NOPE], wq[..., MLA_NOPE:MLA_NOPE + half], wq[..., MLA_NOPE + half:]
    q_a = jnp.concatenate([nope, x1, x2, zq(32)], axis=-1).reshape(r, MLA_HEADS * LANES)
    q_b = jnp.concatenate([zq(MLA_NOPE), -x2, x1, zq(32)], axis=-1).reshape(r, MLA_HEADS * LANES)
    wq2 = jnp.concatenate([q_a, q_b], axis=1)

    rk = w_ukv.shape[0]
    wkv = w_ukv.reshape(rk, MLA_HEADS, MLA_NOPE + MLA_V)
    k_blk = jnp.concatenate([wkv[..., :MLA_NOPE], jnp.zeros((rk, MLA_HEADS, LANES - MLA_NOPE), F32)], axis=-1)
    wkv2 = jnp.concatenate([k_blk.reshape(rk, MLA_HEADS * LANES),
                            wkv[..., MLA_NOPE:].reshape(rk, MLA_HEADS * MLA_V)], axis=1)
    return w0.astype(BF16), wq2.astype(BF16), wkv2.astype(BF16)


def _toeplitz(vec, rows, cols):
    n, width = vec.shape

    def toeplitz_kernel(v_ref, o_ref):
        tiled = jnp.broadcast_to(v_ref[0], (rows, width))
        o_ref[0] = pltpu.roll(tiled, 0, 1, stride=1, stride_axis=0)[:, :cols]

    return pl.pallas_call(
        toeplitz_kernel,
        grid=(n,),
        in_specs=[pl.BlockSpec((1, 1, width), lambda t: (t, 0, 0))],
        out_specs=pl.BlockSpec((1, rows, cols), lambda t: (t, 0, 0)),
        out_shape=jax.ShapeDtypeStruct((n, rows, cols), F32),
        compiler_params=_cparams(("parallel",)),
        name="toeplitz_bias",
    )(vec.reshape(n, 1, width).astype(F32))


def _dil_bias(rel_bias):
    blk = DIL_BLOCK
    width = 4 * blk
    k = jnp.arange(width)
    rel = jnp.where(k < 2 * blk, blk - k, blk + width - k)
    out = []
    for window, dil in DIL_PATTERNS:
        band = (rel >= 0) & (rel <= window // dil)
        bias = rel_bias[_t5_bucket(jnp.maximum(rel, 0) * dil)] * LOG2E
        out.append(jnp.where(band[:, None], bias, NEG).T)
    vec = jnp.stack(out).reshape(3 * DIL_HEADS, width)
    return _toeplitz(vec, blk, 2 * blk).reshape(3, DIL_HEADS, blk, 2 * blk)


def _diff_bias(rel_bias, tile):
    nd = REL_MAX_DIST // tile + 1
    maps = rel_bias.shape[1]
    k = jnp.arange(2 * tile)[None, :]
    dist = jnp.arange(nd)[:, None] * tile + jnp.where(k < tile, k, k - 2 * tile)
    vec = jnp.where((dist >= 0)[..., None], rel_bias[_t5_bucket(jnp.maximum(dist, 0))] * LOG2E, NEG)
    vec = jnp.transpose(vec, (2, 0, 1)).reshape(maps * nd, 2 * tile)
    bias = _toeplitz(vec, tile, tile).reshape(DIFF_HEADS, 2, nd, tile, tile)
    far = rel_bias[_t5_bucket(jnp.array(REL_MAX_DIST))] * LOG2E
    far = jnp.broadcast_to(far.reshape(DIFF_HEADS, 2, 1, 1), (DIFF_HEADS, 2, 8, LANES))
    return bias, far.astype(F32)


def _routing(r, counts, n_tok, tile):
    n_tiles = (2 * n_tok) // tile + N_EXPERTS
    e = jnp.concatenate([r[:, 0], r[:, 1]]).astype(jnp.int32)
    rank = jnp.concatenate([r[:, 4], r[:, 5]]).astype(jnp.int32)
    counts = counts.astype(jnp.int32)
    padded = ((counts + tile - 1) // tile) * tile
    ends = jnp.cumsum(padded)
    starts = ends - padded
    onehot = (e[:, None] == jnp.arange(N_EXPERTS)[None, :]).astype(jnp.int32)
    pos = jnp.sum(onehot * starts[None, :], axis=1) + rank
    token = jnp.tile(jnp.arange(n_tok, dtype=jnp.int32), 2)
    row_token = jnp.zeros((n_tiles * tile,), jnp.int32).at[pos].set(
        token, unique_indices=True, mode="promise_in_bounds")
    tile_start = jnp.arange(n_tiles, dtype=jnp.int32) * tile
    tile_expert = jnp.sum((tile_start[:, None] >= ends[None, :]).astype(jnp.int32), axis=1)
    n_used = (ends[-1] // tile).astype(jnp.int32)
    last = jnp.sum((ends[-1] - 1 >= ends).astype(jnp.int32))
    tile_expert = jnp.minimum(tile_expert, last).astype(jnp.int32)
    return (tile_expert, n_used.reshape(1), row_token.reshape(n_tiles, tile),
            pos[:n_tok].astype(jnp.int32), pos[n_tok:].astype(jnp.int32))


def kernel(x, c, rel_bias, ada_mix_w, ada_mix_b, mix_pre_g, mix_post_g, ada_ffn_w, ada_ffn_b, ffn_pre_g, ffn_post_g, e_w_in, e_q_norm_g, e_w_uq, e_kv_norm_g, e_w_ukv, e_w_out, ffn_w_gate, ffn_w_up, ffn_w_down, o_w_in, diff_lq1, diff_lk1, diff_lq2, diff_lk2, diff_sub_g, o_w_out, router_w, router_b, moe_w_gate, moe_w_up, moe_w_down):
    b, s, d = x.shape
    assert d == D_MODEL and s % DIL_SUPER == 0 and s % TOK_TILE == 0
    row = lambda v: v.reshape(1, -1).astype(F32)

    mix_mod = _ada(c, ada_mix_w, ada_mix_b)
    ffn_mod = _ada(c, ada_ffn_w, ada_ffn_b)

    shift, scale, gate = _split_mod(mix_mod[0])
    w0, wq2, wkv2 = _even_weights(e_w_in[0], e_w_uq[0], e_w_ukv[0])
    cq, ck, sn = _rope_tables(s)
    qat, ka, vat, qb, kb, vb = _even_in(x, row(mix_pre_g[0]), shift, scale, w0, row(e_q_norm_g[0]), wq2,
                                        row(e_kv_norm_g[0]), wkv2, cq, ck, sn)
    o_a = _mla(qat, ka, vat)
    o_b = _dil(qb, kb, vb, _dil_bias(rel_bias))
    fshift, fscale, fgate = _split_mod(ffn_mod[0])
    w_out = e_w_out[0].astype(BF16)
    x = _post_even(x, o_a, o_b, w_out[:512], w_out[512:], gate, row(mix_post_g[0]),
                   row(ffn_pre_g[0]), fshift, fscale, fgate, row(ffn_post_g[0]),
                   ffn_w_gate[0].astype(BF16), ffn_w_up[0].astype(BF16), ffn_w_down[0].astype(BF16))

    layer = 1
    shift, scale, gate = _split_mod(mix_mod[1])
    w_in = o_w_in[0]
    att_scale = DIFF_HD ** -0.5
    w1 = jnp.concatenate([w_in[:, :512] * (att_scale * LOG2E), w_in[:, 512:1536],
                          w_in[:, 1536:2048] * (SB_HD ** -0.5 * LOG2E), w_in[:, 2048:]], axis=1).astype(BF16)
    qdt, kd, vdt, qs, kst, vs = _odd_in(x, row(mix_pre_g[1]), shift, scale, w1)
    lam_init = 0.8 - 0.6 * math.exp(-0.3 * layer)
    lam = (jnp.exp(jnp.sum(diff_lq1[0].astype(F32) * diff_lk1[0].astype(F32)))
           - jnp.exp(jnp.sum(diff_lq2[0].astype(F32) * diff_lk2[0].astype(F32))) + lam_init)
    bias, far = _diff_bias(rel_bias, DIFF_TILE)
    o_c = _diff(qdt, kd, vdt, bias, far, jnp.full((1, LANES), lam, F32), row(diff_sub_g[0]), lam_init)
    o_d = _sb(qs, kst, vs)
    fshift, fscale, fgate = _split_mod(ffn_mod[1])
    w_out = o_w_out[0].astype(BF16)
    rw = jnp.zeros((d, LANES), F32).at[:, :N_EXPERTS].set(router_w[0].astype(F32))
    rb = jnp.full((1, LANES), NEG, F32).at[0, :N_EXPERTS].set(router_b[0].astype(F32))
    x, h, r, counts = _post_odd(x, o_c, o_d, w_out[:512], w_out[512:], gate, row(mix_post_g[1]),
                                row(ffn_pre_g[1]), fshift, fscale, rw, rb)

    n_tok = b * s
    tile_expert, n_used, row_token, pos0, pos1 = _routing(r.reshape(n_tok, LANES), counts[0, :N_EXPERTS],
                                                          n_tok, MOE_TILE)
    y = _moe(tile_expert, n_used, row_token, h.reshape(n_tok, d),
             moe_w_gate[0].astype(BF16), moe_w_up[0].astype(BF16), moe_w_down[0].astype(BF16))
    ct = TOK_TILE
    pos = jnp.concatenate([pos0.reshape(n_tok // ct, ct), pos1.reshape(n_tok // ct, ct)], axis=1)
    out = _combine(pos, y, x.reshape(n_tok, d), r.reshape(n_tok, LANES), fgate, row(ffn_post_g[1]), s)
    return out.reshape(b, s, d)
```

```python
import functools
import math

import jax
import jax.numpy as jnp
from jax import lax
from jax.experimental import pallas as pl
from jax.experimental.pallas import tpu as pltpu

F32 = jnp.float32
BF16 = jnp.bfloat16

D_MODEL = 1024
EPS = 1e-6

MLA_HEADS = 8
MLA_NOPE = 64
MLA_ROPE = 32
MLA_V = 64
MLA_Q_RANK = 256
MLA_KV_RANK = 128
ROPE_THETA = 10000.0

DIL_HEADS = 8
DIL_HD = 64
DIL_PATTERNS = ((128, 1), (512, 4), (2048, 16))
DIL_BLOCK = 128

DIFF_HEADS = 4
DIFF_HD = 64
SB_HEADS = 8
SB_HD = 64

REL_BUCKETS = 32
REL_MAX_DIST = 2048

D_FF = 2816
N_EXPERTS = 8
D_FF_EXPERT = 3584

LANES = 128
ROW_SUB = D_MODEL // LANES
LOG2E = math.log2(math.e)
NEG = -1e30

TOK_TILE = 512
MLA_TILE = 512
DIFF_TILE = 512
SB_TILE = 256
DIL_SUPER = DIL_BLOCK * 16
MOE_TILE = 512
SB_LOG_FLOOR = -104.0

VMEM_LIMIT = 56 * 1024 * 1024


def _cparams(sem):
    return pltpu.CompilerParams(dimension_semantics=sem, vmem_limit_bytes=VMEM_LIMIT)


def _resident(shape, index_map):
    return pl.BlockSpec(shape, index_map, pipeline_mode=pl.Buffered(1))


def _rms(x, g):
    return x * lax.rsqrt(jnp.mean(x * x, axis=-1, keepdims=True) + EPS) * g


def _dot(a, b):
    return jnp.dot(a, b, preferred_element_type=F32)


def _softmax_step_t(logits, values, carry):
    out = []
    for s_list, v_list, (m, l, acc) in zip(logits, values, carry):
        for s, vt in zip(s_list, v_list):
            m_new = jnp.maximum(m, jnp.max(s, axis=0, keepdims=True))
            alpha = jnp.exp2(m - m_new)
            p = jnp.exp2(s - m_new)
            l = alpha * l + jnp.sum(p, axis=0, keepdims=True)
            acc = alpha * acc + _dot(vt, p.astype(BF16))
            m = m_new
        out.append((m, l, acc))
    return tuple(out)


def _loop_pairs(lo, hi, step, carry, group=2):
    n = hi - lo
    carry = lax.fori_loop(
        0, n // group, lambda i, c: step(tuple(lo + group * i + g for g in range(group)), c), carry)
    size = group // 2
    while size >= 1:
        start = lo + (n // (2 * size)) * (2 * size)
        carry = lax.cond((n // size) % 2 == 1,
                         lambda c, start=start, size=size: step(tuple(start + g for g in range(size)), c),
                         lambda c: c, carry)
        size //= 2
    return carry


def _ada_kernel(c_ref, w_ref, b_ref, o_ref):
    c = c_ref[...]
    sc = c / (1.0 + jnp.exp(-c))
    o_ref[0] = _dot(sc.astype(BF16), w_ref[0].astype(BF16)) + b_ref[0]


def _ada(c, w, b):
    nl, d, d3 = w.shape
    bsz = c.shape[0]
    nb = d3 // d
    return pl.pallas_call(
        _ada_kernel,
        grid=(nl, nb),
        in_specs=[
            pl.BlockSpec((bsz, d), lambda l, j: (0, 0)),
            pl.BlockSpec((1, d, d), lambda l, j: (l, 0, j)),
            pl.BlockSpec((1, 1, d), lambda l, j: (l, 0, j)),
        ],
        out_specs=pl.BlockSpec((1, bsz, d), lambda l, j: (l, 0, j)),
        out_shape=jax.ShapeDtypeStruct((nl, bsz, d3), F32),
        compiler_params=_cparams(("arbitrary", "arbitrary")),
        name="ada",
    )(c, w, b.reshape(nl, 1, d3))


def _split_mod(m):
    b = m.shape[0]
    m = m.reshape(b, 3, 1, D_MODEL)
    return m[:, 0], m[:, 1], m[:, 2]


def _prenorm_mod(x, g, shift, scale):
    return _rms(x, g) * (1.0 + scale) + shift


def _even_in_kernel(x_ref, g_ref, sh_ref, sc_ref, w0_ref, qg_ref, wq_ref, kvg_ref, wkv_ref,
                    cq_ref, ck_ref, sn_ref,
                    qa_ref, ka_ref, va_ref, qb_ref, kb_ref, vb_ref):
    h = _prenorm_mod(x_ref[0], g_ref[...], sh_ref[0], sc_ref[0]).astype(BF16)
    proj = _dot(h, w0_ref[...])
    cqn = _rms(proj[:, 0:256], qg_ref[...]).astype(BF16)
    qq = _dot(cqn, wq_ref[...])
    ckvn = _rms(proj[:, 256:384], kvg_ref[...]).astype(BF16)
    kv = _dot(ckvn, wkv_ref[...])
    cq = cq_ref[...]
    ck = ck_ref[...]
    sn = sn_ref[...]
    krope = proj[:, 384:512] * ck + proj[:, 512:640] * sn
    nh = MLA_HEADS
    for hd in range(nh):
        lo = hd * LANES
        qh = qq[:, lo:lo + LANES] * cq + qq[:, nh * LANES + lo:nh * LANES + lo + LANES] * sn
        _store_key_tiles(qa_ref, hd, qh)
        ka_ref[0, :, lo:lo + LANES] = (kv[:, lo:lo + LANES] + krope).astype(BF16)
    for pr in range(nh // 2):
        _store_key_tiles(va_ref, pr, kv[:, nh * LANES + pr * LANES:nh * LANES + (pr + 1) * LANES])
    qb_ref[0] = proj[:, 640:1152].astype(BF16)
    kb_ref[0] = proj[:, 1152:1664].astype(BF16)
    vb_ref[0] = proj[:, 1664:2176].astype(BF16)


def _even_in(x, g, shift, scale, w0, qg, wq, kvg, wkv, cq, ck, sn):
    b, s, d = x.shape
    tm = TOK_TILE
    tkb = MLA_TILE
    ns = s // tm
    tok = lambda w: pl.BlockSpec((1, tm, w), lambda bi, i: (bi, i, 0))
    vec = lambda w: pl.BlockSpec((1, w), lambda bi, i: (0, 0))
    mod = pl.BlockSpec((1, 1, d), lambda bi, i: (bi, 0, 0))
    tab = pl.BlockSpec((tm, LANES), lambda bi, i: (i, 0))
    full = lambda a: _resident(a.shape, lambda bi, i: (0,) * a.ndim)
    out_shapes = (
        jax.ShapeDtypeStruct((b, MLA_HEADS, s // tkb, LANES, tkb), BF16),
        jax.ShapeDtypeStruct((b, s, MLA_HEADS * LANES), BF16),
        jax.ShapeDtypeStruct((b, MLA_HEADS // 2, s // tkb, LANES, tkb), BF16),
        jax.ShapeDtypeStruct((b, s, 512), BF16),
        jax.ShapeDtypeStruct((b, s, 512), BF16),
        jax.ShapeDtypeStruct((b, s, 512), BF16),
    )
    out_specs = (
        pl.BlockSpec((1, MLA_HEADS, tm // tkb, LANES, tkb), lambda bi, i: (bi, 0, i, 0, 0)),
        tok(MLA_HEADS * LANES),
        pl.BlockSpec((1, MLA_HEADS // 2, tm // tkb, LANES, tkb), lambda bi, i: (bi, 0, i, 0, 0)),
        tok(512), tok(512), tok(512),
    )
    return pl.pallas_call(
        _even_in_kernel,
        grid=(b, ns),
        in_specs=[tok(d), vec(d), mod, mod, full(w0), vec(MLA_Q_RANK), full(wq), vec(MLA_KV_RANK), full(wkv),
                  tab, tab, tab],
        out_specs=out_specs,
        out_shape=out_shapes,
        compiler_params=_cparams(("parallel", "parallel")),
        name="even_in",
    )(x, g, shift, scale, w0, qg, wq, kvg, wkv, cq, ck, sn)


def _store_key_tiles(kt_ref, hd, k):
    tkb = kt_ref.shape[4]
    for t in range(k.shape[0] // tkb):
        kt_ref[0, hd, t] = k[t * tkb:(t + 1) * tkb, :].T.astype(BF16)


def _odd_in_kernel(x_ref, g_ref, sh_ref, sc_ref, w_ref,
                   qdt_ref, kd_ref, vdt_ref, qs_ref, kst_ref, vs_ref):
    h = _prenorm_mod(x_ref[0], g_ref[...], sh_ref[0], sc_ref[0]).astype(BF16)
    proj = _dot(h, w_ref[...])
    kd_ref[0] = proj[:, 512:1024].astype(BF16)
    qs_ref[0] = proj[:, 1536:2048].astype(BF16)
    vs_ref[0] = proj[:, 2560:3072].astype(BF16)
    for hd in range(4):
        _store_key_tiles(qdt_ref, hd, proj[:, hd * LANES:(hd + 1) * LANES])
        _store_key_tiles(vdt_ref, hd, proj[:, 1024 + hd * LANES:1024 + (hd + 1) * LANES])
        _store_key_tiles(kst_ref, hd, proj[:, 2048 + hd * LANES:2048 + (hd + 1) * LANES])


def _odd_in(x, g, shift, scale, w):
    b, s, d = x.shape
    tm = TOK_TILE
    ns = s // tm
    tok = lambda wd: pl.BlockSpec((1, tm, wd), lambda bi, i: (bi, i, 0))
    mod = pl.BlockSpec((1, 1, d), lambda bi, i: (bi, 0, 0))
    ktspec = lambda tkb: pl.BlockSpec((1, 4, tm // tkb, LANES, tkb), lambda bi, i: (bi, 0, i, 0, 0))
    act = jax.ShapeDtypeStruct((b, s, 512), BF16)
    kts = lambda tkb: jax.ShapeDtypeStruct((b, 4, s // tkb, LANES, tkb), BF16)
    return pl.pallas_call(
        _odd_in_kernel,
        grid=(b, ns),
        in_specs=[tok(d), pl.BlockSpec((1, d), lambda bi, i: (0, 0)), mod, mod,
                  _resident(w.shape, lambda bi, i: (0, 0))],
        out_specs=(ktspec(DIFF_TILE), tok(512), ktspec(DIFF_TILE), tok(512), ktspec(SB_TILE), tok(512)),
        out_shape=(kts(DIFF_TILE), act, kts(DIFF_TILE), act, kts(SB_TILE), act),
        compiler_params=_cparams(("parallel", "parallel")),
        name="odd_in",
    )(x, g, shift, scale, w)


def _mla_kernel(qt_ref, k_ref, vt_ref, o_ref):
    tq = qt_ref.shape[4]
    tk = vt_ref.shape[4]
    hv = MLA_V
    qi = pl.program_id(2)
    causal = (lax.broadcasted_iota(jnp.int32, (tk, tq), 0)
              <= lax.broadcasted_iota(jnp.int32, (tk, tq), 1))
    qts = (qt_ref[0, 0, 0], qt_ref[0, 1, 0])

    def step(js, carry, masked):
        logits, values = [], []
        for hd in range(2):
            s_list = [_dot(k_ref[0, pl.ds(pl.multiple_of(j * tk, tk), tk), hd * LANES:(hd + 1) * LANES], qts[hd])
                      for j in js]
            if masked:
                s_list = [jnp.where(causal, s, NEG) for s in s_list]
            logits.append(s_list)
            values.append([vt_ref[0, 0, j, hd * hv:(hd + 1) * hv, :] for j in js])
        return _softmax_step_t(logits, values, carry)

    one = (jnp.full((1, tq), NEG, F32), jnp.zeros((1, tq), F32), jnp.zeros((hv, tq), F32))
    carry = _loop_pairs(0, qi, functools.partial(step, masked=False), (one, one), group=4)
    (_, l0, a0), (_, l1, a1) = step((qi,), carry, True)
    o_ref[0] = jnp.concatenate([a0 / l0, a1 / l1], axis=0).T.astype(BF16)


def _mla(qt, k, vt):
    b, s, _ = k.shape
    tq = qt.shape[4]
    nk = vt.shape[2]
    tk = vt.shape[4]
    return pl.pallas_call(
        _mla_kernel,
        grid=(b, MLA_HEADS // 2, s // tq),
        in_specs=[
            pl.BlockSpec((1, 2, 1, LANES, tq), lambda bi, hp, qi: (bi, hp, qi, 0, 0)),
            pl.BlockSpec((1, s, 2 * LANES), lambda bi, hp, qi: (bi, 0, hp)),
            pl.BlockSpec((1, 1, nk, LANES, tk), lambda bi, hp, qi: (bi, hp, 0, 0, 0)),
        ],
        out_specs=pl.BlockSpec((1, tq, LANES), lambda bi, hp, qi: (bi, qi, hp)),
        out_shape=jax.ShapeDtypeStruct((b, s, 512), BF16),
        compiler_params=_cparams(("parallel", "parallel", "arbitrary")),
        name="mla",
    )(qt, k, vt)


def _dil_kernel(q_ref, kc_ref, kp_ref, vc_ref, vp_ref, bias_ref, o_ref,
                q32, k32, v32, acc_s, m_s, d_s):
    sup = DIL_SUPER
    blk = DIL_BLOCK
    n = pl.program_id(2)
    q32[...] = q_ref[0].astype(F32)
    k32[0:sup, :] = kp_ref[0].astype(F32)
    k32[sup:2 * sup, :] = kc_ref[0].astype(F32)
    v32[0:sup, :] = vp_ref[0].astype(F32)
    v32[sup:2 * sup, :] = vc_ref[0].astype(F32)
    low = lax.broadcasted_iota(jnp.int32, (blk, LANES), 1) < 64
    before_start = jnp.where(lax.broadcasted_iota(jnp.int32, (blk, 2 * blk), 1) < blk, NEG, 0.0)

    for g, (_, dil) in enumerate(DIL_PATTERNS):

        def unit(u, carry, g=g, dil=dil):
            n_loc = u // dil
            r = u % dil
            qs = n_loc * (blk * dil) + r
            ks = sup + (n_loc - 1) * (blk * dil) + r
            if dil == 1:
                qsl = pl.ds(pl.multiple_of(qs, blk), blk)
                ksl = pl.ds(pl.multiple_of(ks, blk), 2 * blk)
            else:
                qsl = pl.ds(qs, blk, stride=dil)
                ksl = pl.ds(ks, 2 * blk, stride=dil)
            q = q32[qsl, :]
            k = k32[ksl, :].astype(BF16)
            v = v32[ksl, :].astype(BF16)
            extra = jnp.where(jnp.logical_and(n == 0, n_loc == 0), before_start, 0.0)
            parts = []
            for hd in range(2):
                qh = jnp.where(low if hd == 0 else jnp.logical_not(low), q, 0.0).astype(BF16)
                s = lax.dot_general(qh, k, (((1,), (1,)), ((), ())), preferred_element_type=F32)
                s = s + bias_ref[g, hd] + extra
                m = jnp.max(s, axis=-1, keepdims=True)
                e = jnp.exp2(s - m)
                den = jnp.sum(e, axis=-1, keepdims=True)
                parts.append((_dot(e.astype(BF16), v), m, den))
            acc_s[g, qsl, :] = jnp.where(low, parts[0][0], parts[1][0])
            m_s[g, qsl, :] = jnp.where(low, parts[0][1], parts[1][1])
            d_s[g, qsl, :] = jnp.where(low, parts[0][2], parts[1][2])
            return carry

        lax.fori_loop(0, 16, unit, 0, unroll=16)

    mx = jnp.maximum(jnp.maximum(m_s[0], m_s[1]), m_s[2])
    num = jnp.zeros((sup, LANES), F32)
    den = jnp.zeros((sup, LANES), F32)
    for g in range(3):
        a = jnp.exp2(m_s[g] - mx)
        num = num + a * acc_s[g]
        den = den + a * d_s[g]
    o_ref[0] = (num / den).astype(BF16)


def _dil(q, k, v, bias):
    b, s, _ = q.shape
    sup = DIL_SUPER
    cur = pl.BlockSpec((1, sup, LANES), lambda bi, hp, n: (bi, n, hp))
    prev = pl.BlockSpec((1, sup, LANES), lambda bi, hp, n: (bi, jnp.maximum(n - 1, 0), hp))
    return pl.pallas_call(
        _dil_kernel,
        grid=(b, DIL_HEADS // 2, s // sup),
        in_specs=[cur, cur, prev, cur, prev,
                  pl.BlockSpec((3, 2, DIL_BLOCK, 2 * DIL_BLOCK), lambda bi, hp, n: (0, hp, 0, 0))],
        out_specs=cur,
        out_shape=jax.ShapeDtypeStruct((b, s, 512), BF16),
        scratch_shapes=[
            pltpu.VMEM((sup, LANES), F32),
            pltpu.VMEM((2 * sup, LANES), F32),
            pltpu.VMEM((2 * sup, LANES), F32),
            pltpu.VMEM((3, sup, LANES), F32),
            pltpu.VMEM((3, sup, LANES), F32),
            pltpu.VMEM((3, sup, LANES), F32),
        ],
        compiler_params=_cparams(("parallel", "parallel", "arbitrary")),
        name="dilated",
    )(q, k, k, v, v, bias)


def _diff_kernel(qt_ref, k_ref, vt_ref, bias_ref, far_ref, lam_ref, g_ref, o_ref, *, lam_init):
    tq = qt_ref.shape[4]
    tk = vt_ref.shape[4]
    nd = bias_ref.shape[2]
    qi = pl.program_id(2)
    qt = qt_ref[0, 0, 0]
    row = lax.broadcasted_iota(jnp.int32, (LANES, tq), 0)
    zero = jnp.zeros_like(qt)
    qm = (jnp.where(row < DIFF_HD, qt, zero), jnp.where(row >= DIFF_HD, qt, zero))

    def step(js, carry, bias_of):
        keys = [k_ref[0, pl.ds(pl.multiple_of(j * tk, tk), tk), :] for j in js]
        vts = [vt_ref[0, 0, j] for j in js]
        logits = [[_dot(k, qm[mi]) + bias_of(mi, j) for j, k in zip(js, keys)] for mi in range(2)]
        return _softmax_step_t(logits, [vts, vts], carry)

    one = (jnp.full((1, tq), NEG, F32), jnp.zeros((1, tq), F32), jnp.zeros((LANES, tq), F32))
    carry = (one, one)
    n_far = jnp.maximum(qi - nd + 1, 0)
    carry = _loop_pairs(0, n_far, functools.partial(step, bias_of=lambda mi, j: far_ref[0, mi, 0:1, 0:1]), carry,
                        group=4)
    carry = _loop_pairs(n_far, qi, functools.partial(step, bias_of=lambda mi, j: bias_ref[0, mi, qi - j]), carry,
                        group=4)
    carry = step((qi,), carry, lambda mi, j: bias_ref[0, mi, 0])
    (_, l0, a0), (_, l1, a1) = carry
    o = (a0 / l0 - lam_ref[0:1, 0:1] * (a1 / l1)).T
    o_ref[0] = (_rms(o, g_ref[...]) * (1.0 - lam_init)).astype(BF16)


def _diff(qt, k, vt, bias, far, lam, sub_g, lam_init):
    b, s, _ = k.shape
    tq = qt.shape[4]
    nk, tk = vt.shape[2], vt.shape[4]
    nd = bias.shape[2]
    return pl.pallas_call(
        functools.partial(_diff_kernel, lam_init=lam_init),
        grid=(DIFF_HEADS, b, s // tq),
        in_specs=[
            pl.BlockSpec((1, 1, 1, LANES, tq), lambda h, bi, qi: (bi, h, qi, 0, 0)),
            pl.BlockSpec((1, s, LANES), lambda h, bi, qi: (bi, 0, h)),
            pl.BlockSpec((1, 1, nk, LANES, tk), lambda h, bi, qi: (bi, h, 0, 0, 0)),
            _resident((1, 2, nd, tk, tq), lambda h, bi, qi: (h, 0, 0, 0, 0)),
            pl.BlockSpec((1, 2, 8, LANES), lambda h, bi, qi: (h, 0, 0, 0)),
            pl.BlockSpec((1, LANES), lambda h, bi, qi: (0, 0)),
            pl.BlockSpec((1, LANES), lambda h, bi, qi: (0, 0)),
        ],
        out_specs=pl.BlockSpec((1, tq, LANES), lambda h, bi, qi: (bi, qi, h)),
        out_shape=jax.ShapeDtypeStruct((b, s, 512), BF16),
        compiler_params=_cparams(("parallel", "parallel", "arbitrary")),
        name="diff",
    )(qt, k, vt, bias, far, lam, sub_g)


def _sb_kernel(q_ref, kt_ref, v_ref, o_ref):
    tq = q_ref.shape[1]
    tk = kt_ref.shape[4]
    qi = pl.program_id(2)
    lane = lax.broadcasted_iota(jnp.int32, (tq, LANES), 1)
    strict = (lax.broadcasted_iota(jnp.int32, (tq, tk), 1)
              < lax.broadcasted_iota(jnp.int32, (tq, tk), 0))
    later = (lax.broadcasted_iota(jnp.int32, (tk, tk), 0)
             > lax.broadcasted_iota(jnp.int32, (tk, tk), 1)).astype(BF16)
    q = q_ref[0]
    zero = jnp.zeros_like(q)
    qh = (jnp.where(lane < 64, q, zero), jnp.where(lane >= 64, q, zero))

    def blocks(js, state, masked):
        items = [(bi, hd) for bi in range(len(js)) for hd in range(2)]
        z = {it: _dot(qh[it[1]], kt_ref[0, 0, js[it[0]]]) for it in items}
        log_sig, log_1m, inblock = {}, {}, {}
        for it in items:
            neg = -z[it]
            t = jnp.log2(1.0 + jnp.exp2(jnp.minimum(z[it], neg)))
            log_sig[it] = jnp.minimum(z[it], 0.0) - t
            l1m = jnp.minimum(neg, 0.0) - t
            if masked:
                l1m = jnp.where(strict, l1m, 0.0)
            log_1m[it] = l1m
            inblock[it] = _dot(l1m.astype(BF16), later)
        state = list(state)
        for bi, j in enumerate(js):
            v = v_ref[0, pl.ds(pl.multiple_of(j * tk, tk), tk), :]
            for hd in range(2):
                it = (bi, hd)
                c, acc = state[hd]
                w = jnp.exp2(log_sig[it] + (inblock[it] + c))
                if masked:
                    w = jnp.where(strict, w, 0.0)
                acc = acc + _dot(w.astype(BF16), v)
                c = c + jnp.sum(log_1m[it], axis=-1, keepdims=True)
                state[hd] = (c, acc)
        return tuple(state)

    one = (jnp.zeros((tq, 1), F32), jnp.zeros((tq, LANES), F32))
    state = blocks((qi,), (one, one), True)

    odd = qi % 2
    state = lax.cond(odd == 1, lambda st: blocks((qi - 1,), st, False), lambda st: st, state)
    floor = SB_LOG_FLOOR * LOG2E

    def cond(st):
        j, ((c0, _), (c1, _)) = st
        return jnp.logical_and(j >= 1, jnp.max(jnp.maximum(c0, c1)) > floor)

    def body(st):
        j, state = st
        return j - 2, blocks((j, j - 1), state, False)

    _, ((_, a0), (_, a1)) = lax.while_loop(cond, body, (qi - 1 - odd, state))
    o_ref[0] = jnp.where(lane < 64, a0, a1).astype(BF16)


def _sb(q, kt, v):
    b, s, _ = q.shape
    tq = SB_TILE
    nk, tk = kt.shape[2], kt.shape[4]
    return pl.pallas_call(
        _sb_kernel,
        grid=(b, SB_HEADS // 2, s // tq),
        in_specs=[
            pl.BlockSpec((1, tq, LANES), lambda bi, hp, qi: (bi, qi, hp)),
            pl.BlockSpec((1, 1, nk, LANES, tk), lambda bi, hp, qi: (bi, hp, 0, 0, 0)),
            pl.BlockSpec((1, s, LANES), lambda bi, hp, qi: (bi, 0, hp)),
        ],
        out_specs=pl.BlockSpec((1, tq, LANES), lambda bi, hp, qi: (bi, qi, hp)),
        out_shape=jax.ShapeDtypeStruct((b, s, 512), BF16),
        compiler_params=_cparams(("parallel", "parallel", "arbitrary")),
        name="stick_breaking",
    )(q, kt, v)


FF_CHUNKS = ((0, 768), (768, 1536), (1536, 2304), (2304, 2816))
EXPERT_CHUNKS = ((0, 1280), (1280, 2560), (2560, 3584))


def _swiglu(hb, wg_ref, wu_ref, wd_ref, chunks, lead, between=None):
    acc = None
    hb_of = hb if callable(hb) else (lambda ci: hb)
    for ci, (c0, c1) in enumerate(chunks):
        hb = hb_of(ci)
        g = _dot(hb, wg_ref[lead + (slice(None), slice(c0, c1))])
        u = _dot(hb, wu_ref[lead + (slice(None), slice(c0, c1))])
        a = (g / (1.0 + jnp.exp(-g)) * u).astype(BF16)
        part = _dot(a, wd_ref[lead + (slice(c0, c1), slice(None))])
        acc = part if acc is None else acc + part
        if between is not None:
            between(ci)
    return acc


def _mix_out(x_ref, oa_ref, ob_ref, wa_ref, wb_ref, gate_ref, pg_ref):
    y = _dot(oa_ref[0], wa_ref[...]) + _dot(ob_ref[0], wb_ref[...])
    return x_ref[0] + gate_ref[0] * _rms(y, pg_ref[...])


def _post_even_kernel(x_ref, oa_ref, ob_ref, wa_ref, wb_ref, gate_ref, pg_ref,
                      fg_ref, fsh_ref, fsc_ref, fgate_ref, fpg_ref, wg_ref, wu_ref, wd_ref, o_ref):
    x1 = _mix_out(x_ref, oa_ref, ob_ref, wa_ref, wb_ref, gate_ref, pg_ref)
    hb = _prenorm_mod(x1, fg_ref[...], fsh_ref[0], fsc_ref[0]).astype(BF16)
    y = _swiglu(hb, wg_ref, wu_ref, wd_ref, FF_CHUNKS, ())
    o_ref[0] = x1 + fgate_ref[0] * _rms(y, fpg_ref[...])


def _post_even(x, oa, ob, wa, wb, gate, pg, fg, fsh, fsc, fgate, fpg, wg, wu, wd):
    b, s, d = x.shape
    tm = TOK_TILE
    tok = lambda w: pl.BlockSpec((1, tm, w), lambda bi, i: (bi, i, 0))
    vec = pl.BlockSpec((1, d), lambda bi, i: (0, 0))
    mod = pl.BlockSpec((1, 1, d), lambda bi, i: (bi, 0, 0))
    full = lambda a: _resident(a.shape, lambda bi, i: (0,) * a.ndim)
    return pl.pallas_call(
        _post_even_kernel,
        grid=(b, s // tm),
        in_specs=[tok(d), tok(512), tok(512), full(wa), full(wb), mod, vec,
                  vec, mod, mod, mod, vec, full(wg), full(wu), full(wd)],
        out_specs=tok(d),
        out_shape=jax.ShapeDtypeStruct((b, s, d), F32),
        compiler_params=_cparams(("parallel", "parallel")),
        name="post_even",
    )(x, oa, ob, wa, wb, gate, pg, fg, fsh, fsc, fgate, fpg, wg, wu, wd)


def _post_odd_kernel(x_ref, oa_ref, ob_ref, wa_ref, wb_ref, gate_ref, pg_ref,
                     fg_ref, fsh_ref, fsc_ref, rw_ref, rb_ref, x_out, h_out, r_out, cnt_ref):
    x1 = _mix_out(x_ref, oa_ref, ob_ref, wa_ref, wb_ref, gate_ref, pg_ref)
    x_out[0] = x1
    h = _prenorm_mod(x1, fg_ref[...], fsh_ref[0], fsc_ref[0])
    h_out[0] = h
    logits = _dot(h, rw_ref[...]) + rb_ref[...]
    lane = lax.broadcasted_iota(jnp.int32, logits.shape, 1)
    m1 = jnp.max(logits, axis=-1, keepdims=True)
    i1 = jnp.min(jnp.where(logits == m1, lane, LANES), axis=-1, keepdims=True)
    rest = jnp.where(lane == i1, NEG, logits)
    m2 = jnp.max(rest, axis=-1, keepdims=True)
    i2 = jnp.min(jnp.where(rest == m2, lane, LANES), axis=-1, keepdims=True)
    e2 = jnp.exp(m2 - m1)
    w1 = 1.0 / (1.0 + e2)
    w2 = e2 / (1.0 + e2)
    @pl.when(jnp.logical_and(pl.program_id(0) == 0, pl.program_id(1) == 0))
    def _():
        cnt_ref[...] = jnp.zeros_like(cnt_ref)

    tm = logits.shape[0]
    sel = jnp.logical_or(lane == i1, lane == i2)
    earlier = (lax.broadcasted_iota(jnp.int32, (tm, tm), 1)
               < lax.broadcasted_iota(jnp.int32, (tm, tm), 0)).astype(BF16)
    prefix = _dot(earlier, sel.astype(BF16)) + cnt_ref[0:1, :]
    rank1 = jnp.sum(jnp.where(lane == i1, prefix, 0.0), axis=-1, keepdims=True)
    rank2 = jnp.sum(jnp.where(lane == i2, prefix, 0.0), axis=-1, keepdims=True)
    cnt_ref[...] = cnt_ref[...] + jnp.sum(sel.astype(F32), axis=0, keepdims=True)
    r = jnp.where(lane == 0, i1.astype(F32), 0.0)
    r = jnp.where(lane == 1, i2.astype(F32), r)
    r = jnp.where(lane == 2, w1, r)
    r = jnp.where(lane == 3, w2, r)
    r = jnp.where(lane == 4, rank1, r)
    r = jnp.where(lane == 5, rank2, r)
    r_out[0] = r


def _post_odd(x, oa, ob, wa, wb, gate, pg, fg, fsh, fsc, rw, rb):
    b, s, d = x.shape
    tm = TOK_TILE
    tok = lambda w: pl.BlockSpec((1, tm, w), lambda bi, i: (bi, i, 0))
    vec = pl.BlockSpec((1, d), lambda bi, i: (0, 0))
    mod = pl.BlockSpec((1, 1, d), lambda bi, i: (bi, 0, 0))
    full = lambda a: _resident(a.shape, lambda bi, i: (0,) * a.ndim)
    return pl.pallas_call(
        _post_odd_kernel,
        grid=(b, s // tm),
        in_specs=[tok(d), tok(512), tok(512), full(wa), full(wb), mod, vec,
                  vec, mod, mod, full(rw), pl.BlockSpec((1, LANES), lambda bi, i: (0, 0))],
        out_specs=(tok(d), tok(d), tok(LANES),
                   pl.BlockSpec((8, LANES), lambda bi, i: (0, 0))),
        out_shape=(jax.ShapeDtypeStruct((b, s, d), F32), jax.ShapeDtypeStruct((b, s, d), F32),
                   jax.ShapeDtypeStruct((b, s, LANES), F32), jax.ShapeDtypeStruct((8, LANES), F32)),
        compiler_params=_cparams(("arbitrary", "arbitrary")),
        name="post_odd",
    )(x, oa, ob, wa, wb, gate, pg, fg, fsh, fsc, rw, rb)


def _store_rows(ref, lead, val):
    for c in range(ROW_SUB):
        ref[lead + (slice(None), c, slice(None))] = val[:, c * LANES:(c + 1) * LANES]


def _load_rows(ref, lead, start, size):
    return jnp.concatenate([ref[lead, pl.ds(start, size), c, :] for c in range(ROW_SUB)], axis=1)


class _TileGather:
    def __init__(self, idx_hbm, src_hbm, idx_smem, buf, isem, sem, tile_rows):
        self.idx_hbm, self.src_hbm, self.idx_smem, self.buf = idx_hbm, src_hbm, idx_smem, buf
        self.isem, self.sem, self.tile_rows = isem, sem, tile_rows
        self.i = pl.program_id(0)
        self.nt = pl.num_programs(0)
        self.n = buf.shape[1]
        self.slot = self.i % 2
        self.nxt = 1 - self.slot

    def _idx_copy(self, t, sl):
        t = jnp.minimum(t, self.nt - 1)
        return pltpu.make_async_copy(self.idx_hbm.at[t], self.idx_smem.at[sl], self.isem.at[sl])

    def _rows_wait(self, sl):
        pltpu.make_async_copy(self.src_hbm.at[pl.ds(0, self.n)], self.buf.at[sl], self.sem.at[sl]).wait()

    def _issue_row(self, sl, r):
        t = self.idx_smem[sl, r]
        if self.tile_rows:
            src, dst = self.src_hbm.at[t], self.buf.at[sl, r]
        else:
            src, dst = self.src_hbm.at[pl.ds(t, 1)], self.buf.at[sl, pl.ds(r, 1)]
        pltpu.make_async_copy(src, dst, self.sem.at[sl]).start()

    def _issue_loop(self, sl):
        def body(r, carry):
            self._issue_row(sl, r)
            return carry

        lax.fori_loop(0, self.n, body, 0, unroll=8)

    def begin(self):
        @pl.when(self.i == 0)
        def _():
            first = self._idx_copy(0, 0)
            first.start()
            first.wait()
            self._issue_loop(0)
            self._idx_copy(1, 1).start()

        self._idx_copy(self.i + 1, self.nxt).wait()
        self._idx_copy(self.i + 2, self.slot).start()
        self._rows_wait(self.slot)

    def issue(self, part, parts):
        per = -(-self.n // parts)
        for r in range(part * per, min((part + 1) * per, self.n)):
            self._issue_row(self.nxt, r)

    def issue_all(self):
        self._issue_loop(self.nxt)

    def anchor(self, zero):
        return zero

    def finish(self):
        @pl.when(self.i == self.nt - 1)
        def _():
            self._rows_wait(self.nxt)
            self._idx_copy(self.i + 2, self.slot).wait()


def _moe_kernel(te_ref, nu_ref, zero_ref, tok_hbm, h_hbm, wg_ref, wu_ref, wd_ref, y_ref, idx_smem, buf, isem, sem):
    g = _TileGather(tok_hbm, h_hbm, idx_smem, buf, isem, sem, tile_rows=False)
    n = g.n
    g.begin()

    @pl.when(g.i < nu_ref[0])
    def _():
        anchor = [0]
        head = 16
        rest = buf[g.slot, head:n, :].astype(BF16)

        def rows_of(ci):
            first = buf[g.slot, pl.ds(pl.multiple_of(anchor[0], head), head), :].astype(BF16)
            return jnp.concatenate([first, rest], axis=0)

        def issue_part(ci):
            if ci < len(EXPERT_CHUNKS) - 1:
                g.issue(ci, len(EXPERT_CHUNKS) - 1)
                anchor[0] = g.anchor(zero_ref[0])

        y = _swiglu(rows_of, wg_ref, wu_ref, wd_ref, EXPERT_CHUNKS, (0,), between=issue_part)
        _store_rows(y_ref, (), y)

    @pl.when(g.i >= nu_ref[0])
    def _():
        g.issue_all()
        y_ref[...] = jnp.zeros_like(y_ref)

    g.finish()


def _moe(tile_expert, n_used, row_token, h, wg, wu, wd):
    d = h.shape[1]
    nt, tm = row_token.shape
    dff = wg.shape[2]
    wspec = lambda shp: pl.BlockSpec(shp, lambda i, te, nu, z: (te[i], 0, 0), pipeline_mode=pl.Buffered(1))
    grid_spec = pltpu.PrefetchScalarGridSpec(
        num_scalar_prefetch=3,
        grid=(nt,),
        in_specs=[
            pl.BlockSpec(memory_space=pl.ANY),
            pl.BlockSpec(memory_space=pl.ANY),
            wspec((1, d, dff)), wspec((1, d, dff)), wspec((1, dff, d)),
        ],
        out_specs=pl.BlockSpec((tm, ROW_SUB, LANES), lambda i, te, nu, z: (i, 0, 0)),
        scratch_shapes=[
            pltpu.SMEM((2, tm), jnp.int32),
            pltpu.VMEM((2, tm, d), F32),
            pltpu.SemaphoreType.DMA((2,)),
            pltpu.SemaphoreType.DMA((2,)),
        ],
    )
    return pl.pallas_call(
        _moe_kernel,
        grid_spec=grid_spec,
        out_shape=jax.ShapeDtypeStruct((nt * tm, ROW_SUB, LANES), F32),
        compiler_params=_cparams(("arbitrary",)),
        name="moe_experts",
    )(tile_expert, n_used, jnp.zeros((1,), jnp.int32), row_token, h, wg, wu, wd)


def _combine_kernel(zero_ref, pos_hbm, y_hbm, x_ref, r_ref, gate_ref, pg_ref, o_ref, idx_smem, buf, isem, sem):
    tm = x_ref.shape[0]
    g = _TileGather(pos_hbm, y_hbm, idx_smem, buf, isem, sem, tile_rows=True)
    g.begin()
    parts = 4
    rows = tm // parts
    off = 0
    for c in range(parts):
        lo = c * rows
        first = _load_rows(buf, g.slot, pl.multiple_of(off + lo, 8), rows)
        second = _load_rows(buf, g.slot, pl.multiple_of(off + tm + lo, 8), rows)
        r = r_ref[lo:lo + rows, :]
        y = r[:, 2:3] * first + r[:, 3:4] * second
        o_ref[lo:lo + rows, :] = x_ref[lo:lo + rows, :] + gate_ref[0] * _rms(y, pg_ref[...])
        if c < parts - 1:
            g.issue(c, parts - 1)
            off = g.anchor(zero_ref[0])
    g.finish()


def _combine(pos, y, x, r, gate, pg, tokens_per_seq):
    n_tok, d = x.shape
    nt, tm2 = pos.shape
    tm = tm2 // 2
    per_seq = tokens_per_seq // tm
    tok = lambda w: pl.BlockSpec((tm, w), lambda i, z: (i, 0))
    grid_spec = pltpu.PrefetchScalarGridSpec(
        num_scalar_prefetch=1,
        grid=(nt,),
        in_specs=[
            pl.BlockSpec(memory_space=pl.ANY),
            pl.BlockSpec(memory_space=pl.ANY),
            tok(d), tok(LANES),
            pl.BlockSpec((1, 1, d), lambda i, z: (i // per_seq, 0, 0)),
            pl.BlockSpec((1, d), lambda i, z: (0, 0)),
        ],
        out_specs=tok(d),
        scratch_shapes=[
            pltpu.SMEM((2, tm2), jnp.int32),
            pltpu.VMEM((2, tm2, ROW_SUB, LANES), F32),
            pltpu.SemaphoreType.DMA((2,)),
            pltpu.SemaphoreType.DMA((2,)),
        ],
    )
    return pl.pallas_call(
        _combine_kernel,
        grid_spec=grid_spec,
        out_shape=jax.ShapeDtypeStruct((n_tok, d), F32),
        compiler_params=_cparams(("arbitrary",)),
        name="moe_combine",
    )(jnp.zeros((1,), jnp.int32), pos, y, x, r, gate, pg)


def _t5_bucket(dist):
    max_exact = REL_BUCKETS // 2
    d = jnp.maximum(dist, 1).astype(F32)
    log_b = max_exact + (jnp.log(d / max_exact) / math.log(REL_MAX_DIST / max_exact)
                         * (REL_BUCKETS - max_exact)).astype(jnp.int32)
    log_b = jnp.minimum(log_b, REL_BUCKETS - 1)
    return jnp.where(dist < max_exact, dist, log_b)


def _rope_tables(s):
    half = MLA_ROPE // 2
    freqs = ROPE_THETA ** (-jnp.arange(half, dtype=F32) / half)
    ang = jnp.arange(s, dtype=F32)[:, None] * freqs[None, :]
    cos, sin = jnp.cos(ang), jnp.sin(ang)
    z64 = jnp.zeros((s, MLA_NOPE), F32)
    z32 = jnp.zeros((s, LANES - MLA_NOPE - MLA_ROPE), F32)
    ck = jnp.concatenate([z64, cos, cos, z32], axis=1)
    cq = jnp.concatenate([jnp.ones((s, MLA_NOPE), F32), cos, cos, z32], axis=1)
    sn = jnp.concatenate([z64, sin, sin, z32], axis=1)
    return cq, ck, sn


def _even_weights(w_in, w_uq, w_ukv):
    d = w_in.shape[0]
    half = MLA_ROPE // 2
    w_cq = w_in[:, :MLA_Q_RANK]
    w_ckv = w_in[:, MLA_Q_RANK:MLA_Q_RANK + MLA_KV_RANK]
    w_kr = w_in[:, MLA_Q_RANK + MLA_KV_RANK:MLA_Q_RANK + MLA_KV_RANK + MLA_ROPE]
    w_qkv = w_in[:, MLA_Q_RANK + MLA_KV_RANK + MLA_ROPE:]
    z = lambda n: jnp.zeros((d, n), F32)
    kr_a = jnp.concatenate([z(MLA_NOPE), w_kr, z(32)], axis=1)
    kr_b = jnp.concatenate([z(MLA_NOPE), -w_kr[:, half:], w_kr[:, :half], z(32)], axis=1)
    dil_scale = DIL_HD ** -0.5 * LOG2E
    w0 = jnp.concatenate([w_cq, w_ckv, kr_a, kr_b, w_qkv[:, :512] * dil_scale, w_qkv[:, 512:]], axis=1)

    r = w_uq.shape[0]
    wq = w_uq.reshape(r, MLA_HEADS, MLA_NOPE + MLA_ROPE) * ((MLA_NOPE + MLA_ROPE) ** -0.5 * LOG2E)
    zq = lambda n: jnp.zeros((r, MLA_HEADS, n), F32)
    nope, x1, x2 = wq[..., :MLA_NOPE], wq[..., MLA_NOPE:MLA_NOPE + half], wq[..., MLA_NOPE + half:]
    q_a = jnp.concatenate([nope, x1, x2, zq(32)], axis=-1).reshape(r, MLA_HEADS * LANES)
    q_b = jnp.concatenate([zq(MLA_NOPE), -x2, x1, zq(32)], axis=-1).reshape(r, MLA_HEADS * LANES)
    wq2 = jnp.concatenate([q_a, q_b], axis=1)

    rk = w_ukv.shape[0]
    wkv = w_ukv.reshape(rk, MLA_HEADS, MLA_NOPE + MLA_V)
    k_blk = jnp.concatenate([wkv[..., :MLA_NOPE], jnp.zeros((rk, MLA_HEADS, LANES - MLA_NOPE), F32)], axis=-1)
    wkv2 = jnp.concatenate([k_blk.reshape(rk, MLA_HEADS * LANES),
                            wkv[..., MLA_NOPE:].reshape(rk, MLA_HEADS * MLA_V)], axis=1)
    return w0.astype(BF16), wq2.astype(BF16), wkv2.astype(BF16)


def _toeplitz(vec, rows, cols):
    n, width = vec.shape

    def toeplitz_kernel(v_ref, o_ref):
        tiled = jnp.broadcast_to(v_ref[0], (rows, width))
        o_ref[0] = pltpu.roll(tiled, 0, 1, stride=1, stride_axis=0)[:, :cols]

    return pl.pallas_call(
        toeplitz_kernel,
        grid=(n,),
        in_specs=[pl.BlockSpec((1, 1, width), lambda t: (t, 0, 0))],
        out_specs=pl.BlockSpec((1, rows, cols), lambda t: (t, 0, 0)),
        out_shape=jax.ShapeDtypeStruct((n, rows, cols), F32),
        compiler_params=_cparams(("parallel",)),
        name="toeplitz_bias",
    )(vec.reshape(n, 1, width).astype(F32))


def _dil_bias(rel_bias):
    blk = DIL_BLOCK
    width = 4 * blk
    k = jnp.arange(width)
    rel = jnp.where(k < 2 * blk, blk - k, blk + width - k)
    out = []
    for window, dil in DIL_PATTERNS:
        band = (rel >= 0) & (rel <= window // dil)
        bias = rel_bias[_t5_bucket(jnp.maximum(rel, 0) * dil)] * LOG2E
        out.append(jnp.where(band[:, None], bias, NEG).T)
    vec = jnp.stack(out).reshape(3 * DIL_HEADS, width)
    return _toeplitz(vec, blk, 2 * blk).reshape(3, DIL_HEADS, blk, 2 * blk)


def _diff_bias(rel_bias, tile):
    nd = REL_MAX_DIST // tile + 1
    maps = rel_bias.shape[1]
    k = jnp.arange(2 * tile)[None, :]
    dist = jnp.arange(nd)[:, None] * tile + jnp.where(k < tile, k, k - 2 * tile)
    vec = jnp.where((dist >= 0)[..., None], rel_bias[_t5_bucket(jnp.maximum(dist, 0))] * LOG2E, NEG)
    vec = jnp.transpose(vec, (2, 0, 1)).reshape(maps * nd, 2 * tile)
    bias = _toeplitz(vec, tile, tile).reshape(DIFF_HEADS, 2, nd, tile, tile)
    far = rel_bias[_t5_bucket(jnp.array(REL_MAX_DIST))] * LOG2E
    far = jnp.broadcast_to(far.reshape(DIFF_HEADS, 2, 1, 1), (DIFF_HEADS, 2, 8, LANES))
    return bias, far.astype(F32)


def _routing(r, counts, n_tok, tile):
    n_tiles = (2 * n_tok) // tile + N_EXPERTS
    e = jnp.concatenate([r[:, 0], r[:, 1]]).astype(jnp.int32)
    rank = jnp.concatenate([r[:, 4], r[:, 5]]).astype(jnp.int32)
    counts = counts.astype(jnp.int32)
    padded = ((counts + tile - 1) // tile) * tile
    ends = jnp.cumsum(padded)
    starts = ends - padded
    onehot = (e[:, None] == jnp.arange(N_EXPERTS)[None, :]).astype(jnp.int32)
    pos = jnp.sum(onehot * starts[None, :], axis=1) + rank
    token = jnp.tile(jnp.arange(n_tok, dtype=jnp.int32), 2)
    row_token = jnp.zeros((n_tiles * tile,), jnp.int32).at[pos].set(token)
    tile_start = jnp.arange(n_tiles, dtype=jnp.int32) * tile
    tile_expert = jnp.sum((tile_start[:, None] >= ends[None, :]).astype(jnp.int32), axis=1)
    n_used = (ends[-1] // tile).astype(jnp.int32)
    last = jnp.sum((ends[-1] - 1 >= ends).astype(jnp.int32))
    tile_expert = jnp.minimum(tile_expert, last).astype(jnp.int32)
    return (tile_expert, n_used.reshape(1), row_token.reshape(n_tiles, tile),
            pos[:n_tok].astype(jnp.int32), pos[n_tok:].astype(jnp.int32))


def kernel(x, c, rel_bias, ada_mix_w, ada_mix_b, mix_pre_g, mix_post_g, ada_ffn_w, ada_ffn_b, ffn_pre_g, ffn_post_g, e_w_in, e_q_norm_g, e_w_uq, e_kv_norm_g, e_w_ukv, e_w_out, ffn_w_gate, ffn_w_up, ffn_w_down, o_w_in, diff_lq1, diff_lk1, diff_lq2, diff_lk2, diff_sub_g, o_w_out, router_w, router_b, moe_w_gate, moe_w_up, moe_w_down):
    b, s, d = x.shape
    assert d == D_MODEL and s % DIL_SUPER == 0 and s % TOK_TILE == 0
    row = lambda v: v.reshape(1, -1).astype(F32)

    mix_mod = _ada(c, ada_mix_w, ada_mix_b)
    ffn_mod = _ada(c, ada_ffn_w, ada_ffn_b)

    shift, scale, gate = _split_mod(mix_mod[0])
    w0, wq2, wkv2 = _even_weights(e_w_in[0], e_w_uq[0], e_w_ukv[0])
    cq, ck, sn = _rope_tables(s)
    qat, ka, vat, qb, kb, vb = _even_in(x, row(mix_pre_g[0]), shift, scale, w0, row(e_q_norm_g[0]), wq2,
                                        row(e_kv_norm_g[0]), wkv2, cq, ck, sn)
    o_a = _mla(qat, ka, vat)
    o_b = _dil(qb, kb, vb, _dil_bias(rel_bias))
    fshift, fscale, fgate = _split_mod(ffn_mod[0])
    w_out = e_w_out[0].astype(BF16)
    x = _post_even(x, o_a, o_b, w_out[:512], w_out[512:], gate, row(mix_post_g[0]),
                   row(ffn_pre_g[0]), fshift, fscale, fgate, row(ffn_post_g[0]),
                   ffn_w_gate[0].astype(BF16), ffn_w_up[0].astype(BF16), ffn_w_down[0].astype(BF16))

    layer = 1
    shift, scale, gate = _split_mod(mix_mod[1])
    w_in = o_w_in[0]
    att_scale = DIFF_HD ** -0.5
    w1 = jnp.concatenate([w_in[:, :512] * (att_scale * LOG2E), w_in[:, 512:1536],
                          w_in[:, 1536:2048] * (SB_HD ** -0.5 * LOG2E), w_in[:, 2048:]], axis=1).astype(BF16)
    qdt, kd, vdt, qs, kst, vs = _odd_in(x, row(mix_pre_g[1]), shift, scale, w1)
    lam_init = 0.8 - 0.6 * math.exp(-0.3 * layer)
    lam = (jnp.exp(jnp.sum(diff_lq1[0].astype(F32) * diff_lk1[0].astype(F32)))
           - jnp.exp(jnp.sum(diff_lq2[0].astype(F32) * diff_lk2[0].astype(F32))) + lam_init)
    bias, far = _diff_bias(rel_bias, DIFF_TILE)
    o_c = _diff(qdt, kd, vdt, bias, far, jnp.full((1, LANES), lam, F32), row(diff_sub_g[0]), lam_init)
    o_d = _sb(qs, kst, vs)
    fshift, fscale, fgate = _split_mod(ffn_mod[1])
    w_out = o_w_out[0].astype(BF16)
    rw = jnp.zeros((d, LANES), F32).at[:, :N_EXPERTS].set(router_w[0].astype(F32))
    rb = jnp.full((1, LANES), NEG, F32).at[0, :N_EXPERTS].set(router_b[0].astype(F32))
    x, h, r, counts = _post_odd(x, o_c, o_d, w_out[:512], w_out[512:], gate, row(mix_post_g[1]),
                                row(ffn_pre_g[1]), fshift, fscale, rw, rb)

    n_tok = b * s
    tile_expert, n_used, row_token, pos0, pos1 = _routing(r.reshape(n_tok, LANES), counts[0, :N_EXPERTS],
                                                          n_tok, MOE_TILE)
    y = _moe(tile_expert, n_used, row_token, h.reshape(n_tok, d),
             moe_w_gate[0].astype(BF16), moe_w_up[0].astype(BF16), moe_w_down[0].astype(BF16))
    ct = TOK_TILE
    pos = jnp.concatenate([pos0.reshape(n_tok // ct, ct), pos1.reshape(n_tok // ct, ct)], axis=1)
    out = _combine(pos, y, x.reshape(n_tok, d), r.reshape(n_tok, LANES), fgate, row(ffn_post_g[1]), s)
    return out.reshape(b, s, d)
```

```python
import functools
import math

import jax
import jax.numpy as jnp
from jax import lax
from jax.experimental import pallas as pl
from jax.experimental.pallas import tpu as pltpu

F32 = jnp.float32
BF16 = jnp.bfloat16

D_MODEL = 1024
EPS = 1e-6

MLA_HEADS = 8
MLA_NOPE = 64
MLA_ROPE = 32
MLA_V = 64
MLA_Q_RANK = 256
MLA_KV_RANK = 128
ROPE_THETA = 10000.0

DIL_HEADS = 8
DIL_HD = 64
DIL_PATTERNS = ((128, 1), (512, 4), (2048, 16))
DIL_BLOCK = 128

DIFF_HEADS = 4
DIFF_HD = 64
SB_HEADS = 8
SB_HD = 64

REL_BUCKETS = 32
REL_MAX_DIST = 2048

D_FF = 2816
N_EXPERTS = 8
D_FF_EXPERT = 3584

LANES = 128
ROW_SUB = D_MODEL // LANES
LOG2E = math.log2(math.e)
NEG = -1e30

TOK_TILE = 512
MLA_TILE = 512
DIFF_TILE = 512
SB_TILE = 256
DIL_SUPER = DIL_BLOCK * 16
MOE_TILE = 512
SB_LOG_FLOOR = -104.0

VMEM_LIMIT = 56 * 1024 * 1024


def _cparams(sem):
    return pltpu.CompilerParams(dimension_semantics=sem, vmem_limit_bytes=VMEM_LIMIT)


def _resident(shape, index_map):
    return pl.BlockSpec(shape, index_map, pipeline_mode=pl.Buffered(1))


def _rms(x, g):
    return x * lax.rsqrt(jnp.mean(x * x, axis=-1, keepdims=True) + EPS) * g


def _dot(a, b):
    return jnp.dot(a, b, preferred_element_type=F32)


def _softmax_step_t(logits, values, carry):
    out = []
    for s_list, v_list, (m, l, acc) in zip(logits, values, carry):
        for s, vt in zip(s_list, v_list):
            m_new = jnp.maximum(m, jnp.max(s, axis=0, keepdims=True))
            alpha = jnp.exp2(m - m_new)
            p = jnp.exp2(s - m_new)
            l = alpha * l + jnp.sum(p, axis=0, keepdims=True)
            acc = alpha * acc + _dot(vt, p.astype(BF16))
            m = m_new
        out.append((m, l, acc))
    return tuple(out)


def _loop_pairs(lo, hi, step, carry, group=2):
    n = hi - lo
    carry = lax.fori_loop(
        0, n // group, lambda i, c: step(tuple(lo + group * i + g for g in range(group)), c), carry)
    size = group // 2
    while size >= 1:
        start = lo + (n // (2 * size)) * (2 * size)
        carry = lax.cond((n // size) % 2 == 1,
                         lambda c, start=start, size=size: step(tuple(start + g for g in range(size)), c),
                         lambda c: c, carry)
        size //= 2
    return carry


def _ada_kernel(c_ref, w_ref, b_ref, o_ref):
    c = c_ref[...]
    sc = c / (1.0 + jnp.exp(-c))
    o_ref[0] = _dot(sc.astype(BF16), w_ref[0].astype(BF16)) + b_ref[0]


def _ada(c, w, b):
    nl, d, d3 = w.shape
    bsz = c.shape[0]
    nb = d3 // d
    return pl.pallas_call(
        _ada_kernel,
        grid=(nl, nb),
        in_specs=[
            pl.BlockSpec((bsz, d), lambda l, j: (0, 0)),
            pl.BlockSpec((1, d, d), lambda l, j: (l, 0, j)),
            pl.BlockSpec((1, 1, d), lambda l, j: (l, 0, j)),
        ],
        out_specs=pl.BlockSpec((1, bsz, d), lambda l, j: (l, 0, j)),
        out_shape=jax.ShapeDtypeStruct((nl, bsz, d3), F32),
        compiler_params=_cparams(("arbitrary", "arbitrary")),
        name="ada",
    )(c, w, b.reshape(nl, 1, d3))


def _split_mod(m):
    b = m.shape[0]
    m = m.reshape(b, 3, 1, D_MODEL)
    return m[:, 0], m[:, 1], m[:, 2]


def _prenorm_mod(x, g, shift, scale):
    return _rms(x, g) * (1.0 + scale) + shift


def _even_in_kernel(x_ref, g_ref, sh_ref, sc_ref, w0_ref, qg_ref, wq_ref, kvg_ref, wkv_ref,
                    cq_ref, ck_ref, sn_ref,
                    qa_ref, ka_ref, va_ref, qb_ref, kb_ref, vb_ref):
    h = _prenorm_mod(x_ref[0], g_ref[...], sh_ref[0], sc_ref[0]).astype(BF16)
    proj = _dot(h, w0_ref[...])
    cqn = _rms(proj[:, 0:256], qg_ref[...]).astype(BF16)
    qq = _dot(cqn, wq_ref[...])
    ckvn = _rms(proj[:, 256:384], kvg_ref[...]).astype(BF16)
    kv = _dot(ckvn, wkv_ref[...])
    cq = cq_ref[...]
    ck = ck_ref[...]
    sn = sn_ref[...]
    krope = proj[:, 384:512] * ck + proj[:, 512:640] * sn
    nh = MLA_HEADS
    for hd in range(nh):
        lo = hd * LANES
        qh = qq[:, lo:lo + LANES] * cq + qq[:, nh * LANES + lo:nh * LANES + lo + LANES] * sn
        _store_key_tiles(qa_ref, hd, qh)
        ka_ref[0, :, lo:lo + LANES] = (kv[:, lo:lo + LANES] + krope).astype(BF16)
    for pr in range(nh // 2):
        _store_key_tiles(va_ref, pr, kv[:, nh * LANES + pr * LANES:nh * LANES + (pr + 1) * LANES])
    qb_ref[0] = proj[:, 640:1152].astype(BF16)
    kb_ref[0] = proj[:, 1152:1664].astype(BF16)
    vb_ref[0] = proj[:, 1664:2176].astype(BF16)


def _even_in(x, g, shift, scale, w0, qg, wq, kvg, wkv, cq, ck, sn):
    b, s, d = x.shape
    tm = TOK_TILE
    tkb = MLA_TILE
    ns = s // tm
    tok = lambda w: pl.BlockSpec((1, tm, w), lambda bi, i: (bi, i, 0))
    vec = lambda w: pl.BlockSpec((1, w), lambda bi, i: (0, 0))
    mod = pl.BlockSpec((1, 1, d), lambda bi, i: (bi, 0, 0))
    tab = pl.BlockSpec((tm, LANES), lambda bi, i: (i, 0))
    full = lambda a: _resident(a.shape, lambda bi, i: (0,) * a.ndim)
    out_shapes = (
        jax.ShapeDtypeStruct((b, MLA_HEADS, s // tkb, LANES, tkb), BF16),
        jax.ShapeDtypeStruct((b, s, MLA_HEADS * LANES), BF16),
        jax.ShapeDtypeStruct((b, MLA_HEADS // 2, s // tkb, LANES, tkb), BF16),
        jax.ShapeDtypeStruct((b, s, 512), BF16),
        jax.ShapeDtypeStruct((b, s, 512), BF16),
        jax.ShapeDtypeStruct((b, s, 512), BF16),
    )
    out_specs = (
        pl.BlockSpec((1, MLA_HEADS, tm // tkb, LANES, tkb), lambda bi, i: (bi, 0, i, 0, 0)),
        tok(MLA_HEADS * LANES),
        pl.BlockSpec((1, MLA_HEADS // 2, tm // tkb, LANES, tkb), lambda bi, i: (bi, 0, i, 0, 0)),
        tok(512), tok(512), tok(512),
    )
    return pl.pallas_call(
        _even_in_kernel,
        grid=(b, ns),
        in_specs=[tok(d), vec(d), mod, mod, full(w0), vec(MLA_Q_RANK), full(wq), vec(MLA_KV_RANK), full(wkv),
                  tab, tab, tab],
        out_specs=out_specs,
        out_shape=out_shapes,
        compiler_params=_cparams(("parallel", "parallel")),
        name="even_in",
    )(x, g, shift, scale, w0, qg, wq, kvg, wkv, cq, ck, sn)


def _store_key_tiles(kt_ref, hd, k):
    tkb = kt_ref.shape[4]
    for t in range(k.shape[0] // tkb):
        kt_ref[0, hd, t] = k[t * tkb:(t + 1) * tkb, :].T.astype(BF16)


def _odd_in_kernel(x_ref, g_ref, sh_ref, sc_ref, w_ref,
                   qdt_ref, kd_ref, vdt_ref, qs_ref, kst_ref, vs_ref):
    h = _prenorm_mod(x_ref[0], g_ref[...], sh_ref[0], sc_ref[0]).astype(BF16)
    proj = _dot(h, w_ref[...])
    kd_ref[0] = proj[:, 512:1024].astype(BF16)
    qs_ref[0] = proj[:, 1536:2048].astype(BF16)
    vs_ref[0] = proj[:, 2560:3072].astype(BF16)
    for hd in range(4):
        _store_key_tiles(qdt_ref, hd, proj[:, hd * LANES:(hd + 1) * LANES])
        _store_key_tiles(vdt_ref, hd, proj[:, 1024 + hd * LANES:1024 + (hd + 1) * LANES])
        _store_key_tiles(kst_ref, hd, proj[:, 2048 + hd * LANES:2048 + (hd + 1) * LANES])


def _odd_in(x, g, shift, scale, w):
    b, s, d = x.shape
    tm = TOK_TILE
    ns = s // tm
    tok = lambda wd: pl.BlockSpec((1, tm, wd), lambda bi, i: (bi, i, 0))
    mod = pl.BlockSpec((1, 1, d), lambda bi, i: (bi, 0, 0))
    ktspec = lambda tkb: pl.BlockSpec((1, 4, tm // tkb, LANES, tkb), lambda bi, i: (bi, 0, i, 0, 0))
    act = jax.ShapeDtypeStruct((b, s, 512), BF16)
    kts = lambda tkb: jax.ShapeDtypeStruct((b, 4, s // tkb, LANES, tkb), BF16)
    return pl.pallas_call(
        _odd_in_kernel,
        grid=(b, ns),
        in_specs=[tok(d), pl.BlockSpec((1, d), lambda bi, i: (0, 0)), mod, mod,
                  _resident(w.shape, lambda bi, i: (0, 0))],
        out_specs=(ktspec(DIFF_TILE), tok(512), ktspec(DIFF_TILE), tok(512), ktspec(SB_TILE), tok(512)),
        out_shape=(kts(DIFF_TILE), act, kts(DIFF_TILE), act, kts(SB_TILE), act),
        compiler_params=_cparams(("parallel", "parallel")),
        name="odd_in",
    )(x, g, shift, scale, w)


def _mla_kernel(qt_ref, k_ref, vt_ref, o_ref):
    tq = qt_ref.shape[4]
    tk = vt_ref.shape[4]
    hv = MLA_V
    qi = pl.program_id(2)
    causal = (lax.broadcasted_iota(jnp.int32, (tk, tq), 0)
              <= lax.broadcasted_iota(jnp.int32, (tk, tq), 1))
    qts = (qt_ref[0, 0, 0], qt_ref[0, 1, 0])

    def step(js, carry, masked):
        logits, values = [], []
        for hd in range(2):
            s_list = [_dot(k_ref[0, pl.ds(pl.multiple_of(j * tk, tk), tk), hd * LANES:(hd + 1) * LANES], qts[hd])
                      for j in js]
            if masked:
                s_list = [jnp.where(causal, s, NEG) for s in s_list]
            logits.append(s_list)
            values.append([vt_ref[0, 0, j, hd * hv:(hd + 1) * hv, :] for j in js])
        return _softmax_step_t(logits, values, carry)

    one = (jnp.full((1, tq), NEG, F32), jnp.zeros((1, tq), F32), jnp.zeros((hv, tq), F32))
    carry = _loop_pairs(0, qi, functools.partial(step, masked=False), (one, one), group=4)
    (_, l0, a0), (_, l1, a1) = step((qi,), carry, True)
    o_ref[0] = jnp.concatenate([a0 / l0, a1 / l1], axis=0).T.astype(BF16)


def _mla(qt, k, vt):
    b, s, _ = k.shape
    tq = qt.shape[4]
    nk = vt.shape[2]
    tk = vt.shape[4]
    return pl.pallas_call(
        _mla_kernel,
        grid=(b, MLA_HEADS // 2, s // tq),
        in_specs=[
            pl.BlockSpec((1, 2, 1, LANES, tq), lambda bi, hp, qi: (bi, hp, qi, 0, 0)),
            pl.BlockSpec((1, s, 2 * LANES), lambda bi, hp, qi: (bi, 0, hp)),
            pl.BlockSpec((1, 1, nk, LANES, tk), lambda bi, hp, qi: (bi, hp, 0, 0, 0)),
        ],
        out_specs=pl.BlockSpec((1, tq, LANES), lambda bi, hp, qi: (bi, qi, hp)),
        out_shape=jax.ShapeDtypeStruct((b, s, 512), BF16),
        compiler_params=_cparams(("parallel", "parallel", "arbitrary")),
        name="mla",
    )(qt, k, vt)


def _dil_kernel(q_ref, kc_ref, kp_ref, vc_ref, vp_ref, bias_ref, o_ref,
                q32, k32, v32, acc_s, m_s, d_s):
    sup = DIL_SUPER
    blk = DIL_BLOCK
    n = pl.program_id(2)
    q32[...] = q_ref[0].astype(F32)
    k32[0:sup, :] = kp_ref[0].astype(F32)
    k32[sup:2 * sup, :] = kc_ref[0].astype(F32)
    v32[0:sup, :] = vp_ref[0].astype(F32)
    v32[sup:2 * sup, :] = vc_ref[0].astype(F32)
    low = lax.broadcasted_iota(jnp.int32, (blk, LANES), 1) < 64
    before_start = jnp.where(lax.broadcasted_iota(jnp.int32, (blk, 2 * blk), 1) < blk, NEG, 0.0)

    for g, (_, dil) in enumerate(DIL_PATTERNS):

        def unit(u, carry, g=g, dil=dil):
            n_loc = u // dil
            r = u % dil
            qs = n_loc * (blk * dil) + r
            ks = sup + (n_loc - 1) * (blk * dil) + r
            if dil == 1:
                qsl = pl.ds(pl.multiple_of(qs, blk), blk)
                ksl = pl.ds(pl.multiple_of(ks, blk), 2 * blk)
            else:
                qsl = pl.ds(qs, blk, stride=dil)
                ksl = pl.ds(ks, 2 * blk, stride=dil)
            q = q32[qsl, :]
            k = k32[ksl, :].astype(BF16)
            v = v32[ksl, :].astype(BF16)
            extra = jnp.where(jnp.logical_and(n == 0, n_loc == 0), before_start, 0.0)
            parts = []
            for hd in range(2):
                qh = jnp.where(low if hd == 0 else jnp.logical_not(low), q, 0.0).astype(BF16)
                s = lax.dot_general(qh, k, (((1,), (1,)), ((), ())), preferred_element_type=F32)
                s = s + bias_ref[g, hd] + extra
                m = jnp.max(s, axis=-1, keepdims=True)
                e = jnp.exp2(s - m)
                den = jnp.sum(e, axis=-1, keepdims=True)
                parts.append((_dot(e.astype(BF16), v), m, den))
            acc_s[g, qsl, :] = jnp.where(low, parts[0][0], parts[1][0])
            m_s[g, qsl, :] = jnp.where(low, parts[0][1], parts[1][1])
            d_s[g, qsl, :] = jnp.where(low, parts[0][2], parts[1][2])
            return carry

        lax.fori_loop(0, 16, unit, 0, unroll=16)

    mx = jnp.maximum(jnp.maximum(m_s[0], m_s[1]), m_s[2])
    num = jnp.zeros((sup, LANES), F32)
    den = jnp.zeros((sup, LANES), F32)
    for g in range(3):
        a = jnp.exp2(m_s[g] - mx)
        num = num + a * acc_s[g]
        den = den + a * d_s[g]
    o_ref[0] = (num / den).astype(BF16)


def _dil(q, k, v, bias):
    b, s, _ = q.shape
    sup = DIL_SUPER
    cur = pl.BlockSpec((1, sup, LANES), lambda bi, hp, n: (bi, n, hp))
    prev = pl.BlockSpec((1, sup, LANES), lambda bi, hp, n: (bi, jnp.maximum(n - 1, 0), hp))
    return pl.pallas_call(
        _dil_kernel,
        grid=(b, DIL_HEADS // 2, s // sup),
        in_specs=[cur, cur, prev, cur, prev,
                  pl.BlockSpec((3, 2, DIL_BLOCK, 2 * DIL_BLOCK), lambda bi, hp, n: (0, hp, 0, 0))],
        out_specs=cur,
        out_shape=jax.ShapeDtypeStruct((b, s, 512), BF16),
        scratch_shapes=[
            pltpu.VMEM((sup, LANES), F32),
            pltpu.VMEM((2 * sup, LANES), F32),
            pltpu.VMEM((2 * sup, LANES), F32),
            pltpu.VMEM((3, sup, LANES), F32),
            pltpu.VMEM((3, sup, LANES), F32),
            pltpu.VMEM((3, sup, LANES), F32),
        ],
        compiler_params=_cparams(("parallel", "parallel", "arbitrary")),
        name="dilated",
    )(q, k, k, v, v, bias)


def _diff_kernel(qt_ref, k_ref, vt_ref, bias_ref, far_ref, lam_ref, g_ref, o_ref, *, lam_init):
    tq = qt_ref.shape[4]
    tk = vt_ref.shape[4]
    nd = bias_ref.shape[2]
    qi = pl.program_id(2)
    qt = qt_ref[0, 0, 0]
    row = lax.broadcasted_iota(jnp.int32, (LANES, tq), 0)
    zero = jnp.zeros_like(qt)
    qm = (jnp.where(row < DIFF_HD, qt, zero), jnp.where(row >= DIFF_HD, qt, zero))

    def step(js, carry, bias_of):
        keys = [k_ref[0, pl.ds(pl.multiple_of(j * tk, tk), tk), :] for j in js]
        vts = [vt_ref[0, 0, j] for j in js]
        logits = [[_dot(k, qm[mi]) + bias_of(mi, j) for j, k in zip(js, keys)] for mi in range(2)]
        return _softmax_step_t(logits, [vts, vts], carry)

    one = (jnp.full((1, tq), NEG, F32), jnp.zeros((1, tq), F32), jnp.zeros((LANES, tq), F32))
    carry = (one, one)
    n_far = jnp.maximum(qi - nd + 1, 0)
    carry = _loop_pairs(0, n_far, functools.partial(step, bias_of=lambda mi, j: far_ref[0, mi, 0:1, 0:1]), carry,
                        group=4)
    carry = _loop_pairs(n_far, qi, functools.partial(step, bias_of=lambda mi, j: bias_ref[0, mi, qi - j]), carry,
                        group=4)
    carry = step((qi,), carry, lambda mi, j: bias_ref[0, mi, 0])
    (_, l0, a0), (_, l1, a1) = carry
    o = (a0 / l0 - lam_ref[0:1, 0:1] * (a1 / l1)).T
    o_ref[0] = (_rms(o, g_ref[...]) * (1.0 - lam_init)).astype(BF16)


def _diff(qt, k, vt, bias, far, lam, sub_g, lam_init):
    b, s, _ = k.shape
    tq = qt.shape[4]
    nk, tk = vt.shape[2], vt.shape[4]
    nd = bias.shape[2]
    return pl.pallas_call(
        functools.partial(_diff_kernel, lam_init=lam_init),
        grid=(DIFF_HEADS, b, s // tq),
        in_specs=[
            pl.BlockSpec((1, 1, 1, LANES, tq), lambda h, bi, qi: (bi, h, qi, 0, 0)),
            pl.BlockSpec((1, s, LANES), lambda h, bi, qi: (bi, 0, h)),
            pl.BlockSpec((1, 1, nk, LANES, tk), lambda h, bi, qi: (bi, h, 0, 0, 0)),
            _resident((1, 2, nd, tk, tq), lambda h, bi, qi: (h, 0, 0, 0, 0)),
            pl.BlockSpec((1, 2, 8, LANES), lambda h, bi, qi: (h, 0, 0, 0)),
            pl.BlockSpec((1, LANES), lambda h, bi, qi: (0, 0)),
            pl.BlockSpec((1, LANES), lambda h, bi, qi: (0, 0)),
        ],
        out_specs=pl.BlockSpec((1, tq, LANES), lambda h, bi, qi: (bi, qi, h)),
        out_shape=jax.ShapeDtypeStruct((b, s, 512), BF16),
        compiler_params=_cparams(("parallel", "parallel", "arbitrary")),
        name="diff",
    )(qt, k, vt, bias, far, lam, sub_g)


def _sb_kernel(q_ref, kt_ref, v_ref, o_ref):
    tq = q_ref.shape[1]
    tk = kt_ref.shape[4]
    qi = pl.program_id(2)
    lane = lax.broadcasted_iota(jnp.int32, (tq, LANES), 1)
    strict = (lax.broadcasted_iota(jnp.int32, (tq, tk), 1)
              < lax.broadcasted_iota(jnp.int32, (tq, tk), 0))
    later = (lax.broadcasted_iota(jnp.int32, (tk, tk), 0)
             > lax.broadcasted_iota(jnp.int32, (tk, tk), 1)).astype(BF16)
    q = q_ref[0]
    zero = jnp.zeros_like(q)
    qh = (jnp.where(lane < 64, q, zero), jnp.where(lane >= 64, q, zero))

    def blocks(js, state, masked):
        items = [(bi, hd) for bi in range(len(js)) for hd in range(2)]
        z = {it: _dot(qh[it[1]], kt_ref[0, 0, js[it[0]]]) for it in items}
        log_sig, log_1m, inblock = {}, {}, {}
        for it in items:
            neg = -z[it]
            t = jnp.log2(1.0 + jnp.exp2(jnp.minimum(z[it], neg)))
            log_sig[it] = jnp.minimum(z[it], 0.0) - t
            l1m = jnp.minimum(neg, 0.0) - t
            if masked:
                l1m = jnp.where(strict, l1m, 0.0)
            log_1m[it] = l1m
            inblock[it] = _dot(l1m.astype(BF16), later)
        state = list(state)
        for bi, j in enumerate(js):
            v = v_ref[0, pl.ds(pl.multiple_of(j * tk, tk), tk), :]
            for hd in range(2):
                it = (bi, hd)
                c, acc = state[hd]
                w = jnp.exp2(log_sig[it] + (inblock[it] + c))
                if masked:
                    w = jnp.where(strict, w, 0.0)
                acc = acc + _dot(w.astype(BF16), v)
                c = c + jnp.sum(log_1m[it], axis=-1, keepdims=True)
                state[hd] = (c, acc)
        return tuple(state)

    one = (jnp.zeros((tq, 1), F32), jnp.zeros((tq, LANES), F32))
    state = blocks((qi,), (one, one), True)

    odd = qi % 2
    state = lax.cond(odd == 1, lambda st: blocks((qi - 1,), st, False), lambda st: st, state)
    floor = SB_LOG_FLOOR * LOG2E

    def cond(st):
        j, ((c0, _), (c1, _)) = st
        return jnp.logical_and(j >= 1, jnp.max(jnp.maximum(c0, c1)) > floor)

    def body(st):
        j, state = st
        return j - 2, blocks((j, j - 1), state, False)

    _, ((_, a0), (_, a1)) = lax.while_loop(cond, body, (qi - 1 - odd, state))
    o_ref[0] = jnp.where(lane < 64, a0, a1).astype(BF16)


def _sb(q, kt, v):
    b, s, _ = q.shape
    tq = SB_TILE
    nk, tk = kt.shape[2], kt.shape[4]
    return pl.pallas_call(
        _sb_kernel,
        grid=(b, SB_HEADS // 2, s // tq),
        in_specs=[
            pl.BlockSpec((1, tq, LANES), lambda bi, hp, qi: (bi, qi, hp)),
            pl.BlockSpec((1, 1, nk, LANES, tk), lambda bi, hp, qi: (bi, hp, 0, 0, 0)),
            pl.BlockSpec((1, s, LANES), lambda bi, hp, qi: (bi, 0, hp)),
        ],
        out_specs=pl.BlockSpec((1, tq, LANES), lambda bi, hp, qi: (bi, qi, hp)),
        out_shape=jax.ShapeDtypeStruct((b, s, 512), BF16),
        compiler_params=_cparams(("parallel", "parallel", "arbitrary")),
        name="stick_breaking",
    )(q, kt, v)


FF_CHUNKS = ((0, 768), (768, 1536), (1536, 2304), (2304, 2816))
EXPERT_CHUNKS = ((0, 1280), (1280, 2560), (2560, 3584))


def _swiglu(hb, wg_ref, wu_ref, wd_ref, chunks, lead, between=None):
    acc = None
    hb_of = hb if callable(hb) else (lambda ci: hb)
    for ci, (c0, c1) in enumerate(chunks):
        hb = hb_of(ci)
        g = _dot(hb, wg_ref[lead + (slice(None), slice(c0, c1))])
        u = _dot(hb, wu_ref[lead + (slice(None), slice(c0, c1))])
        a = (g / (1.0 + jnp.exp(-g)) * u).astype(BF16)
        part = _dot(a, wd_ref[lead + (slice(c0, c1), slice(None))])
        acc = part if acc is None else acc + part
        if between is not None:
            between(ci)
    return acc


def _mix_out(x_ref, oa_ref, ob_ref, wa_ref, wb_ref, gate_ref, pg_ref):
    y = _dot(oa_ref[0], wa_ref[...]) + _dot(ob_ref[0], wb_ref[...])
    return x_ref[0] + gate_ref[0] * _rms(y, pg_ref[...])


def _post_even_kernel(x_ref, oa_ref, ob_ref, wa_ref, wb_ref, gate_ref, pg_ref,
                      fg_ref, fsh_ref, fsc_ref, fgate_ref, fpg_ref, wg_ref, wu_ref, wd_ref, o_ref):
    x1 = _mix_out(x_ref, oa_ref, ob_ref, wa_ref, wb_ref, gate_ref, pg_ref)
    hb = _prenorm_mod(x1, fg_ref[...], fsh_ref[0], fsc_ref[0]).astype(BF16)
    y = _swiglu(hb, wg_ref, wu_ref, wd_ref, FF_CHUNKS, ())
    o_ref[0] = x1 + fgate_ref[0] * _rms(y, fpg_ref[...])


def _post_even(x, oa, ob, wa, wb, gate, pg, fg, fsh, fsc, fgate, fpg, wg, wu, wd):
    b, s, d = x.shape
    tm = TOK_TILE
    tok = lambda w: pl.BlockSpec((1, tm, w), lambda bi, i: (bi, i, 0))
    vec = pl.BlockSpec((1, d), lambda bi, i: (0, 0))
    mod = pl.BlockSpec((1, 1, d), lambda bi, i: (bi, 0, 0))
    full = lambda a: _resident(a.shape, lambda bi, i: (0,) * a.ndim)
    return pl.pallas_call(
        _post_even_kernel,
        grid=(b, s // tm),
        in_specs=[tok(d), tok(512), tok(512), full(wa), full(wb), mod, vec,
                  vec, mod, mod, mod, vec, full(wg), full(wu), full(wd)],
        out_specs=tok(d),
        out_shape=jax.ShapeDtypeStruct((b, s, d), F32),
        compiler_params=_cparams(("parallel", "parallel")),
        name="post_even",
    )(x, oa, ob, wa, wb, gate, pg, fg, fsh, fsc, fgate, fpg, wg, wu, wd)


def _post_odd_kernel(x_ref, oa_ref, ob_ref, wa_ref, wb_ref, gate_ref, pg_ref,
                     fg_ref, fsh_ref, fsc_ref, rw_ref, rb_ref, x_out, h_out, r_out, cnt_ref):
    x1 = _mix_out(x_ref, oa_ref, ob_ref, wa_ref, wb_ref, gate_ref, pg_ref)
    x_out[0] = x1
    h = _prenorm_mod(x1, fg_ref[...], fsh_ref[0], fsc_ref[0])
    h_out[0] = h
    logits = _dot(h, rw_ref[...]) + rb_ref[...]
    lane = lax.broadcasted_iota(jnp.int32, logits.shape, 1)
    m1 = jnp.max(logits, axis=-1, keepdims=True)
    i1 = jnp.min(jnp.where(logits == m1, lane, LANES), axis=-1, keepdims=True)
    rest = jnp.where(lane == i1, NEG, logits)
    m2 = jnp.max(rest, axis=-1, keepdims=True)
    i2 = jnp.min(jnp.where(rest == m2, lane, LANES), axis=-1, keepdims=True)
    e2 = jnp.exp(m2 - m1)
    w1 = 1.0 / (1.0 + e2)
    w2 = e2 / (1.0 + e2)
    @pl.when(jnp.logical_and(pl.program_id(0) == 0, pl.program_id(1) == 0))
    def _():
        cnt_ref[...] = jnp.zeros_like(cnt_ref)

    tm = logits.shape[0]
    sel = jnp.logical_or(lane == i1, lane == i2)
    earlier = (lax.broadcasted_iota(jnp.int32, (tm, tm), 1)
               < lax.broadcasted_iota(jnp.int32, (tm, tm), 0)).astype(BF16)
    prefix = _dot(earlier, sel.astype(BF16)) + cnt_ref[0:1, :]
    rank1 = jnp.sum(jnp.where(lane == i1, prefix, 0.0), axis=-1, keepdims=True)
    rank2 = jnp.sum(jnp.where(lane == i2, prefix, 0.0), axis=-1, keepdims=True)
    cnt_ref[...] = cnt_ref[...] + jnp.sum(sel.astype(F32), axis=0, keepdims=True)
    r = jnp.where(lane == 0, i1.astype(F32), 0.0)
    r = jnp.where(lane == 1, i2.astype(F32), r)
    r = jnp.where(lane == 2, w1, r)
    r = jnp.where(lane == 3, w2, r)
    r = jnp.where(lane == 4, rank1, r)
    r = jnp.where(lane == 5, rank2, r)
    r_out[0] = r


def _post_odd(x, oa, ob, wa, wb, gate, pg, fg, fsh, fsc, rw, rb):
    b, s, d = x.shape
    tm = TOK_TILE
    tok = lambda w: pl.BlockSpec((1, tm, w), lambda bi, i: (bi, i, 0))
    vec = pl.BlockSpec((1, d), lambda bi, i: (0, 0))
    mod = pl.BlockSpec((1, 1, d), lambda bi, i: (bi, 0, 0))
    full = lambda a: _resident(a.shape, lambda bi, i: (0,) * a.ndim)
    return pl.pallas_call(
        _post_odd_kernel,
        grid=(b, s // tm),
        in_specs=[tok(d), tok(512), tok(512), full(wa), full(wb), mod, vec,
                  vec, mod, mod, full(rw), pl.BlockSpec((1, LANES), lambda bi, i: (0, 0))],
        out_specs=(tok(d), tok(d), tok(LANES),
                   pl.BlockSpec((8, LANES), lambda bi, i: (0, 0))),
        out_shape=(jax.ShapeDtypeStruct((b, s, d), F32), jax.ShapeDtypeStruct((b, s, d), F32),
                   jax.ShapeDtypeStruct((b, s, LANES), F32), jax.ShapeDtypeStruct((8, LANES), F32)),
        compiler_params=_cparams(("arbitrary", "arbitrary")),
        name="post_odd",
    )(x, oa, ob, wa, wb, gate, pg, fg, fsh, fsc, rw, rb)


def _store_rows(ref, lead, val):
    for c in range(ROW_SUB):
        ref[lead + (slice(None), c, slice(None))] = val[:, c * LANES:(c + 1) * LANES]


def _load_rows(ref, lead, start, size):
    return jnp.concatenate([ref[lead, pl.ds(start, size), c, :] for c in range(ROW_SUB)], axis=1)


class _TileGather:
    def __init__(self, idx_hbm, src_hbm, idx_smem, buf, isem, sem, tile_rows):
        self.idx_hbm, self.src_hbm, self.idx_smem, self.buf = idx_hbm, src_hbm, idx_smem, buf
        self.isem, self.sem, self.tile_rows = isem, sem, tile_rows
        self.i = pl.program_id(0)
        self.nt = pl.num_programs(0)
        self.n = buf.shape[1]
        self.slot = self.i % 2
        self.nxt = 1 - self.slot

    def _idx_copy(self, t, sl):
        t = jnp.minimum(t, self.nt - 1)
        return pltpu.make_async_copy(self.idx_hbm.at[t], self.idx_smem.at[sl], self.isem.at[sl])

    def _rows_wait(self, sl):
        pltpu.make_async_copy(self.src_hbm.at[pl.ds(0, self.n)], self.buf.at[sl], self.sem.at[sl]).wait()

    def _issue_row(self, sl, r, priority=0):
        t = self.idx_smem[sl, r]
        if self.tile_rows:
            src, dst = self.src_hbm.at[t], self.buf.at[sl, r]
        else:
            src, dst = self.src_hbm.at[pl.ds(t, 1)], self.buf.at[sl, pl.ds(r, 1)]
        pltpu.make_async_copy(src, dst, self.sem.at[sl]).start(priority)

    def _issue_loop(self, sl):
        def body(r, carry):
            self._issue_row(sl, r)
            return carry

        lax.fori_loop(0, self.n, body, 0, unroll=8)

    def begin(self):
        @pl.when(self.i == 0)
        def _():
            first = self._idx_copy(0, 0)
            first.start()
            first.wait()
            self._issue_loop(0)
            self._idx_copy(1, 1).start()

        self._idx_copy(self.i + 1, self.nxt).wait()
        self._idx_copy(self.i + 2, self.slot).start()
        self._rows_wait(self.slot)

    def issue(self, part, parts):
        per = -(-self.n // parts)
        for r in range(part * per, min((part + 1) * per, self.n)):
            self._issue_row(self.nxt, r, priority=r % 2)

    def issue_all(self):
        self._issue_loop(self.nxt)

    def anchor(self, zero):
        return zero

    def finish(self):
        @pl.when(self.i == self.nt - 1)
        def _():
            self._rows_wait(self.nxt)
            self._idx_copy(self.i + 2, self.slot).wait()


def _moe_kernel(te_ref, nu_ref, zero_ref, tok_hbm, h_hbm, wg_ref, wu_ref, wd_ref, y_ref, idx_smem, buf, isem, sem):
    g = _TileGather(tok_hbm, h_hbm, idx_smem, buf, isem, sem, tile_rows=False)
    n = g.n
    g.begin()

    @pl.when(g.i < nu_ref[0])
    def _():
        anchor = [0]
        head = 16
        rest = buf[g.slot, head:n, :].astype(BF16)

        def rows_of(ci):
            first = buf[g.slot, pl.ds(pl.multiple_of(anchor[0], head), head), :].astype(BF16)
            return jnp.concatenate([first, rest], axis=0)

        def issue_part(ci):
            if ci < len(EXPERT_CHUNKS) - 1:
                g.issue(ci, len(EXPERT_CHUNKS) - 1)
                anchor[0] = g.anchor(zero_ref[0])

        y = _swiglu(rows_of, wg_ref, wu_ref, wd_ref, EXPERT_CHUNKS, (0,), between=issue_part)
        _store_rows(y_ref, (), y)

    @pl.when(g.i >= nu_ref[0])
    def _():
        g.issue_all()
        y_ref[...] = jnp.zeros_like(y_ref)

    g.finish()


def _moe(tile_expert, n_used, row_token, h, wg, wu, wd):
    d = h.shape[1]
    nt, tm = row_token.shape
    dff = wg.shape[2]
    wspec = lambda shp: pl.BlockSpec(shp, lambda i, te, nu, z: (te[i], 0, 0), pipeline_mode=pl.Buffered(1))
    grid_spec = pltpu.PrefetchScalarGridSpec(
        num_scalar_prefetch=3,
        grid=(nt,),
        in_specs=[
            pl.BlockSpec(memory_space=pl.ANY),
            pl.BlockSpec(memory_space=pl.ANY),
            wspec((1, d, dff)), wspec((1, d, dff)), wspec((1, dff, d)),
        ],
        out_specs=pl.BlockSpec((tm, ROW_SUB, LANES), lambda i, te, nu, z: (i, 0, 0)),
        scratch_shapes=[
            pltpu.SMEM((2, tm), jnp.int32),
            pltpu.VMEM((2, tm, d), F32),
            pltpu.SemaphoreType.DMA((2,)),
            pltpu.SemaphoreType.DMA((2,)),
        ],
    )
    return pl.pallas_call(
        _moe_kernel,
        grid_spec=grid_spec,
        out_shape=jax.ShapeDtypeStruct((nt * tm, ROW_SUB, LANES), F32),
        compiler_params=_cparams(("arbitrary",)),
        name="moe_experts",
    )(tile_expert, n_used, jnp.zeros((1,), jnp.int32), row_token, h, wg, wu, wd)


def _combine_kernel(zero_ref, pos_hbm, y_hbm, x_ref, r_ref, gate_ref, pg_ref, o_ref, idx_smem, buf, isem, sem):
    tm = x_ref.shape[0]
    g = _TileGather(pos_hbm, y_hbm, idx_smem, buf, isem, sem, tile_rows=True)
    g.begin()
    parts = 4
    rows = tm // parts
    off = 0
    for c in range(parts):
        lo = c * rows
        first = _load_rows(buf, g.slot, pl.multiple_of(off + lo, 8), rows)
        second = _load_rows(buf, g.slot, pl.multiple_of(off + tm + lo, 8), rows)
        r = r_ref[lo:lo + rows, :]
        y = r[:, 2:3] * first + r[:, 3:4] * second
        o_ref[lo:lo + rows, :] = x_ref[lo:lo + rows, :] + gate_ref[0] * _rms(y, pg_ref[...])
        if c < parts - 1:
            g.issue(c, parts - 1)
            off = g.anchor(zero_ref[0])
    g.finish()


def _combine(pos, y, x, r, gate, pg, tokens_per_seq):
    n_tok, d = x.shape
    nt, tm2 = pos.shape
    tm = tm2 // 2
    per_seq = tokens_per_seq // tm
    tok = lambda w: pl.BlockSpec((tm, w), lambda i, z: (i, 0))
    grid_spec = pltpu.PrefetchScalarGridSpec(
        num_scalar_prefetch=1,
        grid=(nt,),
        in_specs=[
            pl.BlockSpec(memory_space=pl.ANY),
            pl.BlockSpec(memory_space=pl.ANY),
            tok(d), tok(LANES),
            pl.BlockSpec((1, 1, d), lambda i, z: (i // per_seq, 0, 0)),
            pl.BlockSpec((1, d), lambda i, z: (0, 0)),
        ],
        out_specs=tok(d),
        scratch_shapes=[
            pltpu.SMEM((2, tm2), jnp.int32),
            pltpu.VMEM((2, tm2, ROW_SUB, LANES), F32),
            pltpu.SemaphoreType.DMA((2,)),
            pltpu.SemaphoreType.DMA((2,)),
        ],
    )
    return pl.pallas_call(
        _combine_kernel,
        grid_spec=grid_spec,
        out_shape=jax.ShapeDtypeStruct((n_tok, d), F32),
        compiler_params=_cparams(("arbitrary",)),
        name="moe_combine",
    )(jnp.zeros((1,), jnp.int32), pos, y, x, r, gate, pg)


def _t5_bucket(dist):
    max_exact = REL_BUCKETS // 2
    d = jnp.maximum(dist, 1).astype(F32)
    log_b = max_exact + (jnp.log(d / max_exact) / math.log(REL_MAX_DIST / max_exact)
                         * (REL_BUCKETS - max_exact)).astype(jnp.int32)
    log_b = jnp.minimum(log_b, REL_BUCKETS - 1)
    return jnp.where(dist < max_exact, dist, log_b)


def _rope_tables(s):
    half = MLA_ROPE // 2
    freqs = ROPE_THETA ** (-jnp.arange(half, dtype=F32) / half)
    ang = jnp.arange(s, dtype=F32)[:, None] * freqs[None, :]
    cos, sin = jnp.cos(ang), jnp.sin(ang)
    z64 = jnp.zeros((s, MLA_NOPE), F32)
    z32 = jnp.zeros((s, LANES - MLA_NOPE - MLA_ROPE), F32)
    ck = jnp.concatenate([z64, cos, cos, z32], axis=1)
    cq = jnp.concatenate([jnp.ones((s, MLA_NOPE), F32), cos, cos, z32], axis=1)
    sn = jnp.concatenate([z64, sin, sin, z32], axis=1)
    return cq, ck, sn


def _even_weights(w_in, w_uq, w_ukv):
    d = w_in.shape[0]
    half = MLA_ROPE // 2
    w_cq = w_in[:, :MLA_Q_RANK]
    w_ckv = w_in[:, MLA_Q_RANK:MLA_Q_RANK + MLA_KV_RANK]
    w_kr = w_in[:, MLA_Q_RANK + MLA_KV_RANK:MLA_Q_RANK + MLA_KV_RANK + MLA_ROPE]
    w_qkv = w_in[:, MLA_Q_RANK + MLA_KV_RANK + MLA_ROPE:]
    z = lambda n: jnp.zeros((d, n), F32)
    kr_a = jnp.concatenate([z(MLA_NOPE), w_kr, z(32)], axis=1)
    kr_b = jnp.concatenate([z(MLA_NOPE), -w_kr[:, half:], w_kr[:, :half], z(32)], axis=1)
    dil_scale = DIL_HD ** -0.5 * LOG2E
    w0 = jnp.concatenate([w_cq, w_ckv, kr_a, kr_b, w_qkv[:, :512] * dil_scale, w_qkv[:, 512:]], axis=1)

    r = w_uq.shape[0]
    wq = w_uq.reshape(r, MLA_HEADS, MLA_NOPE + MLA_ROPE) * ((MLA_NOPE + MLA_ROPE) ** -0.5 * LOG2E)
    zq = lambda n: jnp.zeros((r, MLA_HEADS, n), F32)
    nope, x1, x2 = wq[..., :MLA_NOPE], wq[..., MLA_NOPE:MLA_NOPE + half], wq[..., MLA_NOPE + half:]
    q_a = jnp.concatenate([nope, x1, x2, zq(32)], axis=-1).reshape(r, MLA_HEADS * LANES)
    q_b = jnp.concatenate([zq(MLA_NOPE), -x2, x1, zq(32)], axis=-1).reshape(r, MLA_HEADS * LANES)
    wq2 = jnp.concatenate([q_a, q_b], axis=1)

    rk = w_ukv.shape[0]
    wkv = w_ukv.reshape(rk, MLA_HEADS, MLA_NOPE + MLA_V)
    k_blk = jnp.concatenate([wkv[..., :MLA_NOPE], jnp.zeros((rk, MLA_HEADS, LANES - MLA_NOPE), F32)], axis=-1)
    wkv2 = jnp.concatenate([k_blk.reshape(rk, MLA_HEADS * LANES),
                            wkv[..., MLA_NOPE:].reshape(rk, MLA_HEADS * MLA_V)], axis=1)
    return w0.astype(BF16), wq2.astype(BF16), wkv2.astype(BF16)


def _toeplitz(vec, rows, cols):
    n, width = vec.shape

    def toeplitz_kernel(v_ref, o_ref):
        tiled = jnp.broadcast_to(v_ref[0], (rows, width))
        o_ref[0] = pltpu.roll(tiled, 0, 1, stride=1, stride_axis=0)[:, :cols]

    return pl.pallas_call(
        toeplitz_kernel,
        grid=(n,),
        in_specs=[pl.BlockSpec((1, 1, width), lambda t: (t, 0, 0))],
        out_specs=pl.BlockSpec((1, rows, cols), lambda t: (t, 0, 0)),
        out_shape=jax.ShapeDtypeStruct((n, rows, cols), F32),
        compiler_params=_cparams(("parallel",)),
        name="toeplitz_bias",
    )(vec.reshape(n, 1, width).astype(F32))


def _dil_bias(rel_bias):
    blk = DIL_BLOCK
    width = 4 * blk
    k = jnp.arange(width)
    rel = jnp.where(k < 2 * blk, blk - k, blk + width - k)
    out = []
    for window, dil in DIL_PATTERNS:
        band = (rel >= 0) & (rel <= window // dil)
        bias = rel_bias[_t5_bucket(jnp.maximum(rel, 0) * dil)] * LOG2E
        out.append(jnp.where(band[:, None], bias, NEG).T)
    vec = jnp.stack(out).reshape(3 * DIL_HEADS, width)
    return _toeplitz(vec, blk, 2 * blk).reshape(3, DIL_HEADS, blk, 2 * blk)


def _diff_bias(rel_bias, tile):
    nd = REL_MAX_DIST // tile + 1
    maps = rel_bias.shape[1]
    k = jnp.arange(2 * tile)[None, :]
    dist = jnp.arange(nd)[:, None] * tile + jnp.where(k < tile, k, k - 2 * tile)
    vec = jnp.where((dist >= 0)[..., None], rel_bias[_t5_bucket(jnp.maximum(dist, 0))] * LOG2E, NEG)
    vec = jnp.transpose(vec, (2, 0, 1)).reshape(maps * nd, 2 * tile)
    bias = _toeplitz(vec, tile, tile).reshape(DIFF_HEADS, 2, nd, tile, tile)
    far = rel_bias[_t5_bucket(jnp.array(REL_MAX_DIST))] * LOG2E
    far = jnp.broadcast_to(far.reshape(DIFF_HEADS, 2, 1, 1), (DIFF_HEADS, 2, 8, LANES))
    return bias, far.astype(F32)


def _routing(r, counts, n_tok, tile):
    n_tiles = (2 * n_tok) // tile + N_EXPERTS
    e = jnp.concatenate([r[:, 0], r[:, 1]]).astype(jnp.int32)
    rank = jnp.concatenate([r[:, 4], r[:, 5]]).astype(jnp.int32)
    counts = counts.astype(jnp.int32)
    padded = ((counts + tile - 1) // tile) * tile
    ends = jnp.cumsum(padded)
    starts = ends - padded
    onehot = (e[:, None] == jnp.arange(N_EXPERTS)[None, :]).astype(jnp.int32)
    pos = jnp.sum(onehot * starts[None, :], axis=1) + rank
    token = jnp.tile(jnp.arange(n_tok, dtype=jnp.int32), 2)
    row_token = jnp.zeros((n_tiles * tile,), jnp.int32).at[pos].set(token)
    tile_start = jnp.arange(n_tiles, dtype=jnp.int32) * tile
    tile_expert = jnp.sum((tile_start[:, None] >= ends[None, :]).astype(jnp.int32), axis=1)
    n_used = (ends[-1] // tile).astype(jnp.int32)
    last = jnp.sum((ends[-1] - 1 >= ends).astype(jnp.int32))
    tile_expert = jnp.minimum(tile_expert, last).astype(jnp.int32)
    return (tile_expert, n_used.reshape(1), row_token.reshape(n_tiles, tile),
            pos[:n_tok].astype(jnp.int32), pos[n_tok:].astype(jnp.int32))


def kernel(x, c, rel_bias, ada_mix_w, ada_mix_b, mix_pre_g, mix_post_g, ada_ffn_w, ada_ffn_b, ffn_pre_g, ffn_post_g, e_w_in, e_q_norm_g, e_w_uq, e_kv_norm_g, e_w_ukv, e_w_out, ffn_w_gate, ffn_w_up, ffn_w_down, o_w_in, diff_lq1, diff_lk1, diff_lq2, diff_lk2, diff_sub_g, o_w_out, router_w, router_b, moe_w_gate, moe_w_up, moe_w_down):
    b, s, d = x.shape
    assert d == D_MODEL and s % DIL_SUPER == 0 and s % TOK_TILE == 0
    row = lambda v: v.reshape(1, -1).astype(F32)

    mix_mod = _ada(c, ada_mix_w, ada_mix_b)
    ffn_mod = _ada(c, ada_ffn_w, ada_ffn_b)

    shift, scale, gate = _split_mod(mix_mod[0])
    w0, wq2, wkv2 = _even_weights(e_w_in[0], e_w_uq[0], e_w_ukv[0])
    cq, ck, sn = _rope_tables(s)
    qat, ka, vat, qb, kb, vb = _even_in(x, row(mix_pre_g[0]), shift, scale, w0, row(e_q_norm_g[0]), wq2,
                                        row(e_kv_norm_g[0]), wkv2, cq, ck, sn)
    o_a = _mla(qat, ka, vat)
    o_b = _dil(qb, kb, vb, _dil_bias(rel_bias))
    fshift, fscale, fgate = _split_mod(ffn_mod[0])
    w_out = e_w_out[0].astype(BF16)
    x = _post_even(x, o_a, o_b, w_out[:512], w_out[512:], gate, row(mix_post_g[0]),
                   row(ffn_pre_g[0]), fshift, fscale, fgate, row(ffn_post_g[0]),
                   ffn_w_gate[0].astype(BF16), ffn_w_up[0].astype(BF16), ffn_w_down[0].astype(BF16))

    layer = 1
    shift, scale, gate = _split_mod(mix_mod[1])
    w_in = o_w_in[0]
    att_scale = DIFF_HD ** -0.5
    w1 = jnp.concatenate([w_in[:, :512] * (att_scale * LOG2E), w_in[:, 512:1536],
                          w_in[:, 1536:2048] * (SB_HD ** -0.5 * LOG2E), w_in[:, 2048:]], axis=1).astype(BF16)
    qdt, kd, vdt, qs, kst, vs = _odd_in(x, row(mix_pre_g[1]), shift, scale, w1)
    lam_init = 0.8 - 0.6 * math.exp(-0.3 * layer)
    lam = (jnp.exp(jnp.sum(diff_lq1[0].astype(F32) * diff_lk1[0].astype(F32)))
           - jnp.exp(jnp.sum(diff_lq2[0].astype(F32) * diff_lk2[0].astype(F32))) + lam_init)
    bias, far = _diff_bias(rel_bias, DIFF_TILE)
    o_c = _diff(qdt, kd, vdt, bias, far, jnp.full((1, LANES), lam, F32), row(diff_sub_g[0]), lam_init)
    o_d = _sb(qs, kst, vs)
    fshift, fscale, fgate = _split_mod(ffn_mod[1])
    w_out = o_w_out[0].astype(BF16)
    rw = jnp.zeros((d, LANES), F32).at[:, :N_EXPERTS].set(router_w[0].astype(F32))
    rb = jnp.full((1, LANES), NEG, F32).at[0, :N_EXPERTS].set(router_b[0].astype(F32))
    x, h, r, counts = _post_odd(x, o_c, o_d, w_out[:512], w_out[512:], gate, row(mix_post_g[1]),
                                row(ffn_pre_g[1]), fshift, fscale, rw, rb)

    n_tok = b * s
    tile_expert, n_used, row_token, pos0, pos1 = _routing(r.reshape(n_tok, LANES), counts[0, :N_EXPERTS],
                                                          n_tok, MOE_TILE)
    y = _moe(tile_expert, n_used, row_token, h.reshape(n_tok, d),
             moe_w_gate[0].astype(BF16), moe_w_up[0].astype(BF16), moe_w_down[0].astype(BF16))
    ct = TOK_TILE
    pos = jnp.concatenate([pos0.reshape(n_tok // ct, ct), pos1.reshape(n_tok // ct, ct)], axis=1)
    out = _combine(pos, y, x.reshape(n_tok, d), r.reshape(n_tok, LANES), fgate, row(ffn_post_g[1]), s)
    return out.reshape(b, s, d)
```
